```python
import jax
import jax.numpy as jnp
from jax import lax
import numpy as np

D_MODEL = 1024
BATCH = 1
SEQ = 16384
DEPTH = 4

GRID_W = 64
CTX_LEN = 256
N_MIXERS = 2
N_MOD = 6
DN_ALPHA = (2.0 * DEPTH) ** 0.25
DN_BETA = (8.0 * DEPTH) ** -0.25
LN_EPS = 1e-5

MLSTM_HEADS = 4
MLSTM_DH = D_MODEL // MLSTM_HEADS
MLSTM_CHUNK = 128
MLSTM_PROJ = 4 * D_MODEL + 4 * MLSTM_HEADS
CONV_K = 3
M_INIT = -1e30

RWKV_N = 64
RWKV_HEADS = D_MODEL // RWKV_N
DECAY_LORA = 64
AAA_LORA = 64
GATE_LORA = 160
RWKV_GN_EPS = 64e-5

N_KEYS = 128
N_EXPERTS = N_KEYS * N_KEYS
PEER_HEADS = 8
PEER_DQ = 256
PEER_TOPK = 16
PEER_BLOCK = 128

N_MLSTM_LAYERS = (DEPTH + 1) // 2
N_RWKV_LAYERS = DEPTH // 2

kernel_name = "hybrid_mlstm_rwkv7_peer_flow_block"


def _layernorm(x, g, b):
    xf = x.astype(jnp.float32)
    mu = jnp.mean(xf, -1, keepdims=True)
    var = jnp.mean(jnp.square(xf - mu), -1, keepdims=True)
    return ((xf - mu) * lax.rsqrt(var + LN_EPS)).astype(x.dtype) * g + b


def _head_norm(h, eps):
    hf = h.astype(jnp.float32)
    mu = jnp.mean(hf, -1, keepdims=True)
    var = jnp.mean(jnp.square(hf - mu), -1, keepdims=True)
    return (hf - mu) * lax.rsqrt(var + eps)


def _grid_dwconv(u, w, b, rows):
    bsz, n, ch = u.shape
    img = u.reshape(bsz, rows, GRID_W, ch)
    out = lax.conv_general_dilated(img, w[:, :, None, :].astype(u.dtype), (1, 1), "SAME",
                                   dimension_numbers=("NHWC", "HWIO", "NHWC"),
                                   feature_group_count=ch)
    return out.reshape(bsz, n, ch) + b


def _seq_dwconv(u, w, b):
    p = jnp.pad(u, ((0, 0), (1, 1), (0, 0)))
    wc = w[CONV_K // 2]
    return p[:, :-2] * wc[0] + p[:, 1:-1] * wc[1] + p[:, 2:] * wc[2] + b


def _qshift_grid(u, rows):
    bsz, n, d = u.shape
    q = d // 4
    p = jnp.pad(u.reshape(bsz, rows, GRID_W, d), ((0, 0), (1, 1), (1, 1), (0, 0)))
    out = jnp.concatenate([p[:, 1:-1, :-2, :q], p[:, 1:-1, 2:, q:2 * q],
                           p[:, :-2, 1:-1, 2 * q:3 * q], p[:, 2:, 1:-1, 3 * q:]], axis=-1)
    return out.reshape(bsz, n, d)


def _shift_seq(u):
    h = u.shape[-1] // 2
    p = jnp.pad(u, ((0, 0), (1, 1), (0, 0)))
    return jnp.concatenate([p[:, :-2, :h], p[:, 2:, h:]], axis=-1)


def _mlstm_scan(q, k, v, ig, lf, state):
    bsz, nh, t, dh = q.shape
    nc = t // MLSTM_CHUNK

    def to_chunks(a):
        return jnp.moveaxis(a.reshape(bsz, nh, nc, MLSTM_CHUNK, *a.shape[3:]), 2, 0)

    causal = jnp.tril(jnp.ones((MLSTM_CHUNK, MLSTM_CHUNK), bool))

    def step(carry, inp):
        c_st, n_st, m_st = carry
        qc, kc, vc, ic, fc = inp
        b = jnp.cumsum(fc, axis=-1)
        dlog = b[..., :, None] - b[..., None, :] + ic[..., None, :]
        dlog = jnp.where(causal, dlog, -jnp.inf)
        m_inter = b + m_st[..., None]
        m_t = jnp.maximum(m_inter, jnp.max(dlog, axis=-1))
        s = jnp.einsum('bhtd,bhsd->bhts', qc, kc) * jnp.exp(dlog - m_t[..., None])
        dec = jnp.exp(m_inter - m_t)
        num = jnp.einsum('bhts,bhsd->bhtd', s, vc) + dec[..., None] * jnp.einsum('bhtd,bhde->bhte', qc, c_st)
        den = jnp.sum(s, -1) + dec * jnp.einsum('bhtd,bhd->bht', qc, n_st)
        h = num / jnp.maximum(jnp.abs(den), jnp.exp(-m_t))[..., None]
        b_last = b[..., -1]
        w_s = b_last[..., None] - b + ic
        m_new = jnp.maximum(b_last + m_st, jnp.max(w_s, axis=-1))
        a_s = jnp.exp(w_s - m_new[..., None])
        g_prev = jnp.exp(b_last + m_st - m_new)
        c_new = g_prev[..., None, None] * c_st + jnp.einsum('bhsd,bhse->bhde', a_s[..., None] * kc, vc)
        n_new = g_prev[..., None] * n_st + jnp.einsum('bhs,bhsd->bhd', a_s, kc)
        return (c_new, n_new, m_new), h

    state, hs = lax.scan(step, state, tuple(to_chunks(a) for a in (q, k, v, ig, lf)))
    return jnp.moveaxis(hs, 0, 2).reshape(bsz, nh, t, dh), state


def _mlstm_mixer(hc, hx, rows, w_in, b_in, conv_w, conv_b, hn_g, w_out, need_ctx):
    d = D_MODEL
    f32 = jnp.float32

    def prep(h, conv):
        bsz, n = h.shape[:2]
        p = h @ w_in + b_in
        qk = jax.nn.silu(conv(p[..., :2 * d]))

        def heads(a):
            return jnp.swapaxes(a.reshape(bsz, n, MLSTM_HEADS, MLSTM_DH), 1, 2).astype(f32)
        q = heads(qk[..., :d]) * (MLSTM_DH ** -0.5)
        k = heads(qk[..., d:])
        v = heads(p[..., 2 * d:3 * d])
        o = jax.nn.sigmoid(p[..., 3 * d:4 * d])
        g = jnp.transpose(p[..., 4 * d:].astype(f32).reshape(bsz, n, 4, MLSTM_HEADS), (2, 0, 3, 1))
        return q, k, v, o, g

    def bidir(q, k, v, g, st_f, st_b):
        flip = lambda a: jnp.flip(a, axis=2)
        h_f, st_f = _mlstm_scan(q, k, v, g[0], jax.nn.log_sigmoid(g[2]), st_f)
        h_b, st_b = _mlstm_scan(flip(q), flip(k), flip(v), flip(g[1]), flip(jax.nn.log_sigmoid(g[3])), st_b)
        return h_f + flip(h_b), st_f, st_b

    def finish(h, o):
        bsz, _, n, _ = h.shape
        hn = jnp.swapaxes(_head_norm(h, LN_EPS), 1, 2).reshape(bsz, n, d).astype(o.dtype)
        return (o * hn * hn_g) @ w_out

    bsz = hx.shape[0]
    zero = (jnp.zeros((bsz, MLSTM_HEADS, MLSTM_DH, MLSTM_DH), f32),
            jnp.zeros((bsz, MLSTM_HEADS, MLSTM_DH), f32),
            jnp.full((bsz, MLSTM_HEADS), M_INIT, f32))
    qc, kc, vc, oc, gc = prep(hc, lambda u: _seq_dwconv(u, conv_w, conv_b))
    h_c, st_f, st_b = bidir(qc, kc, vc, gc, zero, zero)
    qx, kx, vx, ox, gx = prep(hx, lambda u: _grid_dwconv(u, conv_w, conv_b, rows))
    h_x, _, _ = bidir(qx, kx, vx, gx, st_f, st_b)
    y_x = finish(h_x, ox)
    y_c = finish(h_c, oc) if need_ctx else None
    return y_c, y_x


def _rwkv7_scan(r, w, k, v, kk, kka, state):
    def step(s, inp):
        r_t, w_t, k_t, v_t, kk_t, kka_t = inp
        sa = jnp.einsum('bhvk,bhk->bhv', s, kk_t)
        s = s * w_t[:, :, None, :] - sa[..., None] * kka_t[:, :, None, :] + v_t[..., None] * k_t[:, :, None, :]
        return s, jnp.einsum('bhvk,bhk->bhv', s, r_t)
    state, ys = lax.scan(step, state, (r, w, k, v, kk, kka))
    return ys, state


def _rwkv7_mixer(hc, hx, rows, mu, w_rkv, w0, w1, w2, a0, a1, a2, g1, g2, k_k, k_a, r_k,
                 lnx_g, lnx_b, w_out, need_ctx):
    f32 = jnp.float32
    d = D_MODEL

    def prep(h, shifted):
        bsz, n, _ = h.shape
        xm = h[None] + (shifted - h)[None] * mu[:, None, None, :]
        r, k, v = jnp.einsum('nbtd,nde->nbte', xm[:3], w_rkv)
        wpre = w0[:, None, None, :] + jnp.einsum(
            'zbte,zed->zbtd', jnp.tanh(jnp.einsum('btd,zde->zbte', xm[3], w1)), w2)
        decay = jnp.exp(-jnp.exp(-jax.nn.softplus(-wpre.astype(f32)) - 0.5))
        a = jax.nn.sigmoid((a0[:, None, None, :] + jnp.einsum(
            'zbte,zed->zbtd', jnp.einsum('btd,zde->zbte', xm[4], a1), a2)).astype(f32))
        g = jax.nn.sigmoid(xm[5] @ g1) @ g2
        kk = (k * k_k).astype(f32).reshape(bsz, n, RWKV_HEADS, RWKV_N)
        kk = (kk / jnp.maximum(jnp.linalg.norm(kk, axis=-1, keepdims=True), 1e-12)).reshape(bsz, n, d)
        ktil = k.astype(f32)[None] * (1.0 + (a - 1.0) * k_a.astype(f32))
        return r, v, g, kk, decay, a, ktil

    def tm(t):
        return jnp.swapaxes(t, 0, 1).reshape(t.shape[1], t.shape[0], RWKV_HEADS, RWKV_N).astype(f32)

    def bidir(pp, st_f, st_b):
        r, v, g, kk, decay, a, ktil = pp
        ins_f = [tm(t) for t in (r, decay[0], ktil[0], v, kk, kk * a[0])]
        ins_b = [jnp.flip(tm(t), 0) for t in (r, decay[1], ktil[1], v, kk, kk * a[1])]
        y_f, st_f = _rwkv7_scan(*ins_f, st_f)
        y_b, st_b = _rwkv7_scan(*ins_b, st_b)
        return y_f + jnp.flip(y_b, 0), st_f, st_b

    def finish(y, pp):
        r, v, g, kk, decay, a, ktil = pp
        n, bsz = y.shape[:2]
        yn = jnp.swapaxes(_head_norm(y, RWKV_GN_EPS), 0, 1).reshape(bsz, n, d) * lnx_g + lnx_b
        kbar = 0.5 * (ktil[0] + ktil[1])
        bonus = jnp.sum((r * kbar * r_k).reshape(bsz, n, RWKV_HEADS, RWKV_N), -1, keepdims=True) \
            * v.reshape(bsz, n, RWKV_HEADS, RWKV_N)
        return ((yn + bonus.reshape(bsz, n, d)) * g).astype(v.dtype) @ w_out

    bsz = hx.shape[0]
    zero = jnp.zeros((bsz, RWKV_HEADS, RWKV_N, RWKV_N), f32)
    pc = prep(hc, _shift_seq(hc))
    y_c, st_f, st_b = bidir(pc, zero, zero)
    px = prep(hx, _qshift_grid(hx, rows))
    y_x, _, _ = bidir(px, st_f, st_b)
    out_x = finish(y_x, px)
    out_c = finish(y_c, pc) if need_ctx else None
    return out_c, out_x


def _peer(h, wq, keys, u_tab, v_tab):
    bsz, n, d = h.shape
    hk = PEER_DQ // 2
    kf = keys.astype(jnp.float32)

    def block(xb):
        nt = xb.shape[0]
        q = (xb @ wq).astype(jnp.float32).reshape(nt, PEER_HEADS, 2, hk)
        s = jnp.einsum('thpd,hpkd->thpk', q, kf)
        sv, si = lax.top_k(s, PEER_TOPK)
        cand = (sv[:, :, 0, :, None] + sv[:, :, 1, None, :]).reshape(nt, PEER_HEADS, PEER_TOPK ** 2)
        cidx = (si[:, :, 0, :, None] * N_KEYS + si[:, :, 1, None, :]).reshape(nt, PEER_HEADS, PEER_TOPK ** 2)
        fv, fi = lax.top_k(cand, PEER_TOPK)
        eidx = jnp.take_along_axis(cidx, fi, axis=-1).reshape(nt, PEER_HEADS * PEER_TOPK)
        gate = jax.nn.softmax(fv, axis=-1).reshape(nt, PEER_HEADS * PEER_TOPK)
        u = jnp.take(u_tab, eidx, axis=0)
        act = jax.nn.gelu(jnp.einsum('ted,td->te', u, xb), approximate=False)
        wgt = (gate * act.astype(jnp.float32)).astype(xb.dtype)
        return jnp.einsum('te,ted->td', wgt, jnp.take(v_tab, eidx, axis=0))

    out = lax.map(block, h.reshape(-1, PEER_BLOCK, d))
    return out.reshape(bsz, n, d)


def setup_inputs(seed: int = 0) -> dict:
    key = jax.random.key(seed)
    ks = iter(jax.random.split(key, 48))
    f32 = jnp.float32
    nrm = lambda shape, scale: jax.random.normal(next(ks), shape, f32) * scale
    d = D_MODEL
    na, nb = N_MLSTM_LAYERS, N_RWKV_LAYERS
    hm = MLSTM_HEADS

    x = nrm((BATCH, SEQ, d), 1.0)
    c = nrm((BATCH, d), 1.0)
    ctx = nrm((BATCH, CTX_LEN, d), 1.0)
    c_ctx = nrm((d,), 1.0)
    ada_w = nrm((DEPTH, d, N_MOD * d), 0.5 * d ** -0.5)
    ada_b = nrm((DEPTH, N_MOD * d), 0.02)
    ln_g = 1.0 + nrm((DEPTH, 2, d), 0.02)
    ln_b = nrm((DEPTH, 2, d), 0.02)

    col_scale = jnp.concatenate([jnp.ones((2 * d,), f32), jnp.full((d,), DN_BETA, f32),
                                 jnp.ones((d,), f32), jnp.full((4 * hm,), 0.5, f32)])
    ml_w_in = nrm((na, d, MLSTM_PROJ), d ** -0.5) * col_scale
    gate_bias = jnp.concatenate([jnp.zeros((2 * hm,), f32), jnp.tile(jnp.linspace(3.0, 6.0, hm), 2)])
    ml_b_in = nrm((na, MLSTM_PROJ), 0.02) + jnp.concatenate([jnp.zeros((4 * d,), f32), gate_bias])
    ml_conv_w = nrm((na, CONV_K, CONV_K, 2 * d), 1.0 / 3.0)
    ml_conv_b = nrm((na, 2 * d), 0.02)
    ml_hn_g = 1.0 + nrm((na, d), 0.02)
    ml_w_out = nrm((na, d, d), d ** -0.5 * DN_BETA)

    rw_mu = jax.random.uniform(next(ks), (nb, 6, d), f32)
    rw_w_rkv = nrm((nb, 3, d, d), d ** -0.5) * jnp.array([1.0, 1.0, DN_BETA], f32)[None, :, None, None]
    rw_w0 = jnp.broadcast_to(jnp.linspace(-6.0, 1.0, d), (nb, 2, d)) + nrm((nb, 2, d), 0.1)
    rw_w1 = nrm((nb, 2, d, DECAY_LORA), d ** -0.5)
    rw_w2 = nrm((nb, 2, DECAY_LORA, d), 0.1 * DECAY_LORA ** -0.5)
    rw_a0 = nrm((nb, 2, d), 0.1)
    rw_a1 = nrm((nb, 2, d, AAA_LORA), d ** -0.5)
    rw_a2 = nrm((nb, 2, AAA_LORA, d), 0.5 * AAA_LORA ** -0.5)
    rw_g1 = nrm((nb, d, GATE_LORA), d ** -0.5)
    rw_g2 = nrm((nb, GATE_LORA, d), GATE_LORA ** -0.5)
    rw_k_k = 0.85 + nrm((nb, d), 0.02)
    rw_k_a = 1.0 + nrm((nb, d), 0.02)
    rw_r_k = nrm((nb, d), 0.1)
    rw_lnx_g = 1.0 + nrm((nb, d), 0.02)
    rw_lnx_b = nrm((nb, d), 0.02)
    rw_w_out = nrm((nb, d, d), d ** -0.5 * DN_BETA)

    pk_wq = nrm((DEPTH, d, PEER_HEADS * PEER_DQ), d ** -0.5)
    pk_keys = nrm((DEPTH, PEER_HEADS, 2, N_KEYS, PEER_DQ // 2), (PEER_DQ // 2) ** -0.5)
    pk_u = nrm((DEPTH, N_EXPERTS, d), d ** -0.5)
    pk_v = nrm((DEPTH, N_EXPERTS, d), DN_BETA)
    return {"x": x, "c": c, "ctx": ctx, "c_ctx": c_ctx,
            "ada_w": ada_w, "ada_b": ada_b, "ln_g": ln_g, "ln_b": ln_b,
            "ml_w_in": ml_w_in, "ml_b_in": ml_b_in, "ml_conv_w": ml_conv_w, "ml_conv_b": ml_conv_b,
            "ml_hn_g": ml_hn_g, "ml_w_out": ml_w_out,
            "rw_mu": rw_mu, "rw_w_rkv": rw_w_rkv, "rw_w0": rw_w0, "rw_w1": rw_w1, "rw_w2": rw_w2,
            "rw_a0": rw_a0, "rw_a1": rw_a1, "rw_a2": rw_a2, "rw_g1": rw_g1, "rw_g2": rw_g2,
            "rw_k_k": rw_k_k, "rw_k_a": rw_k_a, "rw_r_k": rw_r_k, "rw_lnx_g": rw_lnx_g,
            "rw_lnx_b": rw_lnx_b, "rw_w_out": rw_w_out,
            "pk_wq": pk_wq, "pk_keys": pk_keys, "pk_u": pk_u, "pk_v": pk_v}


def reference(x, c, ctx, c_ctx, ada_w, ada_b, ln_g, ln_b,
              ml_w_in, ml_b_in, ml_conv_w, ml_conv_b, ml_hn_g, ml_w_out,
              rw_mu, rw_w_rkv, rw_w0, rw_w1, rw_w2, rw_a0, rw_a1, rw_a2, rw_g1, rw_g2,
              rw_k_k, rw_k_a, rw_r_k, rw_lnx_g, rw_lnx_b, rw_w_out,
              pk_wq, pk_keys, pk_u, pk_v):
    rows = x.shape[1] // GRID_W
    n_ctx = ctx.shape[1]
    s_lat = jax.nn.silu(c)
    s_ctx = jax.nn.silu(c_ctx)
    for i in range(DEPTH):
        last = i == DEPTH - 1
        j = i // N_MIXERS
        mx = jnp.split((s_lat @ ada_w[i] + ada_b[i])[:, None, :], N_MOD, axis=-1)
        mc = jnp.split((s_ctx @ ada_w[i] + ada_b[i])[None, None, :], N_MOD, axis=-1)
        hx = x * (1.0 + mx[1]) + mx[0]
        hc = ctx * (1.0 + mc[1]) + mc[0]
        if i % N_MIXERS == 0:
            yc, yx = _mlstm_mixer(hc, hx, rows, ml_w_in[j], ml_b_in[j], ml_conv_w[j], ml_conv_b[j],
                                  ml_hn_g[j], ml_w_out[j], not last)
        else:
            yc, yx = _rwkv7_mixer(hc, hx, rows, rw_mu[j], rw_w_rkv[j], rw_w0[j], rw_w1[j], rw_w2[j],
                                  rw_a0[j], rw_a1[j], rw_a2[j], rw_g1[j], rw_g2[j], rw_k_k[j],
                                  rw_k_a[j], rw_r_k[j], rw_lnx_g[j], rw_lnx_b[j], rw_w_out[j], not last)
        x = _layernorm(DN_ALPHA * x + mx[2] * yx, ln_g[i, 0], ln_b[i, 0])
        hx = x * (1.0 + mx[4]) + mx[3]
        if last:
            yx = _peer(hx, pk_wq[i], pk_keys[i], pk_u[i], pk_v[i])
        else:
            ctx = _layernorm(DN_ALPHA * ctx + mc[2] * yc, ln_g[i, 0], ln_b[i, 0])
            hc = ctx * (1.0 + mc[4]) + mc[3]
            y = _peer(jnp.concatenate([hc, hx], axis=1), pk_wq[i], pk_keys[i], pk_u[i], pk_v[i])
            ctx = _layernorm(DN_ALPHA * ctx + mc[5] * y[:, :n_ctx], ln_g[i, 1], ln_b[i, 1])
            yx = y[:, n_ctx:]
        x = _layernorm(DN_ALPHA * x + mx[5] * yx, ln_g[i, 1], ln_b[i, 1])
    return x
```

```python
import functools

import jax
import jax.numpy as jnp
from jax import lax
from jax.experimental import pallas as pl
from jax.experimental.pallas import tpu as pltpu

F32 = jnp.float32
BF16 = jnp.bfloat16

D_MODEL = 1024
DEPTH = 4
GRID_W = 64
N_MOD = 6
DN_ALPHA = (2.0 * DEPTH) ** 0.25
LN_EPS = 1e-5

MLSTM_HEADS = 4
MLSTM_DH = D_MODEL // MLSTM_HEADS
MLSTM_CHUNK = 128
M_INIT = -1e30

RWKV_N = 64
RWKV_HEADS = D_MODEL // RWKV_N
RWKV_CHUNK = 64
RWKV_GN_EPS = 64e-5

N_KEYS = 128
PEER_HEADS = 8
PEER_DQ = 256
PEER_TOPK = 16
PEER_I_BLOCK = 8
PEER_TE = PEER_I_BLOCK * N_KEYS

LANES = 128
VMEM_LIMIT = 56 * 1024 * 1024

NN = ((1,), (0,))
NT = ((1,), (1,))
TN = ((0,), (0,))


def _split(x, n):
    parts = []
    r = x.astype(F32)
    for i in range(n):
        p = r.astype(BF16)
        parts.append(p)
        if i + 1 < n:
            r = r - p.astype(F32)
    return parts


def _d(a, b, dims):
    return lax.dot_general(a, b, (dims, ((), ())), preferred_element_type=F32)


def _mdot(a, b, dims, passes):
    if passes == 1:
        return _d(a.astype(BF16), b.astype(BF16), dims)
    if passes == 3:
        a0, a1 = _split(a, 2)
        b0, b1 = _split(b, 2)
        return (_d(a0, b1, dims) + _d(a1, b0, dims)) + _d(a0, b0, dims)
    a0, a1, a2 = _split(a, 3)
    b0, b1, b2 = _split(b, 3)
    lo = (_d(a0, b2, dims) + _d(a2, b0, dims)) + _d(a1, b1, dims)
    mid = _d(a0, b1, dims) + _d(a1, b0, dims)
    return (lo + mid) + _d(a0, b0, dims)


def _dot_01_lhs(m01, x):
    mb = m01.astype(BF16)
    x0, x1, x2 = _split(x, 3)
    return (_d(mb, x2, NN) + _d(mb, x1, NN)) + _d(mb, x0, NN)


def _dot_01_rhs(x, m01):
    mb = m01.astype(BF16)
    x0, x1, x2 = _split(x, 3)
    return (_d(x2, mb, NN) + _d(x1, mb, NN)) + _d(x0, mb, NN)


def _pick(n, cands):
    for c in cands:
        if n % c == 0:
            return c
    raise ValueError(f"no tile for {n}")


def _mm_body(x_ref, w_ref, b_ref, o_ref, *, passes):
    o_ref[...] = _mdot(x_ref[...], w_ref[...], NN, passes) + b_ref[...]


def matmul(x, w, b=None, *, passes=1):
    m, k = x.shape
    n = w.shape[1]
    assert n % LANES == 0 and w.shape[0] == k
    tm = m if m <= 1024 else _pick(m, (640, 512, 384, 256, 128))
    tn = _pick(n, (1024, 768, 640, 512, 384, 256, 128))
    if b is None:
        b = jnp.zeros((n,), F32)
    if passes == 1:
        w = w.astype(BF16)
    return pl.pallas_call(
        functools.partial(_mm_body, passes=passes),
        grid=(m // tm, n // tn),
        in_specs=[pl.BlockSpec((tm, k), lambda i, j: (i, 0)),
                  pl.BlockSpec((k, tn), lambda i, j: (0, j)),
                  pl.BlockSpec((1, tn), lambda i, j: (0, j))],
        out_specs=pl.BlockSpec((tm, tn), lambda i, j: (i, j)),
        out_shape=jax.ShapeDtypeStruct((m, n), F32),
        compiler_params=pltpu.CompilerParams(
            dimension_semantics=("arbitrary", "arbitrary"), vmem_limit_bytes=VMEM_LIMIT),
        name="proj_matmul",
    )(x, w, b.reshape(1, n).astype(F32))


def _mlstm_dir(q, k, v, gc, gr, c_ref, n_ref, m_ref, d):
    L = q.shape[0]
    row = lax.broadcasted_iota(jnp.int32, (L, L), 0)
    col = lax.broadcasted_iota(jnp.int32, (L, L), 1)
    lower = col <= row
    upper = col >= row
    seen = upper if d else lower
    tri_c = jnp.where(seen, 1.0, 0.0)
    tri_r = jnp.where(lower if d else upper, 1.0, 0.0)
    b_col = _dot_01_lhs(tri_c, gc)[:, 2 + d:3 + d]
    b_row = _dot_01_rhs(gr, tri_r)[2 + d:3 + d, :]
    ig_col = gc[:, d:d + 1]
    ig_row = gr[d:d + 1, :]
    m_st = m_ref[d, 0:1, 0:1]
    c_st = c_ref[d]
    n_st = n_ref[d]

    dlog = jnp.where(seen, b_col - b_row + ig_row, -jnp.inf)
    m_inter = b_col + m_st
    m_t = jnp.maximum(m_inter, jnp.max(dlog, axis=1, keepdims=True))
    qb = q.astype(BF16)
    kb = k.astype(BF16)
    vb = v.astype(BF16)
    s = _d(qb, kb, NT) * jnp.exp(dlog - m_t)
    dec = jnp.exp(m_inter - m_t)
    num = _d(s.astype(BF16), vb, NN) + dec * _d(qb, c_st.astype(BF16), NN)
    den = jnp.sum(s, axis=1, keepdims=True) + dec * jnp.sum(q * n_st, axis=1, keepdims=True)
    h = num / jnp.maximum(jnp.abs(den), jnp.exp(-m_t))

    b_last = b_col[0:1, :] if d else b_col[L - 1:L, :]
    w_c = b_last - b_col + ig_col
    m_new = jnp.maximum(b_last + m_st, jnp.max(w_c, axis=0, keepdims=True))
    a_c = jnp.exp(w_c - m_new)
    g_prev = jnp.exp(b_last + m_st - m_new)
    ak = a_c * k
    c_ref[d] = g_prev * c_st + _d(ak.T.astype(BF16), vb, NN)
    n_ref[d] = g_prev * n_st + jnp.sum(ak, axis=0, keepdims=True)
    m_ref[d] = jnp.broadcast_to(m_new, m_ref.shape[1:])
    return h


def _mlstm_body(qf, kf, vf, gcf, grf, qb, kb, vb, gcb, grb, hf_ref, hb_ref, c_ref, n_ref, m_ref):
    @pl.when(pl.program_id(1) == 0)
    def _():
        c_ref[...] = jnp.zeros(c_ref.shape, F32)
        n_ref[...] = jnp.zeros(n_ref.shape, F32)
        m_ref[...] = jnp.full(m_ref.shape, M_INIT, F32)

    hf_ref[...] = _mlstm_dir(qf[...], kf[...], vf[...], gcf[...], grf[...], c_ref, n_ref, m_ref, 0)
    hb_ref[...] = _mlstm_dir(qb[...], kb[...], vb[...], gcb[...], grb[...], c_ref, n_ref, m_ref, 1)


def _bwd_chunk(c, nc0, nc):
    return jnp.where(c < nc0, nc0 - 1 - c, nc - 1 - (c - nc0))


def mlstm_scan(q, k, v, gcol, grow, n_ctx):
    t = q.shape[0]
    L, dh = MLSTM_CHUNK, MLSTM_DH
    nc, nc0 = t // L, n_ctx // L
    fwd = lambda h, c: (c, h)
    bwd = lambda h, c: (_bwd_chunk(c, nc0, nc), h)
    qkv = lambda im: pl.BlockSpec((L, dh), im)
    gc_spec = lambda f: pl.BlockSpec((None, L, LANES), lambda h, c: (h, f(c), 0))
    gr_spec = lambda f: pl.BlockSpec((None, 8, L), lambda h, c: (h, 0, f(c)))
    idf = lambda c: c
    idb = lambda c: _bwd_chunk(c, nc0, nc)
    return pl.pallas_call(
        _mlstm_body,
        grid=(MLSTM_HEADS, nc),
        in_specs=[qkv(fwd), qkv(fwd), qkv(fwd), gc_spec(idf), gr_spec(idf),
                  qkv(bwd), qkv(bwd), qkv(bwd), gc_spec(idb), gr_spec(idb)],
        out_specs=[qkv(fwd), qkv(bwd)],
        out_shape=[jax.ShapeDtypeStruct((t, D_MODEL), F32)] * 2,
        scratch_shapes=[pltpu.VMEM((2, dh, dh), F32), pltpu.VMEM((2, 1, dh), F32),
                        pltpu.VMEM((2, 8, LANES), F32)],
        compiler_params=pltpu.CompilerParams(
            dimension_semantics=("arbitrary", "arbitrary"), vmem_limit_bytes=VMEM_LIMIT),
        name="mlstm_scan",
    )(q, k, v, gcol, grow, q, k, v, gcol, grow)


RW_PASSES = 3


def _stack2(x, lane_head):
    return jnp.concatenate([jnp.where(lane_head == 0, x, 0.0), jnp.where(lane_head == 1, x, 0.0)], axis=0)


def _unstack2(x):
    L = x.shape[0] // 2
    return x[:L] + x[L:]


def _rwkv_dir(lw, r, k, v, kap, alp, s_ref, d):
    L = lw.shape[0]
    n2 = 2 * L
    p = RW_PASSES
    row = lax.broadcasted_iota(jnp.int32, (L, L), 0)
    col = lax.broadcasted_iota(jnp.int32, (L, L), 1)
    tri = jnp.where((col >= row) if d else (col <= row), 1.0, 0.0)
    lp = _dot_01_lhs(tri, lw)
    lpx = lp - lw
    lp_end = lp[0:1, :] if d else lp[L - 1:L, :]
    e_neg = jnp.exp(-lp)
    e_end = jnp.exp(lp_end - lp)
    kap_t = kap * jnp.exp(lpx)
    r_t = r * jnp.exp(lp)
    k_h = k * e_neg
    a_h = alp * e_neg
    k_e = k * e_end
    a_e = alp * e_end

    lane_head = lax.broadcasted_iota(jnp.int32, (L, LANES), 1) // RWKV_N
    st = lambda x: _stack2(x, lane_head)
    kap_s, r_s, v_s = st(kap_t), st(r_t), st(v)
    k_s, a_s = st(k_e), st(a_e)
    rhs_k = jnp.concatenate([k_h, k_h], axis=0)
    rhs_a = jnp.concatenate([a_h, a_h], axis=0)

    row2 = lax.broadcasted_iota(jnp.int32, (n2, n2), 0)
    col2 = lax.broadcasted_iota(jnp.int32, (n2, n2), 1)
    same_head = (row2 // L) == (col2 // L)
    strict = same_head & ((col2 > row2) if d else (col2 < row2))
    incl = same_head & ((col2 >= row2) if d else (col2 <= row2))
    zero = jnp.zeros((n2, n2), F32)
    n_ka = jnp.where(strict, _mdot(kap_s, rhs_a, NT, p), zero)
    m_kk = jnp.where(strict, _mdot(kap_s, rhs_k, NT, p), zero)
    a_rk = jnp.where(incl, _mdot(r_s, rhs_k, NT, p), zero)
    a_ra = jnp.where(incl, _mdot(r_s, rhs_a, NT, p), zero)

    b16 = (row2 // 16) == (col2 // 16)
    b32 = (row2 // 32) == (col2 // 32)
    eye = jnp.where(row2 == col2, 1.0, 0.0)
    n16 = jnp.where(b16, n_ka, zero)
    n_2 = _mdot(n16, n16, NN, p)
    n_4 = _mdot(n_2, n_2, NN, p)
    n_8 = _mdot(n_4, n_4, NN, p)
    inv = eye - n16
    inv = inv + _mdot(inv, n_2, NN, p)
    inv = inv + _mdot(inv, n_4, NN, p)
    inv = inv + _mdot(inv, n_8, NN, p)
    for blk_lo, blk_hi in ((b16, b32), (b32, None)):
        off = jnp.where(blk_lo, zero, n_ka) if blk_hi is None else jnp.where(blk_hi & ~blk_lo, n_ka, zero)
        inv = inv - _mdot(_mdot(inv, off, NN, p), inv, NN, p)

    mv = _mdot(m_kk, v_s, NN, p)
    w1u0 = _mdot(inv, jnp.concatenate([kap_s, mv], axis=1), NN, p)
    w1_s, u0_s = w1u0[:, :LANES], w1u0[:, LANES:]
    ar = _mdot(a_ra, w1u0, NN, p)
    r2 = _unstack2(r_s - ar[:, :LANES])
    y0 = _unstack2(_mdot(a_rk, v_s, NN, p) - ar[:, LANES:])

    s0 = s_ref[d]
    y = y0 + _mdot(r2, s0, NT, p)
    dec_end = jnp.where(row2 == col2, jnp.exp(lp_end), zero)
    g = dec_end - _mdot(w1_s.T, a_s, NN, p)
    b = _mdot(v_s.T, k_s, NN, p) - _mdot(u0_s.T, a_s, NN, p)
    s_ref[d] = _mdot(s0, g, NN, p) + b
    return y


def _rwkv_body(lwf, rf, kf, vf, kkf, kaf, lwb, rb, kb, vb, kkb, kab, yf_ref, yb_ref, s_ref):
    @pl.when(pl.program_id(1) == 0)
    def _():
        s_ref[...] = jnp.zeros(s_ref.shape, F32)

    yf_ref[...] = _rwkv_dir(lwf[...], rf[...], kf[...], vf[...], kkf[...], kaf[...], s_ref, 0)
    yb_ref[...] = _rwkv_dir(lwb[...], rb[...], kb[...], vb[...], kkb[...], kab[...], s_ref, 1)


def rwkv_scan(r, v, kk, lw_f, kt_f, ka_f, lw_b, kt_b, ka_b, n_ctx):
    t = r.shape[0]
    L = RWKV_CHUNK
    assert 2 * L == LANES
    nc, nc0 = t // L, n_ctx // L
    fwd = pl.BlockSpec((L, LANES), lambda h, c: (c, h))
    bwd = pl.BlockSpec((L, LANES), lambda h, c: (_bwd_chunk(c, nc0, nc), h))
    return pl.pallas_call(
        _rwkv_body,
        grid=(RWKV_HEADS // 2, nc),
        in_specs=[fwd] * 6 + [bwd] * 6,
        out_specs=[fwd, bwd],
        out_shape=[jax.ShapeDtypeStruct((t, D_MODEL), F32)] * 2,
        scratch_shapes=[pltpu.VMEM((2, LANES, LANES), F32)],
        compiler_params=pltpu.CompilerParams(
            dimension_semantics=("arbitrary", "arbitrary"), vmem_limit_bytes=VMEM_LIMIT),
        name="rwkv7_scan",
    )(lw_f, r, kt_f, v, kk, ka_f, lw_b, r, kt_b, v, kk, ka_b)


def _peer_stats(hx_ref, wq_ref, keys_ref, s_ref, e_ref, top_ref, tau_ref):
    hx = hx_ref[...]
    tm = hx.shape[0]
    q = _mdot(hx, wq_ref[...], NN, 3)
    neg = jnp.full((N_KEYS, tm), -jnp.inf, F32)
    for h in range(PEER_HEADS):
        for p in range(2):
            hp = 2 * h + p
            sc = _mdot(keys_ref[hp], q[:, hp * LANES:(hp + 1) * LANES], NT, 3)
            s_ref[p, h] = sc
            cur = sc
            for a in range(PEER_TOPK):
                m = jnp.max(cur, axis=0, keepdims=True)
                top_ref[p, a, h:h + 1, :] = m
                cur = jnp.where(cur >= m, neg, cur)
    cands = [top_ref[0, a] + top_ref[1, b]
             for a in range(PEER_TOPK) for b in range(PEER_TOPK) if (a + 1) * (b + 1) <= PEER_TOPK]
    c_max = cands[0]
    z = jnp.zeros_like(c_max)
    m = c_max
    for a in range(PEER_TOPK):
        m = functools.reduce(jnp.maximum, cands)
        z = z + jnp.exp(m - c_max)
        cands = [jnp.where(cd >= m, -jnp.inf, cd) for cd in cands]
    tau_ref[...] = m
    inv_z = 1.0 / z
    for h in range(PEER_HEADS):
        e_ref[0, h] = jnp.exp(s_ref[0, h] - top_ref[0, 0, h:h + 1, :]) * inv_z[h:h + 1, :]
        e_ref[1, h] = jnp.exp(s_ref[1, h] - top_ref[1, 0, h:h + 1, :])


def _peer_body(hx_ref, wq_ref, keys_ref, u_ref, vt_ref, o_ref,
               s_ref, e_ref, top_ref, tau_ref, act_ref, w_ref, acc_ref):
    e = pl.program_id(1)
    tm = hx_ref.shape[0]

    @pl.when(e == 0)
    def _():
        _peer_stats(hx_ref, wq_ref, keys_ref, s_ref, e_ref, top_ref, tau_ref)
        acc_ref[...] = jnp.zeros(acc_ref.shape, F32)

    act = _d(u_ref[...], hx_ref[...].astype(BF16), NT)
    act_ref[...] = 0.5 * act * (1.0 + lax.erf(act * (2.0 ** -0.5)))

    def tile(tb, carry):
        lanes = pl.ds(pl.multiple_of(tb * LANES, LANES), LANES)
        i_rows = pl.ds(pl.multiple_of(e * PEER_I_BLOCK, PEER_I_BLOCK), PEER_I_BLOCK)
        s0 = [s_ref[0, h, i_rows, lanes] for h in range(PEER_HEADS)]
        f0 = [e_ref[0, h, i_rows, lanes] for h in range(PEER_HEADS)]
        for ii in range(PEER_I_BLOCK):
            rows = pl.ds(ii * N_KEYS, N_KEYS)
            g = jnp.zeros((N_KEYS, LANES), F32)
            for h in range(PEER_HEADS):
                sel = (s0[h][ii:ii + 1] + s_ref[1, h, :, lanes]) >= tau_ref[h:h + 1, lanes]
                g = g + jnp.where(sel, e_ref[1, h, :, lanes] * f0[h][ii:ii + 1], 0.0)
            w_ref[rows, lanes] = (g * act_ref[rows, lanes]).astype(BF16)
        return carry

    lax.fori_loop(0, tm // LANES, tile, 0)
    acc_ref[...] += _d(vt_ref[...], w_ref[...], NN)

    @pl.when(e == pl.num_programs(1) - 1)
    def _():
        o_ref[...] = acc_ref[...].T


def peer(hx, wq, keys, u_bf, vt_bf):
    t, d = hx.shape
    tm = _pick(t, (640, 384, 256, 128))
    ne = u_bf.shape[0] // PEER_TE
    h = PEER_HEADS
    return pl.pallas_call(
        _peer_body,
        grid=(t // tm, ne),
        in_specs=[pl.BlockSpec((tm, d), lambda i, e: (i, 0), pipeline_mode=pl.Buffered(1)),
                  pl.BlockSpec(wq.shape, lambda i, e: (0, 0), pipeline_mode=pl.Buffered(1)),
                  pl.BlockSpec(keys.shape, lambda i, e: (0, 0, 0), pipeline_mode=pl.Buffered(1)),
                  pl.BlockSpec((PEER_TE, d), lambda i, e: (e, 0)),
                  pl.BlockSpec((d, PEER_TE), lambda i, e: (0, e))],
        out_specs=pl.BlockSpec((tm, d), lambda i, e: (i, 0)),
        out_shape=jax.ShapeDtypeStruct((t, d), F32),
        scratch_shapes=[pltpu.VMEM((2, h, N_KEYS, tm), F32), pltpu.VMEM((2, h, N_KEYS, tm), F32),
                        pltpu.VMEM((2, PEER_TOPK, h, tm), F32), pltpu.VMEM((h, tm), F32),
                        pltpu.VMEM((PEER_TE, tm), F32), pltpu.VMEM((PEER_TE, tm), BF16),
                        pltpu.VMEM((d, tm), F32)],
        compiler_params=pltpu.CompilerParams(
            dimension_semantics=("arbitrary", "arbitrary"), vmem_limit_bytes=VMEM_LIMIT),
        name="peer_dense",
    )(hx, wq, keys, u_bf, vt_bf)


def _ln(x, g, b):
    mu = jnp.mean(x, -1, keepdims=True)
    xc = x - mu
    var = jnp.mean(xc * xc, -1, keepdims=True)
    return xc * lax.rsqrt(var + LN_EPS) * g + b


def _head_norm(h, nheads, eps):
    t = h.shape[0]
    hh = h.reshape(t, nheads, -1)
    mu = jnp.mean(hh, -1, keepdims=True)
    xc = hh - mu
    var = jnp.mean(xc * xc, -1, keepdims=True)
    return (xc * lax.rsqrt(var + eps)).reshape(t, -1)


def _pad_cols(w, n):
    return jnp.pad(w, ((0, 0), (0, n - w.shape[1])))


def _shift_rows(u, k):
    if k > 0:
        return jnp.concatenate([u[k:], jnp.zeros_like(u[:k])], axis=0)
    return jnp.concatenate([jnp.zeros_like(u[:-k]), u[:k]], axis=0)


def _grid_conv(u, w, b):
    s = u.shape[0]
    colid = (jnp.arange(s) % GRID_W)[:, None]
    out = jnp.zeros_like(u) + b
    for di in range(3):
        for dj in range(3):
            off = (di - 1) * GRID_W + (dj - 1)
            sh = _shift_rows(u, off) if off else u
            if dj == 0:
                sh = jnp.where(colid == 0, 0.0, sh)
            elif dj == 2:
                sh = jnp.where(colid == GRID_W - 1, 0.0, sh)
            out = out + sh * w[di, dj]
    return out


def _seq_conv(u, w, b):
    wc = w[1]
    return _shift_rows(u, -1) * wc[0] + u * wc[1] + _shift_rows(u, 1) * wc[2] + b


def _qshift(u):
    q = u.shape[1] // 4
    colid = (jnp.arange(u.shape[0]) % GRID_W)[:, None]
    left = jnp.where(colid == 0, 0.0, _shift_rows(u[:, :q], -1))
    right = jnp.where(colid == GRID_W - 1, 0.0, _shift_rows(u[:, q:2 * q], 1))
    up = _shift_rows(u[:, 2 * q:3 * q], -GRID_W)
    down = _shift_rows(u[:, 3 * q:], GRID_W)
    return jnp.concatenate([left, right, up, down], axis=1)


def _shift_seq(u):
    h = u.shape[1] // 2
    return jnp.concatenate([_shift_rows(u[:, :h], -1), _shift_rows(u[:, h:], 1)], axis=1)


def _mlstm_layer(h, n_ctx, w_in, b_in, conv_w, conv_b, hn_g, w_out):
    d = D_MODEL
    t = h.shape[0]
    npad = 4 * d + LANES
    p = matmul(h, _pad_cols(w_in, npad), jnp.pad(b_in, (0, npad - b_in.shape[0])))
    qk_pre = p[:, :2 * d]
    qk = jax.nn.silu(jnp.concatenate([_seq_conv(qk_pre[:n_ctx], conv_w, conv_b),
                                      _grid_conv(qk_pre[n_ctx:], conv_w, conv_b)], axis=0))
    q = qk[:, :d] * (MLSTM_DH ** -0.5)
    k = qk[:, d:]
    v = p[:, 2 * d:3 * d]
    o = jax.nn.sigmoid(p[:, 3 * d:4 * d])
    g = p[:, 4 * d:4 * d + 4 * MLSTM_HEADS].reshape(t, 4, MLSTM_HEADS)
    g = jnp.concatenate([g[:, :2], jax.nn.log_sigmoid(g[:, 2:])], axis=1)
    gh = jnp.transpose(g, (2, 0, 1))
    gcol = jnp.pad(gh, ((0, 0), (0, 0), (0, LANES - 4)))
    grow = jnp.pad(jnp.transpose(gh, (0, 2, 1)), ((0, 0), (0, 4), (0, 0)))
    hf, hb = mlstm_scan(q, k, v, gcol, grow, n_ctx)
    hn = _head_norm(hf + hb, MLSTM_HEADS, LN_EPS)
    return matmul(o * hn * hn_g, w_out)


def _rwkv_layer(h, n_ctx, mu, w_rkv, w0, w1, w2, a0, a1, a2, g1, g2, k_k, k_a, r_k, lnx_g, lnx_b, w_out):
    d = D_MODEL
    t = h.shape[0]
    shifted = jnp.concatenate([_shift_seq(h[:n_ctx]), _qshift(h[n_ctx:])], axis=0)
    dx = shifted - h
    xm = [h + dx * mu[n] for n in range(6)]
    r = matmul(xm[0], w_rkv[0])
    k = matmul(xm[1], w_rkv[1])
    v = matmul(xm[2], w_rkv[2])
    lora = DECAY_LORA = w1.shape[-1]
    w1c = _pad_cols(jnp.concatenate([w1[0], w1[1]], axis=1), LANES)
    zpad = jnp.zeros((lora, d), F32)
    w2c = jnp.concatenate([jnp.concatenate([w2[0], zpad], axis=1),
                           jnp.concatenate([zpad, w2[1]], axis=1)], axis=0)
    wpre = matmul(jnp.tanh(matmul(xm[3], w1c)), w2c) + jnp.concatenate([w0[0], w0[1]])
    lw = -jnp.exp(-jax.nn.softplus(-wpre) - 0.5)
    a1c = _pad_cols(jnp.concatenate([a1[0], a1[1]], axis=1), LANES)
    a2c = jnp.concatenate([jnp.concatenate([a2[0], zpad], axis=1),
                           jnp.concatenate([zpad, a2[1]], axis=1)], axis=0)
    a = jax.nn.sigmoid(matmul(matmul(xm[4], a1c), a2c) + jnp.concatenate([a0[0], a0[1]]))
    glora = g1.shape[1]
    gpad = 2 * LANES
    gg = jax.nn.sigmoid(matmul(xm[5], _pad_cols(g1, gpad)))
    g = matmul(gg, jnp.pad(g2, ((0, gpad - glora), (0, 0))))
    kk = (k * k_k).reshape(t, RWKV_HEADS, RWKV_N)
    kk = (kk / jnp.maximum(jnp.sqrt(jnp.sum(kk * kk, -1, keepdims=True)), 1e-12)).reshape(t, d)
    a_f, a_b = a[:, :d], a[:, d:]
    kt_f = k * (1.0 + (a_f - 1.0) * k_a)
    kt_b = k * (1.0 + (a_b - 1.0) * k_a)
    yf, yb = rwkv_scan(r, v, kk, lw[:, :d], kt_f, kk * a_f, lw[:, d:], kt_b, kk * a_b, n_ctx)
    yn = _head_norm(yf + yb, RWKV_HEADS, RWKV_GN_EPS) * lnx_g + lnx_b
    kbar = 0.5 * (kt_f + kt_b)
    bonus = jnp.sum((r * kbar * r_k).reshape(t, RWKV_HEADS, RWKV_N), -1, keepdims=True) \
        * v.reshape(t, RWKV_HEADS, RWKV_N)
    return matmul((yn + bonus.reshape(t, d)) * g, w_out)


def _forward(x, c, ctx, c_ctx, ada_w, ada_b, ln_g, ln_b,
             ml_w_in, ml_b_in, ml_conv_w, ml_conv_b, ml_hn_g, ml_w_out,
             rw_mu, rw_w_rkv, rw_w0, rw_w1, rw_w2, rw_a0, rw_a1, rw_a2, rw_g1, rw_g2,
             rw_k_k, rw_k_a, rw_r_k, rw_lnx_g, rw_lnx_b, rw_w_out,
             pk_wq, pk_keys, pk_u, pk_v):
    d = D_MODEL
    n_ctx = ctx.shape[1]
    n_lat = x.shape[1]
    xs = jnp.concatenate([ctx[0], x[0]], axis=0)
    t = xs.shape[0]
    is_ctx = (jnp.arange(t) < n_ctx)[:, None]
    s_in = jnp.zeros((8, d), F32).at[0].set(jax.nn.silu(c[0])).at[1].set(jax.nn.silu(c_ctx))
    depth = ada_w.shape[0]
    for i in range(depth):
        j = i // 2
        mod = matmul(s_in, ada_w[i], ada_b[i], passes=3)
        m = [jnp.where(is_ctx, mod[1, n * d:(n + 1) * d], mod[0, n * d:(n + 1) * d]) for n in range(N_MOD)]
        h = xs * (1.0 + m[1]) + m[0]
        if i % 2 == 0:
            y = _mlstm_layer(h, n_ctx, ml_w_in[j], ml_b_in[j], ml_conv_w[j], ml_conv_b[j],
                             ml_hn_g[j], ml_w_out[j])
        else:
            y = _rwkv_layer(h, n_ctx, rw_mu[j], rw_w_rkv[j], rw_w0[j], rw_w1[j], rw_w2[j],
                            rw_a0[j], rw_a1[j], rw_a2[j], rw_g1[j], rw_g2[j], rw_k_k[j],
                            rw_k_a[j], rw_r_k[j], rw_lnx_g[j], rw_lnx_b[j], rw_w_out[j])
        xs = _ln(DN_ALPHA * xs + m[2] * y, ln_g[i, 0], ln_b[i, 0])
        h = xs * (1.0 + m[4]) + m[3]
        keys = pk_keys[i].reshape(2 * PEER_HEADS, N_KEYS, PEER_DQ // 2)
        y = peer(h, pk_wq[i], keys, pk_u[i].astype(BF16), pk_v[i].astype(BF16).T)
        xs = _ln(DN_ALPHA * xs + m[5] * y, ln_g[i, 1], ln_b[i, 1])
    return xs[n_ctx:][None]


def kernel(x, c, ctx, c_ctx, ada_w, ada_b, ln_g, ln_b, ml_w_in, ml_b_in, ml_conv_w, ml_conv_b, ml_hn_g, ml_w_out, rw_mu, rw_w_rkv, rw_w0, rw_w1, rw_w2, rw_a0, rw_a1, rw_a2, rw_g1, rw_g2, rw_k_k, rw_k_a, rw_r_k, rw_lnx_g, rw_lnx_b, rw_w_out, pk_wq, pk_keys, pk_u, pk_v):
    return _forward(x, c, ctx, c_ctx, ada_w, ada_b, ln_g, ln_b,
                    ml_w_in, ml_b_in, ml_conv_w, ml_conv_b, ml_hn_g, ml_w_out,
                    rw_mu, rw_w_rkv, rw_w0, rw_w1, rw_w2, rw_a0, rw_a1, rw_a2, rw_g1, rw_g2,
                    rw_k_k, rw_k_a, rw_r_k, rw_lnx_g, rw_lnx_b, rw_w_out,
                    pk_wq, pk_keys, pk_u, pk_v)
```

```python
import functools

import jax
import jax.numpy as jnp
from jax import lax
from jax.experimental import pallas as pl
from jax.experimental.pallas import tpu as pltpu

F32 = jnp.float32
BF16 = jnp.bfloat16

D_MODEL = 1024
DEPTH = 4
GRID_W = 64
N_MOD = 6
DN_ALPHA = (2.0 * DEPTH) ** 0.25
LN_EPS = 1e-5

MLSTM_HEADS = 4
MLSTM_DH = D_MODEL // MLSTM_HEADS
MLSTM_CHUNK = 128
M_INIT = -1e30

RWKV_N = 64
RWKV_HEADS = D_MODEL // RWKV_N
RWKV_CHUNK = 64
RWKV_GN_EPS = 64e-5

N_KEYS = 128
PEER_HEADS = 8
PEER_DQ = 256
PEER_TOPK = 16
PEER_I_BLOCK = 8
PEER_TE = PEER_I_BLOCK * N_KEYS

LANES = 128
VMEM_LIMIT = 56 * 1024 * 1024

NN = ((1,), (0,))
NT = ((1,), (1,))
TN = ((0,), (0,))


def _split(x, n):
    parts = []
    r = x.astype(F32)
    for i in range(n):
        p = r.astype(BF16)
        parts.append(p)
        if i + 1 < n:
            r = r - p.astype(F32)
    return parts


def _d(a, b, dims):
    return lax.dot_general(a, b, (dims, ((), ())), preferred_element_type=F32)


def _mdot(a, b, dims, passes):
    if passes == 1:
        return _d(a.astype(BF16), b.astype(BF16), dims)
    if passes == 3:
        a0, a1 = _split(a, 2)
        b0, b1 = _split(b, 2)
        return (_d(a0, b1, dims) + _d(a1, b0, dims)) + _d(a0, b0, dims)
    a0, a1, a2 = _split(a, 3)
    b0, b1, b2 = _split(b, 3)
    lo = (_d(a0, b2, dims) + _d(a2, b0, dims)) + _d(a1, b1, dims)
    mid = _d(a0, b1, dims) + _d(a1, b0, dims)
    return (lo + mid) + _d(a0, b0, dims)


def _dot_01_lhs(m01, x):
    mb = m01.astype(BF16)
    x0, x1, x2 = _split(x, 3)
    return (_d(mb, x2, NN) + _d(mb, x1, NN)) + _d(mb, x0, NN)


def _dot_01_rhs(x, m01):
    mb = m01.astype(BF16)
    x0, x1, x2 = _split(x, 3)
    return (_d(x2, mb, NN) + _d(x1, mb, NN)) + _d(x0, mb, NN)


def _pick(n, cands):
    for c in cands:
        if n % c == 0:
            return c
    raise ValueError(f"no tile for {n}")


def _mm_body(x_ref, w_ref, b_ref, o_ref, *, passes):
    o_ref[...] = _mdot(x_ref[...], w_ref[...], NN, passes) + b_ref[...]


def matmul(x, w, b=None, *, passes=1):
    m, k = x.shape
    n = w.shape[1]
    assert n % LANES == 0 and w.shape[0] == k
    tm = m if m <= 1024 else _pick(m, (640, 512, 384, 256, 128))
    tn = _pick(n, (1024, 768, 640, 512, 384, 256, 128))
    if b is None:
        b = jnp.zeros((n,), F32)
    if passes == 1:
        w = w.astype(BF16)
    return pl.pallas_call(
        functools.partial(_mm_body, passes=passes),
        grid=(m // tm, n // tn),
        in_specs=[pl.BlockSpec((tm, k), lambda i, j: (i, 0)),
                  pl.BlockSpec((k, tn), lambda i, j: (0, j)),
                  pl.BlockSpec((1, tn), lambda i, j: (0, j))],
        out_specs=pl.BlockSpec((tm, tn), lambda i, j: (i, j)),
        out_shape=jax.ShapeDtypeStruct((m, n), F32),
        compiler_params=pltpu.CompilerParams(
            dimension_semantics=("arbitrary", "arbitrary"), vmem_limit_bytes=VMEM_LIMIT),
        name="proj_matmul",
    )(x, w, b.reshape(1, n).astype(F32))


def _mlstm_dir(q, k, v, gc, gr, c_ref, n_ref, m_ref, d):
    L = q.shape[0]
    row = lax.broadcasted_iota(jnp.int32, (L, L), 0)
    col = lax.broadcasted_iota(jnp.int32, (L, L), 1)
    lower = col <= row
    upper = col >= row
    seen = upper if d else lower
    tri_c = jnp.where(seen, 1.0, 0.0)
    tri_r = jnp.where(lower if d else upper, 1.0, 0.0)
    b_col = _dot_01_lhs(tri_c, gc)[:, 2 + d:3 + d]
    b_row = _dot_01_rhs(gr, tri_r)[2 + d:3 + d, :]
    ig_col = gc[:, d:d + 1]
    ig_row = gr[d:d + 1, :]
    m_st = m_ref[d, 0:1, 0:1]
    c_st = c_ref[d]
    n_st = n_ref[d]

    dlog = jnp.where(seen, b_col - b_row + ig_row, -jnp.inf)
    m_inter = b_col + m_st
    m_t = jnp.maximum(m_inter, jnp.max(dlog, axis=1, keepdims=True))
    qb = q.astype(BF16)
    kb = k.astype(BF16)
    vb = v.astype(BF16)
    s = _d(qb, kb, NT) * jnp.exp(dlog - m_t)
    dec = jnp.exp(m_inter - m_t)
    num = _d(s.astype(BF16), vb, NN) + dec * _d(qb, c_st.astype(BF16), NN)
    den = jnp.sum(s, axis=1, keepdims=True) + dec * jnp.sum(q * n_st, axis=1, keepdims=True)
    h = num / jnp.maximum(jnp.abs(den), jnp.exp(-m_t))

    b_last = b_col[0:1, :] if d else b_col[L - 1:L, :]
    w_c = b_last - b_col + ig_col
    m_new = jnp.maximum(b_last + m_st, jnp.max(w_c, axis=0, keepdims=True))
    a_c = jnp.exp(w_c - m_new)
    g_prev = jnp.exp(b_last + m_st - m_new)
    ak = a_c * k
    c_ref[d] = g_prev * c_st + _d(ak.T.astype(BF16), vb, NN)
    n_ref[d] = g_prev * n_st + jnp.sum(ak, axis=0, keepdims=True)
    m_ref[d] = jnp.broadcast_to(m_new, m_ref.shape[1:])
    return h


def _mlstm_body(qf, kf, vf, gcf, grf, qb, kb, vb, gcb, grb, hf_ref, hb_ref, c_ref, n_ref, m_ref):
    @pl.when(pl.program_id(1) == 0)
    def _():
        c_ref[...] = jnp.zeros(c_ref.shape, F32)
        n_ref[...] = jnp.zeros(n_ref.shape, F32)
        m_ref[...] = jnp.full(m_ref.shape, M_INIT, F32)

    hf_ref[...] = _mlstm_dir(qf[...], kf[...], vf[...], gcf[...], grf[...], c_ref, n_ref, m_ref, 0)
    hb_ref[...] = _mlstm_dir(qb[...], kb[...], vb[...], gcb[...], grb[...], c_ref, n_ref, m_ref, 1)


def _bwd_chunk(c, nc0, nc):
    return jnp.where(c < nc0, nc0 - 1 - c, nc - 1 - (c - nc0))


def mlstm_scan(q, k, v, gcol, grow, n_ctx):
    t = q.shape[0]
    L, dh = MLSTM_CHUNK, MLSTM_DH
    nc, nc0 = t // L, n_ctx // L
    fwd = lambda h, c: (c, h)
    bwd = lambda h, c: (_bwd_chunk(c, nc0, nc), h)
    qkv = lambda im: pl.BlockSpec((L, dh), im)
    gc_spec = lambda f: pl.BlockSpec((None, L, LANES), lambda h, c: (h, f(c), 0))
    gr_spec = lambda f: pl.BlockSpec((None, 8, L), lambda h, c: (h, 0, f(c)))
    idf = lambda c: c
    idb = lambda c: _bwd_chunk(c, nc0, nc)
    return pl.pallas_call(
        _mlstm_body,
        grid=(MLSTM_HEADS, nc),
        in_specs=[qkv(fwd), qkv(fwd), qkv(fwd), gc_spec(idf), gr_spec(idf),
                  qkv(bwd), qkv(bwd), qkv(bwd), gc_spec(idb), gr_spec(idb)],
        out_specs=[qkv(fwd), qkv(bwd)],
        out_shape=[jax.ShapeDtypeStruct((t, D_MODEL), F32)] * 2,
        scratch_shapes=[pltpu.VMEM((2, dh, dh), F32), pltpu.VMEM((2, 1, dh), F32),
                        pltpu.VMEM((2, 8, LANES), F32)],
        compiler_params=pltpu.CompilerParams(
            dimension_semantics=("arbitrary", "arbitrary"), vmem_limit_bytes=VMEM_LIMIT),
        name="mlstm_scan",
    )(q, k, v, gcol, grow, q, k, v, gcol, grow)


RW_PASSES = 1
RW_CHUNKS_PER_STEP = 4


def _stack2(x, lane_head):
    return jnp.concatenate([jnp.where(lane_head == 0, x, 0.0), jnp.where(lane_head == 1, x, 0.0)], axis=0)


def _unstack2(x):
    L = x.shape[0] // 2
    return x[:L] + x[L:]


def _rwkv_chunks(chains):
    L = chains[0][0].shape[0]
    n2 = 2 * L
    p = RW_PASSES
    ds = [c[6] for c in chains]
    each = lambda f, *ls: [f(*xs) for xs in zip(*ls)]
    dot = lambda dims: (lambda a, b: _mdot(a, b, dims, p))
    lw, r, k, v, kap, alp = ([c[i] for c in chains] for i in range(6))

    row = lax.broadcasted_iota(jnp.int32, (L, L), 0)
    col = lax.broadcasted_iota(jnp.int32, (L, L), 1)
    tris = (jnp.where(col <= row, 1.0, 0.0), jnp.where(col >= row, 1.0, 0.0))
    lp = [_dot_01_lhs(tris[d], x) for d, x in zip(ds, lw)]
    lp_end = [x[0:1, :] if d else x[L - 1:L, :] for d, x in zip(ds, lp)]
    e_neg = each(lambda x: jnp.exp(-x), lp)
    e_end = each(lambda x, xe: jnp.exp(xe - x), lp, lp_end)
    kap_t = each(lambda x, y, z: x * jnp.exp(y - z), kap, lp, lw)
    r_t = each(lambda x, y: x * jnp.exp(y), r, lp)
    k_h = each(jnp.multiply, k, e_neg)
    a_h = each(jnp.multiply, alp, e_neg)
    k_e = each(jnp.multiply, k, e_end)
    a_e = each(jnp.multiply, alp, e_end)

    lane_head = lax.broadcasted_iota(jnp.int32, (L, LANES), 1) // RWKV_N
    st = lambda x: _stack2(x, lane_head)
    kap_s, r_s, v_s, k_s, a_s = (each(st, x) for x in (kap_t, r_t, v, k_e, a_e))
    rhs_k = each(lambda x: jnp.concatenate([x, x], axis=0), k_h)
    rhs_a = each(lambda x: jnp.concatenate([x, x], axis=0), a_h)

    row2 = lax.broadcasted_iota(jnp.int32, (n2, n2), 0)
    col2 = lax.broadcasted_iota(jnp.int32, (n2, n2), 1)
    same_head = (row2 // L) == (col2 // L)
    strict = (same_head & (col2 < row2), same_head & (col2 > row2))
    incl = (same_head & (col2 <= row2), same_head & (col2 >= row2))
    zero = jnp.zeros((n2, n2), F32)
    masked = lambda masks: (lambda d, x: jnp.where(masks[d], x, zero))
    n_ka = each(masked(strict), ds, each(dot(NT), kap_s, rhs_a))
    m_kk = each(masked(strict), ds, each(dot(NT), kap_s, rhs_k))
    a_rk = each(masked(incl), ds, each(dot(NT), r_s, rhs_k))
    a_ra = each(masked(incl), ds, each(dot(NT), r_s, rhs_a))

    b16 = (row2 // 16) == (col2 // 16)
    b32 = (row2 // 32) == (col2 // 32)
    eye = jnp.where(row2 == col2, 1.0, 0.0)
    n16 = each(lambda x: jnp.where(b16, x, zero), n_ka)
    n_2 = each(dot(NN), n16, n16)
    n_4 = each(dot(NN), n_2, n_2)
    n_8 = each(dot(NN), n_4, n_4)
    inv = each(lambda x: eye - x, n16)
    for pw in (n_2, n_4, n_8):
        inv = each(jnp.add, inv, each(dot(NN), inv, pw))
    for sel in (lambda x: jnp.where(b32 & ~b16, x, zero), lambda x: jnp.where(b32, zero, x)):
        t1 = each(dot(NN), inv, each(sel, n_ka))
        inv = each(jnp.subtract, inv, each(dot(NN), t1, inv))

    mv = each(dot(NN), m_kk, v_s)
    w1u0 = each(dot(NN), inv, each(lambda x, y: jnp.concatenate([x, y], axis=1), kap_s, mv))
    ar = each(dot(NN), a_ra, w1u0)
    av = each(dot(NN), a_rk, v_s)
    r2 = each(lambda x, y: _unstack2(x - y[:, :LANES]), r_s, ar)
    y0 = each(lambda x, y: _unstack2(x - y[:, LANES:]), av, ar)
    w1_t = each(lambda x: x[:, :LANES].T, w1u0)
    u0_t = each(lambda x: x[:, LANES:].T, w1u0)
    v_t = each(lambda x: x.T, v_s)
    wa = each(dot(NN), w1_t, a_s)
    vk = each(dot(NN), v_t, k_s)
    ua = each(dot(NN), u0_t, a_s)
    g = each(lambda xe, x: jnp.where(row2 == col2, jnp.exp(xe), zero) - x, lp_end, wa)
    b = each(jnp.subtract, vk, ua)
    return list(zip(y0, r2, g, b))


def _rwkv_body(lwf, rf, kf, vf, kkf, kaf, lwb, rb, kb, vb, kkb, kab, yf_ref, yb_ref, s_ref):
    @pl.when(pl.program_id(1) == 0)
    def _():
        s_ref[...] = jnp.zeros(s_ref.shape, F32)

    L = RWKV_CHUNK
    n = RW_CHUNKS_PER_STEP
    rows = lambda j: slice(j * L, (j + 1) * L)
    refs = ((lwf, rf, kf, vf, kkf, kaf), (lwb, rb, kb, vb, kkb, kab))
    visit = [(d, j if d == 0 else n - 1 - j) for j in range(n) for d in (0, 1)]
    pre = _rwkv_chunks([tuple(ref[rows(j), :] for ref in refs[d]) + (d,) for d, j in visit])
    y_refs = (yf_ref, yb_ref)
    s = [s_ref[0], s_ref[1]]
    for (d, j), (y0, r2, g, b) in zip(visit, pre):
        y_refs[d][rows(j), :] = y0 + _mdot(r2, s[d], NT, RW_PASSES)
        s[d] = _mdot(s[d], g, NN, RW_PASSES) + b
    s_ref[0] = s[0]
    s_ref[1] = s[1]


def rwkv_scan(r, v, kk, lw_f, kt_f, ka_f, lw_b, kt_b, ka_b, n_ctx):
    t = r.shape[0]
    L = RWKV_CHUNK
    blk = RW_CHUNKS_PER_STEP * L
    assert 2 * L == LANES and t % blk == 0 and n_ctx % blk == 0
    nc, nc0 = t // blk, n_ctx // blk
    fwd = pl.BlockSpec((blk, LANES), lambda h, c: (c, h))
    bwd = pl.BlockSpec((blk, LANES), lambda h, c: (_bwd_chunk(c, nc0, nc), h))
    return pl.pallas_call(
        _rwkv_body,
        grid=(RWKV_HEADS // 2, nc),
        in_specs=[fwd] * 6 + [bwd] * 6,
        out_specs=[fwd, bwd],
        out_shape=[jax.ShapeDtypeStruct((t, D_MODEL), F32)] * 2,
        scratch_shapes=[pltpu.VMEM((2, LANES, LANES), F32)],
        compiler_params=pltpu.CompilerParams(
            dimension_semantics=("arbitrary", "arbitrary"), vmem_limit_bytes=VMEM_LIMIT),
        name="rwkv7_scan",
    )(lw_f, r, kt_f, v, kk, ka_f, lw_b, r, kt_b, v, kk, ka_b)


def _peer_stats(hx_ref, wq_ref, keys_ref, s_ref, e_ref, top_ref, tau_ref):
    hx = hx_ref[...]
    tm = hx.shape[0]
    q = _mdot(hx, wq_ref[...], NN, 3)
    neg = jnp.full((N_KEYS, tm), -jnp.inf, F32)
    for h in range(PEER_HEADS):
        for p in range(2):
            hp = 2 * h + p
            sc = _mdot(keys_ref[hp], q[:, hp * LANES:(hp + 1) * LANES], NT, 3)
            s_ref[p, h] = sc
            cur = sc
            for a in range(PEER_TOPK):
                m = jnp.max(cur, axis=0, keepdims=True)
                top_ref[p, a, h:h + 1, :] = m
                cur = jnp.where(cur >= m, neg, cur)
    cands = [top_ref[0, a] + top_ref[1, b]
             for a in range(PEER_TOPK) for b in range(PEER_TOPK) if (a + 1) * (b + 1) <= PEER_TOPK]
    c_max = cands[0]
    z = jnp.zeros_like(c_max)
    m = c_max
    for a in range(PEER_TOPK):
        m = functools.reduce(jnp.maximum, cands)
        z = z + jnp.exp(m - c_max)
        cands = [jnp.where(cd >= m, -jnp.inf, cd) for cd in cands]
    tau_ref[...] = m
    inv_z = 1.0 / z
    for h in range(PEER_HEADS):
        e_ref[0, h] = jnp.exp(s_ref[0, h] - top_ref[0, 0, h:h + 1, :]) * inv_z[h:h + 1, :]
        e_ref[1, h] = jnp.exp(s_ref[1, h] - top_ref[1, 0, h:h + 1, :])


def _peer_body(hx_ref, wq_ref, keys_ref, u_ref, vt_ref, o_ref,
               s_ref, e_ref, top_ref, tau_ref, act_ref, w_ref, acc_ref):
    e = pl.program_id(1)
    tm = hx_ref.shape[0]

    @pl.when(e == 0)
    def _():
        _peer_stats(hx_ref, wq_ref, keys_ref, s_ref, e_ref, top_ref, tau_ref)
        acc_ref[...] = jnp.zeros(acc_ref.shape, F32)

    act = _d(u_ref[...], hx_ref[...].astype(BF16), NT)
    act_ref[...] = 0.5 * act * (1.0 + lax.erf(act * (2.0 ** -0.5)))

    def tile(tb, carry):
        lanes = pl.ds(pl.multiple_of(tb * LANES, LANES), LANES)
        i_rows = pl.ds(pl.multiple_of(e * PEER_I_BLOCK, PEER_I_BLOCK), PEER_I_BLOCK)
        s0 = [s_ref[0, h, i_rows, lanes] for h in range(PEER_HEADS)]
        f0 = [e_ref[0, h, i_rows, lanes] for h in range(PEER_HEADS)]
        for ii in range(PEER_I_BLOCK):
            rows = pl.ds(ii * N_KEYS, N_KEYS)
            g = jnp.zeros((N_KEYS, LANES), F32)
            for h in range(PEER_HEADS):
                sel = (s0[h][ii:ii + 1] + s_ref[1, h, :, lanes]) >= tau_ref[h:h + 1, lanes]
                g = g + jnp.where(sel, e_ref[1, h, :, lanes] * f0[h][ii:ii + 1], 0.0)
            w_ref[rows, lanes] = (g * act_ref[rows, lanes]).astype(BF16)
        return carry

    lax.fori_loop(0, tm // LANES, tile, 0)
    acc_ref[...] += _d(vt_ref[...], w_ref[...], NN)

    @pl.when(e == pl.num_programs(1) - 1)
    def _():
        o_ref[...] = acc_ref[...].T


def peer(hx, wq, keys, u_bf, vt_bf):
    t, d = hx.shape
    tm = _pick(t, (640, 384, 256, 128))
    ne = u_bf.shape[0] // PEER_TE
    h = PEER_HEADS
    return pl.pallas_call(
        _peer_body,
        grid=(t // tm, ne),
        in_specs=[pl.BlockSpec((tm, d), lambda i, e: (i, 0), pipeline_mode=pl.Buffered(1)),
                  pl.BlockSpec(wq.shape, lambda i, e: (0, 0), pipeline_mode=pl.Buffered(1)),
                  pl.BlockSpec(keys.shape, lambda i, e: (0, 0, 0), pipeline_mode=pl.Buffered(1)),
                  pl.BlockSpec((PEER_TE, d), lambda i, e: (e, 0)),
                  pl.BlockSpec((d, PEER_TE), lambda i, e: (0, e))],
        out_specs=pl.BlockSpec((tm, d), lambda i, e: (i, 0)),
        out_shape=jax.ShapeDtypeStruct((t, d), F32),
        scratch_shapes=[pltpu.VMEM((2, h, N_KEYS, tm), F32), pltpu.VMEM((2, h, N_KEYS, tm), F32),
                        pltpu.VMEM((2, PEER_TOPK, h, tm), F32), pltpu.VMEM((h, tm), F32),
                        pltpu.VMEM((PEER_TE, tm), F32), pltpu.VMEM((PEER_TE, tm), BF16),
                        pltpu.VMEM((d, tm), F32)],
        compiler_params=pltpu.CompilerParams(
            dimension_semantics=("arbitrary", "arbitrary"), vmem_limit_bytes=VMEM_LIMIT),
        name="peer_dense",
    )(hx, wq, keys, u_bf, vt_bf)


def _ln(x, g, b):
    mu = jnp.mean(x, -1, keepdims=True)
    xc = x - mu
    var = jnp.mean(xc * xc, -1, keepdims=True)
    return xc * lax.rsqrt(var + LN_EPS) * g + b


def _head_norm(h, nheads, eps):
    t = h.shape[0]
    hh = h.reshape(t, nheads, -1)
    mu = jnp.mean(hh, -1, keepdims=True)
    xc = hh - mu
    var = jnp.mean(xc * xc, -1, keepdims=True)
    return (xc * lax.rsqrt(var + eps)).reshape(t, -1)


def _pad_cols(w, n):
    return jnp.pad(w, ((0, 0), (0, n - w.shape[1])))


def _shift_rows(u, k):
    if k > 0:
        return jnp.concatenate([u[k:], jnp.zeros_like(u[:k])], axis=0)
    return jnp.concatenate([jnp.zeros_like(u[:-k]), u[:k]], axis=0)


def _grid_conv(u, w, b):
    s = u.shape[0]
    colid = (jnp.arange(s) % GRID_W)[:, None]
    out = jnp.zeros_like(u) + b
    for di in range(3):
        for dj in range(3):
            off = (di - 1) * GRID_W + (dj - 1)
            sh = _shift_rows(u, off) if off else u
            if dj == 0:
                sh = jnp.where(colid == 0, 0.0, sh)
            elif dj == 2:
                sh = jnp.where(colid == GRID_W - 1, 0.0, sh)
            out = out + sh * w[di, dj]
    return out


def _seq_conv(u, w, b):
    wc = w[1]
    return _shift_rows(u, -1) * wc[0] + u * wc[1] + _shift_rows(u, 1) * wc[2] + b


def _qshift(u):
    q = u.shape[1] // 4
    colid = (jnp.arange(u.shape[0]) % GRID_W)[:, None]
    left = jnp.where(colid == 0, 0.0, _shift_rows(u[:, :q], -1))
    right = jnp.where(colid == GRID_W - 1, 0.0, _shift_rows(u[:, q:2 * q], 1))
    up = _shift_rows(u[:, 2 * q:3 * q], -GRID_W)
    down = _shift_rows(u[:, 3 * q:], GRID_W)
    return jnp.concatenate([left, right, up, down], axis=1)


def _shift_seq(u):
    h = u.shape[1] // 2
    return jnp.concatenate([_shift_rows(u[:, :h], -1), _shift_rows(u[:, h:], 1)], axis=1)


def _mlstm_layer(h, n_ctx, w_in, b_in, conv_w, conv_b, hn_g, w_out):
    d = D_MODEL
    t = h.shape[0]
    npad = 4 * d + LANES
    p = matmul(h, _pad_cols(w_in, npad), jnp.pad(b_in, (0, npad - b_in.shape[0])))
    qk_pre = p[:, :2 * d]
    qk = jax.nn.silu(jnp.concatenate([_seq_conv(qk_pre[:n_ctx], conv_w, conv_b),
                                      _grid_conv(qk_pre[n_ctx:], conv_w, conv_b)], axis=0))
    q = qk[:, :d] * (MLSTM_DH ** -0.5)
    k = qk[:, d:]
    v = p[:, 2 * d:3 * d]
    o = jax.nn.sigmoid(p[:, 3 * d:4 * d])
    g = p[:, 4 * d:4 * d + 4 * MLSTM_HEADS].reshape(t, 4, MLSTM_HEADS)
    g = jnp.concatenate([g[:, :2], jax.nn.log_sigmoid(g[:, 2:])], axis=1)
    gh = jnp.transpose(g, (2, 0, 1))
    gcol = jnp.pad(gh, ((0, 0), (0, 0), (0, LANES - 4)))
    grow = jnp.pad(jnp.transpose(gh, (0, 2, 1)), ((0, 0), (0, 4), (0, 0)))
    hf, hb = mlstm_scan(q, k, v, gcol, grow, n_ctx)
    hn = _head_norm(hf + hb, MLSTM_HEADS, LN_EPS)
    return matmul(o * hn * hn_g, w_out)


def _rwkv_layer(h, n_ctx, mu, w_rkv, w0, w1, w2, a0, a1, a2, g1, g2, k_k, k_a, r_k, lnx_g, lnx_b, w_out):
    d = D_MODEL
    t = h.shape[0]
    shifted = jnp.concatenate([_shift_seq(h[:n_ctx]), _qshift(h[n_ctx:])], axis=0)
    dx = shifted - h
    xm = [h + dx * mu[n] for n in range(6)]
    r = matmul(xm[0], w_rkv[0])
    k = matmul(xm[1], w_rkv[1])
    v = matmul(xm[2], w_rkv[2])
    lora = DECAY_LORA = w1.shape[-1]
    w1c = _pad_cols(jnp.concatenate([w1[0], w1[1]], axis=1), LANES)
    zpad = jnp.zeros((lora, d), F32)
    w2c = jnp.concatenate([jnp.concatenate([w2[0], zpad], axis=1),
                           jnp.concatenate([zpad, w2[1]], axis=1)], axis=0)
    wpre = matmul(jnp.tanh(matmul(xm[3], w1c)), w2c) + jnp.concatenate([w0[0], w0[1]])
    lw = -jnp.exp(-jax.nn.softplus(-wpre) - 0.5)
    a1c = _pad_cols(jnp.concatenate([a1[0], a1[1]], axis=1), LANES)
    a2c = jnp.concatenate([jnp.concatenate([a2[0], zpad], axis=1),
                           jnp.concatenate([zpad, a2[1]], axis=1)], axis=0)
    a = jax.nn.sigmoid(matmul(matmul(xm[4], a1c), a2c) + jnp.concatenate([a0[0], a0[1]]))
    glora = g1.shape[1]
    gpad = 2 * LANES
    gg = jax.nn.sigmoid(matmul(xm[5], _pad_cols(g1, gpad)))
    g = matmul(gg, jnp.pad(g2, ((0, gpad - glora), (0, 0))))
    kk = (k * k_k).reshape(t, RWKV_HEADS, RWKV_N)
    kk = (kk / jnp.maximum(jnp.sqrt(jnp.sum(kk * kk, -1, keepdims=True)), 1e-12)).reshape(t, d)
    a_f, a_b = a[:, :d], a[:, d:]
    kt_f = k * (1.0 + (a_f - 1.0) * k_a)
    kt_b = k * (1.0 + (a_b - 1.0) * k_a)
    yf, yb = rwkv_scan(r, v, kk, lw[:, :d], kt_f, kk * a_f, lw[:, d:], kt_b, kk * a_b, n_ctx)
    yn = _head_norm(yf + yb, RWKV_HEADS, RWKV_GN_EPS) * lnx_g + lnx_b
    kbar = 0.5 * (kt_f + kt_b)
    bonus = jnp.sum((r * kbar * r_k).reshape(t, RWKV_HEADS, RWKV_N), -1, keepdims=True) \
        * v.reshape(t, RWKV_HEADS, RWKV_N)
    return matmul((yn + bonus.reshape(t, d)) * g, w_out)


def _forward(x, c, ctx, c_ctx, ada_w, ada_b, ln_g, ln_b,
             ml_w_in, ml_b_in, ml_conv_w, ml_conv_b, ml_hn_g, ml_w_out,
             rw_mu, rw_w_rkv, rw_w0, rw_w1, rw_w2, rw_a0, rw_a1, rw_a2, rw_g1, rw_g2,
             rw_k_k, rw_k_a, rw_r_k, rw_lnx_g, rw_lnx_b, rw_w_out,
             pk_wq, pk_keys, pk_u, pk_v):
    d = D_MODEL
    n_ctx = ctx.shape[1]
    n_lat = x.shape[1]
    xs = jnp.concatenate([ctx[0], x[0]], axis=0)
    t = xs.shape[0]
    is_ctx = (jnp.arange(t) < n_ctx)[:, None]
    s_in = jnp.zeros((8, d), F32).at[0].set(jax.nn.silu(c[0])).at[1].set(jax.nn.silu(c_ctx))
    depth = ada_w.shape[0]
    for i in range(depth):
        j = i // 2
        mod = matmul(s_in, ada_w[i], ada_b[i], passes=3)
        m = [jnp.where(is_ctx, mod[1, n * d:(n + 1) * d], mod[0, n * d:(n + 1) * d]) for n in range(N_MOD)]
        h = xs * (1.0 + m[1]) + m[0]
        if i % 2 == 0:
            y = _mlstm_layer(h, n_ctx, ml_w_in[j], ml_b_in[j], ml_conv_w[j], ml_conv_b[j],
                             ml_hn_g[j], ml_w_out[j])
        else:
            y = _rwkv_layer(h, n_ctx, rw_mu[j], rw_w_rkv[j], rw_w0[j], rw_w1[j], rw_w2[j],
                            rw_a0[j], rw_a1[j], rw_a2[j], rw_g1[j], rw_g2[j], rw_k_k[j],
                            rw_k_a[j], rw_r_k[j], rw_lnx_g[j], rw_lnx_b[j], rw_w_out[j])
        xs = _ln(DN_ALPHA * xs + m[2] * y, ln_g[i, 0], ln_b[i, 0])
        h = xs * (1.0 + m[4]) + m[3]
        keys = pk_keys[i].reshape(2 * PEER_HEADS, N_KEYS, PEER_DQ // 2)
        y = peer(h, pk_wq[i], keys, pk_u[i].astype(BF16), pk_v[i].astype(BF16).T)
        xs = _ln(DN_ALPHA * xs + m[5] * y, ln_g[i, 1], ln_b[i, 1])
    return xs[n_ctx:][None]


def kernel(x, c, ctx, c_ctx, ada_w, ada_b, ln_g, ln_b, ml_w_in, ml_b_in, ml_conv_w, ml_conv_b, ml_hn_g, ml_w_out, rw_mu, rw_w_rkv, rw_w0, rw_w1, rw_w2, rw_a0, rw_a1, rw_a2, rw_g1, rw_g2, rw_k_k, rw_k_a, rw_r_k, rw_lnx_g, rw_lnx_b, rw_w_out, pk_wq, pk_keys, pk_u, pk_v):
    return _forward(x, c, ctx, c_ctx, ada_w, ada_b, ln_g, ln_b,
                    ml_w_in, ml_b_in, ml_conv_w, ml_conv_b, ml_hn_g, ml_w_out,
                    rw_mu, rw_w_rkv, rw_w0, rw_w1, rw_w2, rw_a0, rw_a1, rw_a2, rw_g1, rw_g2,
                    rw_k_k, rw_k_a, rw_r_k, rw_lnx_g, rw_lnx_b, rw_w_out,
                    pk_wq, pk_keys, pk_u, pk_v)
```

```python
import functools

import jax
import jax.numpy as jnp
from jax import lax
from jax.experimental import pallas as pl
from jax.experimental.pallas import tpu as pltpu

F32 = jnp.float32
BF16 = jnp.bfloat16

D_MODEL = 1024
DEPTH = 4
GRID_W = 64
N_MOD = 6
DN_ALPHA = (2.0 * DEPTH) ** 0.25
LN_EPS = 1e-5

MLSTM_HEADS = 4
MLSTM_DH = D_MODEL // MLSTM_HEADS
MLSTM_CHUNK = 128
M_INIT = -1e30

RWKV_N = 64
RWKV_HEADS = D_MODEL // RWKV_N
RWKV_CHUNK = 64
RWKV_GN_EPS = 64e-5

N_KEYS = 128
PEER_HEADS = 8
PEER_DQ = 256
PEER_TOPK = 16
PEER_I_BLOCK = 8
PEER_TE = PEER_I_BLOCK * N_KEYS
PEER_TM = 512
PEER_SB = 256

LANES = 128
VMEM_LIMIT = 62 * 1024 * 1024

NN = ((1,), (0,))
NT = ((1,), (1,))
TN = ((0,), (0,))


def _split(x, n):
    parts = []
    r = x.astype(F32)
    for i in range(n):
        p = r.astype(BF16)
        parts.append(p)
        if i + 1 < n:
            r = r - p.astype(F32)
    return parts


def _d(a, b, dims):
    return lax.dot_general(a, b, (dims, ((), ())), preferred_element_type=F32)


def _mdot(a, b, dims, passes):
    if passes == 1:
        return _d(a.astype(BF16), b.astype(BF16), dims)
    if passes == 3:
        a0, a1 = _split(a, 2)
        b0, b1 = _split(b, 2)
        return (_d(a0, b1, dims) + _d(a1, b0, dims)) + _d(a0, b0, dims)
    a0, a1, a2 = _split(a, 3)
    b0, b1, b2 = _split(b, 3)
    lo = (_d(a0, b2, dims) + _d(a2, b0, dims)) + _d(a1, b1, dims)
    mid = _d(a0, b1, dims) + _d(a1, b0, dims)
    return (lo + mid) + _d(a0, b0, dims)


def _dot_01_lhs(m01, x):
    mb = m01.astype(BF16)
    x0, x1, x2 = _split(x, 3)
    return (_d(mb, x2, NN) + _d(mb, x1, NN)) + _d(mb, x0, NN)


def _dot_01_rhs(x, m01):
    mb = m01.astype(BF16)
    x0, x1, x2 = _split(x, 3)
    return (_d(x2, mb, NN) + _d(x1, mb, NN)) + _d(x0, mb, NN)


def _pick(n, cands):
    for c in cands:
        if n % c == 0:
            return c
    raise ValueError(f"no tile for {n}")


def _mm_body(x_ref, w_ref, b_ref, o_ref, *, passes):
    o_ref[...] = _mdot(x_ref[...], w_ref[...], NN, passes) + b_ref[...]


def matmul(x, w, b=None, *, passes=1):
    m, k = x.shape
    n = w.shape[1]
    assert n % LANES == 0 and w.shape[0] == k
    tm = m if m <= 1024 else _pick(m, (640, 512, 384, 256, 128))
    tn = _pick(n, (1024, 768, 640, 512, 384, 256, 128))
    if b is None:
        b = jnp.zeros((n,), F32)
    if passes == 1:
        w = w.astype(BF16)
    return pl.pallas_call(
        functools.partial(_mm_body, passes=passes),
        grid=(m // tm, n // tn),
        in_specs=[pl.BlockSpec((tm, k), lambda i, j: (i, 0)),
                  pl.BlockSpec((k, tn), lambda i, j: (0, j)),
                  pl.BlockSpec((1, tn), lambda i, j: (0, j))],
        out_specs=pl.BlockSpec((tm, tn), lambda i, j: (i, j)),
        out_shape=jax.ShapeDtypeStruct((m, n), F32),
        compiler_params=pltpu.CompilerParams(
            dimension_semantics=("arbitrary", "arbitrary"), vmem_limit_bytes=VMEM_LIMIT),
        name="proj_matmul",
    )(x, w, b.reshape(1, n).astype(F32))


def _mlstm_dir(q, k, v, gc, gr, c_ref, n_ref, m_ref, d):
    L = q.shape[0]
    row = lax.broadcasted_iota(jnp.int32, (L, L), 0)
    col = lax.broadcasted_iota(jnp.int32, (L, L), 1)
    lower = col <= row
    upper = col >= row
    seen = upper if d else lower
    tri_c = jnp.where(seen, 1.0, 0.0)
    tri_r = jnp.where(lower if d else upper, 1.0, 0.0)
    b_col = _dot_01_lhs(tri_c, gc)[:, 2 + d:3 + d]
    b_row = _dot_01_rhs(gr, tri_r)[2 + d:3 + d, :]
    ig_col = gc[:, d:d + 1]
    ig_row = gr[d:d + 1, :]
    m_st = m_ref[d, 0:1, 0:1]
    c_st = c_ref[d]
    n_st = n_ref[d]

    dlog = jnp.where(seen, b_col - b_row + ig_row, -jnp.inf)
    m_inter = b_col + m_st
    m_t = jnp.maximum(m_inter, jnp.max(dlog, axis=1, keepdims=True))
    qb = q.astype(BF16)
    kb = k.astype(BF16)
    vb = v.astype(BF16)
    s = _d(qb, kb, NT) * jnp.exp(dlog - m_t)
    dec = jnp.exp(m_inter - m_t)
    num = _d(s.astype(BF16), vb, NN) + dec * _d(qb, c_st.astype(BF16), NN)
    den = jnp.sum(s, axis=1, keepdims=True) + dec * jnp.sum(q * n_st, axis=1, keepdims=True)
    h = num / jnp.maximum(jnp.abs(den), jnp.exp(-m_t))

    b_last = b_col[0:1, :] if d else b_col[L - 1:L, :]
    w_c = b_last - b_col + ig_col
    m_new = jnp.maximum(b_last + m_st, jnp.max(w_c, axis=0, keepdims=True))
    a_c = jnp.exp(w_c - m_new)
    g_prev = jnp.exp(b_last + m_st - m_new)
    ak = a_c * k
    c_ref[d] = g_prev * c_st + _d(ak.T.astype(BF16), vb, NN)
    n_ref[d] = g_prev * n_st + jnp.sum(ak, axis=0, keepdims=True)
    m_ref[d] = jnp.broadcast_to(m_new, m_ref.shape[1:])
    return h


def _mlstm_body(qf, kf, vf, gcf, grf, qb, kb, vb, gcb, grb, hf_ref, hb_ref, c_ref, n_ref, m_ref):
    @pl.when(pl.program_id(1) == 0)
    def _():
        c_ref[...] = jnp.zeros(c_ref.shape, F32)
        n_ref[...] = jnp.zeros(n_ref.shape, F32)
        m_ref[...] = jnp.full(m_ref.shape, M_INIT, F32)

    hf_ref[...] = _mlstm_dir(qf[...], kf[...], vf[...], gcf[...], grf[...], c_ref, n_ref, m_ref, 0)
    hb_ref[...] = _mlstm_dir(qb[...], kb[...], vb[...], gcb[...], grb[...], c_ref, n_ref, m_ref, 1)


def _bwd_chunk(c, nc0, nc):
    return jnp.where(c < nc0, nc0 - 1 - c, nc - 1 - (c - nc0))


def mlstm_scan(q, k, v, gcol, grow, n_ctx):
    t = q.shape[0]
    L, dh = MLSTM_CHUNK, MLSTM_DH
    nc, nc0 = t // L, n_ctx // L
    fwd = lambda h, c: (c, h)
    bwd = lambda h, c: (_bwd_chunk(c, nc0, nc), h)
    qkv = lambda im: pl.BlockSpec((L, dh), im)
    gc_spec = lambda f: pl.BlockSpec((None, L, LANES), lambda h, c: (h, f(c), 0))
    gr_spec = lambda f: pl.BlockSpec((None, 8, L), lambda h, c: (h, 0, f(c)))
    idf = lambda c: c
    idb = lambda c: _bwd_chunk(c, nc0, nc)
    return pl.pallas_call(
        _mlstm_body,
        grid=(MLSTM_HEADS, nc),
        in_specs=[qkv(fwd), qkv(fwd), qkv(fwd), gc_spec(idf), gr_spec(idf),
                  qkv(bwd), qkv(bwd), qkv(bwd), gc_spec(idb), gr_spec(idb)],
        out_specs=[qkv(fwd), qkv(bwd)],
        out_shape=[jax.ShapeDtypeStruct((t, D_MODEL), F32)] * 2,
        scratch_shapes=[pltpu.VMEM((2, dh, dh), F32), pltpu.VMEM((2, 1, dh), F32),
                        pltpu.VMEM((2, 8, LANES), F32)],
        compiler_params=pltpu.CompilerParams(
            dimension_semantics=("arbitrary", "arbitrary"), vmem_limit_bytes=VMEM_LIMIT),
        name="mlstm_scan",
    )(q, k, v, gcol, grow, q, k, v, gcol, grow)


RW_PASSES = 1
RW_CHUNKS_PER_STEP = 4


def _stack2(x, lane_head):
    return jnp.concatenate([jnp.where(lane_head == 0, x, 0.0), jnp.where(lane_head == 1, x, 0.0)], axis=0)


def _unstack2(x):
    L = x.shape[0] // 2
    return x[:L] + x[L:]


def _rwkv_chunks(chains):
    L = chains[0][0].shape[0]
    n2 = 2 * L
    p = RW_PASSES
    ds = [c[6] for c in chains]
    each = lambda f, *ls: [f(*xs) for xs in zip(*ls)]
    dot = lambda dims: (lambda a, b: _mdot(a, b, dims, p))
    lw, r, k, v, kap, alp = ([c[i] for c in chains] for i in range(6))

    row = lax.broadcasted_iota(jnp.int32, (L, L), 0)
    col = lax.broadcasted_iota(jnp.int32, (L, L), 1)
    tris = (jnp.where(col <= row, 1.0, 0.0), jnp.where(col >= row, 1.0, 0.0))
    lp = [_dot_01_lhs(tris[d], x) for d, x in zip(ds, lw)]
    lp_end = [x[0:1, :] if d else x[L - 1:L, :] for d, x in zip(ds, lp)]
    e_neg = each(lambda x: jnp.exp(-x), lp)
    e_end = each(lambda x, xe: jnp.exp(xe - x), lp, lp_end)
    kap_t = each(lambda x, y, z: x * jnp.exp(y - z), kap, lp, lw)
    r_t = each(lambda x, y: x * jnp.exp(y), r, lp)
    k_h = each(jnp.multiply, k, e_neg)
    a_h = each(jnp.multiply, alp, e_neg)
    k_e = each(jnp.multiply, k, e_end)
    a_e = each(jnp.multiply, alp, e_end)

    lane_head = lax.broadcasted_iota(jnp.int32, (L, LANES), 1) // RWKV_N
    st = lambda x: _stack2(x, lane_head)
    kap_s, r_s, v_s, k_s, a_s = (each(st, x) for x in (kap_t, r_t, v, k_e, a_e))
    rhs_k = each(lambda x: jnp.concatenate([x, x], axis=0), k_h)
    rhs_a = each(lambda x: jnp.concatenate([x, x], axis=0), a_h)

    row2 = lax.broadcasted_iota(jnp.int32, (n2, n2), 0)
    col2 = lax.broadcasted_iota(jnp.int32, (n2, n2), 1)
    same_head = (row2 // L) == (col2 // L)
    strict = (same_head & (col2 < row2), same_head & (col2 > row2))
    incl = (same_head & (col2 <= row2), same_head & (col2 >= row2))
    zero = jnp.zeros((n2, n2), F32)
    masked = lambda masks: (lambda d, x: jnp.where(masks[d], x, zero))
    n_ka = each(masked(strict), ds, each(dot(NT), kap_s, rhs_a))
    m_kk = each(masked(strict), ds, each(dot(NT), kap_s, rhs_k))
    a_rk = each(masked(incl), ds, each(dot(NT), r_s, rhs_k))
    a_ra = each(masked(incl), ds, each(dot(NT), r_s, rhs_a))

    b16 = (row2 // 16) == (col2 // 16)
    b32 = (row2 // 32) == (col2 // 32)
    eye = jnp.where(row2 == col2, 1.0, 0.0)
    n16 = each(lambda x: jnp.where(b16, x, zero), n_ka)
    n_2 = each(dot(NN), n16, n16)
    n_4 = each(dot(NN), n_2, n_2)
    n_8 = each(dot(NN), n_4, n_4)
    inv = each(lambda x: eye - x, n16)
    for pw in (n_2, n_4, n_8):
        inv = each(jnp.add, inv, each(dot(NN), inv, pw))
    for sel in (lambda x: jnp.where(b32 & ~b16, x, zero), lambda x: jnp.where(b32, zero, x)):
        t1 = each(dot(NN), inv, each(sel, n_ka))
        inv = each(jnp.subtract, inv, each(dot(NN), t1, inv))

    mv = each(dot(NN), m_kk, v_s)
    w1u0 = each(dot(NN), inv, each(lambda x, y: jnp.concatenate([x, y], axis=1), kap_s, mv))
    ar = each(dot(NN), a_ra, w1u0)
    av = each(dot(NN), a_rk, v_s)
    r2 = each(lambda x, y: _unstack2(x - y[:, :LANES]), r_s, ar)
    y0 = each(lambda x, y: _unstack2(x - y[:, LANES:]), av, ar)
    w1_t = each(lambda x: x[:, :LANES].T, w1u0)
    u0_t = each(lambda x: x[:, LANES:].T, w1u0)
    v_t = each(lambda x: x.T, v_s)
    wa = each(dot(NN), w1_t, a_s)
    vk = each(dot(NN), v_t, k_s)
    ua = each(dot(NN), u0_t, a_s)
    g = each(lambda xe, x: jnp.where(row2 == col2, jnp.exp(xe), zero) - x, lp_end, wa)
    b = each(jnp.subtract, vk, ua)
    return list(zip(y0, r2, g, b))


def _rwkv_body(lwf, rf, kf, vf, kkf, kaf, lwb, rb, kb, vb, kkb, kab, yf_ref, yb_ref, s_ref):
    @pl.when(pl.program_id(1) == 0)
    def _():
        s_ref[...] = jnp.zeros(s_ref.shape, F32)

    L = RWKV_CHUNK
    n = RW_CHUNKS_PER_STEP
    rows = lambda j: slice(j * L, (j + 1) * L)
    refs = ((lwf, rf, kf, vf, kkf, kaf), (lwb, rb, kb, vb, kkb, kab))
    visit = [(d, j if d == 0 else n - 1 - j) for j in range(n) for d in (0, 1)]
    pre = _rwkv_chunks([tuple(ref[rows(j), :] for ref in refs[d]) + (d,) for d, j in visit])
    y_refs = (yf_ref, yb_ref)
    s = [s_ref[0], s_ref[1]]
    for (d, j), (y0, r2, g, b) in zip(visit, pre):
        y_refs[d][rows(j), :] = y0 + _mdot(r2, s[d], NT, RW_PASSES)
        s[d] = _mdot(s[d], g, NN, RW_PASSES) + b
    s_ref[0] = s[0]
    s_ref[1] = s[1]


def rwkv_scan(r, v, kk, lw_f, kt_f, ka_f, lw_b, kt_b, ka_b, n_ctx):
    t = r.shape[0]
    L = RWKV_CHUNK
    blk = RW_CHUNKS_PER_STEP * L
    assert 2 * L == LANES and t % blk == 0 and n_ctx % blk == 0
    nc, nc0 = t // blk, n_ctx // blk
    fwd = pl.BlockSpec((blk, LANES), lambda h, c: (c, h))
    bwd = pl.BlockSpec((blk, LANES), lambda h, c: (_bwd_chunk(c, nc0, nc), h))
    return pl.pallas_call(
        _rwkv_body,
        grid=(RWKV_HEADS // 2, nc),
        in_specs=[fwd] * 6 + [bwd] * 6,
        out_specs=[fwd, bwd],
        out_shape=[jax.ShapeDtypeStruct((t, D_MODEL), F32)] * 2,
        scratch_shapes=[pltpu.VMEM((2, LANES, LANES), F32)],
        compiler_params=pltpu.CompilerParams(
            dimension_semantics=("arbitrary", "arbitrary"), vmem_limit_bytes=VMEM_LIMIT),
        name="rwkv7_scan",
    )(lw_f, r, kt_f, v, kk, ka_f, lw_b, r, kt_b, v, kk, ka_b)


def _peer_stats(hx_ref, wkh_ref, wkl_ref, sc_ref, top_ref, hxb_ref, hxl_ref, n_ref, f0_ref, r1_ref, e1_ref):
    tm = hx_ref.shape[0]
    hx_hi, hx_lo = _split(hx_ref[...], 2)
    hxb_ref[...] = hx_hi
    hxl_ref[...] = hx_lo
    rows_per = 2 * LANES

    def scores(c, carry):
        rows = pl.ds(pl.multiple_of(c * rows_per, rows_per), rows_per)
        wk_hi = wkh_ref[rows, :]
        sc_ref[rows, :] = (_d(wk_hi, hxl_ref[...], NT) + _d(wkl_ref[rows, :], hxb_ref[...], NT)) \
            + _d(wk_hi, hxb_ref[...], NT)
        return carry

    lax.fori_loop(0, wkh_ref.shape[0] // rows_per, scores, 0)

    def block(tb, carry):
        lanes = pl.ds(pl.multiple_of(tb * LANES, LANES), LANES)
        neg = jnp.full((N_KEYS, LANES), -jnp.inf, F32)
        for h in range(PEER_HEADS):
            for p in range(2):
                cur = sc_ref[(2 * h + p) * N_KEYS:(2 * h + p + 1) * N_KEYS, lanes]
                rank = jnp.full((N_KEYS, LANES), float(PEER_TOPK), F32)
                for a in range(PEER_TOPK):
                    m = jnp.max(cur, axis=0, keepdims=True)
                    top_ref[p, a, h:h + 1, lanes] = m
                    hit = cur >= m
                    cur = jnp.where(hit, neg, cur)
                    if p == 1:
                        rank = jnp.where(hit, float(a), rank)
                if p == 1:
                    r1_ref[h, :, lanes] = rank.astype(BF16)
        top0 = [top_ref[0, a, :, lanes] for a in range(PEER_TOPK)]
        top1 = [top_ref[1, b, :, lanes] for b in range(PEER_TOPK)]
        cands = [top0[a] + top1[b]
                 for a in range(PEER_TOPK) for b in range(PEER_TOPK) if (a + 1) * (b + 1) <= PEER_TOPK]
        c_max = cands[0]
        z = jnp.zeros_like(c_max)
        tau = c_max
        for a in range(PEER_TOPK):
            tau = functools.reduce(jnp.maximum, cands)
            z = z + jnp.exp(tau - c_max)
            cands = [jnp.where(cd >= tau, -jnp.inf, cd) for cd in cands]
        inv_z = 1.0 / z
        for h in range(PEER_HEADS):
            s0 = sc_ref[2 * h * N_KEYS:(2 * h + 1) * N_KEYS, lanes]
            n = jnp.zeros((N_KEYS, LANES), F32)
            for b in range(PEER_TOPK):
                n = jnp.where(s0 + top1[b][h:h + 1] >= tau[h:h + 1], float(b + 1), n)
            n_ref[h, :, lanes] = n
            f0_ref[h, :, lanes] = jnp.exp(s0 - top0[0][h:h + 1]) * inv_z[h:h + 1]
            e1_ref[h, :, lanes] = jnp.exp(sc_ref[(2 * h + 1) * N_KEYS:(2 * h + 2) * N_KEYS, lanes] - top1[0][h:h + 1]).astype(BF16)
        return carry

    lax.fori_loop(0, tm // LANES, block, 0)


PEER_I_GROUP = 4


def _peer_body(hx_ref, wkh_ref, wkl_ref, u_ref, vt_ref, o_ref,
               sc_ref, top_ref, hxb_ref, hxl_ref, n_ref, f0_ref, r1_ref, e1_ref, w_ref, acc_ref):
    e = pl.program_id(1)
    tm = hx_ref.shape[0]
    nsb = tm // PEER_SB

    @pl.when(e == 0)
    def _():
        _peer_stats(hx_ref, wkh_ref, wkl_ref, sc_ref, top_ref, hxb_ref, hxl_ref, n_ref, f0_ref, r1_ref, e1_ref)
        acc_ref[...] = jnp.zeros(acc_ref.shape, F32)

    i_rows = pl.ds(pl.multiple_of(e * PEER_I_BLOCK, PEER_I_BLOCK), PEER_I_BLOCK)

    def activations(sb):
        act = _d(u_ref[...], hxb_ref[sb * PEER_SB:(sb + 1) * PEER_SB, :], NT)
        return (0.5 * act * (1.0 + lax.erf(act * (2.0 ** -0.5)))).astype(BF16)

    def gates(sb, act):
        for hb in range(PEER_SB // LANES):
            lanes = slice(sb * PEER_SB + hb * LANES, sb * PEER_SB + (hb + 1) * LANES)
            sub = slice(hb * LANES, (hb + 1) * LANES)
            n8 = [n_ref[h, i_rows, lanes].astype(BF16) for h in range(PEER_HEADS)]
            f8 = [f0_ref[h, i_rows, lanes].astype(BF16) for h in range(PEER_HEADS)]
            for ig in range(0, PEER_I_BLOCK, PEER_I_GROUP):
                g = [jnp.zeros((N_KEYS, LANES), BF16) for _ in range(PEER_I_GROUP)]
                for h in range(PEER_HEADS):
                    r1 = r1_ref[h, :, lanes]
                    e1 = e1_ref[h, :, lanes]
                    for k in range(PEER_I_GROUP):
                        ii = ig + k
                        g[k] = g[k] + jnp.where(r1 < n8[h][ii:ii + 1], e1 * f8[h][ii:ii + 1], jnp.zeros_like(e1))
                for k in range(PEER_I_GROUP):
                    rows = slice((ig + k) * N_KEYS, (ig + k + 1) * N_KEYS)
                    w_ref[sb, rows, sub] = g[k] * act[rows, sub]

    def accumulate(sb):
        cols = slice(sb * PEER_SB, (sb + 1) * PEER_SB)
        acc_ref[:, cols] += _d(vt_ref[...], w_ref[sb], NN)

    act = activations(0)
    for sb in range(nsb):
        nxt = activations(sb + 1) if sb + 1 < nsb else None
        gates(sb, act)
        accumulate(sb)
        act = nxt

    @pl.when(e == pl.num_programs(1) - 1)
    def _():
        o_ref[...] = acc_ref[...].T


def _fold_body(k_ref, w_ref, o_ref):
    o_ref[...] = _mdot(k_ref[...], w_ref[...], NN, 6)


def peer_fold_keys(wq, keys):
    d = wq.shape[0]
    nhp, nk, dk = keys.shape
    wqt = wq.T.reshape(nhp, dk, d)
    return pl.pallas_call(
        _fold_body,
        grid=(nhp,),
        in_specs=[pl.BlockSpec((None, nk, dk), lambda i: (i, 0, 0)),
                  pl.BlockSpec((None, dk, d), lambda i: (i, 0, 0))],
        out_specs=pl.BlockSpec((nk, d), lambda i: (i, 0)),
        out_shape=jax.ShapeDtypeStruct((nhp * nk, d), F32),
        compiler_params=pltpu.CompilerParams(dimension_semantics=("arbitrary",), vmem_limit_bytes=VMEM_LIMIT),
        name="peer_fold_keys",
    )(keys, wqt)


def peer(hx, wk, u_bf, vt_bf):
    t0, d = hx.shape
    tm = PEER_TM
    t = -(-t0 // tm) * tm
    hx = jnp.pad(hx, ((0, t - t0), (0, 0)))
    wk_hi, wk_lo = _split(wk, 2)
    ne = u_bf.shape[0] // PEER_TE
    h = PEER_HEADS
    out = pl.pallas_call(
        _peer_body,
        grid=(t // tm, ne),
        in_specs=[pl.BlockSpec((tm, d), lambda i, e: (i, 0), pipeline_mode=pl.Buffered(1)),
                  pl.BlockSpec(wk.shape, lambda i, e: (0, 0), pipeline_mode=pl.Buffered(1)),
                  pl.BlockSpec(wk.shape, lambda i, e: (0, 0), pipeline_mode=pl.Buffered(1)),
                  pl.BlockSpec((PEER_TE, d), lambda i, e: (e, 0)),
                  pl.BlockSpec((d, PEER_TE), lambda i, e: (0, e))],
        out_specs=pl.BlockSpec((tm, d), lambda i, e: (i, 0)),
        out_shape=jax.ShapeDtypeStruct((t, d), F32),
        scratch_shapes=[pltpu.VMEM((2 * h * N_KEYS, tm), F32), pltpu.VMEM((2, PEER_TOPK, h, tm), F32),
                        pltpu.VMEM((tm, d), BF16), pltpu.VMEM((tm, d), BF16),
                        pltpu.VMEM((h, N_KEYS, tm), F32), pltpu.VMEM((h, N_KEYS, tm), F32),
                        pltpu.VMEM((h, N_KEYS, tm), BF16), pltpu.VMEM((h, N_KEYS, tm), BF16),
                        pltpu.VMEM((tm // PEER_SB, PEER_TE, PEER_SB), BF16), pltpu.VMEM((d, tm), F32)],
        compiler_params=pltpu.CompilerParams(
            dimension_semantics=("arbitrary", "arbitrary"), vmem_limit_bytes=VMEM_LIMIT),
        name="peer_dense",
    )(hx, wk_hi, wk_lo, u_bf, vt_bf)
    return out[:t0]


def _ln(x, g, b):
    mu = jnp.mean(x, -1, keepdims=True)
    xc = x - mu
    var = jnp.mean(xc * xc, -1, keepdims=True)
    return xc * lax.rsqrt(var + LN_EPS) * g + b


def _head_norm(h, nheads, eps):
    t = h.shape[0]
    hh = h.reshape(t, nheads, -1)
    mu = jnp.mean(hh, -1, keepdims=True)
    xc = hh - mu
    var = jnp.mean(xc * xc, -1, keepdims=True)
    return (xc * lax.rsqrt(var + eps)).reshape(t, -1)


def _pad_cols(w, n):
    return jnp.pad(w, ((0, 0), (0, n - w.shape[1])))


def _shift_rows(u, k):
    if k > 0:
        return jnp.concatenate([u[k:], jnp.zeros_like(u[:k])], axis=0)
    return jnp.concatenate([jnp.zeros_like(u[:-k]), u[:k]], axis=0)


def _grid_conv(u, w, b):
    s = u.shape[0]
    colid = (jnp.arange(s) % GRID_W)[:, None]
    out = jnp.zeros_like(u) + b
    for di in range(3):
        for dj in range(3):
            off = (di - 1) * GRID_W + (dj - 1)
            sh = _shift_rows(u, off) if off else u
            if dj == 0:
                sh = jnp.where(colid == 0, 0.0, sh)
            elif dj == 2:
                sh = jnp.where(colid == GRID_W - 1, 0.0, sh)
            out = out + sh * w[di, dj]
    return out


def _seq_conv(u, w, b):
    wc = w[1]
    return _shift_rows(u, -1) * wc[0] + u * wc[1] + _shift_rows(u, 1) * wc[2] + b


def _qshift(u):
    q = u.shape[1] // 4
    colid = (jnp.arange(u.shape[0]) % GRID_W)[:, None]
    left = jnp.where(colid == 0, 0.0, _shift_rows(u[:, :q], -1))
    right = jnp.where(colid == GRID_W - 1, 0.0, _shift_rows(u[:, q:2 * q], 1))
    up = _shift_rows(u[:, 2 * q:3 * q], -GRID_W)
    down = _shift_rows(u[:, 3 * q:], GRID_W)
    return jnp.concatenate([left, right, up, down], axis=1)


def _shift_seq(u):
    h = u.shape[1] // 2
    return jnp.concatenate([_shift_rows(u[:, :h], -1), _shift_rows(u[:, h:], 1)], axis=1)


def _mlstm_layer(h, n_ctx, w_in, b_in, conv_w, conv_b, hn_g, w_out):
    d = D_MODEL
    t = h.shape[0]
    npad = 4 * d + LANES
    p = matmul(h, _pad_cols(w_in, npad), jnp.pad(b_in, (0, npad - b_in.shape[0])))
    qk_pre = p[:, :2 * d]
    qk = jax.nn.silu(jnp.concatenate([_seq_conv(qk_pre[:n_ctx], conv_w, conv_b),
                                      _grid_conv(qk_pre[n_ctx:], conv_w, conv_b)], axis=0))
    q = qk[:, :d] * (MLSTM_DH ** -0.5)
    k = qk[:, d:]
    v = p[:, 2 * d:3 * d]
    o = jax.nn.sigmoid(p[:, 3 * d:4 * d])
    g = p[:, 4 * d:4 * d + 4 * MLSTM_HEADS].reshape(t, 4, MLSTM_HEADS)
    g = jnp.concatenate([g[:, :2], jax.nn.log_sigmoid(g[:, 2:])], axis=1)
    gh = jnp.transpose(g, (2, 0, 1))
    gcol = jnp.pad(gh, ((0, 0), (0, 0), (0, LANES - 4)))
    grow = jnp.pad(jnp.transpose(gh, (0, 2, 1)), ((0, 0), (0, 4), (0, 0)))
    hf, hb = mlstm_scan(q, k, v, gcol, grow, n_ctx)
    hn = _head_norm(hf + hb, MLSTM_HEADS, LN_EPS)
    return matmul(o * hn * hn_g, w_out)


def _rwkv_layer(h, n_ctx, mu, w_rkv, w0, w1, w2, a0, a1, a2, g1, g2, k_k, k_a, r_k, lnx_g, lnx_b, w_out):
    d = D_MODEL
    t = h.shape[0]
    shifted = jnp.concatenate([_shift_seq(h[:n_ctx]), _qshift(h[n_ctx:])], axis=0)
    dx = shifted - h
    xm = [h + dx * mu[n] for n in range(6)]
    r = matmul(xm[0], w_rkv[0])
    k = matmul(xm[1], w_rkv[1])
    v = matmul(xm[2], w_rkv[2])
    lora = DECAY_LORA = w1.shape[-1]
    w1c = _pad_cols(jnp.concatenate([w1[0], w1[1]], axis=1), LANES)
    zpad = jnp.zeros((lora, d), F32)
    w2c = jnp.concatenate([jnp.concatenate([w2[0], zpad], axis=1),
                           jnp.concatenate([zpad, w2[1]], axis=1)], axis=0)
    wpre = matmul(jnp.tanh(matmul(xm[3], w1c)), w2c) + jnp.concatenate([w0[0], w0[1]])
    lw = -jnp.exp(-jax.nn.softplus(-wpre) - 0.5)
    a1c = _pad_cols(jnp.concatenate([a1[0], a1[1]], axis=1), LANES)
    a2c = jnp.concatenate([jnp.concatenate([a2[0], zpad], axis=1),
                           jnp.concatenate([zpad, a2[1]], axis=1)], axis=0)
    a = jax.nn.sigmoid(matmul(matmul(xm[4], a1c), a2c) + jnp.concatenate([a0[0], a0[1]]))
    glora = g1.shape[1]
    gpad = 2 * LANES
    gg = jax.nn.sigmoid(matmul(xm[5], _pad_cols(g1, gpad)))
    g = matmul(gg, jnp.pad(g2, ((0, gpad - glora), (0, 0))))
    kk = (k * k_k).reshape(t, RWKV_HEADS, RWKV_N)
    kk = (kk / jnp.maximum(jnp.sqrt(jnp.sum(kk * kk, -1, keepdims=True)), 1e-12)).reshape(t, d)
    a_f, a_b = a[:, :d], a[:, d:]
    kt_f = k * (1.0 + (a_f - 1.0) * k_a)
    kt_b = k * (1.0 + (a_b - 1.0) * k_a)
    yf, yb = rwkv_scan(r, v, kk, lw[:, :d], kt_f, kk * a_f, lw[:, d:], kt_b, kk * a_b, n_ctx)
    yn = _head_norm(yf + yb, RWKV_HEADS, RWKV_GN_EPS) * lnx_g + lnx_b
    kbar = 0.5 * (kt_f + kt_b)
    bonus = jnp.sum((r * kbar * r_k).reshape(t, RWKV_HEADS, RWKV_N), -1, keepdims=True) \
        * v.reshape(t, RWKV_HEADS, RWKV_N)
    return matmul((yn + bonus.reshape(t, d)) * g, w_out)


def _forward(x, c, ctx, c_ctx, ada_w, ada_b, ln_g, ln_b,
             ml_w_in, ml_b_in, ml_conv_w, ml_conv_b, ml_hn_g, ml_w_out,
             rw_mu, rw_w_rkv, rw_w0, rw_w1, rw_w2, rw_a0, rw_a1, rw_a2, rw_g1, rw_g2,
             rw_k_k, rw_k_a, rw_r_k, rw_lnx_g, rw_lnx_b, rw_w_out,
             pk_wq, pk_keys, pk_u, pk_v):
    d = D_MODEL
    n_ctx = ctx.shape[1]
    n_lat = x.shape[1]
    xs = jnp.concatenate([ctx[0], x[0]], axis=0)
    t = xs.shape[0]
    is_ctx = (jnp.arange(t) < n_ctx)[:, None]
    s_in = jnp.zeros((8, d), F32).at[0].set(jax.nn.silu(c[0])).at[1].set(jax.nn.silu(c_ctx))
    depth = ada_w.shape[0]
    for i in range(depth):
        j = i // 2
        mod = matmul(s_in, ada_w[i], ada_b[i], passes=3)
        m = [jnp.where(is_ctx, mod[1, n * d:(n + 1) * d], mod[0, n * d:(n + 1) * d]) for n in range(N_MOD)]
        h = xs * (1.0 + m[1]) + m[0]
        if i % 2 == 0:
            y = _mlstm_layer(h, n_ctx, ml_w_in[j], ml_b_in[j], ml_conv_w[j], ml_conv_b[j],
                             ml_hn_g[j], ml_w_out[j])
        else:
            y = _rwkv_layer(h, n_ctx, rw_mu[j], rw_w_rkv[j], rw_w0[j], rw_w1[j], rw_w2[j],
                            rw_a0[j], rw_a1[j], rw_a2[j], rw_g1[j], rw_g2[j], rw_k_k[j],
                            rw_k_a[j], rw_r_k[j], rw_lnx_g[j], rw_lnx_b[j], rw_w_out[j])
        xs = _ln(DN_ALPHA * xs + m[2] * y, ln_g[i, 0], ln_b[i, 0])
        h = xs * (1.0 + m[4]) + m[3]
        wk = peer_fold_keys(pk_wq[i], pk_keys[i].reshape(2 * PEER_HEADS, N_KEYS, PEER_DQ // 2))
        y = peer(h, wk, pk_u[i].astype(BF16), pk_v[i].astype(BF16).T)
        xs = _ln(DN_ALPHA * xs + m[5] * y, ln_g[i, 1], ln_b[i, 1])
    return xs[n_ctx:][None]


def kernel(x, c, ctx, c_ctx, ada_w, ada_b, ln_g, ln_b, ml_w_in, ml_b_in, ml_conv_w, ml_conv_b, ml_hn_g, ml_w_out, rw_mu, rw_w_rkv, rw_w0, rw_w1, rw_w2, rw_a0, rw_a1, rw_a2, rw_g1, rw_g2, rw_k_k, rw_k_a, rw_r_k, rw_lnx_g, rw_lnx_b, rw_w_out, pk_wq, pk_keys, pk_u, pk_v):
    return _forward(x, c, ctx, c_ctx, ada_w, ada_b, ln_g, ln_b,
                    ml_w_in, ml_b_in, ml_conv_w, ml_conv_b, ml_hn_g, ml_w_out,
                    rw_mu, rw_w_rkv, rw_w0, rw_w1, rw_w2, rw_a0, rw_a1, rw_a2, rw_g1, rw_g2,
                    rw_k_k, rw_k_a, rw_r_k, rw_lnx_g, rw_lnx_b, rw_w_out,
                    pk_wq, pk_keys, pk_u, pk_v)
```

```python
import functools

import jax
import jax.numpy as jnp
from jax import lax
from jax.experimental import pallas as pl
from jax.experimental.pallas import tpu as pltpu

F32 = jnp.float32
BF16 = jnp.bfloat16

D_MODEL = 1024
DEPTH = 4
GRID_W = 64
N_MOD = 6
DN_ALPHA = (2.0 * DEPTH) ** 0.25
LN_EPS = 1e-5

MLSTM_HEADS = 4
MLSTM_DH = D_MODEL // MLSTM_HEADS
MLSTM_CHUNK = 128
M_INIT = -1e30

RWKV_N = 64
RWKV_HEADS = D_MODEL // RWKV_N
RWKV_CHUNK = 64
RWKV_GN_EPS = 64e-5

N_KEYS = 128
PEER_HEADS = 8
PEER_DQ = 256
PEER_TOPK = 16
PEER_I_BLOCK = 8
PEER_TE = PEER_I_BLOCK * N_KEYS
PEER_TM = 512
PEER_SB = 256

LANES = 128
VMEM_LIMIT = 62 * 1024 * 1024

NN = ((1,), (0,))
NT = ((1,), (1,))
TN = ((0,), (0,))


def _split(x, n):
    parts = []
    r = x.astype(F32)
    for i in range(n):
        p = r.astype(BF16)
        parts.append(p)
        if i + 1 < n:
            r = r - p.astype(F32)
    return parts


def _d(a, b, dims):
    return lax.dot_general(a, b, (dims, ((), ())), preferred_element_type=F32)


def _mdot(a, b, dims, passes):
    if passes == 1:
        return _d(a.astype(BF16), b.astype(BF16), dims)
    if passes == 3:
        a0, a1 = _split(a, 2)
        b0, b1 = _split(b, 2)
        return (_d(a0, b1, dims) + _d(a1, b0, dims)) + _d(a0, b0, dims)
    a0, a1, a2 = _split(a, 3)
    b0, b1, b2 = _split(b, 3)
    lo = (_d(a0, b2, dims) + _d(a2, b0, dims)) + _d(a1, b1, dims)
    mid = _d(a0, b1, dims) + _d(a1, b0, dims)
    return (lo + mid) + _d(a0, b0, dims)


def _dot_01_lhs(m01, x):
    mb = m01.astype(BF16)
    x0, x1, x2 = _split(x, 3)
    return (_d(mb, x2, NN) + _d(mb, x1, NN)) + _d(mb, x0, NN)


def _dot_01_rhs(x, m01):
    mb = m01.astype(BF16)
    x0, x1, x2 = _split(x, 3)
    return (_d(x2, mb, NN) + _d(x1, mb, NN)) + _d(x0, mb, NN)


def _pick(n, cands):
    for c in cands:
        if n % c == 0:
            return c
    raise ValueError(f"no tile for {n}")


def _mm_body(x_ref, w_ref, b_ref, o_ref, *, passes):
    o_ref[...] = _mdot(x_ref[...], w_ref[...], NN, passes) + b_ref[...]


def matmul(x, w, b=None, *, passes=1):
    m, k = x.shape
    n = w.shape[1]
    assert n % LANES == 0 and w.shape[0] == k
    tm = m if m <= 1024 else _pick(m, (640, 512, 384, 256, 128))
    tn = _pick(n, (1024, 768, 640, 512, 384, 256, 128))
    if b is None:
        b = jnp.zeros((n,), F32)
    if passes == 1:
        w = w.astype(BF16)
    return pl.pallas_call(
        functools.partial(_mm_body, passes=passes),
        grid=(m // tm, n // tn),
        in_specs=[pl.BlockSpec((tm, k), lambda i, j: (i, 0)),
                  pl.BlockSpec((k, tn), lambda i, j: (0, j)),
                  pl.BlockSpec((1, tn), lambda i, j: (0, j))],
        out_specs=pl.BlockSpec((tm, tn), lambda i, j: (i, j)),
        out_shape=jax.ShapeDtypeStruct((m, n), F32),
        compiler_params=pltpu.CompilerParams(
            dimension_semantics=("arbitrary", "arbitrary"), vmem_limit_bytes=VMEM_LIMIT),
        name="proj_matmul",
    )(x, w, b.reshape(1, n).astype(F32))


def _mlstm_dir(q, k, v, gc, gr, c_ref, n_ref, m_ref, d):
    L = q.shape[0]
    row = lax.broadcasted_iota(jnp.int32, (L, L), 0)
    col = lax.broadcasted_iota(jnp.int32, (L, L), 1)
    lower = col <= row
    upper = col >= row
    seen = upper if d else lower
    tri_c = jnp.where(seen, 1.0, 0.0)
    tri_r = jnp.where(lower if d else upper, 1.0, 0.0)
    b_col = _dot_01_lhs(tri_c, gc)[:, 2 + d:3 + d]
    b_row = _dot_01_rhs(gr, tri_r)[2 + d:3 + d, :]
    ig_col = gc[:, d:d + 1]
    ig_row = gr[d:d + 1, :]
    m_st = m_ref[d, 0:1, 0:1]
    c_st = c_ref[d]
    n_st = n_ref[d]

    dlog = jnp.where(seen, b_col - b_row + ig_row, -jnp.inf)
    m_inter = b_col + m_st
    m_t = jnp.maximum(m_inter, jnp.max(dlog, axis=1, keepdims=True))
    qb = q.astype(BF16)
    kb = k.astype(BF16)
    vb = v.astype(BF16)
    s = _d(qb, kb, NT) * jnp.exp(dlog - m_t)
    dec = jnp.exp(m_inter - m_t)
    num = _d(s.astype(BF16), vb, NN) + dec * _d(qb, c_st.astype(BF16), NN)
    den = jnp.sum(s, axis=1, keepdims=True) + dec * jnp.sum(q * n_st, axis=1, keepdims=True)
    h = num / jnp.maximum(jnp.abs(den), jnp.exp(-m_t))

    b_last = b_col[0:1, :] if d else b_col[L - 1:L, :]
    w_c = b_last - b_col + ig_col
    m_new = jnp.maximum(b_last + m_st, jnp.max(w_c, axis=0, keepdims=True))
    a_c = jnp.exp(w_c - m_new)
    g_prev = jnp.exp(b_last + m_st - m_new)
    ak = a_c * k
    c_ref[d] = g_prev * c_st + _d(ak.T.astype(BF16), vb, NN)
    n_ref[d] = g_prev * n_st + jnp.sum(ak, axis=0, keepdims=True)
    m_ref[d] = jnp.broadcast_to(m_new, m_ref.shape[1:])
    return h


def _mlstm_body(qf, kf, vf, gcf, grf, qb, kb, vb, gcb, grb, hf_ref, hb_ref, c_ref, n_ref, m_ref):
    @pl.when(pl.program_id(1) == 0)
    def _():
        c_ref[...] = jnp.zeros(c_ref.shape, F32)
        n_ref[...] = jnp.zeros(n_ref.shape, F32)
        m_ref[...] = jnp.full(m_ref.shape, M_INIT, F32)

    hf_ref[...] = _mlstm_dir(qf[...], kf[...], vf[...], gcf[...], grf[...], c_ref, n_ref, m_ref, 0)
    hb_ref[...] = _mlstm_dir(qb[...], kb[...], vb[...], gcb[...], grb[...], c_ref, n_ref, m_ref, 1)


def _bwd_chunk(c, nc0, nc):
    return jnp.where(c < nc0, nc0 - 1 - c, nc - 1 - (c - nc0))


def mlstm_scan(q, k, v, gcol, grow, n_ctx):
    t = q.shape[0]
    L, dh = MLSTM_CHUNK, MLSTM_DH
    nc, nc0 = t // L, n_ctx // L
    fwd = lambda h, c: (c, h)
    bwd = lambda h, c: (_bwd_chunk(c, nc0, nc), h)
    qkv = lambda im: pl.BlockSpec((L, dh), im)
    gc_spec = lambda f: pl.BlockSpec((None, L, LANES), lambda h, c: (h, f(c), 0))
    gr_spec = lambda f: pl.BlockSpec((None, 8, L), lambda h, c: (h, 0, f(c)))
    idf = lambda c: c
    idb = lambda c: _bwd_chunk(c, nc0, nc)
    return pl.pallas_call(
        _mlstm_body,
        grid=(MLSTM_HEADS, nc),
        in_specs=[qkv(fwd), qkv(fwd), qkv(fwd), gc_spec(idf), gr_spec(idf),
                  qkv(bwd), qkv(bwd), qkv(bwd), gc_spec(idb), gr_spec(idb)],
        out_specs=[qkv(fwd), qkv(bwd)],
        out_shape=[jax.ShapeDtypeStruct((t, D_MODEL), F32)] * 2,
        scratch_shapes=[pltpu.VMEM((2, dh, dh), F32), pltpu.VMEM((2, 1, dh), F32),
                        pltpu.VMEM((2, 8, LANES), F32)],
        compiler_params=pltpu.CompilerParams(
            dimension_semantics=("arbitrary", "arbitrary"), vmem_limit_bytes=VMEM_LIMIT),
        name="mlstm_scan",
    )(q, k, v, gcol, grow, q, k, v, gcol, grow)


RW_PASSES = 1
RW_CHUNKS_PER_STEP = 4


def _stack2(x, lane_head):
    return jnp.concatenate([jnp.where(lane_head == 0, x, 0.0), jnp.where(lane_head == 1, x, 0.0)], axis=0)


def _unstack2(x):
    L = x.shape[0] // 2
    return x[:L] + x[L:]


def _rwkv_chunks(chains):
    L = chains[0][0].shape[0]
    n2 = 2 * L
    p = RW_PASSES
    ds = [c[6] for c in chains]
    each = lambda f, *ls: [f(*xs) for xs in zip(*ls)]
    dot = lambda dims: (lambda a, b: _mdot(a, b, dims, p))
    lw, r, k, v, kap, alp = ([c[i] for c in chains] for i in range(6))

    row = lax.broadcasted_iota(jnp.int32, (L, L), 0)
    col = lax.broadcasted_iota(jnp.int32, (L, L), 1)
    tris = (jnp.where(col <= row, 1.0, 0.0), jnp.where(col >= row, 1.0, 0.0))
    lp = [_dot_01_lhs(tris[d], x) for d, x in zip(ds, lw)]
    lp_end = [x[0:1, :] if d else x[L - 1:L, :] for d, x in zip(ds, lp)]
    e_neg = each(lambda x: jnp.exp(-x), lp)
    e_end = each(lambda x, xe: jnp.exp(xe - x), lp, lp_end)
    kap_t = each(lambda x, y, z: x * jnp.exp(y - z), kap, lp, lw)
    r_t = each(lambda x, y: x * jnp.exp(y), r, lp)
    k_h = each(jnp.multiply, k, e_neg)
    a_h = each(jnp.multiply, alp, e_neg)
    k_e = each(jnp.multiply, k, e_end)
    a_e = each(jnp.multiply, alp, e_end)

    lane_head = lax.broadcasted_iota(jnp.int32, (L, LANES), 1) // RWKV_N
    st = lambda x: _stack2(x, lane_head)
    kap_s, r_s, v_s, k_s, a_s = (each(st, x) for x in (kap_t, r_t, v, k_e, a_e))
    rhs_k = each(lambda x: jnp.concatenate([x, x], axis=0), k_h)
    rhs_a = each(lambda x: jnp.concatenate([x, x], axis=0), a_h)

    row2 = lax.broadcasted_iota(jnp.int32, (n2, n2), 0)
    col2 = lax.broadcasted_iota(jnp.int32, (n2, n2), 1)
    same_head = (row2 // L) == (col2 // L)
    strict = (same_head & (col2 < row2), same_head & (col2 > row2))
    incl = (same_head & (col2 <= row2), same_head & (col2 >= row2))
    zero = jnp.zeros((n2, n2), F32)
    masked = lambda masks: (lambda d, x: jnp.where(masks[d], x, zero))
    n_ka = each(masked(strict), ds, each(dot(NT), kap_s, rhs_a))
    m_kk = each(masked(strict), ds, each(dot(NT), kap_s, rhs_k))
    a_rk = each(masked(incl), ds, each(dot(NT), r_s, rhs_k))
    a_ra = each(masked(incl), ds, each(dot(NT), r_s, rhs_a))

    b16 = (row2 // 16) == (col2 // 16)
    b32 = (row2 // 32) == (col2 // 32)
    eye = jnp.where(row2 == col2, 1.0, 0.0)
    n16 = each(lambda x: jnp.where(b16, x, zero), n_ka)
    n_2 = each(dot(NN), n16, n16)
    n_4 = each(dot(NN), n_2, n_2)
    n_8 = each(dot(NN), n_4, n_4)
    inv = each(lambda x: eye - x, n16)
    for pw in (n_2, n_4, n_8):
        inv = each(jnp.add, inv, each(dot(NN), inv, pw))
    for sel in (lambda x: jnp.where(b32 & ~b16, x, zero), lambda x: jnp.where(b32, zero, x)):
        t1 = each(dot(NN), inv, each(sel, n_ka))
        inv = each(jnp.subtract, inv, each(dot(NN), t1, inv))

    mv = each(dot(NN), m_kk, v_s)
    w1u0 = each(dot(NN), inv, each(lambda x, y: jnp.concatenate([x, y], axis=1), kap_s, mv))
    ar = each(dot(NN), a_ra, w1u0)
    av = each(dot(NN), a_rk, v_s)
    r2 = each(lambda x, y: _unstack2(x - y[:, :LANES]), r_s, ar)
    y0 = each(lambda x, y: _unstack2(x - y[:, LANES:]), av, ar)
    w1_t = each(lambda x: x[:, :LANES].T, w1u0)
    u0_t = each(lambda x: x[:, LANES:].T, w1u0)
    v_t = each(lambda x: x.T, v_s)
    wa = each(dot(NN), w1_t, a_s)
    vk = each(dot(NN), v_t, k_s)
    ua = each(dot(NN), u0_t, a_s)
    g = each(lambda xe, x: jnp.where(row2 == col2, jnp.exp(xe), zero) - x, lp_end, wa)
    b = each(jnp.subtract, vk, ua)
    return list(zip(y0, r2, g, b))


def _rwkv_body(lwf, rf, kf, vf, kkf, kaf, lwb, rb, kb, vb, kkb, kab, yf_ref, yb_ref, s_ref):
    @pl.when(pl.program_id(1) == 0)
    def _():
        s_ref[...] = jnp.zeros(s_ref.shape, F32)

    L = RWKV_CHUNK
    n = RW_CHUNKS_PER_STEP
    rows = lambda j: slice(j * L, (j + 1) * L)
    refs = ((lwf, rf, kf, vf, kkf, kaf), (lwb, rb, kb, vb, kkb, kab))
    visit = [(d, j if d == 0 else n - 1 - j) for j in range(n) for d in (0, 1)]
    pre = _rwkv_chunks([tuple(ref[rows(j), :] for ref in refs[d]) + (d,) for d, j in visit])
    y_refs = (yf_ref, yb_ref)
    s = [s_ref[0], s_ref[1]]
    for (d, j), (y0, r2, g, b) in zip(visit, pre):
        y_refs[d][rows(j), :] = y0 + _mdot(r2, s[d], NT, RW_PASSES)
        s[d] = _mdot(s[d], g, NN, RW_PASSES) + b
    s_ref[0] = s[0]
    s_ref[1] = s[1]


def rwkv_scan(r, v, kk, lw_f, kt_f, ka_f, lw_b, kt_b, ka_b, n_ctx):
    t = r.shape[0]
    L = RWKV_CHUNK
    blk = RW_CHUNKS_PER_STEP * L
    assert 2 * L == LANES and t % blk == 0 and n_ctx % blk == 0
    nc, nc0 = t // blk, n_ctx // blk
    fwd = pl.BlockSpec((blk, LANES), lambda h, c: (c, h))
    bwd = pl.BlockSpec((blk, LANES), lambda h, c: (_bwd_chunk(c, nc0, nc), h))
    return pl.pallas_call(
        _rwkv_body,
        grid=(RWKV_HEADS // 2, nc),
        in_specs=[fwd] * 6 + [bwd] * 6,
        out_specs=[fwd, bwd],
        out_shape=[jax.ShapeDtypeStruct((t, D_MODEL), F32)] * 2,
        scratch_shapes=[pltpu.VMEM((2, LANES, LANES), F32)],
        compiler_params=pltpu.CompilerParams(
            dimension_semantics=("arbitrary", "arbitrary"), vmem_limit_bytes=VMEM_LIMIT),
        name="rwkv7_scan",
    )(lw_f, r, kt_f, v, kk, ka_f, lw_b, r, kt_b, v, kk, ka_b)


def _peer_stats(hx_ref, wkh_ref, wkl_ref, sc_ref, top_ref, hxb_ref, hxl_ref, n_ref, f0_ref, r1_ref, e1_ref):
    tm = hx_ref.shape[0]
    hx_hi, hx_lo = _split(hx_ref[...], 2)
    hxb_ref[...] = hx_hi
    hxl_ref[...] = hx_lo
    rows_per = 2 * LANES

    def scores(c, carry):
        rows = pl.ds(pl.multiple_of(c * rows_per, rows_per), rows_per)
        wk_hi = wkh_ref[rows, :]
        sc_ref[rows, :] = (_d(wk_hi, hxl_ref[...], NT) + _d(wkl_ref[rows, :], hxb_ref[...], NT)) \
            + _d(wk_hi, hxb_ref[...], NT)
        return carry

    lax.fori_loop(0, wkh_ref.shape[0] // rows_per, scores, 0)

    def block(tb, carry):
        lanes = pl.ds(pl.multiple_of(tb * LANES, LANES), LANES)
        neg = jnp.full((N_KEYS, LANES), -jnp.inf, F32)
        for h in range(PEER_HEADS):
            for p in range(2):
                cur = sc_ref[(2 * h + p) * N_KEYS:(2 * h + p + 1) * N_KEYS, lanes]
                rank = jnp.full((N_KEYS, LANES), float(PEER_TOPK), F32)
                for a in range(PEER_TOPK):
                    m = jnp.max(cur, axis=0, keepdims=True)
                    top_ref[p, a, h:h + 1, lanes] = m
                    hit = cur >= m
                    cur = jnp.where(hit, neg, cur)
                    if p == 1:
                        rank = jnp.where(hit, float(a), rank)
                if p == 1:
                    r1_ref[h, :, lanes] = rank.astype(BF16)
        top0 = [top_ref[0, a, :, lanes] for a in range(PEER_TOPK)]
        top1 = [top_ref[1, b, :, lanes] for b in range(PEER_TOPK)]
        cands = [top0[a] + top1[b]
                 for a in range(PEER_TOPK) for b in range(PEER_TOPK) if (a + 1) * (b + 1) <= PEER_TOPK]
        c_max = cands[0]
        z = jnp.zeros_like(c_max)
        tau = c_max
        for a in range(PEER_TOPK):
            tau = functools.reduce(jnp.maximum, cands)
            z = z + jnp.exp(tau - c_max)
            cands = [jnp.where(cd >= tau, -jnp.inf, cd) for cd in cands]
        inv_z = 1.0 / z
        for h in range(PEER_HEADS):
            s0 = sc_ref[2 * h * N_KEYS:(2 * h + 1) * N_KEYS, lanes]
            n = jnp.zeros((N_KEYS, LANES), F32)
            for b in range(PEER_TOPK):
                n = jnp.where(s0 + top1[b][h:h + 1] >= tau[h:h + 1], float(b + 1), n)
            n_ref[h, :, lanes] = n
            f0_ref[h, :, lanes] = jnp.exp(s0 - top0[0][h:h + 1]) * inv_z[h:h + 1]
            e1_ref[h, :, lanes] = jnp.exp(sc_ref[(2 * h + 1) * N_KEYS:(2 * h + 2) * N_KEYS, lanes] - top1[0][h:h + 1]).astype(BF16)
        return carry

    lax.fori_loop(0, tm // LANES, block, 0)


PEER_I_GROUP = 4


def _peer_body(hx_ref, wkh_ref, wkl_ref, u_ref, vt_ref, o_ref,
               sc_ref, top_ref, hxb_ref, hxl_ref, n_ref, f0_ref, r1_ref, e1_ref, w_ref, acc_ref):
    e = pl.program_id(1)
    tm = hx_ref.shape[0]
    nsb = tm // PEER_SB

    @pl.when(e == 0)
    def _():
        _peer_stats(hx_ref, wkh_ref, wkl_ref, sc_ref, top_ref, hxb_ref, hxl_ref, n_ref, f0_ref, r1_ref, e1_ref)
        acc_ref[...] = jnp.zeros(acc_ref.shape, F32)

    i_rows = pl.ds(pl.multiple_of(e * PEER_I_BLOCK, PEER_I_BLOCK), PEER_I_BLOCK)

    def activations(sb):
        act = _d(u_ref[...], hxb_ref[sb * PEER_SB:(sb + 1) * PEER_SB, :], NT)
        return (0.5 * act * (1.0 + lax.erf(act * (2.0 ** -0.5)))).astype(BF16)

    def gates(sb, act):
        for hb in range(PEER_SB // LANES):
            lanes = slice(sb * PEER_SB + hb * LANES, sb * PEER_SB + (hb + 1) * LANES)
            sub = slice(hb * LANES, (hb + 1) * LANES)
            n8 = [n_ref[h, i_rows, lanes].astype(BF16) for h in range(PEER_HEADS)]
            f8 = [f0_ref[h, i_rows, lanes].astype(BF16) for h in range(PEER_HEADS)]
            for ig in range(0, PEER_I_BLOCK, PEER_I_GROUP):
                g = [jnp.zeros((N_KEYS, LANES), BF16) for _ in range(PEER_I_GROUP)]
                for h in range(PEER_HEADS):
                    r1 = r1_ref[h, :, lanes]
                    e1 = e1_ref[h, :, lanes]
                    for k in range(PEER_I_GROUP):
                        ii = ig + k
                        g[k] = g[k] + jnp.where(r1 < n8[h][ii:ii + 1], e1 * f8[h][ii:ii + 1], jnp.zeros_like(e1))
                for k in range(PEER_I_GROUP):
                    rows = slice((ig + k) * N_KEYS, (ig + k + 1) * N_KEYS)
                    w_ref[sb, rows, sub] = g[k] * act[rows, sub]

    def accumulate(sb):
        cols = slice(sb * PEER_SB, (sb + 1) * PEER_SB)
        acc_ref[:, cols] += _d(vt_ref[...], w_ref[sb], NN)

    act = activations(0)
    for sb in range(nsb):
        nxt = activations(sb + 1) if sb + 1 < nsb else None
        gates(sb, act)
        accumulate(sb)
        act = nxt

    @pl.when(e == pl.num_programs(1) - 1)
    def _():
        o_ref[...] = acc_ref[...].T


def _fold_body(k_ref, w_ref, o_ref):
    o_ref[...] = _mdot(k_ref[...], w_ref[...], NN, 6)


def peer_fold_keys(wq, keys):
    d = wq.shape[0]
    nhp, nk, dk = keys.shape
    wqt = wq.T.reshape(nhp, dk, d)
    return pl.pallas_call(
        _fold_body,
        grid=(nhp,),
        in_specs=[pl.BlockSpec((None, nk, dk), lambda i: (i, 0, 0)),
                  pl.BlockSpec((None, dk, d), lambda i: (i, 0, 0))],
        out_specs=pl.BlockSpec((nk, d), lambda i: (i, 0)),
        out_shape=jax.ShapeDtypeStruct((nhp * nk, d), F32),
        compiler_params=pltpu.CompilerParams(dimension_semantics=("arbitrary",), vmem_limit_bytes=VMEM_LIMIT),
        name="peer_fold_keys",
    )(keys, wqt)


def peer(hx, wk, u_bf, vt_bf):
    t, d = hx.shape
    tm = PEER_TM
    assert t % tm == 0
    wk_hi, wk_lo = _split(wk, 2)
    ne = u_bf.shape[0] // PEER_TE
    h = PEER_HEADS
    return pl.pallas_call(
        _peer_body,
        grid=(t // tm, ne),
        in_specs=[pl.BlockSpec((tm, d), lambda i, e: (i, 0), pipeline_mode=pl.Buffered(1)),
                  pl.BlockSpec(wk.shape, lambda i, e: (0, 0), pipeline_mode=pl.Buffered(1)),
                  pl.BlockSpec(wk.shape, lambda i, e: (0, 0), pipeline_mode=pl.Buffered(1)),
                  pl.BlockSpec((PEER_TE, d), lambda i, e: (e, 0)),
                  pl.BlockSpec((None, d, PEER_TE), lambda i, e: (e, 0, 0))],
        out_specs=pl.BlockSpec((tm, d), lambda i, e: (i, 0)),
        out_shape=jax.ShapeDtypeStruct((t, d), F32),
        scratch_shapes=[pltpu.VMEM((2 * h * N_KEYS, tm), F32), pltpu.VMEM((2, PEER_TOPK, h, tm), F32),
                        pltpu.VMEM((tm, d), BF16), pltpu.VMEM((tm, d), BF16),
                        pltpu.VMEM((h, N_KEYS, tm), F32), pltpu.VMEM((h, N_KEYS, tm), F32),
                        pltpu.VMEM((h, N_KEYS, tm), BF16), pltpu.VMEM((h, N_KEYS, tm), BF16),
                        pltpu.VMEM((tm // PEER_SB, PEER_TE, PEER_SB), BF16), pltpu.VMEM((d, tm), F32)],
        compiler_params=pltpu.CompilerParams(
            dimension_semantics=("arbitrary", "arbitrary"), vmem_limit_bytes=VMEM_LIMIT),
        name="peer_dense",
    )(hx, wk_hi, wk_lo, u_bf, vt_bf)


LN_TM = 256


def _ln_mod_body(xs_ref, y_ref, mg_ref, mn_ref, lng_ref, lnb_ref, xo_ref, ho_ref, *, gate_col, mod_col, nct, nt):
    i = pl.program_id(0)
    d = xs_ref.shape[1]
    is_ctx = i < nct

    def pick(ref, col):
        return jnp.where(is_ctx, ref[1:2, col * d:(col + 1) * d], ref[0:1, col * d:(col + 1) * d])

    z = DN_ALPHA * xs_ref[...] + pick(mg_ref, gate_col) * y_ref[...]
    mu = jnp.mean(z, axis=-1, keepdims=True)
    zc = z - mu
    var = jnp.mean(zc * zc, axis=-1, keepdims=True)
    xn = zc * lax.rsqrt(var + LN_EPS) * lng_ref[...] + lnb_ref[...]
    xo_ref[...] = xn
    h = xn * (1.0 + pick(mn_ref, mod_col + 1)) + pick(mn_ref, mod_col)
    ho_ref[...] = jnp.where(i < nt, h, 0.0)


def ln_mod(xs, y, mod_gate, gate_col, mod_next, mod_col, ln_g, ln_b, n_ctx, pad_to):
    t, d = xs.shape
    tm = LN_TM
    assert t % tm == 0 and n_ctx % tm == 0 and pad_to % tm == 0 and pad_to >= t
    nt = t // tm
    row = lambda i: (jnp.minimum(i, nt - 1), 0)
    full = lambda a: pl.BlockSpec(a.shape, lambda i: (0, 0))
    return pl.pallas_call(
        functools.partial(_ln_mod_body, gate_col=gate_col, mod_col=mod_col, nct=n_ctx // tm, nt=nt),
        grid=(pad_to // tm,),
        in_specs=[pl.BlockSpec((tm, d), row), pl.BlockSpec((tm, d), row), full(mod_gate), full(mod_next),
                  pl.BlockSpec((1, d), lambda i: (0, 0)), pl.BlockSpec((1, d), lambda i: (0, 0))],
        out_specs=[pl.BlockSpec((tm, d), row), pl.BlockSpec((tm, d), lambda i: (i, 0))],
        out_shape=[jax.ShapeDtypeStruct((t, d), F32), jax.ShapeDtypeStruct((pad_to, d), F32)],
        compiler_params=pltpu.CompilerParams(dimension_semantics=("arbitrary",), vmem_limit_bytes=VMEM_LIMIT),
        name="ln_mod",
    )(xs, y, mod_gate, mod_next, ln_g.reshape(1, d), ln_b.reshape(1, d))


def _ln(x, g, b):
    mu = jnp.mean(x, -1, keepdims=True)
    xc = x - mu
    var = jnp.mean(xc * xc, -1, keepdims=True)
    return xc * lax.rsqrt(var + LN_EPS) * g + b


def _head_norm(h, nheads, eps):
    t = h.shape[0]
    hh = h.reshape(t, nheads, -1)
    mu = jnp.mean(hh, -1, keepdims=True)
    xc = hh - mu
    var = jnp.mean(xc * xc, -1, keepdims=True)
    return (xc * lax.rsqrt(var + eps)).reshape(t, -1)


def _pad_cols(w, n):
    return jnp.pad(w, ((0, 0), (0, n - w.shape[1])))


def _shift_rows(u, k):
    if k > 0:
        return jnp.concatenate([u[k:], jnp.zeros_like(u[:k])], axis=0)
    return jnp.concatenate([jnp.zeros_like(u[:-k]), u[:k]], axis=0)


def _grid_conv(u, w, b):
    s = u.shape[0]
    colid = (jnp.arange(s) % GRID_W)[:, None]
    out = jnp.zeros_like(u) + b
    for di in range(3):
        for dj in range(3):
            off = (di - 1) * GRID_W + (dj - 1)
            sh = _shift_rows(u, off) if off else u
            if dj == 0:
                sh = jnp.where(colid == 0, 0.0, sh)
            elif dj == 2:
                sh = jnp.where(colid == GRID_W - 1, 0.0, sh)
            out = out + sh * w[di, dj]
    return out


def _seq_conv(u, w, b):
    wc = w[1]
    return _shift_rows(u, -1) * wc[0] + u * wc[1] + _shift_rows(u, 1) * wc[2] + b


def _qshift(u):
    q = u.shape[1] // 4
    colid = (jnp.arange(u.shape[0]) % GRID_W)[:, None]
    left = jnp.where(colid == 0, 0.0, _shift_rows(u[:, :q], -1))
    right = jnp.where(colid == GRID_W - 1, 0.0, _shift_rows(u[:, q:2 * q], 1))
    up = _shift_rows(u[:, 2 * q:3 * q], -GRID_W)
    down = _shift_rows(u[:, 3 * q:], GRID_W)
    return jnp.concatenate([left, right, up, down], axis=1)


def _shift_seq(u):
    h = u.shape[1] // 2
    return jnp.concatenate([_shift_rows(u[:, :h], -1), _shift_rows(u[:, h:], 1)], axis=1)


def _mlstm_layer(h, n_ctx, w_in, b_in, conv_w, conv_b, hn_g, w_out):
    d = D_MODEL
    t = h.shape[0]
    npad = 4 * d + LANES
    p = matmul(h, _pad_cols(w_in, npad), jnp.pad(b_in, (0, npad - b_in.shape[0])))
    qk_pre = p[:, :2 * d]
    qk = jax.nn.silu(jnp.concatenate([_seq_conv(qk_pre[:n_ctx], conv_w, conv_b),
                                      _grid_conv(qk_pre[n_ctx:], conv_w, conv_b)], axis=0))
    q = qk[:, :d] * (MLSTM_DH ** -0.5)
    k = qk[:, d:]
    v = p[:, 2 * d:3 * d]
    o = jax.nn.sigmoid(p[:, 3 * d:4 * d])
    g = p[:, 4 * d:4 * d + 4 * MLSTM_HEADS].reshape(t, 4, MLSTM_HEADS)
    g = jnp.concatenate([g[:, :2], jax.nn.log_sigmoid(g[:, 2:])], axis=1)
    gh = jnp.transpose(g, (2, 0, 1))
    gcol = jnp.pad(gh, ((0, 0), (0, 0), (0, LANES - 4)))
    grow = jnp.pad(jnp.transpose(gh, (0, 2, 1)), ((0, 0), (0, 4), (0, 0)))
    hf, hb = mlstm_scan(q, k, v, gcol, grow, n_ctx)
    hn = _head_norm(hf + hb, MLSTM_HEADS, LN_EPS)
    return matmul(o * hn * hn_g, w_out)


def _rwkv_layer(h, n_ctx, mu, w_rkv, w0, w1, w2, a0, a1, a2, g1, g2, k_k, k_a, r_k, lnx_g, lnx_b, w_out):
    d = D_MODEL
    t = h.shape[0]
    shifted = jnp.concatenate([_shift_seq(h[:n_ctx]), _qshift(h[n_ctx:])], axis=0)
    dx = shifted - h
    xm = [h + dx * mu[n] for n in range(6)]
    r = matmul(xm[0], w_rkv[0])
    k = matmul(xm[1], w_rkv[1])
    v = matmul(xm[2], w_rkv[2])
    lora = DECAY_LORA = w1.shape[-1]
    w1c = _pad_cols(jnp.concatenate([w1[0], w1[1]], axis=1), LANES)
    zpad = jnp.zeros((lora, d), F32)
    w2c = jnp.concatenate([jnp.concatenate([w2[0], zpad], axis=1),
                           jnp.concatenate([zpad, w2[1]], axis=1)], axis=0)
    wpre = matmul(jnp.tanh(matmul(xm[3], w1c)), w2c) + jnp.concatenate([w0[0], w0[1]])
    lw = -jnp.exp(-jax.nn.softplus(-wpre) - 0.5)
    a1c = _pad_cols(jnp.concatenate([a1[0], a1[1]], axis=1), LANES)
    a2c = jnp.concatenate([jnp.concatenate([a2[0], zpad], axis=1),
                           jnp.concatenate([zpad, a2[1]], axis=1)], axis=0)
    a = jax.nn.sigmoid(matmul(matmul(xm[4], a1c), a2c) + jnp.concatenate([a0[0], a0[1]]))
    glora = g1.shape[1]
    gpad = 2 * LANES
    gg = jax.nn.sigmoid(matmul(xm[5], _pad_cols(g1, gpad)))
    g = matmul(gg, jnp.pad(g2, ((0, gpad - glora), (0, 0))))
    kk = (k * k_k).reshape(t, RWKV_HEADS, RWKV_N)
    kk = (kk / jnp.maximum(jnp.sqrt(jnp.sum(kk * kk, -1, keepdims=True)), 1e-12)).reshape(t, d)
    a_f, a_b = a[:, :d], a[:, d:]
    kt_f = k * (1.0 + (a_f - 1.0) * k_a)
    kt_b = k * (1.0 + (a_b - 1.0) * k_a)
    yf, yb = rwkv_scan(r, v, kk, lw[:, :d], kt_f, kk * a_f, lw[:, d:], kt_b, kk * a_b, n_ctx)
    yn = _head_norm(yf + yb, RWKV_HEADS, RWKV_GN_EPS) * lnx_g + lnx_b
    kbar = 0.5 * (kt_f + kt_b)
    bonus = jnp.sum((r * kbar * r_k).reshape(t, RWKV_HEADS, RWKV_N), -1, keepdims=True) \
        * v.reshape(t, RWKV_HEADS, RWKV_N)
    return matmul((yn + bonus.reshape(t, d)) * g, w_out)


def _forward(x, c, ctx, c_ctx, ada_w, ada_b, ln_g, ln_b,
             ml_w_in, ml_b_in, ml_conv_w, ml_conv_b, ml_hn_g, ml_w_out,
             rw_mu, rw_w_rkv, rw_w0, rw_w1, rw_w2, rw_a0, rw_a1, rw_a2, rw_g1, rw_g2,
             rw_k_k, rw_k_a, rw_r_k, rw_lnx_g, rw_lnx_b, rw_w_out,
             pk_wq, pk_keys, pk_u, pk_v):
    d = D_MODEL
    n_ctx = ctx.shape[1]
    n_lat = x.shape[1]
    xs = jnp.concatenate([ctx[0], x[0]], axis=0)
    t = xs.shape[0]
    t_pad = -(-t // PEER_TM) * PEER_TM
    s_in = jnp.zeros((8, d), F32).at[0].set(jax.nn.silu(c[0])).at[1].set(jax.nn.silu(c_ctx))
    depth = ada_w.shape[0]
    mods = [matmul(s_in, ada_w[i], ada_b[i], passes=3) for i in range(depth)]
    is_ctx = (jnp.arange(t) < n_ctx)[:, None]
    m0 = [jnp.where(is_ctx, mods[0][1, n * d:(n + 1) * d], mods[0][0, n * d:(n + 1) * d]) for n in range(2)]
    h = xs * (1.0 + m0[1]) + m0[0]
    for i in range(depth):
        j = i // 2
        if i % 2 == 0:
            y = _mlstm_layer(h, n_ctx, ml_w_in[j], ml_b_in[j], ml_conv_w[j], ml_conv_b[j],
                             ml_hn_g[j], ml_w_out[j])
        else:
            y = _rwkv_layer(h, n_ctx, rw_mu[j], rw_w_rkv[j], rw_w0[j], rw_w1[j], rw_w2[j],
                            rw_a0[j], rw_a1[j], rw_a2[j], rw_g1[j], rw_g2[j], rw_k_k[j],
                            rw_k_a[j], rw_r_k[j], rw_lnx_g[j], rw_lnx_b[j], rw_w_out[j])
        xs, h = ln_mod(xs, y, mods[i], 2, mods[i], 3, ln_g[i, 0], ln_b[i, 0], n_ctx, t_pad)
        wk = peer_fold_keys(pk_wq[i], pk_keys[i].reshape(2 * PEER_HEADS, N_KEYS, PEER_DQ // 2))
        vt = jnp.swapaxes(pk_v[i].astype(BF16).reshape(-1, PEER_TE, d), 1, 2)
        y = peer(h, wk, pk_u[i].astype(BF16), vt)
        xs, h = ln_mod(xs, y, mods[i], 5, mods[min(i + 1, depth - 1)], 0, ln_g[i, 1], ln_b[i, 1], n_ctx, t)
    return xs[n_ctx:][None]


def kernel(x, c, ctx, c_ctx, ada_w, ada_b, ln_g, ln_b, ml_w_in, ml_b_in, ml_conv_w, ml_conv_b, ml_hn_g, ml_w_out, rw_mu, rw_w_rkv, rw_w0, rw_w1, rw_w2, rw_a0, rw_a1, rw_a2, rw_g1, rw_g2, rw_k_k, rw_k_a, rw_r_k, rw_lnx_g, rw_lnx_b, rw_w_out, pk_wq, pk_keys, pk_u, pk_v):
    return _forward(x, c, ctx, c_ctx, ada_w, ada_b, ln_g, ln_b,
                    ml_w_in, ml_b_in, ml_conv_w, ml_conv_b, ml_hn_g, ml_w_out,
                    rw_mu, rw_w_rkv, rw_w0, rw_w1, rw_w2, rw_a0, rw_a1, rw_a2, rw_g1, rw_g2,
                    rw_k_k, rw_k_a, rw_r_k, rw_lnx_g, rw_lnx_b, rw_w_out,
                    pk_wq, pk_keys, pk_u, pk_v)
```

```python
import functools

import jax
import jax.numpy as jnp
from jax import lax
from jax.experimental import pallas as pl
from jax.experimental.pallas import tpu as pltpu

F32 = jnp.float32
BF16 = jnp.bfloat16

D_MODEL = 1024
DEPTH = 4
GRID_W = 64
N_MOD = 6
DN_ALPHA = (2.0 * DEPTH) ** 0.25
LN_EPS = 1e-5

MLSTM_HEADS = 4
MLSTM_DH = D_MODEL // MLSTM_HEADS
MLSTM_CHUNK = 128
M_INIT = -1e30

RWKV_N = 64
RWKV_HEADS = D_MODEL // RWKV_N
RWKV_CHUNK = 64
RWKV_GN_EPS = 64e-5

N_KEYS = 128
PEER_HEADS = 8
PEER_DQ = 256
PEER_TOPK = 16
PEER_I_BLOCK = 8
PEER_TE = PEER_I_BLOCK * N_KEYS
PEER_TM = 512
PEER_SB = 256

LANES = 128
VMEM_LIMIT = 62 * 1024 * 1024

NN = ((1,), (0,))
NT = ((1,), (1,))
TN = ((0,), (0,))


def _split(x, n):
    parts = []
    r = x.astype(F32)
    for i in range(n):
        p = r.astype(BF16)
        parts.append(p)
        if i + 1 < n:
            r = r - p.astype(F32)
    return parts


def _d(a, b, dims):
    return lax.dot_general(a, b, (dims, ((), ())), preferred_element_type=F32)


def _mdot(a, b, dims, passes):
    if passes == 1:
        return _d(a.astype(BF16), b.astype(BF16), dims)
    if passes == 3:
        a0, a1 = _split(a, 2)
        b0, b1 = _split(b, 2)
        return (_d(a0, b1, dims) + _d(a1, b0, dims)) + _d(a0, b0, dims)
    a0, a1, a2 = _split(a, 3)
    b0, b1, b2 = _split(b, 3)
    lo = (_d(a0, b2, dims) + _d(a2, b0, dims)) + _d(a1, b1, dims)
    mid = _d(a0, b1, dims) + _d(a1, b0, dims)
    return (lo + mid) + _d(a0, b0, dims)


def _dot_01_lhs(m01, x):
    mb = m01.astype(BF16)
    x0, x1, x2 = _split(x, 3)
    return (_d(mb, x2, NN) + _d(mb, x1, NN)) + _d(mb, x0, NN)


def _dot_01_rhs(x, m01):
    mb = m01.astype(BF16)
    x0, x1, x2 = _split(x, 3)
    return (_d(x2, mb, NN) + _d(x1, mb, NN)) + _d(x0, mb, NN)


def _pick(n, cands):
    for c in cands:
        if n % c == 0:
            return c
    raise ValueError(f"no tile for {n}")


def _mm_body(x_ref, w_ref, b_ref, o_ref, *, passes):
    o_ref[...] = _mdot(x_ref[...], w_ref[...], NN, passes) + b_ref[...]


def matmul(x, w, b=None, *, passes=1):
    m, k = x.shape
    n = w.shape[1]
    assert n % LANES == 0 and w.shape[0] == k
    tm = m if m <= 1024 else _pick(m, (640, 512, 384, 256, 128))
    tn = _pick(n, (1024, 768, 640, 512, 384, 256, 128))
    if b is None:
        b = jnp.zeros((n,), F32)
    if passes == 1:
        w = w.astype(BF16)
    return pl.pallas_call(
        functools.partial(_mm_body, passes=passes),
        grid=(m // tm, n // tn),
        in_specs=[pl.BlockSpec((tm, k), lambda i, j: (i, 0)),
                  pl.BlockSpec((k, tn), lambda i, j: (0, j)),
                  pl.BlockSpec((1, tn), lambda i, j: (0, j))],
        out_specs=pl.BlockSpec((tm, tn), lambda i, j: (i, j)),
        out_shape=jax.ShapeDtypeStruct((m, n), F32),
        compiler_params=pltpu.CompilerParams(
            dimension_semantics=("arbitrary", "arbitrary"), vmem_limit_bytes=VMEM_LIMIT),
        name="proj_matmul",
    )(x, w, b.reshape(1, n).astype(F32))


def _mlstm_body(qf, kf, vf, gcf, grf, qb, kb, vb, gcb, grb, hf_ref, hb_ref, c_ref, n_ref, m_ref):
    @pl.when(pl.program_id(1) == 0)
    def _():
        c_ref[...] = jnp.zeros(c_ref.shape, F32)
        n_ref[...] = jnp.zeros(n_ref.shape, F32)
        m_ref[...] = jnp.full(m_ref.shape, M_INIT, F32)

    ds = (0, 1)
    each = lambda f, *ls: [f(*xs) for xs in zip(*ls)]
    q, k, v = [qf[...], qb[...]], [kf[...], kb[...]], [vf[...], vb[...]]
    gc, gr = [gcf[...], gcb[...]], [grf[...], grb[...]]
    L = q[0].shape[0]
    row = lax.broadcasted_iota(jnp.int32, (L, L), 0)
    col = lax.broadcasted_iota(jnp.int32, (L, L), 1)
    seen = (col <= row, col >= row)
    tri = [jnp.where(m, 1.0, 0.0) for m in seen]
    b_col = [_dot_01_lhs(tri[d], gc[d])[:, 2 + d:3 + d] for d in ds]
    b_row = [_dot_01_rhs(gr[d], tri[1 - d])[2 + d:3 + d, :] for d in ds]
    ig_col = [gc[d][:, d:d + 1] for d in ds]
    ig_row = [gr[d][d:d + 1, :] for d in ds]
    m_st = [m_ref[d, 0:1, 0:1] for d in ds]
    c_st = [c_ref[d] for d in ds]
    n_st = [n_ref[d] for d in ds]

    dlog = [jnp.where(seen[d], b_col[d] - b_row[d] + ig_row[d], -jnp.inf) for d in ds]
    m_inter = each(jnp.add, b_col, m_st)
    m_t = each(lambda mi, dl: jnp.maximum(mi, jnp.max(dl, axis=1, keepdims=True)), m_inter, dlog)
    qb16, kb16, vb16 = (each(lambda x: x.astype(BF16), a) for a in (q, k, v))
    qk = each(lambda a, b: _d(a, b, NT), qb16, kb16)
    s = each(lambda x, dl, mt: x * jnp.exp(dl - mt), qk, dlog, m_t)
    dec = each(lambda mi, mt: jnp.exp(mi - mt), m_inter, m_t)
    sv = each(lambda a, b: _d(a.astype(BF16), b, NN), s, vb16)
    qc = each(lambda a, b: _d(a, b.astype(BF16), NN), qb16, c_st)
    num = each(lambda a, dc, b: a + dc * b, sv, dec, qc)
    den = each(lambda x, dc, qq, nn: jnp.sum(x, axis=1, keepdims=True) + dc * jnp.sum(qq * nn, axis=1, keepdims=True),
               s, dec, q, n_st)
    h = each(lambda nu, de, mt: nu / jnp.maximum(jnp.abs(de), jnp.exp(-mt)), num, den, m_t)
    hf_ref[...] = h[0]
    hb_ref[...] = h[1]

    b_last = [b_col[0][L - 1:L, :], b_col[1][0:1, :]]
    w_c = each(lambda bl, bc, ic: bl - bc + ic, b_last, b_col, ig_col)
    m_new = each(lambda bl, ms, w: jnp.maximum(bl + ms, jnp.max(w, axis=0, keepdims=True)), b_last, m_st, w_c)
    a_c = each(lambda w, mn: jnp.exp(w - mn), w_c, m_new)
    g_prev = each(lambda bl, ms, mn: jnp.exp(bl + ms - mn), b_last, m_st, m_new)
    ak = each(jnp.multiply, a_c, k)
    kv = each(lambda a, b: _d(a.T.astype(BF16), b, NN), ak, vb16)
    for d in ds:
        c_ref[d] = g_prev[d] * c_st[d] + kv[d]
        n_ref[d] = g_prev[d] * n_st[d] + jnp.sum(ak[d], axis=0, keepdims=True)
        m_ref[d] = jnp.broadcast_to(m_new[d], m_ref.shape[1:])


def _bwd_chunk(c, nc0, nc):
    return jnp.where(c < nc0, nc0 - 1 - c, nc - 1 - (c - nc0))


def mlstm_scan(q, k, v, gcol, grow, n_ctx):
    t = q.shape[0]
    L, dh = MLSTM_CHUNK, MLSTM_DH
    nc, nc0 = t // L, n_ctx // L
    fwd = lambda h, c: (c, h)
    bwd = lambda h, c: (_bwd_chunk(c, nc0, nc), h)
    qkv = lambda im: pl.BlockSpec((L, dh), im)
    gc_spec = lambda f: pl.BlockSpec((None, L, LANES), lambda h, c: (h, f(c), 0))
    gr_spec = lambda f: pl.BlockSpec((None, 8, L), lambda h, c: (h, 0, f(c)))
    idf = lambda c: c
    idb = lambda c: _bwd_chunk(c, nc0, nc)
    return pl.pallas_call(
        _mlstm_body,
        grid=(MLSTM_HEADS, nc),
        in_specs=[qkv(fwd), qkv(fwd), qkv(fwd), gc_spec(idf), gr_spec(idf),
                  qkv(bwd), qkv(bwd), qkv(bwd), gc_spec(idb), gr_spec(idb)],
        out_specs=[qkv(fwd), qkv(bwd)],
        out_shape=[jax.ShapeDtypeStruct((t, D_MODEL), F32)] * 2,
        scratch_shapes=[pltpu.VMEM((2, dh, dh), F32), pltpu.VMEM((2, 1, dh), F32),
                        pltpu.VMEM((2, 8, LANES), F32)],
        compiler_params=pltpu.CompilerParams(
            dimension_semantics=("arbitrary", "arbitrary"), vmem_limit_bytes=VMEM_LIMIT),
        name="mlstm_scan",
    )(q, k, v, gcol, grow, q, k, v, gcol, grow)


RW_PASSES = 1
RW_CHUNKS_PER_STEP = 4


def _stack2(x, lane_head):
    return jnp.concatenate([jnp.where(lane_head == 0, x, 0.0), jnp.where(lane_head == 1, x, 0.0)], axis=0)


def _unstack2(x):
    L = x.shape[0] // 2
    return x[:L] + x[L:]


def _rwkv_chunks(chains):
    L = chains[0][0].shape[0]
    n2 = 2 * L
    p = RW_PASSES
    ds = [c[6] for c in chains]
    each = lambda f, *ls: [f(*xs) for xs in zip(*ls)]
    dot = lambda dims: (lambda a, b: _mdot(a, b, dims, p))
    lw, r, k, v, kap, alp = ([c[i] for c in chains] for i in range(6))

    row = lax.broadcasted_iota(jnp.int32, (L, L), 0)
    col = lax.broadcasted_iota(jnp.int32, (L, L), 1)
    tris = (jnp.where(col <= row, 1.0, 0.0), jnp.where(col >= row, 1.0, 0.0))
    lp = [_dot_01_lhs(tris[d], x) for d, x in zip(ds, lw)]
    lp_end = [x[0:1, :] if d else x[L - 1:L, :] for d, x in zip(ds, lp)]
    e_neg = each(lambda x: jnp.exp(-x), lp)
    e_end = each(lambda x, xe: jnp.exp(xe - x), lp, lp_end)
    kap_t = each(lambda x, y, z: x * jnp.exp(y - z), kap, lp, lw)
    r_t = each(lambda x, y: x * jnp.exp(y), r, lp)
    k_h = each(jnp.multiply, k, e_neg)
    a_h = each(jnp.multiply, alp, e_neg)
    k_e = each(jnp.multiply, k, e_end)
    a_e = each(jnp.multiply, alp, e_end)

    lane_head = lax.broadcasted_iota(jnp.int32, (L, LANES), 1) // RWKV_N
    st = lambda x: _stack2(x, lane_head)
    kap_s, r_s, v_s, k_s, a_s = (each(st, x) for x in (kap_t, r_t, v, k_e, a_e))
    rhs_k = each(lambda x: jnp.concatenate([x, x], axis=0), k_h)
    rhs_a = each(lambda x: jnp.concatenate([x, x], axis=0), a_h)

    row2 = lax.broadcasted_iota(jnp.int32, (n2, n2), 0)
    col2 = lax.broadcasted_iota(jnp.int32, (n2, n2), 1)
    same_head = (row2 // L) == (col2 // L)
    strict = (same_head & (col2 < row2), same_head & (col2 > row2))
    incl = (same_head & (col2 <= row2), same_head & (col2 >= row2))
    zero = jnp.zeros((n2, n2), F32)
    masked = lambda masks: (lambda d, x: jnp.where(masks[d], x, zero))
    n_ka = each(masked(strict), ds, each(dot(NT), kap_s, rhs_a))
    m_kk = each(masked(strict), ds, each(dot(NT), kap_s, rhs_k))
    a_rk = each(masked(incl), ds, each(dot(NT), r_s, rhs_k))
    a_ra = each(masked(incl), ds, each(dot(NT), r_s, rhs_a))

    b16 = (row2 // 16) == (col2 // 16)
    b32 = (row2 // 32) == (col2 // 32)
    eye = jnp.where(row2 == col2, 1.0, 0.0)
    n16 = each(lambda x: jnp.where(b16, x, zero), n_ka)
    n_2 = each(dot(NN), n16, n16)
    n_4 = each(dot(NN), n_2, n_2)
    n_8 = each(dot(NN), n_4, n_4)
    inv = each(lambda x: eye - x, n16)
    for pw in (n_2, n_4, n_8):
        inv = each(jnp.add, inv, each(dot(NN), inv, pw))
    for sel in (lambda x: jnp.where(b32 & ~b16, x, zero), lambda x: jnp.where(b32, zero, x)):
        t1 = each(dot(NN), inv, each(sel, n_ka))
        inv = each(jnp.subtract, inv, each(dot(NN), t1, inv))

    mv = each(dot(NN), m_kk, v_s)
    w1u0 = each(dot(NN), inv, each(lambda x, y: jnp.concatenate([x, y], axis=1), kap_s, mv))
    ar = each(dot(NN), a_ra, w1u0)
    av = each(dot(NN), a_rk, v_s)
    r2 = each(lambda x, y: _unstack2(x - y[:, :LANES]), r_s, ar)
    y0 = each(lambda x, y: _unstack2(x - y[:, LANES:]), av, ar)
    w1_t = each(lambda x: x[:, :LANES].T, w1u0)
    u0_t = each(lambda x: x[:, LANES:].T, w1u0)
    v_t = each(lambda x: x.T, v_s)
    wa = each(dot(NN), w1_t, a_s)
    vk = each(dot(NN), v_t, k_s)
    ua = each(dot(NN), u0_t, a_s)
    g = each(lambda xe, x: jnp.where(row2 == col2, jnp.exp(xe), zero) - x, lp_end, wa)
    b = each(jnp.subtract, vk, ua)
    return list(zip(y0, r2, g, b))


def _rwkv_body(lwf, rf, kf, vf, kkf, kaf, lwb, rb, kb, vb, kkb, kab, yf_ref, yb_ref, s_ref):
    @pl.when(pl.program_id(1) == 0)
    def _():
        s_ref[...] = jnp.zeros(s_ref.shape, F32)

    L = RWKV_CHUNK
    n = RW_CHUNKS_PER_STEP
    rows = lambda j: slice(j * L, (j + 1) * L)
    refs = ((lwf, rf, kf, vf, kkf, kaf), (lwb, rb, kb, vb, kkb, kab))
    visit = [(d, j if d == 0 else n - 1 - j) for j in range(n) for d in (0, 1)]
    pre = _rwkv_chunks([tuple(ref[rows(j), :] for ref in refs[d]) + (d,) for d, j in visit])
    y_refs = (yf_ref, yb_ref)
    s = [s_ref[0], s_ref[1]]
    for (d, j), (y0, r2, g, b) in zip(visit, pre):
        y_refs[d][rows(j), :] = y0 + _mdot(r2, s[d], NT, RW_PASSES)
        s[d] = _mdot(s[d], g, NN, RW_PASSES) + b
    s_ref[0] = s[0]
    s_ref[1] = s[1]


def rwkv_scan(r, v, kk, lw_f, kt_f, ka_f, lw_b, kt_b, ka_b, n_ctx):
    t = r.shape[0]
    L = RWKV_CHUNK
    blk = RW_CHUNKS_PER_STEP * L
    assert 2 * L == LANES and t % blk == 0 and n_ctx % blk == 0
    nc, nc0 = t // blk, n_ctx // blk
    fwd = pl.BlockSpec((blk, LANES), lambda h, c: (c, h))
    bwd = pl.BlockSpec((blk, LANES), lambda h, c: (_bwd_chunk(c, nc0, nc), h))
    return pl.pallas_call(
        _rwkv_body,
        grid=(RWKV_HEADS // 2, nc),
        in_specs=[fwd] * 6 + [bwd] * 6,
        out_specs=[fwd, bwd],
        out_shape=[jax.ShapeDtypeStruct((t, D_MODEL), F32)] * 2,
        scratch_shapes=[pltpu.VMEM((2, LANES, LANES), F32)],
        compiler_params=pltpu.CompilerParams(
            dimension_semantics=("arbitrary", "arbitrary"), vmem_limit_bytes=VMEM_LIMIT),
        name="rwkv7_scan",
    )(lw_f, r, kt_f, v, kk, ka_f, lw_b, r, kt_b, v, kk, ka_b)


def _peer_stats(hx_ref, wkh_ref, wkl_ref, sc_ref, top_ref, hxb_ref, hxl_ref, n_ref, f0_ref, r1_ref, e1_ref):
    tm = hx_ref.shape[0]
    hx_hi, hx_lo = _split(hx_ref[...], 2)
    hxb_ref[...] = hx_hi
    hxl_ref[...] = hx_lo
    rows_per = 2 * LANES

    def scores(c, carry):
        rows = pl.ds(pl.multiple_of(c * rows_per, rows_per), rows_per)
        wk_hi = wkh_ref[rows, :]
        sc_ref[rows, :] = (_d(wk_hi, hxl_ref[...], NT) + _d(wkl_ref[rows, :], hxb_ref[...], NT)) \
            + _d(wk_hi, hxb_ref[...], NT)
        return carry

    lax.fori_loop(0, wkh_ref.shape[0] // rows_per, scores, 0)

    def block(tb, carry):
        lanes = pl.ds(pl.multiple_of(tb * LANES, LANES), LANES)
        neg = jnp.full((N_KEYS, LANES), -jnp.inf, F32)
        for h in range(PEER_HEADS):
            for p in range(2):
                cur = sc_ref[(2 * h + p) * N_KEYS:(2 * h + p + 1) * N_KEYS, lanes]
                rank = jnp.full((N_KEYS, LANES), float(PEER_TOPK), F32)
                for a in range(PEER_TOPK):
                    m = jnp.max(cur, axis=0, keepdims=True)
                    top_ref[p, a, h:h + 1, lanes] = m
                    hit = cur >= m
                    cur = jnp.where(hit, neg, cur)
                    if p == 1:
                        rank = jnp.where(hit, float(a), rank)
                if p == 1:
                    r1_ref[h, :, lanes] = rank.astype(BF16)
        top0 = [top_ref[0, a, :, lanes] for a in range(PEER_TOPK)]
        top1 = [top_ref[1, b, :, lanes] for b in range(PEER_TOPK)]
        cands = [top0[a] + top1[b]
                 for a in range(PEER_TOPK) for b in range(PEER_TOPK) if (a + 1) * (b + 1) <= PEER_TOPK]
        c_max = cands[0]
        z = jnp.zeros_like(c_max)
        tau = c_max
        for a in range(PEER_TOPK):
            tau = functools.reduce(jnp.maximum, cands)
            z = z + jnp.exp(tau - c_max)
            cands = [jnp.where(cd >= tau, -jnp.inf, cd) for cd in cands]
        inv_z = 1.0 / z
        for h in range(PEER_HEADS):
            s0 = sc_ref[2 * h * N_KEYS:(2 * h + 1) * N_KEYS, lanes]
            n = jnp.zeros((N_KEYS, LANES), F32)
            for b in range(PEER_TOPK):
                n = jnp.where(s0 + top1[b][h:h + 1] >= tau[h:h + 1], float(b + 1), n)
            n_ref[h, :, lanes] = n
            f0_ref[h, :, lanes] = jnp.exp(s0 - top0[0][h:h + 1]) * inv_z[h:h + 1]
            e1_ref[h, :, lanes] = jnp.exp(sc_ref[(2 * h + 1) * N_KEYS:(2 * h + 2) * N_KEYS, lanes] - top1[0][h:h + 1]).astype(BF16)
        return carry

    lax.fori_loop(0, tm // LANES, block, 0)


PEER_I_GROUP = 4
PEER_STREAMS = 4


def _peer_body(hx_ref, wkh_ref, wkl_ref, *rest):
    u_refs, vt_refs = rest[:PEER_STREAMS], rest[PEER_STREAMS:2 * PEER_STREAMS]
    o_ref, sc_ref, top_ref, hxb_ref, hxl_ref, n_ref, f0_ref, r1_ref, e1_ref, w_ref, acc_ref = rest[2 * PEER_STREAMS:]
    e = pl.program_id(1)
    tm = hx_ref.shape[0]
    nsb = tm // PEER_SB

    @pl.when(e == 0)
    def _():
        _peer_stats(hx_ref, wkh_ref, wkl_ref, sc_ref, top_ref, hxb_ref, hxl_ref, n_ref, f0_ref, r1_ref, e1_ref)
        acc_ref[...] = jnp.zeros(acc_ref.shape, F32)

    i_rows = pl.ds(pl.multiple_of(e * PEER_I_BLOCK, PEER_I_BLOCK), PEER_I_BLOCK)

    def activations(sb):
        parts = []
        for u_ref in u_refs:
            act = _d(u_ref[...], hxb_ref[sb * PEER_SB:(sb + 1) * PEER_SB, :], NT)
            parts.append((0.5 * act * (1.0 + lax.erf(act * (2.0 ** -0.5)))).astype(BF16))
        return parts

    def gates(sb, act):
        for hb in range(PEER_SB // LANES):
            lanes = slice(sb * PEER_SB + hb * LANES, sb * PEER_SB + (hb + 1) * LANES)
            sub = slice(hb * LANES, (hb + 1) * LANES)
            n8 = [n_ref[h, i_rows, lanes].astype(BF16) for h in range(PEER_HEADS)]
            f8 = [f0_ref[h, i_rows, lanes].astype(BF16) for h in range(PEER_HEADS)]
            for ig in range(0, PEER_I_BLOCK, PEER_I_GROUP):
                g = [jnp.zeros((N_KEYS, LANES), BF16) for _ in range(PEER_I_GROUP)]
                for h in range(PEER_HEADS):
                    r1 = r1_ref[h, :, lanes]
                    e1 = e1_ref[h, :, lanes]
                    for k in range(PEER_I_GROUP):
                        ii = ig + k
                        g[k] = g[k] + jnp.where(r1 < n8[h][ii:ii + 1], e1 * f8[h][ii:ii + 1], jnp.zeros_like(e1))
                for k in range(PEER_I_GROUP):
                    rows = slice((ig + k) * N_KEYS, (ig + k + 1) * N_KEYS)
                    part, off = divmod((ig + k) * N_KEYS, PEER_TE // PEER_STREAMS)
                    w_ref[sb, rows, sub] = g[k] * act[part][off:off + N_KEYS, sub]

    def accumulate(sb):
        cols = slice(sb * PEER_SB, (sb + 1) * PEER_SB)
        dr = acc_ref.shape[0] // PEER_STREAMS
        for k, vt_ref in enumerate(vt_refs):
            acc_ref[k * dr:(k + 1) * dr, cols] += _d(vt_ref[...], w_ref[sb], NN)

    act = activations(0)
    for sb in range(nsb):
        nxt = activations(sb + 1) if sb + 1 < nsb else None
        gates(sb, act)
        accumulate(sb)
        act = nxt

    @pl.when(e == pl.num_programs(1) - 1)
    def _():
        o_ref[...] = acc_ref[...].T


def _fold_body(k_ref, w_ref, o_ref):
    o_ref[...] = _mdot(k_ref[...], w_ref[...], NN, 6)


def peer_fold_keys(wq, keys):
    d = wq.shape[0]
    nhp, nk, dk = keys.shape
    wqt = wq.T.reshape(nhp, dk, d)
    return pl.pallas_call(
        _fold_body,
        grid=(nhp,),
        in_specs=[pl.BlockSpec((None, nk, dk), lambda i: (i, 0, 0)),
                  pl.BlockSpec((None, dk, d), lambda i: (i, 0, 0))],
        out_specs=pl.BlockSpec((nk, d), lambda i: (i, 0)),
        out_shape=jax.ShapeDtypeStruct((nhp * nk, d), F32),
        compiler_params=pltpu.CompilerParams(dimension_semantics=("arbitrary",), vmem_limit_bytes=VMEM_LIMIT),
        name="peer_fold_keys",
    )(keys, wqt)


def peer(hx, wk, u_bf, vt_bf):
    t, d = hx.shape
    tm = PEER_TM
    assert t % tm == 0
    wk_hi, wk_lo = _split(wk, 2)
    ne = u_bf.shape[0] // PEER_TE
    h = PEER_HEADS
    ns = PEER_STREAMS
    return pl.pallas_call(
        _peer_body,
        grid=(t // tm, ne),
        in_specs=[pl.BlockSpec((tm, d), lambda i, e: (i, 0), pipeline_mode=pl.Buffered(1)),
                  pl.BlockSpec(wk.shape, lambda i, e: (0, 0), pipeline_mode=pl.Buffered(1)),
                  pl.BlockSpec(wk.shape, lambda i, e: (0, 0), pipeline_mode=pl.Buffered(1)),
                  *[pl.BlockSpec((PEER_TE // ns, d), functools.partial(lambda i, e, k: (e * ns + k, 0), k=k))
                    for k in range(ns)],
                  *[pl.BlockSpec((None, d // ns, PEER_TE), functools.partial(lambda i, e, k: (e, k, 0), k=k))
                    for k in range(ns)]],
        out_specs=pl.BlockSpec((tm, d), lambda i, e: (i, 0)),
        out_shape=jax.ShapeDtypeStruct((t, d), F32),
        scratch_shapes=[pltpu.VMEM((2 * h * N_KEYS, tm), F32), pltpu.VMEM((2, PEER_TOPK, h, tm), F32),
                        pltpu.VMEM((tm, d), BF16), pltpu.VMEM((tm, d), BF16),
                        pltpu.VMEM((h, N_KEYS, tm), F32), pltpu.VMEM((h, N_KEYS, tm), F32),
                        pltpu.VMEM((h, N_KEYS, tm), BF16), pltpu.VMEM((h, N_KEYS, tm), BF16),
                        pltpu.VMEM((tm // PEER_SB, PEER_TE, PEER_SB), BF16), pltpu.VMEM((d, tm), F32)],
        compiler_params=pltpu.CompilerParams(
            dimension_semantics=("arbitrary", "arbitrary"), vmem_limit_bytes=VMEM_LIMIT),
        name="peer_dense",
    )(hx, wk_hi, wk_lo, *([u_bf] * ns), *([vt_bf] * ns))


LN_TM = 256


def _ln_mod_body(xs_ref, y_ref, mg_ref, mn_ref, lng_ref, lnb_ref, xo_ref, ho_ref, *, gate_col, mod_col, nct, nt):
    i = pl.program_id(0)
    d = xs_ref.shape[1]
    is_ctx = i < nct

    def pick(ref, col):
        return jnp.where(is_ctx, ref[1:2, col * d:(col + 1) * d], ref[0:1, col * d:(col + 1) * d])

    z = DN_ALPHA * xs_ref[...] + pick(mg_ref, gate_col) * y_ref[...]
    mu = jnp.mean(z, axis=-1, keepdims=True)
    zc = z - mu
    var = jnp.mean(zc * zc, axis=-1, keepdims=True)
    xn = zc * lax.rsqrt(var + LN_EPS) * lng_ref[...] + lnb_ref[...]
    xo_ref[...] = xn
    h = xn * (1.0 + pick(mn_ref, mod_col + 1)) + pick(mn_ref, mod_col)
    ho_ref[...] = jnp.where(i < nt, h, 0.0)


def ln_mod(xs, y, mod_gate, gate_col, mod_next, mod_col, ln_g, ln_b, n_ctx, pad_to):
    t, d = xs.shape
    tm = LN_TM
    assert t % tm == 0 and n_ctx % tm == 0 and pad_to % tm == 0 and pad_to >= t
    nt = t // tm
    row = lambda i: (jnp.minimum(i, nt - 1), 0)
    full = lambda a: pl.BlockSpec(a.shape, lambda i: (0, 0))
    return pl.pallas_call(
        functools.partial(_ln_mod_body, gate_col=gate_col, mod_col=mod_col, nct=n_ctx // tm, nt=nt),
        grid=(pad_to // tm,),
        in_specs=[pl.BlockSpec((tm, d), row), pl.BlockSpec((tm, d), row), full(mod_gate), full(mod_next),
                  pl.BlockSpec((1, d), lambda i: (0, 0)), pl.BlockSpec((1, d), lambda i: (0, 0))],
        out_specs=[pl.BlockSpec((tm, d), row), pl.BlockSpec((tm, d), lambda i: (i, 0))],
        out_shape=[jax.ShapeDtypeStruct((t, d), F32), jax.ShapeDtypeStruct((pad_to, d), F32)],
        compiler_params=pltpu.CompilerParams(dimension_semantics=("arbitrary",), vmem_limit_bytes=VMEM_LIMIT),
        name="ln_mod",
    )(xs, y, mod_gate, mod_next, ln_g.reshape(1, d), ln_b.reshape(1, d))


def _ln(x, g, b):
    mu = jnp.mean(x, -1, keepdims=True)
    xc = x - mu
    var = jnp.mean(xc * xc, -1, keepdims=True)
    return xc * lax.rsqrt(var + LN_EPS) * g + b


def _head_norm(h, nheads, eps):
    t = h.shape[0]
    hh = h.reshape(t, nheads, -1)
    mu = jnp.mean(hh, -1, keepdims=True)
    xc = hh - mu
    var = jnp.mean(xc * xc, -1, keepdims=True)
    return (xc * lax.rsqrt(var + eps)).reshape(t, -1)


def _pad_cols(w, n):
    return jnp.pad(w, ((0, 0), (0, n - w.shape[1])))


def _shift_rows(u, k):
    if k > 0:
        return jnp.concatenate([u[k:], jnp.zeros_like(u[:k])], axis=0)
    return jnp.concatenate([jnp.zeros_like(u[:-k]), u[:k]], axis=0)


def _grid_conv(u, w, b):
    s = u.shape[0]
    colid = (jnp.arange(s) % GRID_W)[:, None]
    out = jnp.zeros_like(u) + b
    for di in range(3):
        for dj in range(3):
            off = (di - 1) * GRID_W + (dj - 1)
            sh = _shift_rows(u, off) if off else u
            if dj == 0:
                sh = jnp.where(colid == 0, 0.0, sh)
            elif dj == 2:
                sh = jnp.where(colid == GRID_W - 1, 0.0, sh)
            out = out + sh * w[di, dj]
    return out


def _seq_conv(u, w, b):
    wc = w[1]
    return _shift_rows(u, -1) * wc[0] + u * wc[1] + _shift_rows(u, 1) * wc[2] + b


def _qshift(u):
    q = u.shape[1] // 4
    colid = (jnp.arange(u.shape[0]) % GRID_W)[:, None]
    left = jnp.where(colid == 0, 0.0, _shift_rows(u[:, :q], -1))
    right = jnp.where(colid == GRID_W - 1, 0.0, _shift_rows(u[:, q:2 * q], 1))
    up = _shift_rows(u[:, 2 * q:3 * q], -GRID_W)
    down = _shift_rows(u[:, 3 * q:], GRID_W)
    return jnp.concatenate([left, right, up, down], axis=1)


def _shift_seq(u):
    h = u.shape[1] // 2
    return jnp.concatenate([_shift_rows(u[:, :h], -1), _shift_rows(u[:, h:], 1)], axis=1)


def _mlstm_layer(h, n_ctx, w_in, b_in, conv_w, conv_b, hn_g, w_out):
    d = D_MODEL
    t = h.shape[0]
    npad = 4 * d + LANES
    p = matmul(h, _pad_cols(w_in, npad), jnp.pad(b_in, (0, npad - b_in.shape[0])))
    qk_pre = p[:, :2 * d]
    qk = jax.nn.silu(jnp.concatenate([_seq_conv(qk_pre[:n_ctx], conv_w, conv_b),
                                      _grid_conv(qk_pre[n_ctx:], conv_w, conv_b)], axis=0))
    q = qk[:, :d] * (MLSTM_DH ** -0.5)
    k = qk[:, d:]
    v = p[:, 2 * d:3 * d]
    o = jax.nn.sigmoid(p[:, 3 * d:4 * d])
    g = p[:, 4 * d:4 * d + 4 * MLSTM_HEADS].reshape(t, 4, MLSTM_HEADS)
    g = jnp.concatenate([g[:, :2], jax.nn.log_sigmoid(g[:, 2:])], axis=1)
    gh = jnp.transpose(g, (2, 0, 1))
    gcol = jnp.pad(gh, ((0, 0), (0, 0), (0, LANES - 4)))
    grow = jnp.pad(jnp.transpose(gh, (0, 2, 1)), ((0, 0), (0, 4), (0, 0)))
    hf, hb = mlstm_scan(q, k, v, gcol, grow, n_ctx)
    hn = _head_norm(hf + hb, MLSTM_HEADS, LN_EPS)
    return matmul(o * hn * hn_g, w_out)


def _rwkv_layer(h, n_ctx, mu, w_rkv, w0, w1, w2, a0, a1, a2, g1, g2, k_k, k_a, r_k, lnx_g, lnx_b, w_out):
    d = D_MODEL
    t = h.shape[0]
    shifted = jnp.concatenate([_shift_seq(h[:n_ctx]), _qshift(h[n_ctx:])], axis=0)
    dx = shifted - h
    xm = [h + dx * mu[n] for n in range(6)]
    r = matmul(xm[0], w_rkv[0])
    k = matmul(xm[1], w_rkv[1])
    v = matmul(xm[2], w_rkv[2])
    lora = DECAY_LORA = w1.shape[-1]
    w1c = _pad_cols(jnp.concatenate([w1[0], w1[1]], axis=1), LANES)
    zpad = jnp.zeros((lora, d), F32)
    w2c = jnp.concatenate([jnp.concatenate([w2[0], zpad], axis=1),
                           jnp.concatenate([zpad, w2[1]], axis=1)], axis=0)
    wpre = matmul(jnp.tanh(matmul(xm[3], w1c)), w2c) + jnp.concatenate([w0[0], w0[1]])
    lw = -jnp.exp(-jax.nn.softplus(-wpre) - 0.5)
    a1c = _pad_cols(jnp.concatenate([a1[0], a1[1]], axis=1), LANES)
    a2c = jnp.concatenate([jnp.concatenate([a2[0], zpad], axis=1),
                           jnp.concatenate([zpad, a2[1]], axis=1)], axis=0)
    a = jax.nn.sigmoid(matmul(matmul(xm[4], a1c), a2c) + jnp.concatenate([a0[0], a0[1]]))
    glora = g1.shape[1]
    gpad = 2 * LANES
    gg = jax.nn.sigmoid(matmul(xm[5], _pad_cols(g1, gpad)))
    g = matmul(gg, jnp.pad(g2, ((0, gpad - glora), (0, 0))))
    kk = (k * k_k).reshape(t, RWKV_HEADS, RWKV_N)
    kk = (kk / jnp.maximum(jnp.sqrt(jnp.sum(kk * kk, -1, keepdims=True)), 1e-12)).reshape(t, d)
    a_f, a_b = a[:, :d], a[:, d:]
    kt_f = k * (1.0 + (a_f - 1.0) * k_a)
    kt_b = k * (1.0 + (a_b - 1.0) * k_a)
    yf, yb = rwkv_scan(r, v, kk, lw[:, :d], kt_f, kk * a_f, lw[:, d:], kt_b, kk * a_b, n_ctx)
    yn = _head_norm(yf + yb, RWKV_HEADS, RWKV_GN_EPS) * lnx_g + lnx_b
    kbar = 0.5 * (kt_f + kt_b)
    bonus = jnp.sum((r * kbar * r_k).reshape(t, RWKV_HEADS, RWKV_N), -1, keepdims=True) \
        * v.reshape(t, RWKV_HEADS, RWKV_N)
    return matmul((yn + bonus.reshape(t, d)) * g, w_out)


def _forward(x, c, ctx, c_ctx, ada_w, ada_b, ln_g, ln_b,
             ml_w_in, ml_b_in, ml_conv_w, ml_conv_b, ml_hn_g, ml_w_out,
             rw_mu, rw_w_rkv, rw_w0, rw_w1, rw_w2, rw_a0, rw_a1, rw_a2, rw_g1, rw_g2,
             rw_k_k, rw_k_a, rw_r_k, rw_lnx_g, rw_lnx_b, rw_w_out,
             pk_wq, pk_keys, pk_u, pk_v):
    d = D_MODEL
    n_ctx = ctx.shape[1]
    n_lat = x.shape[1]
    xs = jnp.concatenate([ctx[0], x[0]], axis=0)
    t = xs.shape[0]
    t_pad = -(-t // PEER_TM) * PEER_TM
    s_in = jnp.zeros((8, d), F32).at[0].set(jax.nn.silu(c[0])).at[1].set(jax.nn.silu(c_ctx))
    depth = ada_w.shape[0]
    mods = [matmul(s_in, ada_w[i], ada_b[i], passes=3) for i in range(depth)]
    is_ctx = (jnp.arange(t) < n_ctx)[:, None]
    m0 = [jnp.where(is_ctx, mods[0][1, n * d:(n + 1) * d], mods[0][0, n * d:(n + 1) * d]) for n in range(2)]
    h = xs * (1.0 + m0[1]) + m0[0]
    for i in range(depth):
        j = i // 2
        if i % 2 == 0:
            y = _mlstm_layer(h, n_ctx, ml_w_in[j], ml_b_in[j], ml_conv_w[j], ml_conv_b[j],
                             ml_hn_g[j], ml_w_out[j])
        else:
            y = _rwkv_layer(h, n_ctx, rw_mu[j], rw_w_rkv[j], rw_w0[j], rw_w1[j], rw_w2[j],
                            rw_a0[j], rw_a1[j], rw_a2[j], rw_g1[j], rw_g2[j], rw_k_k[j],
                            rw_k_a[j], rw_r_k[j], rw_lnx_g[j], rw_lnx_b[j], rw_w_out[j])
        xs, h = ln_mod(xs, y, mods[i], 2, mods[i], 3, ln_g[i, 0], ln_b[i, 0], n_ctx, t_pad)
        wk = peer_fold_keys(pk_wq[i], pk_keys[i].reshape(2 * PEER_HEADS, N_KEYS, PEER_DQ // 2))
        vt = jnp.swapaxes(pk_v[i].astype(BF16).reshape(-1, PEER_TE, d), 1, 2)
        y = peer(h, wk, pk_u[i].astype(BF16), vt)
        xs, h = ln_mod(xs, y, mods[i], 5, mods[min(i + 1, depth - 1)], 0, ln_g[i, 1], ln_b[i, 1], n_ctx, t)
    return xs[n_ctx:][None]


def kernel(x, c, ctx, c_ctx, ada_w, ada_b, ln_g, ln_b, ml_w_in, ml_b_in, ml_conv_w, ml_conv_b, ml_hn_g, ml_w_out, rw_mu, rw_w_rkv, rw_w0, rw_w1, rw_w2, rw_a0, rw_a1, rw_a2, rw_g1, rw_g2, rw_k_k, rw_k_a, rw_r_k, rw_lnx_g, rw_lnx_b, rw_w_out, pk_wq, pk_keys, pk_u, pk_v):
    return _forward(x, c, ctx, c_ctx, ada_w, ada_b, ln_g, ln_b,
                    ml_w_in, ml_b_in, ml_conv_w, ml_conv_b, ml_hn_g, ml_w_out,
                    rw_mu, rw_w_rkv, rw_w0, rw_w1, rw_w2, rw_a0, rw_a1, rw_a2, rw_g1, rw_g2,
                    rw_k_k, rw_k_a, rw_r_k, rw_lnx_g, rw_lnx_b, rw_w_out,
                    pk_wq, pk_keys, pk_u, pk_v)
```

```python
import functools

import jax
import jax.numpy as jnp
from jax import lax
from jax.experimental import pallas as pl
from jax.experimental.pallas import tpu as pltpu

F32 = jnp.float32
BF16 = jnp.bfloat16

D_MODEL = 1024
DEPTH = 4
GRID_W = 64
N_MOD = 6
DN_ALPHA = (2.0 * DEPTH) ** 0.25
LN_EPS = 1e-5

MLSTM_HEADS = 4
MLSTM_DH = D_MODEL // MLSTM_HEADS
MLSTM_CHUNK = 128
M_INIT = -1e30

RWKV_N = 64
RWKV_HEADS = D_MODEL // RWKV_N
RWKV_CHUNK = 64
RWKV_GN_EPS = 64e-5

N_KEYS = 128
PEER_HEADS = 8
PEER_DQ = 256
PEER_TOPK = 16
PEER_I_BLOCK = 8
PEER_TE = PEER_I_BLOCK * N_KEYS
PEER_TM = 512
PEER_SB = 256

LANES = 128
VMEM_LIMIT = 62 * 1024 * 1024

NN = ((1,), (0,))
NT = ((1,), (1,))
TN = ((0,), (0,))


def _split(x, n):
    parts = []
    r = x.astype(F32)
    for i in range(n):
        p = r.astype(BF16)
        parts.append(p)
        if i + 1 < n:
            r = r - p.astype(F32)
    return parts


def _d(a, b, dims):
    return lax.dot_general(a, b, (dims, ((), ())), preferred_element_type=F32)


def _mdot(a, b, dims, passes):
    if passes == 1:
        return _d(a.astype(BF16), b.astype(BF16), dims)
    if passes == 3:
        a0, a1 = _split(a, 2)
        b0, b1 = _split(b, 2)
        return (_d(a0, b1, dims) + _d(a1, b0, dims)) + _d(a0, b0, dims)
    a0, a1, a2 = _split(a, 3)
    b0, b1, b2 = _split(b, 3)
    lo = (_d(a0, b2, dims) + _d(a2, b0, dims)) + _d(a1, b1, dims)
    mid = _d(a0, b1, dims) + _d(a1, b0, dims)
    return (lo + mid) + _d(a0, b0, dims)


def _dot_01_lhs(m01, x):
    mb = m01.astype(BF16)
    x0, x1, x2 = _split(x, 3)
    return (_d(mb, x2, NN) + _d(mb, x1, NN)) + _d(mb, x0, NN)


def _dot_01_rhs(x, m01):
    mb = m01.astype(BF16)
    x0, x1, x2 = _split(x, 3)
    return (_d(x2, mb, NN) + _d(x1, mb, NN)) + _d(x0, mb, NN)


def _pick(n, cands):
    for c in cands:
        if n % c == 0:
            return c
    raise ValueError(f"no tile for {n}")


_ACTS = {None: lambda x: x, "sigmoid": jax.nn.sigmoid, "tanh": jnp.tanh}


def _mm_body(x_ref, w_ref, b_ref, o_ref, *, passes, act):
    o_ref[...] = _ACTS[act](_mdot(x_ref[...], w_ref[...], NN, passes) + b_ref[...]).astype(o_ref.dtype)


def matmul(x, w, b=None, *, passes=1, act=None, out_dtype=F32):
    m, k = x.shape
    n = w.shape[1]
    assert n % LANES == 0 and w.shape[0] == k
    tm = m if m <= 1024 else _pick(m, (640, 512, 384, 256, 128))
    tn = _pick(n, (1024, 768, 640, 512, 384, 256, 128))
    if b is None:
        b = jnp.zeros((n,), F32)
    if passes == 1:
        w = w.astype(BF16)
    return pl.pallas_call(
        functools.partial(_mm_body, passes=passes, act=act),
        grid=(m // tm, n // tn),
        in_specs=[pl.BlockSpec((tm, k), lambda i, j: (i, 0)),
                  pl.BlockSpec((k, tn), lambda i, j: (0, j)),
                  pl.BlockSpec((1, tn), lambda i, j: (0, j))],
        out_specs=pl.BlockSpec((tm, tn), lambda i, j: (i, j)),
        out_shape=jax.ShapeDtypeStruct((m, n), out_dtype),
        compiler_params=pltpu.CompilerParams(
            dimension_semantics=("arbitrary", "arbitrary"), vmem_limit_bytes=VMEM_LIMIT),
        name="proj_matmul",
    )(x, w, b.reshape(1, n).astype(F32))


CONV_TM = 256
CONV_CB = 512


def _conv_body(cur_ref, prev_ref, next_ref, w_ref, b_ref, o_ref, *, nct, nt, q_blocks, q_scale):
    i = pl.program_id(0)
    j = pl.program_id(1)
    is_ctx = i < nct
    tm = cur_ref.shape[0]
    above = jnp.where(is_ctx | (i == nct), 0.0, prev_ref[...])
    below = jnp.where(is_ctx | (i == nt - 1), 0.0, next_ref[...])
    ext = jnp.concatenate([above, cur_ref[...], below], axis=0)
    n = ext.shape[0]
    colid = lax.broadcasted_iota(jnp.int32, ext.shape, 0) % GRID_W
    left = jnp.where(is_ctx | (colid != 0), pltpu.roll(ext, 1, 0), 0.0)
    right = jnp.where(is_ctx | (colid != GRID_W - 1), pltpu.roll(ext, n - 1, 0), 0.0)
    w = w_ref[...]
    acc = jnp.zeros((tm, ext.shape[1]), F32) + b_ref[...]
    for di in range(3):
        rows = slice(di * GRID_W, di * GRID_W + tm)
        tap = left[rows] * w[3 * di:3 * di + 1] + ext[rows] * w[3 * di + 1:3 * di + 2] \
            + right[rows] * w[3 * di + 2:3 * di + 3]
        acc = acc + (tap if di == 1 else jnp.where(is_ctx, 0.0, tap))
    y = acc * jax.nn.sigmoid(acc)
    o_ref[...] = (y * jnp.where(j < q_blocks, q_scale, 1.0)).astype(o_ref.dtype)


def mlstm_conv(qk_pre, conv_w, conv_b, n_ctx):
    t, c = qk_pre.shape
    tm, cb = CONV_TM, CONV_CB
    assert t % tm == 0 and n_ctx % tm == 0 and tm % GRID_W == 0 and c % (2 * cb) == 0
    r = tm // GRID_W
    nt, nu = t // tm, t // GRID_W
    return pl.pallas_call(
        functools.partial(_conv_body, nct=n_ctx // tm, nt=nt, q_blocks=c // (2 * cb), q_scale=MLSTM_DH ** -0.5),
        grid=(nt, c // cb),
        in_specs=[pl.BlockSpec((tm, cb), lambda i, j: (i, j)),
                  pl.BlockSpec((GRID_W, cb), lambda i, j: (jnp.maximum(i * r - 1, 0), j)),
                  pl.BlockSpec((GRID_W, cb), lambda i, j: (jnp.minimum((i + 1) * r, nu - 1), j)),
                  pl.BlockSpec((9, cb), lambda i, j: (0, j)),
                  pl.BlockSpec((1, cb), lambda i, j: (0, j))],
        out_specs=pl.BlockSpec((tm, cb), lambda i, j: (i, j)),
        out_shape=jax.ShapeDtypeStruct((t, c), BF16),
        compiler_params=pltpu.CompilerParams(
            dimension_semantics=("arbitrary", "arbitrary"), vmem_limit_bytes=VMEM_LIMIT),
        name="mlstm_conv",
    )(qk_pre, qk_pre, qk_pre, conv_w.reshape(9, c), conv_b.reshape(1, c))


def _mlstm_out_body(hf_ref, hb_ref, o_ref, g_ref, w_ref, y_ref):
    h = hf_ref[...] + hb_ref[...]
    parts = []
    for a in range(MLSTM_HEADS):
        x = h[:, a * MLSTM_DH:(a + 1) * MLSTM_DH]
        mu = jnp.mean(x, axis=-1, keepdims=True)
        xc = x - mu
        var = jnp.mean(xc * xc, axis=-1, keepdims=True)
        parts.append(xc * lax.rsqrt(var + LN_EPS))
    hn = jnp.concatenate(parts, axis=1)
    z = o_ref[...].astype(F32) * hn * g_ref[...]
    y_ref[...] = _d(z.astype(BF16), w_ref[...], NN)


def mlstm_out(hf, hb, o, hn_g, w_out):
    t, d = hf.shape
    tm = LN_TM
    row = pl.BlockSpec((tm, d), lambda i: (i, 0))
    return pl.pallas_call(
        _mlstm_out_body,
        grid=(t // tm,),
        in_specs=[row, row, row, pl.BlockSpec((1, d), lambda i: (0, 0)), pl.BlockSpec((d, d), lambda i: (0, 0))],
        out_specs=row,
        out_shape=jax.ShapeDtypeStruct((t, d), F32),
        compiler_params=pltpu.CompilerParams(dimension_semantics=("arbitrary",), vmem_limit_bytes=VMEM_LIMIT),
        name="mlstm_out",
    )(hf, hb, o, hn_g.reshape(1, d), w_out.astype(BF16))


def _mlstm_body(qf, kf, vf, gcf, grf, qb, kb, vb, gcb, grb, hf_ref, hb_ref, c_ref, n_ref, m_ref):
    @pl.when(pl.program_id(1) == 0)
    def _():
        c_ref[...] = jnp.zeros(c_ref.shape, F32)
        n_ref[...] = jnp.zeros(n_ref.shape, F32)
        m_ref[...] = jnp.full(m_ref.shape, M_INIT, F32)

    ds = (0, 1)
    each = lambda f, *ls: [f(*xs) for xs in zip(*ls)]
    q, k, v = [qf[...], qb[...]], [kf[...], kb[...]], [vf[...], vb[...]]
    gc, gr = [gcf[...], gcb[...]], [grf[...], grb[...]]
    L = q[0].shape[0]
    row = lax.broadcasted_iota(jnp.int32, (L, L), 0)
    col = lax.broadcasted_iota(jnp.int32, (L, L), 1)
    seen = (col <= row, col >= row)
    tri = [jnp.where(m, 1.0, 0.0) for m in seen]
    b_col = [_dot_01_lhs(tri[d], gc[d])[:, 2 + d:3 + d] for d in ds]
    b_row = [_dot_01_rhs(gr[d], tri[1 - d])[2 + d:3 + d, :] for d in ds]
    ig_col = [gc[d][:, d:d + 1] for d in ds]
    ig_row = [gr[d][d:d + 1, :] for d in ds]
    m_st = [m_ref[d, 0:1, 0:1] for d in ds]
    c_st = [c_ref[d] for d in ds]
    n_st = [n_ref[d] for d in ds]

    dlog = [jnp.where(seen[d], b_col[d] - b_row[d] + ig_row[d], -jnp.inf) for d in ds]
    m_inter = each(jnp.add, b_col, m_st)
    m_t = each(lambda mi, dl: jnp.maximum(mi, jnp.max(dl, axis=1, keepdims=True)), m_inter, dlog)
    qb16, kb16, vb16 = q, k, v
    q, k = (each(lambda x: x.astype(F32), a) for a in (q, k))
    qk = each(lambda a, b: _d(a, b, NT), qb16, kb16)
    s = each(lambda x, dl, mt: x * jnp.exp(dl - mt), qk, dlog, m_t)
    dec = each(lambda mi, mt: jnp.exp(mi - mt), m_inter, m_t)
    sv = each(lambda a, b: _d(a.astype(BF16), b, NN), s, vb16)
    qc = each(lambda a, b: _d(a, b.astype(BF16), NN), qb16, c_st)
    num = each(lambda a, dc, b: a + dc * b, sv, dec, qc)
    den = each(lambda x, dc, qq, nn: jnp.sum(x, axis=1, keepdims=True) + dc * jnp.sum(qq * nn, axis=1, keepdims=True),
               s, dec, q, n_st)
    h = each(lambda nu, de, mt: nu / jnp.maximum(jnp.abs(de), jnp.exp(-mt)), num, den, m_t)
    hf_ref[...] = h[0]
    hb_ref[...] = h[1]

    b_last = [b_col[0][L - 1:L, :], b_col[1][0:1, :]]
    w_c = each(lambda bl, bc, ic: bl - bc + ic, b_last, b_col, ig_col)
    m_new = each(lambda bl, ms, w: jnp.maximum(bl + ms, jnp.max(w, axis=0, keepdims=True)), b_last, m_st, w_c)
    a_c = each(lambda w, mn: jnp.exp(w - mn), w_c, m_new)
    g_prev = each(lambda bl, ms, mn: jnp.exp(bl + ms - mn), b_last, m_st, m_new)
    ak = each(jnp.multiply, a_c, k)
    kv = each(lambda a, b: _d(a.T.astype(BF16), b, NN), ak, vb16)
    for d in ds:
        c_ref[d] = g_prev[d] * c_st[d] + kv[d]
        n_ref[d] = g_prev[d] * n_st[d] + jnp.sum(ak[d], axis=0, keepdims=True)
        m_ref[d] = jnp.broadcast_to(m_new[d], m_ref.shape[1:])


def _bwd_chunk(c, nc0, nc):
    return jnp.where(c < nc0, nc0 - 1 - c, nc - 1 - (c - nc0))


def mlstm_scan(qk, v, gcol, grow, n_ctx):
    t = v.shape[0]
    L, dh = MLSTM_CHUNK, MLSTM_DH
    nc, nc0 = t // L, n_ctx // L
    fwd = lambda h, c: (c, h)
    bwd = lambda h, c: (_bwd_chunk(c, nc0, nc), h)
    fwd_k = lambda h, c: (c, MLSTM_HEADS + h)
    bwd_k = lambda h, c: (_bwd_chunk(c, nc0, nc), MLSTM_HEADS + h)
    qkv = lambda im: pl.BlockSpec((L, dh), im)
    gc_spec = lambda f: pl.BlockSpec((None, L, LANES), lambda h, c: (h, f(c), 0))
    gr_spec = lambda f: pl.BlockSpec((None, 8, L), lambda h, c: (h, 0, f(c)))
    idf = lambda c: c
    idb = lambda c: _bwd_chunk(c, nc0, nc)
    return pl.pallas_call(
        _mlstm_body,
        grid=(MLSTM_HEADS, nc),
        in_specs=[qkv(fwd), qkv(fwd_k), qkv(fwd), gc_spec(idf), gr_spec(idf),
                  qkv(bwd), qkv(bwd_k), qkv(bwd), gc_spec(idb), gr_spec(idb)],
        out_specs=[qkv(fwd), qkv(bwd)],
        out_shape=[jax.ShapeDtypeStruct((t, D_MODEL), F32)] * 2,
        scratch_shapes=[pltpu.VMEM((2, dh, dh), F32), pltpu.VMEM((2, 1, dh), F32),
                        pltpu.VMEM((2, 8, LANES), F32)],
        compiler_params=pltpu.CompilerParams(
            dimension_semantics=("arbitrary", "arbitrary"), vmem_limit_bytes=VMEM_LIMIT),
        name="mlstm_scan",
    )(qk, qk, v, gcol, grow, qk, qk, v, gcol, grow)


RW_PASSES = 1
RW_CHUNKS_PER_STEP = 4


def _stack2(x, lane_head):
    return jnp.concatenate([jnp.where(lane_head == 0, x, 0.0), jnp.where(lane_head == 1, x, 0.0)], axis=0)


def _unstack2(x):
    L = x.shape[0] // 2
    return x[:L] + x[L:]


def _rwkv_chunks(chains):
    L = chains[0][0].shape[0]
    n2 = 2 * L
    p = RW_PASSES
    ds = [c[6] for c in chains]
    each = lambda f, *ls: [f(*xs) for xs in zip(*ls)]
    dot = lambda dims: (lambda a, b: _mdot(a, b, dims, p))
    lw, r, k, v, kap, alp = ([c[i] for c in chains] for i in range(6))

    row = lax.broadcasted_iota(jnp.int32, (L, L), 0)
    col = lax.broadcasted_iota(jnp.int32, (L, L), 1)
    tris = (jnp.where(col <= row, 1.0, 0.0), jnp.where(col >= row, 1.0, 0.0))
    lp = [_dot_01_lhs(tris[d], x) for d, x in zip(ds, lw)]
    lp_end = [x[0:1, :] if d else x[L - 1:L, :] for d, x in zip(ds, lp)]
    e_neg = each(lambda x: jnp.exp(-x), lp)
    e_end = each(lambda x, xe: jnp.exp(xe - x), lp, lp_end)
    kap_t = each(lambda x, y, z: x * jnp.exp(y - z), kap, lp, lw)
    r_t = each(lambda x, y: x * jnp.exp(y), r, lp)
    k_h = each(jnp.multiply, k, e_neg)
    a_h = each(jnp.multiply, alp, e_neg)
    k_e = each(jnp.multiply, k, e_end)
    a_e = each(jnp.multiply, alp, e_end)

    lane_head = lax.broadcasted_iota(jnp.int32, (L, LANES), 1) // RWKV_N
    st = lambda x: _stack2(x, lane_head)
    kap_s, r_s, v_s, k_s, a_s = (each(st, x) for x in (kap_t, r_t, v, k_e, a_e))
    rhs_k = each(lambda x: jnp.concatenate([x, x], axis=0), k_h)
    rhs_a = each(lambda x: jnp.concatenate([x, x], axis=0), a_h)

    row2 = lax.broadcasted_iota(jnp.int32, (n2, n2), 0)
    col2 = lax.broadcasted_iota(jnp.int32, (n2, n2), 1)
    same_head = (row2 // L) == (col2 // L)
    strict = (same_head & (col2 < row2), same_head & (col2 > row2))
    incl = (same_head & (col2 <= row2), same_head & (col2 >= row2))
    zero = jnp.zeros((n2, n2), F32)
    masked = lambda masks: (lambda d, x: jnp.where(masks[d], x, zero))
    n_ka = each(masked(strict), ds, each(dot(NT), kap_s, rhs_a))
    m_kk = each(masked(strict), ds, each(dot(NT), kap_s, rhs_k))
    a_rk = each(masked(incl), ds, each(dot(NT), r_s, rhs_k))
    a_ra = each(masked(incl), ds, each(dot(NT), r_s, rhs_a))

    b16 = (row2 // 16) == (col2 // 16)
    b32 = (row2 // 32) == (col2 // 32)
    eye = jnp.where(row2 == col2, 1.0, 0.0)
    n16 = each(lambda x: jnp.where(b16, x, zero), n_ka)
    n_2 = each(dot(NN), n16, n16)
    n_4 = each(dot(NN), n_2, n_2)
    n_8 = each(dot(NN), n_4, n_4)
    inv = each(lambda x: eye - x, n16)
    for pw in (n_2, n_4, n_8):
        inv = each(jnp.add, inv, each(dot(NN), inv, pw))
    for sel in (lambda x: jnp.where(b32 & ~b16, x, zero), lambda x: jnp.where(b32, zero, x)):
        t1 = each(dot(NN), inv, each(sel, n_ka))
        inv = each(jnp.subtract, inv, each(dot(NN), t1, inv))

    mv = each(dot(NN), m_kk, v_s)
    w1u0 = each(dot(NN), inv, each(lambda x, y: jnp.concatenate([x, y], axis=1), kap_s, mv))
    ar = each(dot(NN), a_ra, w1u0)
    av = each(dot(NN), a_rk, v_s)
    r2 = each(lambda x, y: _unstack2(x - y[:, :LANES]), r_s, ar)
    y0 = each(lambda x, y: _unstack2(x - y[:, LANES:]), av, ar)
    w1_t = each(lambda x: x[:, :LANES].T, w1u0)
    u0_t = each(lambda x: x[:, LANES:].T, w1u0)
    v_t = each(lambda x: x.T, v_s)
    wa = each(dot(NN), w1_t, a_s)
    vk = each(dot(NN), v_t, k_s)
    ua = each(dot(NN), u0_t, a_s)
    g = each(lambda xe, x: jnp.where(row2 == col2, jnp.exp(xe), zero) - x, lp_end, wa)
    b = each(jnp.subtract, vk, ua)
    return list(zip(y0, r2, g, b))


def _rwkv_body(lwf, rf, kf, vf, kkf, kaf, lwb, rb, kb, vb, kkb, kab, yf_ref, yb_ref, s_ref):
    @pl.when(pl.program_id(1) == 0)
    def _():
        s_ref[...] = jnp.zeros(s_ref.shape, F32)

    L = RWKV_CHUNK
    n = RW_CHUNKS_PER_STEP
    rows = lambda j: slice(j * L, (j + 1) * L)
    refs = ((lwf, rf, kf, vf, kkf, kaf), (lwb, rb, kb, vb, kkb, kab))
    visit = [(d, j if d == 0 else n - 1 - j) for j in range(n) for d in (0, 1)]
    pre = _rwkv_chunks([tuple(ref[rows(j), :] for ref in refs[d]) + (d,) for d, j in visit])
    y_refs = (yf_ref, yb_ref)
    s = [s_ref[0], s_ref[1]]
    for (d, j), (y0, r2, g, b) in zip(visit, pre):
        y_refs[d][rows(j), :] = y0 + _mdot(r2, s[d], NT, RW_PASSES)
        s[d] = _mdot(s[d], g, NN, RW_PASSES) + b
    s_ref[0] = s[0]
    s_ref[1] = s[1]


def rwkv_scan(r, v, kk, lw_f, kt_f, ka_f, lw_b, kt_b, ka_b, n_ctx):
    t = r.shape[0]
    L = RWKV_CHUNK
    blk = RW_CHUNKS_PER_STEP * L
    assert 2 * L == LANES and t % blk == 0 and n_ctx % blk == 0
    nc, nc0 = t // blk, n_ctx // blk
    fwd = pl.BlockSpec((blk, LANES), lambda h, c: (c, h))
    bwd = pl.BlockSpec((blk, LANES), lambda h, c: (_bwd_chunk(c, nc0, nc), h))
    return pl.pallas_call(
        _rwkv_body,
        grid=(RWKV_HEADS // 2, nc),
        in_specs=[fwd] * 6 + [bwd] * 6,
        out_specs=[fwd, bwd],
        out_shape=[jax.ShapeDtypeStruct((t, D_MODEL), F32)] * 2,
        scratch_shapes=[pltpu.VMEM((2, LANES, LANES), F32)],
        compiler_params=pltpu.CompilerParams(
            dimension_semantics=("arbitrary", "arbitrary"), vmem_limit_bytes=VMEM_LIMIT),
        name="rwkv7_scan",
    )(lw_f, r, kt_f, v, kk, ka_f, lw_b, r, kt_b, v, kk, ka_b)


def _peer_stats(hx_ref, wkh_ref, wkl_ref, sc_ref, top_ref, hxb_ref, hxl_ref, n_ref, f0_ref, r1_ref, e1_ref):
    tm = hx_ref.shape[0]
    hx_hi, hx_lo = _split(hx_ref[...], 2)
    hxb_ref[...] = hx_hi
    hxl_ref[...] = hx_lo
    rows_per = 2 * LANES

    def scores(c, carry):
        rows = pl.ds(pl.multiple_of(c * rows_per, rows_per), rows_per)
        wk_hi = wkh_ref[rows, :]
        sc_ref[rows, :] = (_d(wk_hi, hxl_ref[...], NT) + _d(wkl_ref[rows, :], hxb_ref[...], NT)) \
            + _d(wk_hi, hxb_ref[...], NT)
        return carry

    lax.fori_loop(0, wkh_ref.shape[0] // rows_per, scores, 0)

    def block(tb, carry):
        lanes = pl.ds(pl.multiple_of(tb * LANES, LANES), LANES)
        neg = jnp.full((N_KEYS, LANES), -jnp.inf, F32)
        for h in range(PEER_HEADS):
            for p in range(2):
                cur = sc_ref[(2 * h + p) * N_KEYS:(2 * h + p + 1) * N_KEYS, lanes]
                rank = jnp.full((N_KEYS, LANES), float(PEER_TOPK), F32)
                for a in range(PEER_TOPK):
                    m = jnp.max(cur, axis=0, keepdims=True)
                    top_ref[p, a, h:h + 1, lanes] = m
                    hit = cur >= m
                    cur = jnp.where(hit, neg, cur)
                    if p == 1:
                        rank = jnp.where(hit, float(a), rank)
                if p == 1:
                    r1_ref[h, :, lanes] = rank.astype(BF16)
        top0 = [top_ref[0, a, :, lanes] for a in range(PEER_TOPK)]
        top1 = [top_ref[1, b, :, lanes] for b in range(PEER_TOPK)]
        cands = [top0[a] + top1[b]
                 for a in range(PEER_TOPK) for b in range(PEER_TOPK) if (a + 1) * (b + 1) <= PEER_TOPK]
        c_max = cands[0]
        z = jnp.zeros_like(c_max)
        tau = c_max
        for a in range(PEER_TOPK):
            tau = functools.reduce(jnp.maximum, cands)
            z = z + jnp.exp(tau - c_max)
            cands = [jnp.where(cd >= tau, -jnp.inf, cd) for cd in cands]
        inv_z = 1.0 / z
        for h in range(PEER_HEADS):
            s0 = sc_ref[2 * h * N_KEYS:(2 * h + 1) * N_KEYS, lanes]
            n = jnp.zeros((N_KEYS, LANES), F32)
            for b in range(PEER_TOPK):
                n = jnp.where(s0 + top1[b][h:h + 1] >= tau[h:h + 1], float(b + 1), n)
            n_ref[h, :, lanes] = n
            f0_ref[h, :, lanes] = jnp.exp(s0 - top0[0][h:h + 1]) * inv_z[h:h + 1]
            e1_ref[h, :, lanes] = jnp.exp(sc_ref[(2 * h + 1) * N_KEYS:(2 * h + 2) * N_KEYS, lanes] - top1[0][h:h + 1]).astype(BF16)
        return carry

    lax.fori_loop(0, tm // LANES, block, 0)


PEER_I_GROUP = 4
PEER_STREAMS = 4


def _peer_body(hx_ref, wkh_ref, wkl_ref, *rest):
    u_refs, vt_refs = rest[:PEER_STREAMS], rest[PEER_STREAMS:2 * PEER_STREAMS]
    o_ref, sc_ref, top_ref, hxb_ref, hxl_ref, n_ref, f0_ref, r1_ref, e1_ref, w_ref, acc_ref = rest[2 * PEER_STREAMS:]
    e = pl.program_id(1)
    tm = hx_ref.shape[0]
    nsb = tm // PEER_SB

    @pl.when(e == 0)
    def _():
        _peer_stats(hx_ref, wkh_ref, wkl_ref, sc_ref, top_ref, hxb_ref, hxl_ref, n_ref, f0_ref, r1_ref, e1_ref)
        acc_ref[...] = jnp.zeros(acc_ref.shape, F32)

    i_rows = pl.ds(pl.multiple_of(e * PEER_I_BLOCK, PEER_I_BLOCK), PEER_I_BLOCK)

    def activations(sb):
        parts = []
        for u_ref in u_refs:
            act = _d(u_ref[...], hxb_ref[sb * PEER_SB:(sb + 1) * PEER_SB, :], NT)
            parts.append((0.5 * act * (1.0 + lax.erf(act * (2.0 ** -0.5)))).astype(BF16))
        return parts

    def gates(sb, act):
        for hb in range(PEER_SB // LANES):
            lanes = slice(sb * PEER_SB + hb * LANES, sb * PEER_SB + (hb + 1) * LANES)
            sub = slice(hb * LANES, (hb + 1) * LANES)
            n8 = [n_ref[h, i_rows, lanes].astype(BF16) for h in range(PEER_HEADS)]
            f8 = [f0_ref[h, i_rows, lanes].astype(BF16) for h in range(PEER_HEADS)]
            for ig in range(0, PEER_I_BLOCK, PEER_I_GROUP):
                g = [jnp.zeros((N_KEYS, LANES), BF16) for _ in range(PEER_I_GROUP)]
                for h in range(PEER_HEADS):
                    r1 = r1_ref[h, :, lanes]
                    e1 = e1_ref[h, :, lanes]
                    for k in range(PEER_I_GROUP):
                        ii = ig + k
                        g[k] = g[k] + jnp.where(r1 < n8[h][ii:ii + 1], e1 * f8[h][ii:ii + 1], jnp.zeros_like(e1))
                for k in range(PEER_I_GROUP):
                    rows = slice((ig + k) * N_KEYS, (ig + k + 1) * N_KEYS)
                    part, off = divmod((ig + k) * N_KEYS, PEER_TE // PEER_STREAMS)
                    w_ref[sb, rows, sub] = g[k] * act[part][off:off + N_KEYS, sub]

    def accumulate(sb):
        cols = slice(sb * PEER_SB, (sb + 1) * PEER_SB)
        dr = acc_ref.shape[0] // PEER_STREAMS
        for k, vt_ref in enumerate(vt_refs):
            acc_ref[k * dr:(k + 1) * dr, cols] += _d(vt_ref[...], w_ref[sb], NN)

    act = activations(0)
    for sb in range(nsb):
        nxt = activations(sb + 1) if sb + 1 < nsb else None
        gates(sb, act)
        accumulate(sb)
        act = nxt

    @pl.when(e == pl.num_programs(1) - 1)
    def _():
        o_ref[...] = acc_ref[...].T


def _fold_body(k_ref, w_ref, o_ref):
    o_ref[...] = _mdot(k_ref[...], w_ref[...], NN, 6)


def peer_fold_keys(wq, keys):
    d = wq.shape[0]
    nhp, nk, dk = keys.shape
    wqt = wq.T.reshape(nhp, dk, d)
    return pl.pallas_call(
        _fold_body,
        grid=(nhp,),
        in_specs=[pl.BlockSpec((None, nk, dk), lambda i: (i, 0, 0)),
                  pl.BlockSpec((None, dk, d), lambda i: (i, 0, 0))],
        out_specs=pl.BlockSpec((nk, d), lambda i: (i, 0)),
        out_shape=jax.ShapeDtypeStruct((nhp * nk, d), F32),
        compiler_params=pltpu.CompilerParams(dimension_semantics=("arbitrary",), vmem_limit_bytes=VMEM_LIMIT),
        name="peer_fold_keys",
    )(keys, wqt)


def peer(hx, wk, u_bf, vt_bf):
    t, d = hx.shape
    tm = PEER_TM
    assert t % tm == 0
    wk_hi, wk_lo = _split(wk, 2)
    ne = u_bf.shape[0] // PEER_TE
    h = PEER_HEADS
    ns = PEER_STREAMS
    return pl.pallas_call(
        _peer_body,
        grid=(t // tm, ne),
        in_specs=[pl.BlockSpec((tm, d), lambda i, e: (i, 0), pipeline_mode=pl.Buffered(1)),
                  pl.BlockSpec(wk.shape, lambda i, e: (0, 0), pipeline_mode=pl.Buffered(1)),
                  pl.BlockSpec(wk.shape, lambda i, e: (0, 0), pipeline_mode=pl.Buffered(1)),
                  *[pl.BlockSpec((PEER_TE // ns, d), functools.partial(lambda i, e, k: (e * ns + k, 0), k=k))
                    for k in range(ns)],
                  *[pl.BlockSpec((None, d // ns, PEER_TE), functools.partial(lambda i, e, k: (e, k, 0), k=k))
                    for k in range(ns)]],
        out_specs=pl.BlockSpec((tm, d), lambda i, e: (i, 0)),
        out_shape=jax.ShapeDtypeStruct((t, d), F32),
        scratch_shapes=[pltpu.VMEM((2 * h * N_KEYS, tm), F32), pltpu.VMEM((2, PEER_TOPK, h, tm), F32),
                        pltpu.VMEM((tm, d), BF16), pltpu.VMEM((tm, d), BF16),
                        pltpu.VMEM((h, N_KEYS, tm), F32), pltpu.VMEM((h, N_KEYS, tm), F32),
                        pltpu.VMEM((h, N_KEYS, tm), BF16), pltpu.VMEM((h, N_KEYS, tm), BF16),
                        pltpu.VMEM((tm // PEER_SB, PEER_TE, PEER_SB), BF16), pltpu.VMEM((d, tm), F32)],
        compiler_params=pltpu.CompilerParams(
            dimension_semantics=("arbitrary", "arbitrary"), vmem_limit_bytes=VMEM_LIMIT),
        name="peer_dense",
    )(hx, wk_hi, wk_lo, *([u_bf] * ns), *([vt_bf] * ns))


LN_TM = 256


def _ln_mod_body(xs_ref, y_ref, mg_ref, mn_ref, lng_ref, lnb_ref, xo_ref, ho_ref, *, gate_col, mod_col, nct, nt):
    i = pl.program_id(0)
    d = xs_ref.shape[1]
    is_ctx = i < nct

    def pick(ref, col):
        return jnp.where(is_ctx, ref[1:2, col * d:(col + 1) * d], ref[0:1, col * d:(col + 1) * d])

    z = DN_ALPHA * xs_ref[...] + pick(mg_ref, gate_col) * y_ref[...]
    mu = jnp.mean(z, axis=-1, keepdims=True)
    zc = z - mu
    var = jnp.mean(zc * zc, axis=-1, keepdims=True)
    xn = zc * lax.rsqrt(var + LN_EPS) * lng_ref[...] + lnb_ref[...]
    xo_ref[...] = xn
    h = xn * (1.0 + pick(mn_ref, mod_col + 1)) + pick(mn_ref, mod_col)
    ho_ref[...] = jnp.where(i < nt, h, 0.0)


def ln_mod(xs, y, mod_gate, gate_col, mod_next, mod_col, ln_g, ln_b, n_ctx, pad_to):
    t, d = xs.shape
    tm = LN_TM
    assert t % tm == 0 and n_ctx % tm == 0 and pad_to % tm == 0 and pad_to >= t
    nt = t // tm
    row = lambda i: (jnp.minimum(i, nt - 1), 0)
    full = lambda a: pl.BlockSpec(a.shape, lambda i: (0, 0))
    return pl.pallas_call(
        functools.partial(_ln_mod_body, gate_col=gate_col, mod_col=mod_col, nct=n_ctx // tm, nt=nt),
        grid=(pad_to // tm,),
        in_specs=[pl.BlockSpec((tm, d), row), pl.BlockSpec((tm, d), row), full(mod_gate), full(mod_next),
                  pl.BlockSpec((1, d), lambda i: (0, 0)), pl.BlockSpec((1, d), lambda i: (0, 0))],
        out_specs=[pl.BlockSpec((tm, d), row), pl.BlockSpec((tm, d), lambda i: (i, 0))],
        out_shape=[jax.ShapeDtypeStruct((t, d), F32), jax.ShapeDtypeStruct((pad_to, d), F32)],
        compiler_params=pltpu.CompilerParams(dimension_semantics=("arbitrary",), vmem_limit_bytes=VMEM_LIMIT),
        name="ln_mod",
    )(xs, y, mod_gate, mod_next, ln_g.reshape(1, d), ln_b.reshape(1, d))


def _ln(x, g, b):
    mu = jnp.mean(x, -1, keepdims=True)
    xc = x - mu
    var = jnp.mean(xc * xc, -1, keepdims=True)
    return xc * lax.rsqrt(var + LN_EPS) * g + b


def _head_norm(h, nheads, eps):
    t = h.shape[0]
    hh = h.reshape(t, nheads, -1)
    mu = jnp.mean(hh, -1, keepdims=True)
    xc = hh - mu
    var = jnp.mean(xc * xc, -1, keepdims=True)
    return (xc * lax.rsqrt(var + eps)).reshape(t, -1)


def _pad_cols(w, n):
    return jnp.pad(w, ((0, 0), (0, n - w.shape[1])))


def _shift_rows(u, k):
    if k > 0:
        return jnp.concatenate([u[k:], jnp.zeros_like(u[:k])], axis=0)
    return jnp.concatenate([jnp.zeros_like(u[:-k]), u[:k]], axis=0)


def _grid_conv(u, w, b):
    s = u.shape[0]
    colid = (jnp.arange(s) % GRID_W)[:, None]
    out = jnp.zeros_like(u) + b
    for di in range(3):
        for dj in range(3):
            off = (di - 1) * GRID_W + (dj - 1)
            sh = _shift_rows(u, off) if off else u
            if dj == 0:
                sh = jnp.where(colid == 0, 0.0, sh)
            elif dj == 2:
                sh = jnp.where(colid == GRID_W - 1, 0.0, sh)
            out = out + sh * w[di, dj]
    return out


def _seq_conv(u, w, b):
    wc = w[1]
    return _shift_rows(u, -1) * wc[0] + u * wc[1] + _shift_rows(u, 1) * wc[2] + b


def _qshift(u):
    q = u.shape[1] // 4
    colid = (jnp.arange(u.shape[0]) % GRID_W)[:, None]
    left = jnp.where(colid == 0, 0.0, _shift_rows(u[:, :q], -1))
    right = jnp.where(colid == GRID_W - 1, 0.0, _shift_rows(u[:, q:2 * q], 1))
    up = _shift_rows(u[:, 2 * q:3 * q], -GRID_W)
    down = _shift_rows(u[:, 3 * q:], GRID_W)
    return jnp.concatenate([left, right, up, down], axis=1)


def _shift_seq(u):
    h = u.shape[1] // 2
    return jnp.concatenate([_shift_rows(u[:, :h], -1), _shift_rows(u[:, h:], 1)], axis=1)


def _mlstm_layer(h, n_ctx, w_in, b_in, conv_w, conv_b, hn_g, w_out):
    d = D_MODEL
    t = h.shape[0]
    qk_pre = matmul(h, w_in[:, :2 * d], b_in[:2 * d])
    v = matmul(h, w_in[:, 2 * d:3 * d], b_in[2 * d:3 * d], out_dtype=BF16)
    o = matmul(h, w_in[:, 3 * d:4 * d], b_in[3 * d:4 * d], act="sigmoid", out_dtype=BF16)
    g = matmul(h, _pad_cols(w_in[:, 4 * d:], LANES), jnp.pad(b_in[4 * d:], (0, LANES - 4 * MLSTM_HEADS)))
    g = g[:, :4 * MLSTM_HEADS].reshape(t, 4, MLSTM_HEADS)
    g = jnp.concatenate([g[:, :2], jax.nn.log_sigmoid(g[:, 2:])], axis=1)
    gh = jnp.transpose(g, (2, 0, 1))
    gcol = jnp.pad(gh, ((0, 0), (0, 0), (0, LANES - 4)))
    grow = jnp.pad(jnp.transpose(gh, (0, 2, 1)), ((0, 0), (0, 4), (0, 0)))
    qk = mlstm_conv(qk_pre, conv_w, conv_b, n_ctx)
    hf, hb = mlstm_scan(qk, v, gcol, grow, n_ctx)
    return mlstm_out(hf, hb, o, hn_g, w_out)


def _rwkv_layer(h, n_ctx, mu, w_rkv, w0, w1, w2, a0, a1, a2, g1, g2, k_k, k_a, r_k, lnx_g, lnx_b, w_out):
    d = D_MODEL
    t = h.shape[0]
    shifted = jnp.concatenate([_shift_seq(h[:n_ctx]), _qshift(h[n_ctx:])], axis=0)
    dx = shifted - h
    xm = [h + dx * mu[n] for n in range(6)]
    r = matmul(xm[0], w_rkv[0])
    k = matmul(xm[1], w_rkv[1])
    v = matmul(xm[2], w_rkv[2])
    lora = DECAY_LORA = w1.shape[-1]
    w1c = _pad_cols(jnp.concatenate([w1[0], w1[1]], axis=1), LANES)
    zpad = jnp.zeros((lora, d), F32)
    w2c = jnp.concatenate([jnp.concatenate([w2[0], zpad], axis=1),
                           jnp.concatenate([zpad, w2[1]], axis=1)], axis=0)
    wpre = matmul(jnp.tanh(matmul(xm[3], w1c)), w2c) + jnp.concatenate([w0[0], w0[1]])
    lw = -jnp.exp(-jax.nn.softplus(-wpre) - 0.5)
    a1c = _pad_cols(jnp.concatenate([a1[0], a1[1]], axis=1), LANES)
    a2c = jnp.concatenate([jnp.concatenate([a2[0], zpad], axis=1),
                           jnp.concatenate([zpad, a2[1]], axis=1)], axis=0)
    a = jax.nn.sigmoid(matmul(matmul(xm[4], a1c), a2c) + jnp.concatenate([a0[0], a0[1]]))
    glora = g1.shape[1]
    gpad = 2 * LANES
    gg = jax.nn.sigmoid(matmul(xm[5], _pad_cols(g1, gpad)))
    g = matmul(gg, jnp.pad(g2, ((0, gpad - glora), (0, 0))))
    kk = (k * k_k).reshape(t, RWKV_HEADS, RWKV_N)
    kk = (kk / jnp.maximum(jnp.sqrt(jnp.sum(kk * kk, -1, keepdims=True)), 1e-12)).reshape(t, d)
    a_f, a_b = a[:, :d], a[:, d:]
    kt_f = k * (1.0 + (a_f - 1.0) * k_a)
    kt_b = k * (1.0 + (a_b - 1.0) * k_a)
    yf, yb = rwkv_scan(r, v, kk, lw[:, :d], kt_f, kk * a_f, lw[:, d:], kt_b, kk * a_b, n_ctx)
    yn = _head_norm(yf + yb, RWKV_HEADS, RWKV_GN_EPS) * lnx_g + lnx_b
    kbar = 0.5 * (kt_f + kt_b)
    bonus = jnp.sum((r * kbar * r_k).reshape(t, RWKV_HEADS, RWKV_N), -1, keepdims=True) \
        * v.reshape(t, RWKV_HEADS, RWKV_N)
    return matmul((yn + bonus.reshape(t, d)) * g, w_out)


def _forward(x, c, ctx, c_ctx, ada_w, ada_b, ln_g, ln_b,
             ml_w_in, ml_b_in, ml_conv_w, ml_conv_b, ml_hn_g, ml_w_out,
             rw_mu, rw_w_rkv, rw_w0, rw_w1, rw_w2, rw_a0, rw_a1, rw_a2, rw_g1, rw_g2,
             rw_k_k, rw_k_a, rw_r_k, rw_lnx_g, rw_lnx_b, rw_w_out,
             pk_wq, pk_keys, pk_u, pk_v):
    d = D_MODEL
    n_ctx = ctx.shape[1]
    n_lat = x.shape[1]
    xs = jnp.concatenate([ctx[0], x[0]], axis=0)
    t = xs.shape[0]
    t_pad = -(-t // PEER_TM) * PEER_TM
    s_in = jnp.zeros((8, d), F32).at[0].set(jax.nn.silu(c[0])).at[1].set(jax.nn.silu(c_ctx))
    depth = ada_w.shape[0]
    mods = [matmul(s_in, ada_w[i], ada_b[i], passes=3) for i in range(depth)]
    is_ctx = (jnp.arange(t) < n_ctx)[:, None]
    m0 = [jnp.where(is_ctx, mods[0][1, n * d:(n + 1) * d], mods[0][0, n * d:(n + 1) * d]) for n in range(2)]
    h = xs * (1.0 + m0[1]) + m0[0]
    for i in range(depth):
        j = i // 2
        if i % 2 == 0:
            y = _mlstm_layer(h, n_ctx, ml_w_in[j], ml_b_in[j], ml_conv_w[j], ml_conv_b[j],
                             ml_hn_g[j], ml_w_out[j])
        else:
            y = _rwkv_layer(h, n_ctx, rw_mu[j], rw_w_rkv[j], rw_w0[j], rw_w1[j], rw_w2[j],
                            rw_a0[j], rw_a1[j], rw_a2[j], rw_g1[j], rw_g2[j], rw_k_k[j],
                            rw_k_a[j], rw_r_k[j], rw_lnx_g[j], rw_lnx_b[j], rw_w_out[j])
        xs, h = ln_mod(xs, y, mods[i], 2, mods[i], 3, ln_g[i, 0], ln_b[i, 0], n_ctx, t_pad)
        wk = peer_fold_keys(pk_wq[i], pk_keys[i].reshape(2 * PEER_HEADS, N_KEYS, PEER_DQ // 2))
        vt = jnp.swapaxes(pk_v[i].astype(BF16).reshape(-1, PEER_TE, d), 1, 2)
        y = peer(h, wk, pk_u[i].astype(BF16), vt)
        xs, h = ln_mod(xs, y, mods[i], 5, mods[min(i + 1, depth - 1)], 0, ln_g[i, 1], ln_b[i, 1], n_ctx, t)
    return xs[n_ctx:][None]


def kernel(x, c, ctx, c_ctx, ada_w, ada_b, ln_g, ln_b, ml_w_in, ml_b_in, ml_conv_w, ml_conv_b, ml_hn_g, ml_w_out, rw_mu, rw_w_rkv, rw_w0, rw_w1, rw_w2, rw_a0, rw_a1, rw_a2, rw_g1, rw_g2, rw_k_k, rw_k_a, rw_r_k, rw_lnx_g, rw_lnx_b, rw_w_out, pk_wq, pk_keys, pk_u, pk_v):
    return _forward(x, c, ctx, c_ctx, ada_w, ada_b, ln_g, ln_b,
                    ml_w_in, ml_b_in, ml_conv_w, ml_conv_b, ml_hn_g, ml_w_out,
                    rw_mu, rw_w_rkv, rw_w0, rw_w1, rw_w2, rw_a0, rw_a1, rw_a2, rw_g1, rw_g2,
                    rw_k_k, rw_k_a, rw_r_k, rw_lnx_g, rw_lnx_b, rw_w_out,
                    pk_wq, pk_keys, pk_u, pk_v)
```

```python
import functools

import jax
import jax.numpy as jnp
from jax import lax
from jax.experimental import pallas as pl
from jax.experimental.pallas import tpu as pltpu

F32 = jnp.float32
BF16 = jnp.bfloat16

D_MODEL = 1024
DEPTH = 4
GRID_W = 64
N_MOD = 6
DN_ALPHA = (2.0 * DEPTH) ** 0.25
LN_EPS = 1e-5

MLSTM_HEADS = 4
MLSTM_DH = D_MODEL // MLSTM_HEADS
MLSTM_CHUNK = 128
M_INIT = -1e30

RWKV_N = 64
RWKV_HEADS = D_MODEL // RWKV_N
RWKV_CHUNK = 64
RWKV_GN_EPS = 64e-5

N_KEYS = 128
PEER_HEADS = 8
PEER_DQ = 256
PEER_TOPK = 16
PEER_I_BLOCK = 8
PEER_TE = PEER_I_BLOCK * N_KEYS
PEER_TM = 512
PEER_SB = 256

LANES = 128
VMEM_LIMIT = 62 * 1024 * 1024

NN = ((1,), (0,))
NT = ((1,), (1,))
TN = ((0,), (0,))


def _split(x, n):
    parts = []
    r = x.astype(F32)
    for i in range(n):
        p = r.astype(BF16)
        parts.append(p)
        if i + 1 < n:
            r = r - p.astype(F32)
    return parts


def _d(a, b, dims):
    return lax.dot_general(a, b, (dims, ((), ())), preferred_element_type=F32)


def _mdot(a, b, dims, passes):
    if passes == 1:
        return _d(a.astype(BF16), b.astype(BF16), dims)
    if passes == 3:
        a0, a1 = _split(a, 2)
        b0, b1 = _split(b, 2)
        return (_d(a0, b1, dims) + _d(a1, b0, dims)) + _d(a0, b0, dims)
    a0, a1, a2 = _split(a, 3)
    b0, b1, b2 = _split(b, 3)
    lo = (_d(a0, b2, dims) + _d(a2, b0, dims)) + _d(a1, b1, dims)
    mid = _d(a0, b1, dims) + _d(a1, b0, dims)
    return (lo + mid) + _d(a0, b0, dims)


def _dot_01_lhs(m01, x):
    mb = m01.astype(BF16)
    x0, x1, x2 = _split(x, 3)
    return (_d(mb, x2, NN) + _d(mb, x1, NN)) + _d(mb, x0, NN)


def _dot_01_rhs(x, m01):
    mb = m01.astype(BF16)
    x0, x1, x2 = _split(x, 3)
    return (_d(x2, mb, NN) + _d(x1, mb, NN)) + _d(x0, mb, NN)


def _pick(n, cands):
    for c in cands:
        if n % c == 0:
            return c
    raise ValueError(f"no tile for {n}")


_ACTS = {None: lambda x: x, "sigmoid": jax.nn.sigmoid, "tanh": jnp.tanh,
         "logdecay": lambda x: -(2.718281828459045 ** -0.5) * jax.nn.sigmoid(x)}


def _mm_body(x_ref, w_ref, b_ref, o_ref, *, passes, act):
    o_ref[...] = _ACTS[act](_mdot(x_ref[...], w_ref[...], NN, passes) + b_ref[...]).astype(o_ref.dtype)


def matmul(x, w, b=None, *, passes=1, act=None, out_dtype=F32):
    m, k = x.shape
    n = w.shape[1]
    assert n % LANES == 0 and w.shape[0] == k
    tm = m if m <= 1024 else _pick(m, (640, 512, 384, 256, 128))
    tn = _pick(n, (1024, 768, 640, 512, 384, 256, 128))
    if b is None:
        b = jnp.zeros((n,), F32)
    if passes == 1:
        w = w.astype(BF16)
    return pl.pallas_call(
        functools.partial(_mm_body, passes=passes, act=act),
        grid=(m // tm, n // tn),
        in_specs=[pl.BlockSpec((tm, k), lambda i, j: (i, 0)),
                  pl.BlockSpec((k, tn), lambda i, j: (0, j)),
                  pl.BlockSpec((1, tn), lambda i, j: (0, j))],
        out_specs=pl.BlockSpec((tm, tn), lambda i, j: (i, j)),
        out_shape=jax.ShapeDtypeStruct((m, n), out_dtype),
        compiler_params=pltpu.CompilerParams(
            dimension_semantics=("arbitrary", "arbitrary"), vmem_limit_bytes=VMEM_LIMIT),
        name="proj_matmul",
    )(x, w, b.reshape(1, n).astype(F32))


CONV_TM = 256
CONV_CB = 512


def _conv_body(cur_ref, prev_ref, next_ref, w_ref, b_ref, o_ref, *, nct, nt, q_blocks, q_scale):
    i = pl.program_id(0)
    j = pl.program_id(1)
    is_ctx = i < nct
    tm = cur_ref.shape[0]
    above = jnp.where(is_ctx | (i == nct), 0.0, prev_ref[...])
    below = jnp.where(is_ctx | (i == nt - 1), 0.0, next_ref[...])
    ext = jnp.concatenate([above, cur_ref[...], below], axis=0)
    n = ext.shape[0]
    colid = lax.broadcasted_iota(jnp.int32, ext.shape, 0) % GRID_W
    left = jnp.where(is_ctx | (colid != 0), pltpu.roll(ext, 1, 0), 0.0)
    right = jnp.where(is_ctx | (colid != GRID_W - 1), pltpu.roll(ext, n - 1, 0), 0.0)
    w = w_ref[...]
    acc = jnp.zeros((tm, ext.shape[1]), F32) + b_ref[...]
    for di in range(3):
        rows = slice(di * GRID_W, di * GRID_W + tm)
        tap = left[rows] * w[3 * di:3 * di + 1] + ext[rows] * w[3 * di + 1:3 * di + 2] \
            + right[rows] * w[3 * di + 2:3 * di + 3]
        acc = acc + (tap if di == 1 else jnp.where(is_ctx, 0.0, tap))
    y = acc * jax.nn.sigmoid(acc)
    o_ref[...] = (y * jnp.where(j < q_blocks, q_scale, 1.0)).astype(o_ref.dtype)


def mlstm_conv(qk_pre, conv_w, conv_b, n_ctx):
    t, c = qk_pre.shape
    tm, cb = CONV_TM, CONV_CB
    assert t % tm == 0 and n_ctx % tm == 0 and tm % GRID_W == 0 and c % (2 * cb) == 0
    r = tm // GRID_W
    nt, nu = t // tm, t // GRID_W
    return pl.pallas_call(
        functools.partial(_conv_body, nct=n_ctx // tm, nt=nt, q_blocks=c // (2 * cb), q_scale=MLSTM_DH ** -0.5),
        grid=(nt, c // cb),
        in_specs=[pl.BlockSpec((tm, cb), lambda i, j: (i, j)),
                  pl.BlockSpec((GRID_W, cb), lambda i, j: (jnp.maximum(i * r - 1, 0), j)),
                  pl.BlockSpec((GRID_W, cb), lambda i, j: (jnp.minimum((i + 1) * r, nu - 1), j)),
                  pl.BlockSpec((9, cb), lambda i, j: (0, j)),
                  pl.BlockSpec((1, cb), lambda i, j: (0, j))],
        out_specs=pl.BlockSpec((tm, cb), lambda i, j: (i, j)),
        out_shape=jax.ShapeDtypeStruct((t, c), BF16),
        compiler_params=pltpu.CompilerParams(
            dimension_semantics=("arbitrary", "arbitrary"), vmem_limit_bytes=VMEM_LIMIT),
        name="mlstm_conv",
    )(qk_pre, qk_pre, qk_pre, conv_w.reshape(9, c), conv_b.reshape(1, c))


def _mlstm_out_body(hf_ref, hb_ref, o_ref, g_ref, w_ref, y_ref):
    h = hf_ref[...] + hb_ref[...]
    parts = []
    for a in range(MLSTM_HEADS):
        x = h[:, a * MLSTM_DH:(a + 1) * MLSTM_DH]
        mu = jnp.mean(x, axis=-1, keepdims=True)
        xc = x - mu
        var = jnp.mean(xc * xc, axis=-1, keepdims=True)
        parts.append(xc * lax.rsqrt(var + LN_EPS))
    hn = jnp.concatenate(parts, axis=1)
    z = o_ref[...].astype(F32) * hn * g_ref[...]
    y_ref[...] = _d(z.astype(BF16), w_ref[...], NN)


def mlstm_out(hf, hb, o, hn_g, w_out):
    t, d = hf.shape
    tm = LN_TM
    row = pl.BlockSpec((tm, d), lambda i: (i, 0))
    return pl.pallas_call(
        _mlstm_out_body,
        grid=(t // tm,),
        in_specs=[row, row, row, pl.BlockSpec((1, d), lambda i: (0, 0)), pl.BlockSpec((d, d), lambda i: (0, 0))],
        out_specs=row,
        out_shape=jax.ShapeDtypeStruct((t, d), F32),
        compiler_params=pltpu.CompilerParams(dimension_semantics=("arbitrary",), vmem_limit_bytes=VMEM_LIMIT),
        name="mlstm_out",
    )(hf, hb, o, hn_g.reshape(1, d), w_out.astype(BF16))


def _mlstm_body(qf, kf, vf, gcf, grf, qb, kb, vb, gcb, grb, hf_ref, hb_ref, c_ref, n_ref, m_ref):
    @pl.when(pl.program_id(1) == 0)
    def _():
        c_ref[...] = jnp.zeros(c_ref.shape, F32)
        n_ref[...] = jnp.zeros(n_ref.shape, F32)
        m_ref[...] = jnp.full(m_ref.shape, M_INIT, F32)

    ds = (0, 1)
    each = lambda f, *ls: [f(*xs) for xs in zip(*ls)]
    q, k, v = [qf[...], qb[...]], [kf[...], kb[...]], [vf[...], vb[...]]
    gc, gr = [gcf[...], gcb[...]], [grf[...], grb[...]]
    L = q[0].shape[0]
    row = lax.broadcasted_iota(jnp.int32, (L, L), 0)
    col = lax.broadcasted_iota(jnp.int32, (L, L), 1)
    seen = (col <= row, col >= row)
    tri = [jnp.where(m, 1.0, 0.0) for m in seen]
    b_col = [_dot_01_lhs(tri[d], gc[d])[:, 2 + d:3 + d] for d in ds]
    b_row = [_dot_01_rhs(gr[d], tri[1 - d])[2 + d:3 + d, :] for d in ds]
    ig_col = [gc[d][:, d:d + 1] for d in ds]
    ig_row = [gr[d][d:d + 1, :] for d in ds]
    m_st = [m_ref[d, 0:1, 0:1] for d in ds]
    c_st = [c_ref[d] for d in ds]
    n_st = [n_ref[d] for d in ds]

    dlog = [jnp.where(seen[d], b_col[d] - b_row[d] + ig_row[d], -jnp.inf) for d in ds]
    m_inter = each(jnp.add, b_col, m_st)
    m_t = each(lambda mi, dl: jnp.maximum(mi, jnp.max(dl, axis=1, keepdims=True)), m_inter, dlog)
    qb16, kb16, vb16 = q, k, v
    q, k = (each(lambda x: x.astype(F32), a) for a in (q, k))
    qk = each(lambda a, b: _d(a, b, NT), qb16, kb16)
    s = each(lambda x, dl, mt: x * jnp.exp(dl - mt), qk, dlog, m_t)
    dec = each(lambda mi, mt: jnp.exp(mi - mt), m_inter, m_t)
    sv = each(lambda a, b: _d(a.astype(BF16), b, NN), s, vb16)
    qc = each(lambda a, b: _d(a, b.astype(BF16), NN), qb16, c_st)
    num = each(lambda a, dc, b: a + dc * b, sv, dec, qc)
    den = each(lambda x, dc, qq, nn: jnp.sum(x, axis=1, keepdims=True) + dc * jnp.sum(qq * nn, axis=1, keepdims=True),
               s, dec, q, n_st)
    h = each(lambda nu, de, mt: nu / jnp.maximum(jnp.abs(de), jnp.exp(-mt)), num, den, m_t)
    hf_ref[...] = h[0]
    hb_ref[...] = h[1]

    b_last = [b_col[0][L - 1:L, :], b_col[1][0:1, :]]
    w_c = each(lambda bl, bc, ic: bl - bc + ic, b_last, b_col, ig_col)
    m_new = each(lambda bl, ms, w: jnp.maximum(bl + ms, jnp.max(w, axis=0, keepdims=True)), b_last, m_st, w_c)
    a_c = each(lambda w, mn: jnp.exp(w - mn), w_c, m_new)
    g_prev = each(lambda bl, ms, mn: jnp.exp(bl + ms - mn), b_last, m_st, m_new)
    ak = each(jnp.multiply, a_c, k)
    kv = each(lambda a, b: _d(a.T.astype(BF16), b, NN), ak, vb16)
    for d in ds:
        c_ref[d] = g_prev[d] * c_st[d] + kv[d]
        n_ref[d] = g_prev[d] * n_st[d] + jnp.sum(ak[d], axis=0, keepdims=True)
        m_ref[d] = jnp.broadcast_to(m_new[d], m_ref.shape[1:])


def _bwd_chunk(c, nc0, nc):
    return jnp.where(c < nc0, nc0 - 1 - c, nc - 1 - (c - nc0))


def mlstm_scan(qk, v, gcol, grow, n_ctx):
    t = v.shape[0]
    L, dh = MLSTM_CHUNK, MLSTM_DH
    nc, nc0 = t // L, n_ctx // L
    fwd = lambda h, c: (c, h)
    bwd = lambda h, c: (_bwd_chunk(c, nc0, nc), h)
    fwd_k = lambda h, c: (c, MLSTM_HEADS + h)
    bwd_k = lambda h, c: (_bwd_chunk(c, nc0, nc), MLSTM_HEADS + h)
    qkv = lambda im: pl.BlockSpec((L, dh), im)
    gc_spec = lambda f: pl.BlockSpec((None, L, LANES), lambda h, c: (h, f(c), 0))
    gr_spec = lambda f: pl.BlockSpec((None, 8, L), lambda h, c: (h, 0, f(c)))
    idf = lambda c: c
    idb = lambda c: _bwd_chunk(c, nc0, nc)
    return pl.pallas_call(
        _mlstm_body,
        grid=(MLSTM_HEADS, nc),
        in_specs=[qkv(fwd), qkv(fwd_k), qkv(fwd), gc_spec(idf), gr_spec(idf),
                  qkv(bwd), qkv(bwd_k), qkv(bwd), gc_spec(idb), gr_spec(idb)],
        out_specs=[qkv(fwd), qkv(bwd)],
        out_shape=[jax.ShapeDtypeStruct((t, D_MODEL), F32)] * 2,
        scratch_shapes=[pltpu.VMEM((2, dh, dh), F32), pltpu.VMEM((2, 1, dh), F32),
                        pltpu.VMEM((2, 8, LANES), F32)],
        compiler_params=pltpu.CompilerParams(
            dimension_semantics=("arbitrary", "arbitrary"), vmem_limit_bytes=VMEM_LIMIT),
        name="mlstm_scan",
    )(qk, qk, v, gcol, grow, qk, qk, v, gcol, grow)


RW_PASSES = 1
RW_CHUNKS_PER_STEP = 4


def _stack2(x, lane_head):
    return jnp.concatenate([jnp.where(lane_head == 0, x, 0.0), jnp.where(lane_head == 1, x, 0.0)], axis=0)


def _unstack2(x):
    L = x.shape[0] // 2
    return x[:L] + x[L:]


def _rwkv_chunks(chains, k_k, k_a):
    L = chains[0][0].shape[0]
    n2 = 2 * L
    p = RW_PASSES
    ds = [c[5] for c in chains]
    each = lambda f, *ls: [f(*xs) for xs in zip(*ls)]
    dot = lambda dims: (lambda a, b: _mdot(a, b, dims, p))
    lw, r, k_raw, v, a = ([c[i] for c in chains] for i in range(5))

    lane_r = lax.broadcasted_iota(jnp.int32, (LANES, LANES), 0) // RWKV_N
    lane_c = lax.broadcasted_iota(jnp.int32, (LANES, LANES), 1) // RWKV_N
    head_ones = jnp.where(lane_r == lane_c, 1.0, 0.0)
    kkr = each(lambda x: x * k_k, k_raw)
    ss = each(lambda x: _dot_01_rhs(x * x, head_ones), kkr)
    kap = each(lambda x, q: x / jnp.maximum(jnp.sqrt(q), 1e-12), kkr, ss)
    alp = each(jnp.multiply, kap, a)
    k = each(lambda x, y: x * (1.0 + (y - 1.0) * k_a), k_raw, a)

    row = lax.broadcasted_iota(jnp.int32, (L, L), 0)
    col = lax.broadcasted_iota(jnp.int32, (L, L), 1)
    tris = (jnp.where(col <= row, 1.0, 0.0), jnp.where(col >= row, 1.0, 0.0))
    lp = [_dot_01_lhs(tris[d], x) for d, x in zip(ds, lw)]
    lp_end = [x[0:1, :] if d else x[L - 1:L, :] for d, x in zip(ds, lp)]
    e_neg = each(lambda x: jnp.exp(-x), lp)
    e_end = each(lambda x, xe: jnp.exp(xe - x), lp, lp_end)
    kap_t = each(lambda x, y, z: x * jnp.exp(y - z), kap, lp, lw)
    r_t = each(lambda x, y: x * jnp.exp(y), r, lp)
    k_h = each(jnp.multiply, k, e_neg)
    a_h = each(jnp.multiply, alp, e_neg)
    k_e = each(jnp.multiply, k, e_end)
    a_e = each(jnp.multiply, alp, e_end)

    lane_head = lax.broadcasted_iota(jnp.int32, (L, LANES), 1) // RWKV_N
    st = lambda x: _stack2(x, lane_head)
    kap_s, r_s, v_s, k_s, a_s = (each(st, x) for x in (kap_t, r_t, v, k_e, a_e))
    rhs_k = each(lambda x: jnp.concatenate([x, x], axis=0), k_h)
    rhs_a = each(lambda x: jnp.concatenate([x, x], axis=0), a_h)

    row2 = lax.broadcasted_iota(jnp.int32, (n2, n2), 0)
    col2 = lax.broadcasted_iota(jnp.int32, (n2, n2), 1)
    same_head = (row2 // L) == (col2 // L)
    strict = (same_head & (col2 < row2), same_head & (col2 > row2))
    incl = (same_head & (col2 <= row2), same_head & (col2 >= row2))
    zero = jnp.zeros((n2, n2), F32)
    masked = lambda masks: (lambda d, x: jnp.where(masks[d], x, zero))
    n_ka = each(masked(strict), ds, each(dot(NT), kap_s, rhs_a))
    m_kk = each(masked(strict), ds, each(dot(NT), kap_s, rhs_k))
    a_rk = each(masked(incl), ds, each(dot(NT), r_s, rhs_k))
    a_ra = each(masked(incl), ds, each(dot(NT), r_s, rhs_a))

    b16 = (row2 // 16) == (col2 // 16)
    b32 = (row2 // 32) == (col2 // 32)
    eye = jnp.where(row2 == col2, 1.0, 0.0)
    n16 = each(lambda x: jnp.where(b16, x, zero), n_ka)
    n_2 = each(dot(NN), n16, n16)
    n_4 = each(dot(NN), n_2, n_2)
    n_8 = each(dot(NN), n_4, n_4)
    inv = each(lambda x: eye - x, n16)
    for pw in (n_2, n_4, n_8):
        inv = each(jnp.add, inv, each(dot(NN), inv, pw))
    for sel in (lambda x: jnp.where(b32 & ~b16, x, zero), lambda x: jnp.where(b32, zero, x)):
        t1 = each(dot(NN), inv, each(sel, n_ka))
        inv = each(jnp.subtract, inv, each(dot(NN), t1, inv))

    mv = each(dot(NN), m_kk, v_s)
    w1u0 = each(dot(NN), inv, each(lambda x, y: jnp.concatenate([x, y], axis=1), kap_s, mv))
    ar = each(dot(NN), a_ra, w1u0)
    av = each(dot(NN), a_rk, v_s)
    r2 = each(lambda x, y: _unstack2(x - y[:, :LANES]), r_s, ar)
    y0 = each(lambda x, y: _unstack2(x - y[:, LANES:]), av, ar)
    w1_t = each(lambda x: x[:, :LANES].T, w1u0)
    u0_t = each(lambda x: x[:, LANES:].T, w1u0)
    v_t = each(lambda x: x.T, v_s)
    wa = each(dot(NN), w1_t, a_s)
    vk = each(dot(NN), v_t, k_s)
    ua = each(dot(NN), u0_t, a_s)
    g = each(lambda xe, x: jnp.where(row2 == col2, jnp.exp(xe), zero) - x, lp_end, wa)
    b = each(jnp.subtract, vk, ua)
    return list(zip(y0, r2, g, b))


def _rwkv_body(lwf, rf, kf, vf, af, lwb, rb, kb, vb, ab, kk_ref, ka_ref, yf_ref, yb_ref, s_ref):
    @pl.when(pl.program_id(1) == 0)
    def _():
        s_ref[...] = jnp.zeros(s_ref.shape, F32)

    L = RWKV_CHUNK
    n = RW_CHUNKS_PER_STEP
    rows = lambda j: slice(j * L, (j + 1) * L)
    refs = ((lwf, rf, kf, vf, af), (lwb, rb, kb, vb, ab))
    visit = [(d, j if d == 0 else n - 1 - j) for j in range(n) for d in (0, 1)]
    pre = _rwkv_chunks([tuple(ref[rows(j), :] for ref in refs[d]) + (d,) for d, j in visit],
                       kk_ref[...], ka_ref[...])
    y_refs = (yf_ref, yb_ref)
    s = [s_ref[0], s_ref[1]]
    for (d, j), (y0, r2, g, b) in zip(visit, pre):
        y_refs[d][rows(j), :] = y0 + _mdot(r2, s[d], NT, RW_PASSES)
        s[d] = _mdot(s[d], g, NN, RW_PASSES) + b
    s_ref[0] = s[0]
    s_ref[1] = s[1]


def rwkv_scan(r, k, v, lw, a, k_k, k_a, n_ctx):
    t, d = r.shape
    L = RWKV_CHUNK
    blk = RW_CHUNKS_PER_STEP * L
    assert 2 * L == LANES and t % blk == 0 and n_ctx % blk == 0
    nc, nc0 = t // blk, n_ctx // blk
    nh = d // LANES
    fwd = pl.BlockSpec((blk, LANES), lambda h, c: (c, h))
    bwd = pl.BlockSpec((blk, LANES), lambda h, c: (_bwd_chunk(c, nc0, nc), h))
    bwd2 = pl.BlockSpec((blk, LANES), lambda h, c: (_bwd_chunk(c, nc0, nc), nh + h))
    par = pl.BlockSpec((1, LANES), lambda h, c: (0, h))
    return pl.pallas_call(
        _rwkv_body,
        grid=(nh, nc),
        in_specs=[fwd] * 5 + [bwd2, bwd, bwd, bwd, bwd2, par, par],
        out_specs=[fwd, bwd],
        out_shape=[jax.ShapeDtypeStruct((t, d), F32)] * 2,
        scratch_shapes=[pltpu.VMEM((2, LANES, LANES), F32)],
        compiler_params=pltpu.CompilerParams(
            dimension_semantics=("arbitrary", "arbitrary"), vmem_limit_bytes=VMEM_LIMIT),
        name="rwkv7_scan",
    )(lw, r, k, v, a, lw, r, k, v, a, k_k.reshape(1, d), k_a.reshape(1, d))


MIX_TM = 256


def _mix_body(cur_ref, prev_ref, next_ref, mu_ref, *o_refs, nct, nt):
    i = pl.program_id(0)
    is_ctx = i < nct
    cur = cur_ref[...]
    tm, d = cur.shape
    qd = d // 4
    above = jnp.where(is_ctx | (i == nct), 0.0, prev_ref[...])
    below = jnp.where(is_ctx | (i == nt - 1), 0.0, next_ref[...])
    ext = jnp.concatenate([above, cur, below], axis=0)
    n = ext.shape[0]
    colid = lax.broadcasted_iota(jnp.int32, (n, 2 * qd), 0) % GRID_W
    left = jnp.where(is_ctx | (colid != 0), pltpu.roll(ext[:, :2 * qd], 1, 0), 0.0)
    colid3 = lax.broadcasted_iota(jnp.int32, (n, 3 * qd), 0) % GRID_W
    right = jnp.where(is_ctx | (colid3 != GRID_W - 1), pltpu.roll(ext[:, qd:], n - 1, 0), 0.0)
    mid = slice(GRID_W, GRID_W + tm)
    sh = jnp.concatenate([
        left[mid, :qd],
        jnp.where(is_ctx, left[mid, qd:], right[mid, :qd]),
        jnp.where(is_ctx, right[mid, qd:2 * qd], ext[0:tm, 2 * qd:3 * qd]),
        jnp.where(is_ctx, right[mid, 2 * qd:], ext[2 * GRID_W:2 * GRID_W + tm, 3 * qd:])], axis=1)
    dx = sh - cur
    for b, o_ref in enumerate(o_refs):
        o_ref[...] = (cur + dx * mu_ref[b:b + 1, :]).astype(o_ref.dtype)


def rwkv_mix(h, mu, n_ctx):
    t, d = h.shape
    tm = MIX_TM
    assert t % tm == 0 and n_ctx % tm == 0 and tm % GRID_W == 0
    r = tm // GRID_W
    nt, nu = t // tm, t // GRID_W
    nb = mu.shape[0]
    return pl.pallas_call(
        functools.partial(_mix_body, nct=n_ctx // tm, nt=nt),
        grid=(nt,),
        in_specs=[pl.BlockSpec((tm, d), lambda i: (i, 0)),
                  pl.BlockSpec((GRID_W, d), lambda i: (jnp.maximum(i * r - 1, 0), 0)),
                  pl.BlockSpec((GRID_W, d), lambda i: (jnp.minimum((i + 1) * r, nu - 1), 0)),
                  pl.BlockSpec((nb, d), lambda i: (0, 0))],
        out_specs=[pl.BlockSpec((tm, d), lambda i: (i, 0))] * nb,
        out_shape=[jax.ShapeDtypeStruct((t, d), BF16)] * nb,
        compiler_params=pltpu.CompilerParams(dimension_semantics=("arbitrary",), vmem_limit_bytes=VMEM_LIMIT),
        name="rwkv_mix",
    )(h, h, h, mu)


def _rwkv_out_body(yf_ref, yb_ref, r_ref, k_ref, v_ref, g_ref, af_ref, ab_ref, p_ref, e_ref, et_ref, w_ref, o_ref):
    e, et = e_ref[...], et_ref[...]
    head_sum = lambda x: _dot_01_rhs(_dot_01_rhs(x, e), et)
    k_a, r_k, gain, bias = (p_ref[n:n + 1, :] for n in range(4))
    y = yf_ref[...] + yb_ref[...]
    yc = y - head_sum(y) * (1.0 / RWKV_N)
    var = head_sum(yc * yc) * (1.0 / RWKV_N)
    yn = yc * lax.rsqrt(var + RWKV_GN_EPS) * gain + bias
    kbar = k_ref[...] * (1.0 + (0.5 * (af_ref[...] + ab_ref[...]) - 1.0) * k_a)
    bonus = head_sum(r_ref[...] * kbar * r_k) * v_ref[...]
    z = (yn + bonus) * g_ref[...]
    o_ref[...] = _d(z.astype(BF16), w_ref[...], NN)


def rwkv_out(yf, yb, r, k, v, g, a, k_a, r_k, lnx_g, lnx_b, w_out):
    t, d = yf.shape
    tm = LN_TM
    nh = d // RWKV_N
    row = pl.BlockSpec((tm, d), lambda i: (i, 0))
    const = lambda shape: pl.BlockSpec(shape, lambda i: (0, 0))
    e = (jnp.arange(d)[:, None] // RWKV_N == jnp.arange(LANES)[None, :]).astype(BF16)
    params = jnp.stack([k_a, r_k, lnx_g, lnx_b])
    return pl.pallas_call(
        _rwkv_out_body,
        grid=(t // tm,),
        in_specs=[row] * 6 + [row, pl.BlockSpec((tm, d), lambda i: (i, 1)),
                  const((4, d)), const((d, LANES)), const((LANES, d)), const((d, d))],
        out_specs=row,
        out_shape=jax.ShapeDtypeStruct((t, d), F32),
        compiler_params=pltpu.CompilerParams(dimension_semantics=("arbitrary",), vmem_limit_bytes=VMEM_LIMIT),
        name="rwkv_out",
    )(yf, yb, r, k, v, g, a, a, params, e, e.T, w_out.astype(BF16))


def _peer_stats(hx_ref, wkh_ref, wkl_ref, sc_ref, top_ref, hxb_ref, hxl_ref, n_ref, f0_ref, r1_ref, e1_ref):
    tm = hx_ref.shape[0]
    hx_hi, hx_lo = _split(hx_ref[...], 2)
    hxb_ref[...] = hx_hi
    hxl_ref[...] = hx_lo
    rows_per = 2 * LANES

    def scores(c, carry):
        rows = pl.ds(pl.multiple_of(c * rows_per, rows_per), rows_per)
        wk_hi = wkh_ref[rows, :]
        sc_ref[rows, :] = (_d(wk_hi, hxl_ref[...], NT) + _d(wkl_ref[rows, :], hxb_ref[...], NT)) \
            + _d(wk_hi, hxb_ref[...], NT)
        return carry

    lax.fori_loop(0, wkh_ref.shape[0] // rows_per, scores, 0)

    def block(tb, carry):
        lanes = pl.ds(pl.multiple_of(tb * LANES, LANES), LANES)
        neg = jnp.full((N_KEYS, LANES), -jnp.inf, F32)
        for h in range(PEER_HEADS):
            for p in range(2):
                cur = sc_ref[(2 * h + p) * N_KEYS:(2 * h + p + 1) * N_KEYS, lanes]
                rank = jnp.full((N_KEYS, LANES), float(PEER_TOPK), F32)
                for a in range(PEER_TOPK):
                    m = jnp.max(cur, axis=0, keepdims=True)
                    top_ref[p, a, h:h + 1, lanes] = m
                    hit = cur >= m
                    cur = jnp.where(hit, neg, cur)
                    if p == 1:
                        rank = jnp.where(hit, float(a), rank)
                if p == 1:
                    r1_ref[h, :, lanes] = rank.astype(BF16)
        top0 = [top_ref[0, a, :, lanes] for a in range(PEER_TOPK)]
        top1 = [top_ref[1, b, :, lanes] for b in range(PEER_TOPK)]
        cands = [top0[a] + top1[b]
                 for a in range(PEER_TOPK) for b in range(PEER_TOPK) if (a + 1) * (b + 1) <= PEER_TOPK]
        c_max = cands[0]
        z = jnp.zeros_like(c_max)
        tau = c_max
        for a in range(PEER_TOPK):
            tau = functools.reduce(jnp.maximum, cands)
            z = z + jnp.exp(tau - c_max)
            cands = [jnp.where(cd >= tau, -jnp.inf, cd) for cd in cands]
        inv_z = 1.0 / z
        for h in range(PEER_HEADS):
            s0 = sc_ref[2 * h * N_KEYS:(2 * h + 1) * N_KEYS, lanes]
            n = jnp.zeros((N_KEYS, LANES), F32)
            for b in range(PEER_TOPK):
                n = jnp.where(s0 + top1[b][h:h + 1] >= tau[h:h + 1], float(b + 1), n)
            n_ref[h, :, lanes] = n
            f0_ref[h, :, lanes] = jnp.exp(s0 - top0[0][h:h + 1]) * inv_z[h:h + 1]
            e1_ref[h, :, lanes] = jnp.exp(sc_ref[(2 * h + 1) * N_KEYS:(2 * h + 2) * N_KEYS, lanes] - top1[0][h:h + 1]).astype(BF16)
        return carry

    lax.fori_loop(0, tm // LANES, block, 0)


PEER_I_GROUP = 4
PEER_STREAMS = 4


def _peer_body(hx_ref, wkh_ref, wkl_ref, *rest):
    u_refs, vt_refs = rest[:PEER_STREAMS], rest[PEER_STREAMS:2 * PEER_STREAMS]
    o_ref, sc_ref, top_ref, hxb_ref, hxl_ref, n_ref, f0_ref, r1_ref, e1_ref, w_ref, acc_ref = rest[2 * PEER_STREAMS:]
    e = pl.program_id(1)
    tm = hx_ref.shape[0]
    nsb = tm // PEER_SB

    @pl.when(e == 0)
    def _():
        _peer_stats(hx_ref, wkh_ref, wkl_ref, sc_ref, top_ref, hxb_ref, hxl_ref, n_ref, f0_ref, r1_ref, e1_ref)
        acc_ref[...] = jnp.zeros(acc_ref.shape, F32)

    i_rows = pl.ds(pl.multiple_of(e * PEER_I_BLOCK, PEER_I_BLOCK), PEER_I_BLOCK)

    def activations(sb):
        parts = []
        for u_ref in u_refs:
            act = _d(u_ref[...], hxb_ref[sb * PEER_SB:(sb + 1) * PEER_SB, :], NT)
            parts.append((0.5 * act * (1.0 + lax.erf(act * (2.0 ** -0.5)))).astype(BF16))
        return parts

    def gates(sb, act):
        for hb in range(PEER_SB // LANES):
            lanes = slice(sb * PEER_SB + hb * LANES, sb * PEER_SB + (hb + 1) * LANES)
            sub = slice(hb * LANES, (hb + 1) * LANES)
            n8 = [n_ref[h, i_rows, lanes].astype(BF16) for h in range(PEER_HEADS)]
            f8 = [f0_ref[h, i_rows, lanes].astype(BF16) for h in range(PEER_HEADS)]
            for ig in range(0, PEER_I_BLOCK, PEER_I_GROUP):
                g = [jnp.zeros((N_KEYS, LANES), BF16) for _ in range(PEER_I_GROUP)]
                for h in range(PEER_HEADS):
                    r1 = r1_ref[h, :, lanes]
                    e1 = e1_ref[h, :, lanes]
                    for k in range(PEER_I_GROUP):
                        ii = ig + k
                        g[k] = g[k] + jnp.where(r1 < n8[h][ii:ii + 1], e1 * f8[h][ii:ii + 1], jnp.zeros_like(e1))
                for k in range(PEER_I_GROUP):
                    rows = slice((ig + k) * N_KEYS, (ig + k + 1) * N_KEYS)
                    part, off = divmod((ig + k) * N_KEYS, PEER_TE // PEER_STREAMS)
                    w_ref[sb, rows, sub] = g[k] * act[part][off:off + N_KEYS, sub]

    def accumulate(sb):
        cols = slice(sb * PEER_SB, (sb + 1) * PEER_SB)
        dr = acc_ref.shape[0] // PEER_STREAMS
        for k, vt_ref in enumerate(vt_refs):
            acc_ref[k * dr:(k + 1) * dr, cols] += _d(vt_ref[...], w_ref[sb], NN)

    act = activations(0)
    for sb in range(nsb):
        nxt = activations(sb + 1) if sb + 1 < nsb else None
        gates(sb, act)
        accumulate(sb)
        act = nxt

    @pl.when(e == pl.num_programs(1) - 1)
    def _():
        o_ref[...] = acc_ref[...].T


def _fold_body(k_ref, w_ref, o_ref):
    o_ref[...] = _mdot(k_ref[...], w_ref[...], NN, 6)


def peer_fold_keys(wq, keys):
    d = wq.shape[0]
    nhp, nk, dk = keys.shape
    wqt = wq.T.reshape(nhp, dk, d)
    return pl.pallas_call(
        _fold_body,
        grid=(nhp,),
        in_specs=[pl.BlockSpec((None, nk, dk), lambda i: (i, 0, 0)),
                  pl.BlockSpec((None, dk, d), lambda i: (i, 0, 0))],
        out_specs=pl.BlockSpec((nk, d), lambda i: (i, 0)),
        out_shape=jax.ShapeDtypeStruct((nhp * nk, d), F32),
        compiler_params=pltpu.CompilerParams(dimension_semantics=("arbitrary",), vmem_limit_bytes=VMEM_LIMIT),
        name="peer_fold_keys",
    )(keys, wqt)


def peer(hx, wk, u_bf, vt_bf):
    t, d = hx.shape
    tm = PEER_TM
    assert t % tm == 0
    wk_hi, wk_lo = _split(wk, 2)
    ne = u_bf.shape[0] // PEER_TE
    h = PEER_HEADS
    ns = PEER_STREAMS
    return pl.pallas_call(
        _peer_body,
        grid=(t // tm, ne),
        in_specs=[pl.BlockSpec((tm, d), lambda i, e: (i, 0), pipeline_mode=pl.Buffered(1)),
                  pl.BlockSpec(wk.shape, lambda i, e: (0, 0), pipeline_mode=pl.Buffered(1)),
                  pl.BlockSpec(wk.shape, lambda i, e: (0, 0), pipeline_mode=pl.Buffered(1)),
                  *[pl.BlockSpec((PEER_TE // ns, d), functools.partial(lambda i, e, k: (e * ns + k, 0), k=k))
                    for k in range(ns)],
                  *[pl.BlockSpec((None, d // ns, PEER_TE), functools.partial(lambda i, e, k: (e, k, 0), k=k))
                    for k in range(ns)]],
        out_specs=pl.BlockSpec((tm, d), lambda i, e: (i, 0)),
        out_shape=jax.ShapeDtypeStruct((t, d), F32),
        scratch_shapes=[pltpu.VMEM((2 * h * N_KEYS, tm), F32), pltpu.VMEM((2, PEER_TOPK, h, tm), F32),
                        pltpu.VMEM((tm, d), BF16), pltpu.VMEM((tm, d), BF16),
                        pltpu.VMEM((h, N_KEYS, tm), F32), pltpu.VMEM((h, N_KEYS, tm), F32),
                        pltpu.VMEM((h, N_KEYS, tm), BF16), pltpu.VMEM((h, N_KEYS, tm), BF16),
                        pltpu.VMEM((tm // PEER_SB, PEER_TE, PEER_SB), BF16), pltpu.VMEM((d, tm), F32)],
        compiler_params=pltpu.CompilerParams(
            dimension_semantics=("arbitrary", "arbitrary"), vmem_limit_bytes=VMEM_LIMIT),
        name="peer_dense",
    )(hx, wk_hi, wk_lo, *([u_bf] * ns), *([vt_bf] * ns))


LN_TM = 256


def _ln_mod_body(xs_ref, y_ref, mg_ref, mn_ref, lng_ref, lnb_ref, xo_ref, ho_ref, *, gate_col, mod_col, nct, nt):
    i = pl.program_id(0)
    d = xs_ref.shape[1]
    is_ctx = i < nct

    def pick(ref, col):
        return jnp.where(is_ctx, ref[1:2, col * d:(col + 1) * d], ref[0:1, col * d:(col + 1) * d])

    z = DN_ALPHA * xs_ref[...] + pick(mg_ref, gate_col) * y_ref[...]
    mu = jnp.mean(z, axis=-1, keepdims=True)
    zc = z - mu
    var = jnp.mean(zc * zc, axis=-1, keepdims=True)
    xn = zc * lax.rsqrt(var + LN_EPS) * lng_ref[...] + lnb_ref[...]
    xo_ref[...] = xn
    h = xn * (1.0 + pick(mn_ref, mod_col + 1)) + pick(mn_ref, mod_col)
    ho_ref[...] = jnp.where(i < nt, h, 0.0)


def ln_mod(xs, y, mod_gate, gate_col, mod_next, mod_col, ln_g, ln_b, n_ctx, pad_to):
    t, d = xs.shape
    tm = LN_TM
    assert t % tm == 0 and n_ctx % tm == 0 and pad_to % tm == 0 and pad_to >= t
    nt = t // tm
    row = lambda i: (jnp.minimum(i, nt - 1), 0)
    full = lambda a: pl.BlockSpec(a.shape, lambda i: (0, 0))
    return pl.pallas_call(
        functools.partial(_ln_mod_body, gate_col=gate_col, mod_col=mod_col, nct=n_ctx // tm, nt=nt),
        grid=(pad_to // tm,),
        in_specs=[pl.BlockSpec((tm, d), row), pl.BlockSpec((tm, d), row), full(mod_gate), full(mod_next),
                  pl.BlockSpec((1, d), lambda i: (0, 0)), pl.BlockSpec((1, d), lambda i: (0, 0))],
        out_specs=[pl.BlockSpec((tm, d), row), pl.BlockSpec((tm, d), lambda i: (i, 0))],
        out_shape=[jax.ShapeDtypeStruct((t, d), F32), jax.ShapeDtypeStruct((pad_to, d), F32)],
        compiler_params=pltpu.CompilerParams(dimension_semantics=("arbitrary",), vmem_limit_bytes=VMEM_LIMIT),
        name="ln_mod",
    )(xs, y, mod_gate, mod_next, ln_g.reshape(1, d), ln_b.reshape(1, d))


def _ln(x, g, b):
    mu = jnp.mean(x, -1, keepdims=True)
    xc = x - mu
    var = jnp.mean(xc * xc, -1, keepdims=True)
    return xc * lax.rsqrt(var + LN_EPS) * g + b


def _head_norm(h, nheads, eps):
    t = h.shape[0]
    hh = h.reshape(t, nheads, -1)
    mu = jnp.mean(hh, -1, keepdims=True)
    xc = hh - mu
    var = jnp.mean(xc * xc, -1, keepdims=True)
    return (xc * lax.rsqrt(var + eps)).reshape(t, -1)


def _pad_cols(w, n):
    return jnp.pad(w, ((0, 0), (0, n - w.shape[1])))


def _shift_rows(u, k):
    if k > 0:
        return jnp.concatenate([u[k:], jnp.zeros_like(u[:k])], axis=0)
    return jnp.concatenate([jnp.zeros_like(u[:-k]), u[:k]], axis=0)


def _grid_conv(u, w, b):
    s = u.shape[0]
    colid = (jnp.arange(s) % GRID_W)[:, None]
    out = jnp.zeros_like(u) + b
    for di in range(3):
        for dj in range(3):
            off = (di - 1) * GRID_W + (dj - 1)
            sh = _shift_rows(u, off) if off else u
            if dj == 0:
                sh = jnp.where(colid == 0, 0.0, sh)
            elif dj == 2:
                sh = jnp.where(colid == GRID_W - 1, 0.0, sh)
            out = out + sh * w[di, dj]
    return out


def _seq_conv(u, w, b):
    wc = w[1]
    return _shift_rows(u, -1) * wc[0] + u * wc[1] + _shift_rows(u, 1) * wc[2] + b


def _qshift(u):
    q = u.shape[1] // 4
    colid = (jnp.arange(u.shape[0]) % GRID_W)[:, None]
    left = jnp.where(colid == 0, 0.0, _shift_rows(u[:, :q], -1))
    right = jnp.where(colid == GRID_W - 1, 0.0, _shift_rows(u[:, q:2 * q], 1))
    up = _shift_rows(u[:, 2 * q:3 * q], -GRID_W)
    down = _shift_rows(u[:, 3 * q:], GRID_W)
    return jnp.concatenate([left, right, up, down], axis=1)


def _shift_seq(u):
    h = u.shape[1] // 2
    return jnp.concatenate([_shift_rows(u[:, :h], -1), _shift_rows(u[:, h:], 1)], axis=1)


def _mlstm_layer(h, n_ctx, w_in, b_in, conv_w, conv_b, hn_g, w_out):
    d = D_MODEL
    t = h.shape[0]
    qk_pre = matmul(h, w_in[:, :2 * d], b_in[:2 * d])
    v = matmul(h, w_in[:, 2 * d:3 * d], b_in[2 * d:3 * d], out_dtype=BF16)
    o = matmul(h, w_in[:, 3 * d:4 * d], b_in[3 * d:4 * d], act="sigmoid", out_dtype=BF16)
    g = matmul(h, _pad_cols(w_in[:, 4 * d:], LANES), jnp.pad(b_in[4 * d:], (0, LANES - 4 * MLSTM_HEADS)))
    g = g[:, :4 * MLSTM_HEADS].reshape(t, 4, MLSTM_HEADS)
    g = jnp.concatenate([g[:, :2], jax.nn.log_sigmoid(g[:, 2:])], axis=1)
    gh = jnp.transpose(g, (2, 0, 1))
    gcol = jnp.pad(gh, ((0, 0), (0, 0), (0, LANES - 4)))
    grow = jnp.pad(jnp.transpose(gh, (0, 2, 1)), ((0, 0), (0, 4), (0, 0)))
    qk = mlstm_conv(qk_pre, conv_w, conv_b, n_ctx)
    hf, hb = mlstm_scan(qk, v, gcol, grow, n_ctx)
    return mlstm_out(hf, hb, o, hn_g, w_out)


def _rwkv_layer(h, n_ctx, mu, w_rkv, w0, w1, w2, a0, a1, a2, g1, g2, k_k, k_a, r_k, lnx_g, lnx_b, w_out):
    d = D_MODEL
    xm = rwkv_mix(h, mu, n_ctx)
    r = matmul(xm[0], w_rkv[0])
    k = matmul(xm[1], w_rkv[1])
    v = matmul(xm[2], w_rkv[2])

    def lora_pair(x, w_in, w_mid, bias, act_mid, act_out):
        rank = w_in.shape[-1]
        w_a = _pad_cols(jnp.concatenate([w_in[0], w_in[1]], axis=1), LANES)
        zpad = jnp.zeros((rank, d), F32)
        w_b = jnp.concatenate([jnp.concatenate([w_mid[0], zpad], axis=1),
                               jnp.concatenate([zpad, w_mid[1]], axis=1)], axis=0)
        w_b = jnp.pad(w_b, ((0, LANES - 2 * rank), (0, 0)))
        mid = matmul(x, w_a, act=act_mid, out_dtype=BF16)
        return matmul(mid, w_b, jnp.concatenate([bias[0], bias[1]]), act=act_out)

    lw = lora_pair(xm[3], w1, w2, w0, "tanh", "logdecay")
    a = lora_pair(xm[4], a1, a2, a0, None, "sigmoid")
    gpad = 2 * LANES
    gg = matmul(xm[5], _pad_cols(g1, gpad), act="sigmoid", out_dtype=BF16)
    g = matmul(gg, jnp.pad(g2, ((0, gpad - g1.shape[1]), (0, 0))))
    yf, yb = rwkv_scan(r, k, v, lw, a, k_k, k_a, n_ctx)
    return rwkv_out(yf, yb, r, k, v, g, a, k_a, r_k, lnx_g, lnx_b, w_out)


def _forward(x, c, ctx, c_ctx, ada_w, ada_b, ln_g, ln_b,
             ml_w_in, ml_b_in, ml_conv_w, ml_conv_b, ml_hn_g, ml_w_out,
             rw_mu, rw_w_rkv, rw_w0, rw_w1, rw_w2, rw_a0, rw_a1, rw_a2, rw_g1, rw_g2,
             rw_k_k, rw_k_a, rw_r_k, rw_lnx_g, rw_lnx_b, rw_w_out,
             pk_wq, pk_keys, pk_u, pk_v):
    d = D_MODEL
    n_ctx = ctx.shape[1]
    n_lat = x.shape[1]
    xs = jnp.concatenate([ctx[0], x[0]], axis=0)
    t = xs.shape[0]
    t_pad = -(-t // PEER_TM) * PEER_TM
    s_in = jnp.zeros((8, d), F32).at[0].set(jax.nn.silu(c[0])).at[1].set(jax.nn.silu(c_ctx))
    depth = ada_w.shape[0]
    mods = [matmul(s_in, ada_w[i], ada_b[i], passes=3) for i in range(depth)]
    is_ctx = (jnp.arange(t) < n_ctx)[:, None]
    m0 = [jnp.where(is_ctx, mods[0][1, n * d:(n + 1) * d], mods[0][0, n * d:(n + 1) * d]) for n in range(2)]
    h = xs * (1.0 + m0[1]) + m0[0]
    for i in range(depth):
        j = i // 2
        if i % 2 == 0:
            y = _mlstm_layer(h, n_ctx, ml_w_in[j], ml_b_in[j], ml_conv_w[j], ml_conv_b[j],
                             ml_hn_g[j], ml_w_out[j])
        else:
            y = _rwkv_layer(h, n_ctx, rw_mu[j], rw_w_rkv[j], rw_w0[j], rw_w1[j], rw_w2[j],
                            rw_a0[j], rw_a1[j], rw_a2[j], rw_g1[j], rw_g2[j], rw_k_k[j],
                            rw_k_a[j], rw_r_k[j], rw_lnx_g[j], rw_lnx_b[j], rw_w_out[j])
        xs, h = ln_mod(xs, y, mods[i], 2, mods[i], 3, ln_g[i, 0], ln_b[i, 0], n_ctx, t_pad)
        wk = peer_fold_keys(pk_wq[i], pk_keys[i].reshape(2 * PEER_HEADS, N_KEYS, PEER_DQ // 2))
        vt = jnp.swapaxes(pk_v[i].astype(BF16).reshape(-1, PEER_TE, d), 1, 2)
        y = peer(h, wk, pk_u[i].astype(BF16), vt)
        xs, h = ln_mod(xs, y, mods[i], 5, mods[min(i + 1, depth - 1)], 0, ln_g[i, 1], ln_b[i, 1], n_ctx, t)
    return xs[n_ctx:][None]


def kernel(x, c, ctx, c_ctx, ada_w, ada_b, ln_g, ln_b, ml_w_in, ml_b_in, ml_conv_w, ml_conv_b, ml_hn_g, ml_w_out, rw_mu, rw_w_rkv, rw_w0, rw_w1, rw_w2, rw_a0, rw_a1, rw_a2, rw_g1, rw_g2, rw_k_k, rw_k_a, rw_r_k, rw_lnx_g, rw_lnx_b, rw_w_out, pk_wq, pk_keys, pk_u, pk_v):
    return _forward(x, c, ctx, c_ctx, ada_w, ada_b, ln_g, ln_b,
                    ml_w_in, ml_b_in, ml_conv_w, ml_conv_b, ml_hn_g, ml_w_out,
                    rw_mu, rw_w_rkv, rw_w0, rw_w1, rw_w2, rw_a0, rw_a1, rw_a2, rw_g1, rw_g2,
                    rw_k_k, rw_k_a, rw_r_k, rw_lnx_g, rw_lnx_b, rw_w_out,
                    pk_wq, pk_keys, pk_u, pk_v)
```

```python
import functools

import jax
import jax.numpy as jnp
from jax import lax
from jax.experimental import pallas as pl
from jax.experimental.pallas import tpu as pltpu

F32 = jnp.float32
BF16 = jnp.bfloat16

D_MODEL = 1024
DEPTH = 4
GRID_W = 64
N_MOD = 6
DN_ALPHA = (2.0 * DEPTH) ** 0.25
LN_EPS = 1e-5

MLSTM_HEADS = 4
MLSTM_DH = D_MODEL // MLSTM_HEADS
MLSTM_CHUNK = 128
M_INIT = -1e30

RWKV_N = 64
RWKV_HEADS = D_MODEL // RWKV_N
RWKV_CHUNK = 64
RWKV_GN_EPS = 64e-5

N_KEYS = 128
PEER_HEADS = 8
PEER_DQ = 256
PEER_TOPK = 16
PEER_I_BLOCK = 8
PEER_TE = PEER_I_BLOCK * N_KEYS
PEER_TM = 512
PEER_SB = 256

LANES = 128
VMEM_LIMIT = 62 * 1024 * 1024

NN = ((1,), (0,))
NT = ((1,), (1,))
TN = ((0,), (0,))


def _split(x, n):
    parts = []
    r = x.astype(F32)
    for i in range(n):
        p = r.astype(BF16)
        parts.append(p)
        if i + 1 < n:
            r = r - p.astype(F32)
    return parts


def _d(a, b, dims):
    return lax.dot_general(a, b, (dims, ((), ())), preferred_element_type=F32)


def _mdot(a, b, dims, passes):
    if passes == 1:
        return _d(a.astype(BF16), b.astype(BF16), dims)
    if passes == 3:
        a0, a1 = _split(a, 2)
        b0, b1 = _split(b, 2)
        return (_d(a0, b1, dims) + _d(a1, b0, dims)) + _d(a0, b0, dims)
    a0, a1, a2 = _split(a, 3)
    b0, b1, b2 = _split(b, 3)
    lo = (_d(a0, b2, dims) + _d(a2, b0, dims)) + _d(a1, b1, dims)
    mid = _d(a0, b1, dims) + _d(a1, b0, dims)
    return (lo + mid) + _d(a0, b0, dims)


def _dot_01_lhs(m01, x):
    mb = m01.astype(BF16)
    x0, x1, x2 = _split(x, 3)
    return (_d(mb, x2, NN) + _d(mb, x1, NN)) + _d(mb, x0, NN)


def _dot_01_rhs(x, m01):
    mb = m01.astype(BF16)
    x0, x1, x2 = _split(x, 3)
    return (_d(x2, mb, NN) + _d(x1, mb, NN)) + _d(x0, mb, NN)


def _pick(n, cands):
    for c in cands:
        if n % c == 0:
            return c
    raise ValueError(f"no tile for {n}")


_ACTS = {None: lambda x: x, "sigmoid": jax.nn.sigmoid, "tanh": jnp.tanh,
         "logdecay": lambda x: -(2.718281828459045 ** -0.5) * jax.nn.sigmoid(x)}


def _mm_body(x_ref, w_ref, b_ref, o_ref, *, passes, act):
    o_ref[...] = _ACTS[act](_mdot(x_ref[...], w_ref[...], NN, passes) + b_ref[...]).astype(o_ref.dtype)


def matmul(x, w, b=None, *, passes=1, act=None, out_dtype=F32):
    m, k = x.shape
    n = w.shape[1]
    assert n % LANES == 0 and w.shape[0] == k
    tm = m if m <= 1024 else _pick(m, (640, 512, 384, 256, 128))
    tn = _pick(n, (1024, 768, 640, 512, 384, 256, 128))
    if b is None:
        b = jnp.zeros((n,), F32)
    if passes == 1:
        w = w.astype(BF16)
    return pl.pallas_call(
        functools.partial(_mm_body, passes=passes, act=act),
        grid=(m // tm, n // tn),
        in_specs=[pl.BlockSpec((tm, k), lambda i, j: (i, 0)),
                  pl.BlockSpec((k, tn), lambda i, j: (0, j)),
                  pl.BlockSpec((1, tn), lambda i, j: (0, j))],
        out_specs=pl.BlockSpec((tm, tn), lambda i, j: (i, j)),
        out_shape=jax.ShapeDtypeStruct((m, n), out_dtype),
        compiler_params=pltpu.CompilerParams(
            dimension_semantics=("arbitrary", "arbitrary"), vmem_limit_bytes=VMEM_LIMIT),
        name="proj_matmul",
    )(x, w, b.reshape(1, n).astype(F32))


CONV_TM = 256
CONV_CB = 512


def _conv_body(cur_ref, prev_ref, next_ref, w_ref, b_ref, o_ref, *, nct, nt, q_blocks, q_scale):
    i = pl.program_id(0)
    j = pl.program_id(1)
    is_ctx = i < nct
    tm = cur_ref.shape[0]
    above = jnp.where(is_ctx | (i == nct), 0.0, prev_ref[...])
    below = jnp.where(is_ctx | (i == nt - 1), 0.0, next_ref[...])
    ext = jnp.concatenate([above, cur_ref[...], below], axis=0)
    n = ext.shape[0]
    colid = lax.broadcasted_iota(jnp.int32, ext.shape, 0) % GRID_W
    left = jnp.where(is_ctx | (colid != 0), pltpu.roll(ext, 1, 0), 0.0)
    right = jnp.where(is_ctx | (colid != GRID_W - 1), pltpu.roll(ext, n - 1, 0), 0.0)
    w = w_ref[...]
    acc = jnp.zeros((tm, ext.shape[1]), F32) + b_ref[...]
    for di in range(3):
        rows = slice(di * GRID_W, di * GRID_W + tm)
        tap = left[rows] * w[3 * di:3 * di + 1] + ext[rows] * w[3 * di + 1:3 * di + 2] \
            + right[rows] * w[3 * di + 2:3 * di + 3]
        acc = acc + (tap if di == 1 else jnp.where(is_ctx, 0.0, tap))
    y = acc * jax.nn.sigmoid(acc)
    o_ref[...] = (y * jnp.where(j < q_blocks, q_scale, 1.0)).astype(o_ref.dtype)


def mlstm_conv(qk_pre, conv_w, conv_b, n_ctx):
    t, c = qk_pre.shape
    tm, cb = CONV_TM, CONV_CB
    assert t % tm == 0 and n_ctx % tm == 0 and tm % GRID_W == 0 and c % (2 * cb) == 0
    r = tm // GRID_W
    nt, nu = t // tm, t // GRID_W
    return pl.pallas_call(
        functools.partial(_conv_body, nct=n_ctx // tm, nt=nt, q_blocks=c // (2 * cb), q_scale=MLSTM_DH ** -0.5),
        grid=(nt, c // cb),
        in_specs=[pl.BlockSpec((tm, cb), lambda i, j: (i, j)),
                  pl.BlockSpec((GRID_W, cb), lambda i, j: (jnp.maximum(i * r - 1, 0), j)),
                  pl.BlockSpec((GRID_W, cb), lambda i, j: (jnp.minimum((i + 1) * r, nu - 1), j)),
                  pl.BlockSpec((9, cb), lambda i, j: (0, j)),
                  pl.BlockSpec((1, cb), lambda i, j: (0, j))],
        out_specs=pl.BlockSpec((tm, cb), lambda i, j: (i, j)),
        out_shape=jax.ShapeDtypeStruct((t, c), BF16),
        compiler_params=pltpu.CompilerParams(
            dimension_semantics=("arbitrary", "arbitrary"), vmem_limit_bytes=VMEM_LIMIT),
        name="mlstm_conv",
    )(qk_pre, qk_pre, qk_pre, conv_w.reshape(9, c), conv_b.reshape(1, c))


def _mlstm_out_body(hf_ref, hb_ref, o_ref, g_ref, w_ref, y_ref):
    h = hf_ref[...] + hb_ref[...]
    parts = []
    for a in range(MLSTM_HEADS):
        x = h[:, a * MLSTM_DH:(a + 1) * MLSTM_DH]
        mu = jnp.mean(x, axis=-1, keepdims=True)
        xc = x - mu
        var = jnp.mean(xc * xc, axis=-1, keepdims=True)
        parts.append(xc * lax.rsqrt(var + LN_EPS))
    hn = jnp.concatenate(parts, axis=1)
    z = o_ref[...].astype(F32) * hn * g_ref[...]
    y_ref[...] = _d(z.astype(BF16), w_ref[...], NN)


def mlstm_out(hf, hb, o, hn_g, w_out):
    t, d = hf.shape
    tm = LN_TM
    row = pl.BlockSpec((tm, d), lambda i: (i, 0))
    return pl.pallas_call(
        _mlstm_out_body,
        grid=(t // tm,),
        in_specs=[row, row, row, pl.BlockSpec((1, d), lambda i: (0, 0)), pl.BlockSpec((d, d), lambda i: (0, 0))],
        out_specs=row,
        out_shape=jax.ShapeDtypeStruct((t, d), F32),
        compiler_params=pltpu.CompilerParams(dimension_semantics=("arbitrary",), vmem_limit_bytes=VMEM_LIMIT),
        name="mlstm_out",
    )(hf, hb, o, hn_g.reshape(1, d), w_out.astype(BF16))


def _mlstm_body(qf, kf, vf, gcf, grf, qb, kb, vb, gcb, grb, hf_ref, hb_ref, c_ref, n_ref, m_ref):
    @pl.when(pl.program_id(1) == 0)
    def _():
        c_ref[...] = jnp.zeros(c_ref.shape, F32)
        n_ref[...] = jnp.zeros(n_ref.shape, F32)
        m_ref[...] = jnp.full(m_ref.shape, M_INIT, F32)

    ds = (0, 1)
    each = lambda f, *ls: [f(*xs) for xs in zip(*ls)]
    q, k, v = [qf[...], qb[...]], [kf[...], kb[...]], [vf[...], vb[...]]
    gc, gr = [gcf[...], gcb[...]], [grf[...], grb[...]]
    L = q[0].shape[0]
    row = lax.broadcasted_iota(jnp.int32, (L, L), 0)
    col = lax.broadcasted_iota(jnp.int32, (L, L), 1)
    seen = (col <= row, col >= row)
    tri = [jnp.where(m, 1.0, 0.0) for m in seen]
    b_col = [_dot_01_lhs(tri[d], gc[d])[:, 2 + d:3 + d] for d in ds]
    b_row = [_dot_01_rhs(gr[d], tri[1 - d])[2 + d:3 + d, :] for d in ds]
    ig_col = [gc[d][:, d:d + 1] for d in ds]
    ig_row = [gr[d][d:d + 1, :] for d in ds]
    m_st = [m_ref[d, 0:1, 0:1] for d in ds]
    c_st = [c_ref[d] for d in ds]
    n_st = [n_ref[d] for d in ds]

    dlog = [jnp.where(seen[d], b_col[d] - b_row[d] + ig_row[d], -jnp.inf) for d in ds]
    m_inter = each(jnp.add, b_col, m_st)
    m_t = each(lambda mi, dl: jnp.maximum(mi, jnp.max(dl, axis=1, keepdims=True)), m_inter, dlog)
    qb16, kb16, vb16 = q, k, v
    q, k = (each(lambda x: x.astype(F32), a) for a in (q, k))
    qk = each(lambda a, b: _d(a, b, NT), qb16, kb16)
    s = each(lambda x, dl, mt: x * jnp.exp(dl - mt), qk, dlog, m_t)
    dec = each(lambda mi, mt: jnp.exp(mi - mt), m_inter, m_t)
    sv = each(lambda a, b: _d(a.astype(BF16), b, NN), s, vb16)
    qc = each(lambda a, b: _d(a, b.astype(BF16), NN), qb16, c_st)
    num = each(lambda a, dc, b: a + dc * b, sv, dec, qc)
    den = each(lambda x, dc, qq, nn: jnp.sum(x, axis=1, keepdims=True) + dc * jnp.sum(qq * nn, axis=1, keepdims=True),
               s, dec, q, n_st)
    h = each(lambda nu, de, mt: nu / jnp.maximum(jnp.abs(de), jnp.exp(-mt)), num, den, m_t)
    hf_ref[...] = h[0]
    hb_ref[...] = h[1]

    b_last = [b_col[0][L - 1:L, :], b_col[1][0:1, :]]
    w_c = each(lambda bl, bc, ic: bl - bc + ic, b_last, b_col, ig_col)
    m_new = each(lambda bl, ms, w: jnp.maximum(bl + ms, jnp.max(w, axis=0, keepdims=True)), b_last, m_st, w_c)
    a_c = each(lambda w, mn: jnp.exp(w - mn), w_c, m_new)
    g_prev = each(lambda bl, ms, mn: jnp.exp(bl + ms - mn), b_last, m_st, m_new)
    ak = each(jnp.multiply, a_c, k)
    kv = each(lambda a, b: _d(a.T.astype(BF16), b, NN), ak, vb16)
    for d in ds:
        c_ref[d] = g_prev[d] * c_st[d] + kv[d]
        n_ref[d] = g_prev[d] * n_st[d] + jnp.sum(ak[d], axis=0, keepdims=True)
        m_ref[d] = jnp.broadcast_to(m_new[d], m_ref.shape[1:])


def _bwd_chunk(c, nc0, nc):
    return jnp.where(c < nc0, nc0 - 1 - c, nc - 1 - (c - nc0))


def mlstm_scan(qk, v, gcol, grow, n_ctx):
    t = v.shape[0]
    L, dh = MLSTM_CHUNK, MLSTM_DH
    nc, nc0 = t // L, n_ctx // L
    fwd = lambda h, c: (c, h)
    bwd = lambda h, c: (_bwd_chunk(c, nc0, nc), h)
    fwd_k = lambda h, c: (c, MLSTM_HEADS + h)
    bwd_k = lambda h, c: (_bwd_chunk(c, nc0, nc), MLSTM_HEADS + h)
    qkv = lambda im: pl.BlockSpec((L, dh), im)
    gc_spec = lambda f: pl.BlockSpec((None, L, LANES), lambda h, c: (h, f(c), 0))
    gr_spec = lambda f: pl.BlockSpec((None, 8, L), lambda h, c: (h, 0, f(c)))
    idf = lambda c: c
    idb = lambda c: _bwd_chunk(c, nc0, nc)
    return pl.pallas_call(
        _mlstm_body,
        grid=(MLSTM_HEADS, nc),
        in_specs=[qkv(fwd), qkv(fwd_k), qkv(fwd), gc_spec(idf), gr_spec(idf),
                  qkv(bwd), qkv(bwd_k), qkv(bwd), gc_spec(idb), gr_spec(idb)],
        out_specs=[qkv(fwd), qkv(bwd)],
        out_shape=[jax.ShapeDtypeStruct((t, D_MODEL), F32)] * 2,
        scratch_shapes=[pltpu.VMEM((2, dh, dh), F32), pltpu.VMEM((2, 1, dh), F32),
                        pltpu.VMEM((2, 8, LANES), F32)],
        compiler_params=pltpu.CompilerParams(
            dimension_semantics=("arbitrary", "arbitrary"), vmem_limit_bytes=VMEM_LIMIT),
        name="mlstm_scan",
    )(qk, qk, v, gcol, grow, qk, qk, v, gcol, grow)


RW_PASSES = 1
RW_CHUNKS_PER_STEP = 4


def _stack2(x, lane_head):
    return jnp.concatenate([jnp.where(lane_head == 0, x, 0.0), jnp.where(lane_head == 1, x, 0.0)], axis=0)


def _unstack2(x):
    L = x.shape[0] // 2
    return x[:L] + x[L:]


def _rwkv_chunks(chains, k_k, k_a):
    L = chains[0][0].shape[0]
    n2 = 2 * L
    p = RW_PASSES
    ds = [c[5] for c in chains]
    each = lambda f, *ls: [f(*xs) for xs in zip(*ls)]
    dot = lambda dims: (lambda a, b: _mdot(a, b, dims, p))
    lw, r, k_raw, v, a = ([c[i] for c in chains] for i in range(5))

    lane_r = lax.broadcasted_iota(jnp.int32, (LANES, LANES), 0) // RWKV_N
    lane_c = lax.broadcasted_iota(jnp.int32, (LANES, LANES), 1) // RWKV_N
    head_ones = jnp.where(lane_r == lane_c, 1.0, 0.0)
    kkr = each(lambda x: x * k_k, k_raw)
    ss = each(lambda x: _dot_01_rhs(x * x, head_ones), kkr)
    kap = each(lambda x, q: x / jnp.maximum(jnp.sqrt(q), 1e-12), kkr, ss)
    alp = each(jnp.multiply, kap, a)
    k = each(lambda x, y: x * (1.0 + (y - 1.0) * k_a), k_raw, a)

    row = lax.broadcasted_iota(jnp.int32, (L, L), 0)
    col = lax.broadcasted_iota(jnp.int32, (L, L), 1)
    tris = (jnp.where(col <= row, 1.0, 0.0), jnp.where(col >= row, 1.0, 0.0))
    lp = [_dot_01_lhs(tris[d], x) for d, x in zip(ds, lw)]
    lp_end = [x[0:1, :] if d else x[L - 1:L, :] for d, x in zip(ds, lp)]
    e_neg = each(lambda x: jnp.exp(-x), lp)
    e_end = each(lambda x, xe: jnp.exp(xe - x), lp, lp_end)
    kap_t = each(lambda x, y, z: x * jnp.exp(y - z), kap, lp, lw)
    r_t = each(lambda x, y: x * jnp.exp(y), r, lp)
    k_h = each(jnp.multiply, k, e_neg)
    a_h = each(jnp.multiply, alp, e_neg)
    k_e = each(jnp.multiply, k, e_end)
    a_e = each(jnp.multiply, alp, e_end)

    lane_head = lax.broadcasted_iota(jnp.int32, (L, LANES), 1) // RWKV_N
    st = lambda x: _stack2(x, lane_head)
    kap_s, r_s, v_s, k_s, a_s = (each(st, x) for x in (kap_t, r_t, v, k_e, a_e))
    rhs_k = each(lambda x: jnp.concatenate([x, x], axis=0), k_h)
    rhs_a = each(lambda x: jnp.concatenate([x, x], axis=0), a_h)

    row2 = lax.broadcasted_iota(jnp.int32, (n2, n2), 0)
    col2 = lax.broadcasted_iota(jnp.int32, (n2, n2), 1)
    same_head = (row2 // L) == (col2 // L)
    strict = (same_head & (col2 < row2), same_head & (col2 > row2))
    incl = (same_head & (col2 <= row2), same_head & (col2 >= row2))
    zero = jnp.zeros((n2, n2), F32)
    masked = lambda masks: (lambda d, x: jnp.where(masks[d], x, zero))
    n_ka = each(masked(strict), ds, each(dot(NT), kap_s, rhs_a))
    m_kk = each(masked(strict), ds, each(dot(NT), kap_s, rhs_k))
    a_rk = each(masked(incl), ds, each(dot(NT), r_s, rhs_k))
    a_ra = each(masked(incl), ds, each(dot(NT), r_s, rhs_a))

    b16 = (row2 // 16) == (col2 // 16)
    b32 = (row2 // 32) == (col2 // 32)
    eye = jnp.where(row2 == col2, 1.0, 0.0)
    n16 = each(lambda x: jnp.where(b16, x, zero), n_ka)
    n_2 = each(dot(NN), n16, n16)
    n_4 = each(dot(NN), n_2, n_2)
    n_8 = each(dot(NN), n_4, n_4)
    inv = each(lambda x: eye - x, n16)
    for pw in (n_2, n_4, n_8):
        inv = each(jnp.add, inv, each(dot(NN), inv, pw))
    for sel in (lambda x: jnp.where(b32 & ~b16, x, zero), lambda x: jnp.where(b32, zero, x)):
        t1 = each(dot(NN), inv, each(sel, n_ka))
        inv = each(jnp.subtract, inv, each(dot(NN), t1, inv))

    mv = each(dot(NN), m_kk, v_s)
    w1u0 = each(dot(NN), inv, each(lambda x, y: jnp.concatenate([x, y], axis=1), kap_s, mv))
    ar = each(dot(NN), a_ra, w1u0)
    av = each(dot(NN), a_rk, v_s)
    r2 = each(lambda x, y: _unstack2(x - y[:, :LANES]), r_s, ar)
    y0 = each(lambda x, y: _unstack2(x - y[:, LANES:]), av, ar)
    w1_t = each(lambda x: x[:, :LANES].T, w1u0)
    u0_t = each(lambda x: x[:, LANES:].T, w1u0)
    v_t = each(lambda x: x.T, v_s)
    wa = each(dot(NN), w1_t, a_s)
    vk = each(dot(NN), v_t, k_s)
    ua = each(dot(NN), u0_t, a_s)
    g = each(lambda xe, x: jnp.where(row2 == col2, jnp.exp(xe), zero) - x, lp_end, wa)
    b = each(jnp.subtract, vk, ua)
    return list(zip(y0, r2, g, b))


def _rwkv_body(lwf, rf, kf, vf, af, lwb, rb, kb, vb, ab, kk_ref, ka_ref, yf_ref, yb_ref, s_ref):
    @pl.when(pl.program_id(1) == 0)
    def _():
        s_ref[...] = jnp.zeros(s_ref.shape, F32)

    L = RWKV_CHUNK
    n = RW_CHUNKS_PER_STEP
    rows = lambda j: slice(j * L, (j + 1) * L)
    refs = ((lwf, rf, kf, vf, af), (lwb, rb, kb, vb, ab))
    visit = [(d, j if d == 0 else n - 1 - j) for j in range(n) for d in (0, 1)]
    pre = _rwkv_chunks([tuple(ref[rows(j), :] for ref in refs[d]) + (d,) for d, j in visit],
                       kk_ref[...], ka_ref[...])
    y_refs = (yf_ref, yb_ref)
    s = [s_ref[0], s_ref[1]]
    for (d, j), (y0, r2, g, b) in zip(visit, pre):
        y_refs[d][rows(j), :] = y0 + _mdot(r2, s[d], NT, RW_PASSES)
        s[d] = _mdot(s[d], g, NN, RW_PASSES) + b
    s_ref[0] = s[0]
    s_ref[1] = s[1]


def rwkv_scan(r, k, v, lw, a, k_k, k_a, n_ctx):
    t, d = r.shape
    L = RWKV_CHUNK
    blk = RW_CHUNKS_PER_STEP * L
    assert 2 * L == LANES and t % blk == 0 and n_ctx % blk == 0
    nc, nc0 = t // blk, n_ctx // blk
    nh = d // LANES
    fwd = pl.BlockSpec((blk, LANES), lambda h, c: (c, h))
    bwd = pl.BlockSpec((blk, LANES), lambda h, c: (_bwd_chunk(c, nc0, nc), h))
    bwd2 = pl.BlockSpec((blk, LANES), lambda h, c: (_bwd_chunk(c, nc0, nc), nh + h))
    par = pl.BlockSpec((1, LANES), lambda h, c: (0, h))
    return pl.pallas_call(
        _rwkv_body,
        grid=(nh, nc),
        in_specs=[fwd] * 5 + [bwd2, bwd, bwd, bwd, bwd2, par, par],
        out_specs=[fwd, bwd],
        out_shape=[jax.ShapeDtypeStruct((t, d), F32)] * 2,
        scratch_shapes=[pltpu.VMEM((2, LANES, LANES), F32)],
        compiler_params=pltpu.CompilerParams(
            dimension_semantics=("arbitrary", "arbitrary"), vmem_limit_bytes=VMEM_LIMIT),
        name="rwkv7_scan",
    )(lw, r, k, v, a, lw, r, k, v, a, k_k.reshape(1, d), k_a.reshape(1, d))


MIX_TM = 256


def _mix_body(cur_ref, prev_ref, next_ref, mu_ref, *o_refs, nct, nt):
    i = pl.program_id(0)
    is_ctx = i < nct
    cur = cur_ref[...]
    tm, d = cur.shape
    qd = d // 4
    above = jnp.where(is_ctx | (i == nct), 0.0, prev_ref[...])
    below = jnp.where(is_ctx | (i == nt - 1), 0.0, next_ref[...])
    ext = jnp.concatenate([above, cur, below], axis=0)
    n = ext.shape[0]
    colid = lax.broadcasted_iota(jnp.int32, (n, 2 * qd), 0) % GRID_W
    left = jnp.where(is_ctx | (colid != 0), pltpu.roll(ext[:, :2 * qd], 1, 0), 0.0)
    colid3 = lax.broadcasted_iota(jnp.int32, (n, 3 * qd), 0) % GRID_W
    right = jnp.where(is_ctx | (colid3 != GRID_W - 1), pltpu.roll(ext[:, qd:], n - 1, 0), 0.0)
    mid = slice(GRID_W, GRID_W + tm)
    sh = jnp.concatenate([
        left[mid, :qd],
        jnp.where(is_ctx, left[mid, qd:], right[mid, :qd]),
        jnp.where(is_ctx, right[mid, qd:2 * qd], ext[0:tm, 2 * qd:3 * qd]),
        jnp.where(is_ctx, right[mid, 2 * qd:], ext[2 * GRID_W:2 * GRID_W + tm, 3 * qd:])], axis=1)
    dx = sh - cur
    for b, o_ref in enumerate(o_refs):
        o_ref[...] = (cur + dx * mu_ref[b:b + 1, :]).astype(o_ref.dtype)


def rwkv_mix(h, mu, n_ctx):
    t, d = h.shape
    tm = MIX_TM
    assert t % tm == 0 and n_ctx % tm == 0 and tm % GRID_W == 0
    r = tm // GRID_W
    nt, nu = t // tm, t // GRID_W
    nb = mu.shape[0]
    return pl.pallas_call(
        functools.partial(_mix_body, nct=n_ctx // tm, nt=nt),
        grid=(nt,),
        in_specs=[pl.BlockSpec((tm, d), lambda i: (i, 0)),
                  pl.BlockSpec((GRID_W, d), lambda i: (jnp.maximum(i * r - 1, 0), 0)),
                  pl.BlockSpec((GRID_W, d), lambda i: (jnp.minimum((i + 1) * r, nu - 1), 0)),
                  pl.BlockSpec((nb, d), lambda i: (0, 0))],
        out_specs=[pl.BlockSpec((tm, d), lambda i: (i, 0))] * nb,
        out_shape=[jax.ShapeDtypeStruct((t, d), BF16)] * nb,
        compiler_params=pltpu.CompilerParams(dimension_semantics=("arbitrary",), vmem_limit_bytes=VMEM_LIMIT),
        name="rwkv_mix",
    )(h, h, h, mu)


def _rwkv_out_body(yf_ref, yb_ref, r_ref, k_ref, v_ref, g_ref, af_ref, ab_ref, p_ref, e_ref, et_ref, w_ref, o_ref):
    e, et = e_ref[...], et_ref[...]
    head_sum = lambda x: _dot_01_rhs(_dot_01_rhs(x, e), et)
    k_a, r_k, gain, bias = (p_ref[n:n + 1, :] for n in range(4))
    y = yf_ref[...] + yb_ref[...]
    yc = y - head_sum(y) * (1.0 / RWKV_N)
    var = head_sum(yc * yc) * (1.0 / RWKV_N)
    yn = yc * lax.rsqrt(var + RWKV_GN_EPS) * gain + bias
    kbar = k_ref[...] * (1.0 + (0.5 * (af_ref[...] + ab_ref[...]) - 1.0) * k_a)
    bonus = head_sum(r_ref[...] * kbar * r_k) * v_ref[...]
    z = (yn + bonus) * g_ref[...]
    o_ref[...] = _d(z.astype(BF16), w_ref[...], NN)


def rwkv_out(yf, yb, r, k, v, g, a, k_a, r_k, lnx_g, lnx_b, w_out):
    t, d = yf.shape
    tm = LN_TM
    nh = d // RWKV_N
    row = pl.BlockSpec((tm, d), lambda i: (i, 0))
    const = lambda shape: pl.BlockSpec(shape, lambda i: (0, 0))
    e = (jnp.arange(d)[:, None] // RWKV_N == jnp.arange(LANES)[None, :]).astype(BF16)
    params = jnp.stack([k_a, r_k, lnx_g, lnx_b])
    return pl.pallas_call(
        _rwkv_out_body,
        grid=(t // tm,),
        in_specs=[row] * 6 + [row, pl.BlockSpec((tm, d), lambda i: (i, 1)),
                  const((4, d)), const((d, LANES)), const((LANES, d)), const((d, d))],
        out_specs=row,
        out_shape=jax.ShapeDtypeStruct((t, d), F32),
        compiler_params=pltpu.CompilerParams(dimension_semantics=("arbitrary",), vmem_limit_bytes=VMEM_LIMIT),
        name="rwkv_out",
    )(yf, yb, r, k, v, g, a, a, params, e, e.T, w_out.astype(BF16))


def _peer_stats(hx_ref, wkh_ref, wkl_ref, sc_ref, hxb_ref, hxl_ref, n_ref, f0_ref, r1_ref, e1_ref):
    tm = hx_ref.shape[0]
    hx_hi, hx_lo = _split(hx_ref[...].T, 2)
    hxb_ref[...] = hx_hi
    hxl_ref[...] = hx_lo
    rows_per = 4 * LANES

    def scores(c, carry):
        rows = pl.ds(pl.multiple_of(c * rows_per, rows_per), rows_per)
        wk_hi = wkh_ref[rows, :]
        sc_ref[rows, :] = (_d(wk_hi, hxl_ref[...], NN) + _d(wkl_ref[rows, :], hxb_ref[...], NN)) \
            + _d(wk_hi, hxb_ref[...], NN)
        return carry

    lax.fori_loop(0, wkh_ref.shape[0] // rows_per, scores, 0)

    def block(tb, carry):
        lanes = pl.ds(pl.multiple_of(tb * LANES, LANES), LANES)
        neg = jnp.full((N_KEYS, LANES), -jnp.inf, F32)
        head_row = lax.broadcasted_iota(jnp.int32, (PEER_HEADS, LANES), 0)

        def extract(h, tops):
            tops = [list(t) for t in tops]
            rows = [pl.ds(pl.multiple_of((2 * h + p) * N_KEYS, N_KEYS), N_KEYS) for p in range(2)]
            cur = [sc_ref[rows[p], lanes] for p in range(2)]
            rank = jnp.full((N_KEYS, LANES), float(PEER_TOPK), F32)
            for a in range(PEER_TOPK):
                m = [jnp.max(c, axis=0, keepdims=True) for c in cur]
                hit = [c >= mm for c, mm in zip(cur, m)]
                cur = [jnp.where(ht, neg, c) for ht, c in zip(hit, cur)]
                rank = jnp.where(hit[1], float(a), rank)
                for p in range(2):
                    tops[p][a] = jnp.where(head_row == h, m[p], tops[p][a])
            r1_ref[h, :, lanes] = rank.astype(BF16)
            return tuple(tuple(t) for t in tops)

        zero = jnp.zeros((PEER_HEADS, LANES), F32)
        top0, top1 = lax.fori_loop(0, PEER_HEADS, extract, ((zero,) * PEER_TOPK,) * 2)
        cands = [top0[a] + top1[b]
                 for a in range(PEER_TOPK) for b in range(PEER_TOPK) if (a + 1) * (b + 1) <= PEER_TOPK]
        c_max = cands[0]
        z = jnp.zeros_like(c_max)
        tau = c_max
        for a in range(PEER_TOPK):
            tau = functools.reduce(jnp.maximum, cands)
            z = z + jnp.exp(tau - c_max)
            cands = [jnp.where(cd >= tau, -jnp.inf, cd) for cd in cands]
        inv_z = 1.0 / z

        def factors(h, carry):
            row_of = lambda x: jnp.max(jnp.where(head_row == h, x, -jnp.inf), axis=0, keepdims=True)
            s0 = sc_ref[pl.ds(pl.multiple_of(2 * h * N_KEYS, N_KEYS), N_KEYS), lanes]
            s1 = sc_ref[pl.ds(pl.multiple_of((2 * h + 1) * N_KEYS, N_KEYS), N_KEYS), lanes]
            tau_h = row_of(tau)
            n = jnp.zeros((N_KEYS, LANES), F32)
            for b in range(PEER_TOPK):
                n = jnp.where(s0 + row_of(top1[b]) >= tau_h, float(b + 1), n)
            n_ref[h, :, lanes] = n
            f0_ref[h, :, lanes] = jnp.exp(s0 - row_of(top0[0])) * row_of(inv_z)
            e1_ref[h, :, lanes] = jnp.exp(s1 - row_of(top1[0])).astype(BF16)
            return carry

        lax.fori_loop(0, PEER_HEADS, factors, 0)
        return carry

    lax.fori_loop(0, tm // LANES, block, 0)


PEER_I_GROUP = 4
PEER_STREAMS = 4


def _peer_body(hx_ref, wkh_ref, wkl_ref, *rest):
    u_refs, vt_refs = rest[:PEER_STREAMS], rest[PEER_STREAMS:2 * PEER_STREAMS]
    o_ref, sc_ref, hxb_ref, hxl_ref, n_ref, f0_ref, r1_ref, e1_ref, w_ref, acc_ref = rest[2 * PEER_STREAMS:]
    e = pl.program_id(1)
    tm = hx_ref.shape[0]
    nsb = tm // PEER_SB

    @pl.when(e == 0)
    def _():
        _peer_stats(hx_ref, wkh_ref, wkl_ref, sc_ref, hxb_ref, hxl_ref, n_ref, f0_ref, r1_ref, e1_ref)
        acc_ref[...] = jnp.zeros(acc_ref.shape, F32)

    i_rows = pl.ds(pl.multiple_of(e * PEER_I_BLOCK, PEER_I_BLOCK), PEER_I_BLOCK)

    def activations(sb):
        parts = []
        for u_ref in u_refs:
            act = _d(u_ref[...], hxb_ref[:, sb * PEER_SB:(sb + 1) * PEER_SB], NN)
            parts.append((0.5 * act * (1.0 + lax.erf(act * (2.0 ** -0.5)))).astype(BF16))
        return parts

    def gates(sb, act):
        for hb in range(PEER_SB // LANES):
            lanes = slice(sb * PEER_SB + hb * LANES, sb * PEER_SB + (hb + 1) * LANES)
            sub = slice(hb * LANES, (hb + 1) * LANES)
            n8 = [n_ref[h, i_rows, lanes].astype(BF16) for h in range(PEER_HEADS)]
            f8 = [f0_ref[h, i_rows, lanes].astype(BF16) for h in range(PEER_HEADS)]
            for ig in range(0, PEER_I_BLOCK, PEER_I_GROUP):
                g = [jnp.zeros((N_KEYS, LANES), BF16) for _ in range(PEER_I_GROUP)]
                for h in range(PEER_HEADS):
                    r1 = r1_ref[h, :, lanes]
                    e1 = e1_ref[h, :, lanes]
                    for k in range(PEER_I_GROUP):
                        ii = ig + k
                        g[k] = g[k] + jnp.where(r1 < n8[h][ii:ii + 1], e1 * f8[h][ii:ii + 1], jnp.zeros_like(e1))
                for k in range(PEER_I_GROUP):
                    rows = slice((ig + k) * N_KEYS, (ig + k + 1) * N_KEYS)
                    part, off = divmod((ig + k) * N_KEYS, PEER_TE // PEER_STREAMS)
                    w_ref[sb, rows, sub] = g[k] * act[part][off:off + N_KEYS, sub]

    def accumulate(sb):
        cols = slice(sb * PEER_SB, (sb + 1) * PEER_SB)
        dr = acc_ref.shape[0] // PEER_STREAMS
        for k, vt_ref in enumerate(vt_refs):
            acc_ref[k * dr:(k + 1) * dr, cols] += _d(vt_ref[...], w_ref[sb], NN)

    act = activations(0)
    for sb in range(nsb):
        nxt = activations(sb + 1) if sb + 1 < nsb else None
        gates(sb, act)
        accumulate(sb)
        act = nxt

    @pl.when(e == pl.num_programs(1) - 1)
    def _():
        o_ref[...] = acc_ref[...].T


def _fold_body(k_ref, w_ref, o_ref):
    o_ref[...] = _mdot(k_ref[...], w_ref[...], NN, 6)


def peer_fold_keys(wq, keys):
    d = wq.shape[0]
    nhp, nk, dk = keys.shape
    wqt = wq.T.reshape(nhp, dk, d)
    return pl.pallas_call(
        _fold_body,
        grid=(nhp,),
        in_specs=[pl.BlockSpec((None, nk, dk), lambda i: (i, 0, 0)),
                  pl.BlockSpec((None, dk, d), lambda i: (i, 0, 0))],
        out_specs=pl.BlockSpec((nk, d), lambda i: (i, 0)),
        out_shape=jax.ShapeDtypeStruct((nhp * nk, d), F32),
        compiler_params=pltpu.CompilerParams(dimension_semantics=("arbitrary",), vmem_limit_bytes=VMEM_LIMIT),
        name="peer_fold_keys",
    )(keys, wqt)


def peer(hx, wk, u_bf, vt_bf):
    t, d = hx.shape
    tm = PEER_TM
    assert t % tm == 0
    wk_hi, wk_lo = _split(wk, 2)
    ne = u_bf.shape[0] // PEER_TE
    h = PEER_HEADS
    ns = PEER_STREAMS
    return pl.pallas_call(
        _peer_body,
        grid=(t // tm, ne),
        in_specs=[pl.BlockSpec((tm, d), lambda i, e: (i, 0), pipeline_mode=pl.Buffered(1)),
                  pl.BlockSpec(wk.shape, lambda i, e: (0, 0), pipeline_mode=pl.Buffered(1)),
                  pl.BlockSpec(wk.shape, lambda i, e: (0, 0), pipeline_mode=pl.Buffered(1)),
                  *[pl.BlockSpec((PEER_TE // ns, d), functools.partial(lambda i, e, k: (e * ns + k, 0), k=k))
                    for k in range(ns)],
                  *[pl.BlockSpec((None, d // ns, PEER_TE), functools.partial(lambda i, e, k: (e, k, 0), k=k))
                    for k in range(ns)]],
        out_specs=pl.BlockSpec((tm, d), lambda i, e: (i, 0)),
        out_shape=jax.ShapeDtypeStruct((t, d), F32),
        scratch_shapes=[pltpu.VMEM((2 * h * N_KEYS, tm), F32),
                        pltpu.VMEM((d, tm), BF16), pltpu.VMEM((d, tm), BF16),
                        pltpu.VMEM((h, N_KEYS, tm), F32), pltpu.VMEM((h, N_KEYS, tm), F32),
                        pltpu.VMEM((h, N_KEYS, tm), BF16), pltpu.VMEM((h, N_KEYS, tm), BF16),
                        pltpu.VMEM((tm // PEER_SB, PEER_TE, PEER_SB), BF16), pltpu.VMEM((d, tm), F32)],
        compiler_params=pltpu.CompilerParams(
            dimension_semantics=("arbitrary", "arbitrary"), vmem_limit_bytes=VMEM_LIMIT),
        name="peer_dense",
    )(hx, wk_hi, wk_lo, *([u_bf] * ns), *([vt_bf] * ns))


LN_TM = 256


def _ln_mod_body(xs_ref, y_ref, mg_ref, mn_ref, lng_ref, lnb_ref, xo_ref, ho_ref, *, gate_col, mod_col, nct, nt):
    i = pl.program_id(0)
    d = xs_ref.shape[1]
    is_ctx = i < nct

    def pick(ref, col):
        return jnp.where(is_ctx, ref[1:2, col * d:(col + 1) * d], ref[0:1, col * d:(col + 1) * d])

    z = DN_ALPHA * xs_ref[...] + pick(mg_ref, gate_col) * y_ref[...]
    mu = jnp.mean(z, axis=-1, keepdims=True)
    zc = z - mu
    var = jnp.mean(zc * zc, axis=-1, keepdims=True)
    xn = zc * lax.rsqrt(var + LN_EPS) * lng_ref[...] + lnb_ref[...]
    xo_ref[...] = xn
    h = xn * (1.0 + pick(mn_ref, mod_col + 1)) + pick(mn_ref, mod_col)
    ho_ref[...] = jnp.where(i < nt, h, 0.0)


def ln_mod(xs, y, mod_gate, gate_col, mod_next, mod_col, ln_g, ln_b, n_ctx, pad_to):
    t, d = xs.shape
    tm = LN_TM
    assert t % tm == 0 and n_ctx % tm == 0 and pad_to % tm == 0 and pad_to >= t
    nt = t // tm
    row = lambda i: (jnp.minimum(i, nt - 1), 0)
    full = lambda a: pl.BlockSpec(a.shape, lambda i: (0, 0))
    return pl.pallas_call(
        functools.partial(_ln_mod_body, gate_col=gate_col, mod_col=mod_col, nct=n_ctx // tm, nt=nt),
        grid=(pad_to // tm,),
        in_specs=[pl.BlockSpec((tm, d), row), pl.BlockSpec((tm, d), row), full(mod_gate), full(mod_next),
                  pl.BlockSpec((1, d), lambda i: (0, 0)), pl.BlockSpec((1, d), lambda i: (0, 0))],
        out_specs=[pl.BlockSpec((tm, d), row), pl.BlockSpec((tm, d), lambda i: (i, 0))],
        out_shape=[jax.ShapeDtypeStruct((t, d), F32), jax.ShapeDtypeStruct((pad_to, d), F32)],
        compiler_params=pltpu.CompilerParams(dimension_semantics=("arbitrary",), vmem_limit_bytes=VMEM_LIMIT),
        name="ln_mod",
    )(xs, y, mod_gate, mod_next, ln_g.reshape(1, d), ln_b.reshape(1, d))


def _ln(x, g, b):
    mu = jnp.mean(x, -1, keepdims=True)
    xc = x - mu
    var = jnp.mean(xc * xc, -1, keepdims=True)
    return xc * lax.rsqrt(var + LN_EPS) * g + b


def _head_norm(h, nheads, eps):
    t = h.shape[0]
    hh = h.reshape(t, nheads, -1)
    mu = jnp.mean(hh, -1, keepdims=True)
    xc = hh - mu
    var = jnp.mean(xc * xc, -1, keepdims=True)
    return (xc * lax.rsqrt(var + eps)).reshape(t, -1)


def _pad_cols(w, n):
    return jnp.pad(w, ((0, 0), (0, n - w.shape[1])))


def _shift_rows(u, k):
    if k > 0:
        return jnp.concatenate([u[k:], jnp.zeros_like(u[:k])], axis=0)
    return jnp.concatenate([jnp.zeros_like(u[:-k]), u[:k]], axis=0)


def _grid_conv(u, w, b):
    s = u.shape[0]
    colid = (jnp.arange(s) % GRID_W)[:, None]
    out = jnp.zeros_like(u) + b
    for di in range(3):
        for dj in range(3):
            off = (di - 1) * GRID_W + (dj - 1)
            sh = _shift_rows(u, off) if off else u
            if dj == 0:
                sh = jnp.where(colid == 0, 0.0, sh)
            elif dj == 2:
                sh = jnp.where(colid == GRID_W - 1, 0.0, sh)
            out = out + sh * w[di, dj]
    return out


def _seq_conv(u, w, b):
    wc = w[1]
    return _shift_rows(u, -1) * wc[0] + u * wc[1] + _shift_rows(u, 1) * wc[2] + b


def _qshift(u):
    q = u.shape[1] // 4
    colid = (jnp.arange(u.shape[0]) % GRID_W)[:, None]
    left = jnp.where(colid == 0, 0.0, _shift_rows(u[:, :q], -1))
    right = jnp.where(colid == GRID_W - 1, 0.0, _shift_rows(u[:, q:2 * q], 1))
    up = _shift_rows(u[:, 2 * q:3 * q], -GRID_W)
    down = _shift_rows(u[:, 3 * q:], GRID_W)
    return jnp.concatenate([left, right, up, down], axis=1)


def _shift_seq(u):
    h = u.shape[1] // 2
    return jnp.concatenate([_shift_rows(u[:, :h], -1), _shift_rows(u[:, h:], 1)], axis=1)


def _mlstm_layer(h, n_ctx, w_in, b_in, conv_w, conv_b, hn_g, w_out):
    d = D_MODEL
    t = h.shape[0]
    qk_pre = matmul(h, w_in[:, :2 * d], b_in[:2 * d])
    v = matmul(h, w_in[:, 2 * d:3 * d], b_in[2 * d:3 * d], out_dtype=BF16)
    o = matmul(h, w_in[:, 3 * d:4 * d], b_in[3 * d:4 * d], act="sigmoid", out_dtype=BF16)
    g = matmul(h, _pad_cols(w_in[:, 4 * d:], LANES), jnp.pad(b_in[4 * d:], (0, LANES - 4 * MLSTM_HEADS)))
    g = g[:, :4 * MLSTM_HEADS].reshape(t, 4, MLSTM_HEADS)
    g = jnp.concatenate([g[:, :2], jax.nn.log_sigmoid(g[:, 2:])], axis=1)
    gh = jnp.transpose(g, (2, 0, 1))
    gcol = jnp.pad(gh, ((0, 0), (0, 0), (0, LANES - 4)))
    grow = jnp.pad(jnp.transpose(gh, (0, 2, 1)), ((0, 0), (0, 4), (0, 0)))
    qk = mlstm_conv(qk_pre, conv_w, conv_b, n_ctx)
    hf, hb = mlstm_scan(qk, v, gcol, grow, n_ctx)
    return mlstm_out(hf, hb, o, hn_g, w_out)


def _rwkv_layer(h, n_ctx, mu, w_rkv, w0, w1, w2, a0, a1, a2, g1, g2, k_k, k_a, r_k, lnx_g, lnx_b, w_out):
    d = D_MODEL
    xm = rwkv_mix(h, mu, n_ctx)
    r = matmul(xm[0], w_rkv[0])
    k = matmul(xm[1], w_rkv[1])
    v = matmul(xm[2], w_rkv[2])

    def lora_pair(x, w_in, w_mid, bias, act_mid, act_out):
        rank = w_in.shape[-1]
        w_a = _pad_cols(jnp.concatenate([w_in[0], w_in[1]], axis=1), LANES)
        zpad = jnp.zeros((rank, d), F32)
        w_b = jnp.concatenate([jnp.concatenate([w_mid[0], zpad], axis=1),
                               jnp.concatenate([zpad, w_mid[1]], axis=1)], axis=0)
        w_b = jnp.pad(w_b, ((0, LANES - 2 * rank), (0, 0)))
        mid = matmul(x, w_a, act=act_mid, out_dtype=BF16)
        return matmul(mid, w_b, jnp.concatenate([bias[0], bias[1]]), act=act_out)

    lw = lora_pair(xm[3], w1, w2, w0, "tanh", "logdecay")
    a = lora_pair(xm[4], a1, a2, a0, None, "sigmoid")
    gpad = 2 * LANES
    gg = matmul(xm[5], _pad_cols(g1, gpad), act="sigmoid", out_dtype=BF16)
    g = matmul(gg, jnp.pad(g2, ((0, gpad - g1.shape[1]), (0, 0))))
    yf, yb = rwkv_scan(r, k, v, lw, a, k_k, k_a, n_ctx)
    return rwkv_out(yf, yb, r, k, v, g, a, k_a, r_k, lnx_g, lnx_b, w_out)


def _forward(x, c, ctx, c_ctx, ada_w, ada_b, ln_g, ln_b,
             ml_w_in, ml_b_in, ml_conv_w, ml_conv_b, ml_hn_g, ml_w_out,
             rw_mu, rw_w_rkv, rw_w0, rw_w1, rw_w2, rw_a0, rw_a1, rw_a2, rw_g1, rw_g2,
             rw_k_k, rw_k_a, rw_r_k, rw_lnx_g, rw_lnx_b, rw_w_out,
             pk_wq, pk_keys, pk_u, pk_v):
    d = D_MODEL
    n_ctx = ctx.shape[1]
    n_lat = x.shape[1]
    xs = jnp.concatenate([ctx[0], x[0]], axis=0)
    t = xs.shape[0]
    t_pad = -(-t // PEER_TM) * PEER_TM
    s_in = jnp.zeros((8, d), F32).at[0].set(jax.nn.silu(c[0])).at[1].set(jax.nn.silu(c_ctx))
    depth = ada_w.shape[0]
    mods = [matmul(s_in, ada_w[i], ada_b[i], passes=3) for i in range(depth)]
    is_ctx = (jnp.arange(t) < n_ctx)[:, None]
    m0 = [jnp.where(is_ctx, mods[0][1, n * d:(n + 1) * d], mods[0][0, n * d:(n + 1) * d]) for n in range(2)]
    h = xs * (1.0 + m0[1]) + m0[0]
    for i in range(depth):
        j = i // 2
        if i % 2 == 0:
            y = _mlstm_layer(h, n_ctx, ml_w_in[j], ml_b_in[j], ml_conv_w[j], ml_conv_b[j],
                             ml_hn_g[j], ml_w_out[j])
        else:
            y = _rwkv_layer(h, n_ctx, rw_mu[j], rw_w_rkv[j], rw_w0[j], rw_w1[j], rw_w2[j],
                            rw_a0[j], rw_a1[j], rw_a2[j], rw_g1[j], rw_g2[j], rw_k_k[j],
                            rw_k_a[j], rw_r_k[j], rw_lnx_g[j], rw_lnx_b[j], rw_w_out[j])
        xs, h = ln_mod(xs, y, mods[i], 2, mods[i], 3, ln_g[i, 0], ln_b[i, 0], n_ctx, t_pad)
        wk = peer_fold_keys(pk_wq[i], pk_keys[i].reshape(2 * PEER_HEADS, N_KEYS, PEER_DQ // 2))
        vt = jnp.swapaxes(pk_v[i].astype(BF16).reshape(-1, PEER_TE, d), 1, 2)
        y = peer(h, wk, pk_u[i].astype(BF16), vt)
        xs, h = ln_mod(xs, y, mods[i], 5, mods[min(i + 1, depth - 1)], 0, ln_g[i, 1], ln_b[i, 1], n_ctx, t)
    return xs[n_ctx:][None]


def kernel(x, c, ctx, c_ctx, ada_w, ada_b, ln_g, ln_b, ml_w_in, ml_b_in, ml_conv_w, ml_conv_b, ml_hn_g, ml_w_out, rw_mu, rw_w_rkv, rw_w0, rw_w1, rw_w2, rw_a0, rw_a1, rw_a2, rw_g1, rw_g2, rw_k_k, rw_k_a, rw_r_k, rw_lnx_g, rw_lnx_b, rw_w_out, pk_wq, pk_keys, pk_u, pk_v):
    return _forward(x, c, ctx, c_ctx, ada_w, ada_b, ln_g, ln_b,
                    ml_w_in, ml_b_in, ml_conv_w, ml_conv_b, ml_hn_g, ml_w_out,
                    rw_mu, rw_w_rkv, rw_w0, rw_w1, rw_w2, rw_a0, rw_a1, rw_a2, rw_g1, rw_g2,
                    rw_k_k, rw_k_a, rw_r_k, rw_lnx_g, rw_lnx_b, rw_w_out,
                    pk_wq, pk_keys, pk_u, pk_v)
```

```python
import functools

import jax
import jax.numpy as jnp
from jax import lax
from jax.experimental import pallas as pl
from jax.experimental.pallas import tpu as pltpu

F32 = jnp.float32
BF16 = jnp.bfloat16

D_MODEL = 1024
DEPTH = 4
GRID_W = 64
N_MOD = 6
DN_ALPHA = (2.0 * DEPTH) ** 0.25
LN_EPS = 1e-5

MLSTM_HEADS = 4
MLSTM_DH = D_MODEL // MLSTM_HEADS
MLSTM_CHUNK = 128
M_INIT = -1e30

RWKV_N = 64
RWKV_HEADS = D_MODEL // RWKV_N
RWKV_CHUNK = 64
RWKV_GN_EPS = 64e-5

N_KEYS = 128
PEER_HEADS = 8
PEER_DQ = 256
PEER_TOPK = 16
PEER_I_BLOCK = 8
PEER_TE = PEER_I_BLOCK * N_KEYS
PEER_TM = 512
PEER_SB = 256

LANES = 128
VMEM_LIMIT = 62 * 1024 * 1024

NN = ((1,), (0,))
NT = ((1,), (1,))
TN = ((0,), (0,))


def _split(x, n):
    parts = []
    r = x.astype(F32)
    for i in range(n):
        p = r.astype(BF16)
        parts.append(p)
        if i + 1 < n:
            r = r - p.astype(F32)
    return parts


def _d(a, b, dims):
    return lax.dot_general(a, b, (dims, ((), ())), preferred_element_type=F32)


def _mdot(a, b, dims, passes):
    if passes == 1:
        return _d(a.astype(BF16), b.astype(BF16), dims)
    if passes == 3:
        a0, a1 = _split(a, 2)
        b0, b1 = _split(b, 2)
        return (_d(a0, b1, dims) + _d(a1, b0, dims)) + _d(a0, b0, dims)
    a0, a1, a2 = _split(a, 3)
    b0, b1, b2 = _split(b, 3)
    lo = (_d(a0, b2, dims) + _d(a2, b0, dims)) + _d(a1, b1, dims)
    mid = _d(a0, b1, dims) + _d(a1, b0, dims)
    return (lo + mid) + _d(a0, b0, dims)


def _dot_01_lhs(m01, x):
    mb = m01.astype(BF16)
    x0, x1, x2 = _split(x, 3)
    return (_d(mb, x2, NN) + _d(mb, x1, NN)) + _d(mb, x0, NN)


def _dot_01_rhs(x, m01):
    mb = m01.astype(BF16)
    x0, x1, x2 = _split(x, 3)
    return (_d(x2, mb, NN) + _d(x1, mb, NN)) + _d(x0, mb, NN)


def _pick(n, cands):
    for c in cands:
        if n % c == 0:
            return c
    raise ValueError(f"no tile for {n}")


_ACTS = {None: lambda x: x, "sigmoid": jax.nn.sigmoid, "tanh": jnp.tanh,
         "logdecay": lambda x: -(2.718281828459045 ** -0.5) * jax.nn.sigmoid(x)}


def _mm_body(x_ref, w_ref, b_ref, o_ref, *, passes, act):
    o_ref[...] = _ACTS[act](_mdot(x_ref[...], w_ref[...], NN, passes) + b_ref[...]).astype(o_ref.dtype)


def matmul(x, w, b=None, *, passes=1, act=None, out_dtype=F32):
    m, k = x.shape
    n = w.shape[1]
    assert n % LANES == 0 and w.shape[0] == k
    tm = m if m <= 1024 else _pick(m, (640, 512, 384, 256, 128))
    tn = _pick(n, (1024, 768, 640, 512, 384, 256, 128))
    if b is None:
        b = jnp.zeros((n,), F32)
    if passes == 1:
        w = w.astype(BF16)
    return pl.pallas_call(
        functools.partial(_mm_body, passes=passes, act=act),
        grid=(m // tm, n // tn),
        in_specs=[pl.BlockSpec((tm, k), lambda i, j: (i, 0)),
                  pl.BlockSpec((k, tn), lambda i, j: (0, j)),
                  pl.BlockSpec((1, tn), lambda i, j: (0, j))],
        out_specs=pl.BlockSpec((tm, tn), lambda i, j: (i, j)),
        out_shape=jax.ShapeDtypeStruct((m, n), out_dtype),
        compiler_params=pltpu.CompilerParams(
            dimension_semantics=("arbitrary", "arbitrary"), vmem_limit_bytes=VMEM_LIMIT),
        name="proj_matmul",
    )(x, w, b.reshape(1, n).astype(F32))


CONV_TM = 256
CONV_CB = 512


def _conv_body(cur_ref, prev_ref, next_ref, w_ref, b_ref, o_ref, *, nct, nt, q_blocks, q_scale):
    i = pl.program_id(0)
    j = pl.program_id(1)
    is_ctx = i < nct
    tm = cur_ref.shape[0]
    above = jnp.where(is_ctx | (i == nct), 0.0, prev_ref[...])
    below = jnp.where(is_ctx | (i == nt - 1), 0.0, next_ref[...])
    ext = jnp.concatenate([above, cur_ref[...], below], axis=0)
    n = ext.shape[0]
    colid = lax.broadcasted_iota(jnp.int32, ext.shape, 0) % GRID_W
    left = jnp.where(is_ctx | (colid != 0), pltpu.roll(ext, 1, 0), 0.0)
    right = jnp.where(is_ctx | (colid != GRID_W - 1), pltpu.roll(ext, n - 1, 0), 0.0)
    w = w_ref[...]
    acc = jnp.zeros((tm, ext.shape[1]), F32) + b_ref[...]
    for di in range(3):
        rows = slice(di * GRID_W, di * GRID_W + tm)
        tap = left[rows] * w[3 * di:3 * di + 1] + ext[rows] * w[3 * di + 1:3 * di + 2] \
            + right[rows] * w[3 * di + 2:3 * di + 3]
        acc = acc + (tap if di == 1 else jnp.where(is_ctx, 0.0, tap))
    y = acc * jax.nn.sigmoid(acc)
    o_ref[...] = (y * jnp.where(j < q_blocks, q_scale, 1.0)).astype(o_ref.dtype)


def mlstm_conv(qk_pre, conv_w, conv_b, n_ctx):
    t, c = qk_pre.shape
    tm, cb = CONV_TM, CONV_CB
    assert t % tm == 0 and n_ctx % tm == 0 and tm % GRID_W == 0 and c % (2 * cb) == 0
    r = tm // GRID_W
    nt, nu = t // tm, t // GRID_W
    return pl.pallas_call(
        functools.partial(_conv_body, nct=n_ctx // tm, nt=nt, q_blocks=c // (2 * cb), q_scale=MLSTM_DH ** -0.5),
        grid=(nt, c // cb),
        in_specs=[pl.BlockSpec((tm, cb), lambda i, j: (i, j)),
                  pl.BlockSpec((GRID_W, cb), lambda i, j: (jnp.maximum(i * r - 1, 0), j)),
                  pl.BlockSpec((GRID_W, cb), lambda i, j: (jnp.minimum((i + 1) * r, nu - 1), j)),
                  pl.BlockSpec((9, cb), lambda i, j: (0, j)),
                  pl.BlockSpec((1, cb), lambda i, j: (0, j))],
        out_specs=pl.BlockSpec((tm, cb), lambda i, j: (i, j)),
        out_shape=jax.ShapeDtypeStruct((t, c), BF16),
        compiler_params=pltpu.CompilerParams(
            dimension_semantics=("arbitrary", "arbitrary"), vmem_limit_bytes=VMEM_LIMIT),
        name="mlstm_conv",
    )(qk_pre, qk_pre, qk_pre, conv_w.reshape(9, c), conv_b.reshape(1, c))


def _mlstm_out_body(hf_ref, hb_ref, o_ref, g_ref, w_ref, y_ref):
    h = hf_ref[...] + hb_ref[...]
    parts = []
    for a in range(MLSTM_HEADS):
        x = h[:, a * MLSTM_DH:(a + 1) * MLSTM_DH]
        mu = jnp.mean(x, axis=-1, keepdims=True)
        xc = x - mu
        var = jnp.mean(xc * xc, axis=-1, keepdims=True)
        parts.append(xc * lax.rsqrt(var + LN_EPS))
    hn = jnp.concatenate(parts, axis=1)
    z = o_ref[...].astype(F32) * hn * g_ref[...]
    y_ref[...] = _d(z.astype(BF16), w_ref[...], NN)


def mlstm_out(hf, hb, o, hn_g, w_out):
    t, d = hf.shape
    tm = LN_TM
    row = pl.BlockSpec((tm, d), lambda i: (i, 0))
    return pl.pallas_call(
        _mlstm_out_body,
        grid=(t // tm,),
        in_specs=[row, row, row, pl.BlockSpec((1, d), lambda i: (0, 0)), pl.BlockSpec((d, d), lambda i: (0, 0))],
        out_specs=row,
        out_shape=jax.ShapeDtypeStruct((t, d), F32),
        compiler_params=pltpu.CompilerParams(dimension_semantics=("arbitrary",), vmem_limit_bytes=VMEM_LIMIT),
        name="mlstm_out",
    )(hf, hb, o, hn_g.reshape(1, d), w_out.astype(BF16))


def _mlstm_body(qf, kf, vf, gcf, grf, qb, kb, vb, gcb, grb, hf_ref, hb_ref, c_ref, n_ref, m_ref):
    @pl.when(pl.program_id(1) == 0)
    def _():
        c_ref[...] = jnp.zeros(c_ref.shape, F32)
        n_ref[...] = jnp.zeros(n_ref.shape, F32)
        m_ref[...] = jnp.full(m_ref.shape, M_INIT, F32)

    ds = (0, 1)
    each = lambda f, *ls: [f(*xs) for xs in zip(*ls)]
    q, k, v = [qf[...], qb[...]], [kf[...], kb[...]], [vf[...], vb[...]]
    gc, gr = [gcf[...], gcb[...]], [grf[...], grb[...]]
    L = q[0].shape[0]
    row = lax.broadcasted_iota(jnp.int32, (L, L), 0)
    col = lax.broadcasted_iota(jnp.int32, (L, L), 1)
    seen = (col <= row, col >= row)
    tri = [jnp.where(m, 1.0, 0.0) for m in seen]
    b_col = [_dot_01_lhs(tri[d], gc[d])[:, 2 + d:3 + d] for d in ds]
    b_row = [_dot_01_rhs(gr[d], tri[1 - d])[2 + d:3 + d, :] for d in ds]
    ig_col = [gc[d][:, d:d + 1] for d in ds]
    ig_row = [gr[d][d:d + 1, :] for d in ds]
    m_st = [m_ref[d, 0:1, 0:1] for d in ds]
    c_st = [c_ref[d] for d in ds]
    n_st = [n_ref[d] for d in ds]

    dlog = [jnp.where(seen[d], b_col[d] - b_row[d] + ig_row[d], -jnp.inf) for d in ds]
    m_inter = each(jnp.add, b_col, m_st)
    m_t = each(lambda mi, dl: jnp.maximum(mi, jnp.max(dl, axis=1, keepdims=True)), m_inter, dlog)
    qb16, kb16, vb16 = q, k, v
    q, k = (each(lambda x: x.astype(F32), a) for a in (q, k))
    qk = each(lambda a, b: _d(a, b, NT), qb16, kb16)
    s = each(lambda x, dl, mt: x * jnp.exp(dl - mt), qk, dlog, m_t)
    dec = each(lambda mi, mt: jnp.exp(mi - mt), m_inter, m_t)
    sv = each(lambda a, b: _d(a.astype(BF16), b, NN), s, vb16)
    qc = each(lambda a, b: _d(a, b.astype(BF16), NN), qb16, c_st)
    num = each(lambda a, dc, b: a + dc * b, sv, dec, qc)
    den = each(lambda x, dc, qq, nn: jnp.sum(x, axis=1, keepdims=True) + dc * jnp.sum(qq * nn, axis=1, keepdims=True),
               s, dec, q, n_st)
    h = each(lambda nu, de, mt: nu / jnp.maximum(jnp.abs(de), jnp.exp(-mt)), num, den, m_t)
    hf_ref[...] = h[0]
    hb_ref[...] = h[1]

    b_last = [b_col[0][L - 1:L, :], b_col[1][0:1, :]]
    w_c = each(lambda bl, bc, ic: bl - bc + ic, b_last, b_col, ig_col)
    m_new = each(lambda bl, ms, w: jnp.maximum(bl + ms, jnp.max(w, axis=0, keepdims=True)), b_last, m_st, w_c)
    a_c = each(lambda w, mn: jnp.exp(w - mn), w_c, m_new)
    g_prev = each(lambda bl, ms, mn: jnp.exp(bl + ms - mn), b_last, m_st, m_new)
    ak = each(jnp.multiply, a_c, k)
    kv = each(lambda a, b: _d(a.T.astype(BF16), b, NN), ak, vb16)
    for d in ds:
        c_ref[d] = g_prev[d] * c_st[d] + kv[d]
        n_ref[d] = g_prev[d] * n_st[d] + jnp.sum(ak[d], axis=0, keepdims=True)
        m_ref[d] = jnp.broadcast_to(m_new[d], m_ref.shape[1:])


def _bwd_chunk(c, nc0, nc):
    return jnp.where(c < nc0, nc0 - 1 - c, nc - 1 - (c - nc0))


def mlstm_scan(qk, v, gcol, grow, n_ctx):
    t = v.shape[0]
    L, dh = MLSTM_CHUNK, MLSTM_DH
    nc, nc0 = t // L, n_ctx // L
    fwd = lambda h, c: (c, h)
    bwd = lambda h, c: (_bwd_chunk(c, nc0, nc), h)
    fwd_k = lambda h, c: (c, MLSTM_HEADS + h)
    bwd_k = lambda h, c: (_bwd_chunk(c, nc0, nc), MLSTM_HEADS + h)
    qkv = lambda im: pl.BlockSpec((L, dh), im)
    gc_spec = lambda f: pl.BlockSpec((None, L, LANES), lambda h, c: (h, f(c), 0))
    gr_spec = lambda f: pl.BlockSpec((None, 8, L), lambda h, c: (h, 0, f(c)))
    idf = lambda c: c
    idb = lambda c: _bwd_chunk(c, nc0, nc)
    return pl.pallas_call(
        _mlstm_body,
        grid=(MLSTM_HEADS, nc),
        in_specs=[qkv(fwd), qkv(fwd_k), qkv(fwd), gc_spec(idf), gr_spec(idf),
                  qkv(bwd), qkv(bwd_k), qkv(bwd), gc_spec(idb), gr_spec(idb)],
        out_specs=[qkv(fwd), qkv(bwd)],
        out_shape=[jax.ShapeDtypeStruct((t, D_MODEL), F32)] * 2,
        scratch_shapes=[pltpu.VMEM((2, dh, dh), F32), pltpu.VMEM((2, 1, dh), F32),
                        pltpu.VMEM((2, 8, LANES), F32)],
        compiler_params=pltpu.CompilerParams(
            dimension_semantics=("arbitrary", "arbitrary"), vmem_limit_bytes=VMEM_LIMIT),
        name="mlstm_scan",
    )(qk, qk, v, gcol, grow, qk, qk, v, gcol, grow)


RW_PASSES = 1
RW_CHUNKS_PER_STEP = 4


def _stack2(x, lane_head):
    return jnp.concatenate([jnp.where(lane_head == 0, x, 0.0), jnp.where(lane_head == 1, x, 0.0)], axis=0)


def _unstack2(x):
    L = x.shape[0] // 2
    return x[:L] + x[L:]


def _rwkv_chunks(chains, k_k, k_a):
    L = chains[0][0].shape[0]
    n2 = 2 * L
    p = RW_PASSES
    ds = [c[5] for c in chains]
    each = lambda f, *ls: [f(*xs) for xs in zip(*ls)]
    dot = lambda dims: (lambda a, b: _mdot(a, b, dims, p))
    lw, r, k_raw, v, a = ([c[i] for c in chains] for i in range(5))

    lane_r = lax.broadcasted_iota(jnp.int32, (LANES, LANES), 0) // RWKV_N
    lane_c = lax.broadcasted_iota(jnp.int32, (LANES, LANES), 1) // RWKV_N
    head_ones = jnp.where(lane_r == lane_c, 1.0, 0.0)
    kkr = each(lambda x: x * k_k, k_raw)
    ss = each(lambda x: _dot_01_rhs(x * x, head_ones), kkr)
    kap = each(lambda x, q: x / jnp.maximum(jnp.sqrt(q), 1e-12), kkr, ss)
    alp = each(jnp.multiply, kap, a)
    k = each(lambda x, y: x * (1.0 + (y - 1.0) * k_a), k_raw, a)

    row = lax.broadcasted_iota(jnp.int32, (L, L), 0)
    col = lax.broadcasted_iota(jnp.int32, (L, L), 1)
    tris = (jnp.where(col <= row, 1.0, 0.0), jnp.where(col >= row, 1.0, 0.0))
    lp = [_dot_01_lhs(tris[d], x) for d, x in zip(ds, lw)]
    lp_end = [x[0:1, :] if d else x[L - 1:L, :] for d, x in zip(ds, lp)]
    e_neg = each(lambda x: jnp.exp(-x), lp)
    e_end = each(lambda x, xe: jnp.exp(xe - x), lp, lp_end)
    kap_t = each(lambda x, y, z: x * jnp.exp(y - z), kap, lp, lw)
    r_t = each(lambda x, y: x * jnp.exp(y), r, lp)
    k_h = each(jnp.multiply, k, e_neg)
    a_h = each(jnp.multiply, alp, e_neg)
    k_e = each(jnp.multiply, k, e_end)
    a_e = each(jnp.multiply, alp, e_end)

    lane_head = lax.broadcasted_iota(jnp.int32, (L, LANES), 1) // RWKV_N
    st = lambda x: _stack2(x, lane_head)
    kap_s, r_s, v_s, k_s, a_s = (each(st, x) for x in (kap_t, r_t, v, k_e, a_e))
    rhs_k = each(lambda x: jnp.concatenate([x, x], axis=0), k_h)
    rhs_a = each(lambda x: jnp.concatenate([x, x], axis=0), a_h)

    row2 = lax.broadcasted_iota(jnp.int32, (n2, n2), 0)
    col2 = lax.broadcasted_iota(jnp.int32, (n2, n2), 1)
    same_head = (row2 // L) == (col2 // L)
    strict = (same_head & (col2 < row2), same_head & (col2 > row2))
    incl = (same_head & (col2 <= row2), same_head & (col2 >= row2))
    zero = jnp.zeros((n2, n2), F32)
    masked = lambda masks: (lambda d, x: jnp.where(masks[d], x, zero))
    n_ka = each(masked(strict), ds, each(dot(NT), kap_s, rhs_a))
    m_kk = each(masked(strict), ds, each(dot(NT), kap_s, rhs_k))
    a_rk = each(masked(incl), ds, each(dot(NT), r_s, rhs_k))
    a_ra = each(masked(incl), ds, each(dot(NT), r_s, rhs_a))

    b16 = (row2 // 16) == (col2 // 16)
    b32 = (row2 // 32) == (col2 // 32)
    eye = jnp.where(row2 == col2, 1.0, 0.0)
    n16 = each(lambda x: jnp.where(b16, x, zero), n_ka)
    n_2 = each(dot(NN), n16, n16)
    n_4 = each(dot(NN), n_2, n_2)
    n_8 = each(dot(NN), n_4, n_4)
    inv = each(lambda x: eye - x, n16)
    for pw in (n_2, n_4, n_8):
        inv = each(jnp.add, inv, each(dot(NN), inv, pw))
    for sel in (lambda x: jnp.where(b32 & ~b16, x, zero), lambda x: jnp.where(b32, zero, x)):
        t1 = each(dot(NN), inv, each(sel, n_ka))
        inv = each(jnp.subtract, inv, each(dot(NN), t1, inv))

    mv = each(dot(NN), m_kk, v_s)
    w1u0 = each(dot(NN), inv, each(lambda x, y: jnp.concatenate([x, y], axis=1), kap_s, mv))
    ar = each(dot(NN), a_ra, w1u0)
    av = each(dot(NN), a_rk, v_s)
    r2 = each(lambda x, y: _unstack2(x - y[:, :LANES]), r_s, ar)
    y0 = each(lambda x, y: _unstack2(x - y[:, LANES:]), av, ar)
    w1_t = each(lambda x: x[:, :LANES].T, w1u0)
    u0_t = each(lambda x: x[:, LANES:].T, w1u0)
    v_t = each(lambda x: x.T, v_s)
    wa = each(dot(NN), w1_t, a_s)
    vk = each(dot(NN), v_t, k_s)
    ua = each(dot(NN), u0_t, a_s)
    g = each(lambda xe, x: jnp.where(row2 == col2, jnp.exp(xe), zero) - x, lp_end, wa)
    b = each(jnp.subtract, vk, ua)
    return list(zip(y0, r2, g, b))


def _rwkv_body(lwf, rf, kf, vf, af, lwb, rb, kb, vb, ab, kk_ref, ka_ref, yf_ref, yb_ref, s_ref):
    @pl.when(pl.program_id(1) == 0)
    def _():
        s_ref[...] = jnp.zeros(s_ref.shape, F32)

    L = RWKV_CHUNK
    n = RW_CHUNKS_PER_STEP
    rows = lambda j: slice(j * L, (j + 1) * L)
    refs = ((lwf, rf, kf, vf, af), (lwb, rb, kb, vb, ab))
    visit = [(d, j if d == 0 else n - 1 - j) for j in range(n) for d in (0, 1)]
    pre = _rwkv_chunks([tuple(ref[rows(j), :] for ref in refs[d]) + (d,) for d, j in visit],
                       kk_ref[...], ka_ref[...])
    y_refs = (yf_ref, yb_ref)
    s = [s_ref[0], s_ref[1]]
    for (d, j), (y0, r2, g, b) in zip(visit, pre):
        y_refs[d][rows(j), :] = y0 + _mdot(r2, s[d], NT, RW_PASSES)
        s[d] = _mdot(s[d], g, NN, RW_PASSES) + b
    s_ref[0] = s[0]
    s_ref[1] = s[1]


def rwkv_scan(r, k, v, lw, a, k_k, k_a, n_ctx):
    t, d = r.shape
    L = RWKV_CHUNK
    blk = RW_CHUNKS_PER_STEP * L
    assert 2 * L == LANES and t % blk == 0 and n_ctx % blk == 0
    nc, nc0 = t // blk, n_ctx // blk
    nh = d // LANES
    fwd = pl.BlockSpec((blk, LANES), lambda h, c: (c, h))
    bwd = pl.BlockSpec((blk, LANES), lambda h, c: (_bwd_chunk(c, nc0, nc), h))
    bwd2 = pl.BlockSpec((blk, LANES), lambda h, c: (_bwd_chunk(c, nc0, nc), nh + h))
    par = pl.BlockSpec((1, LANES), lambda h, c: (0, h))
    return pl.pallas_call(
        _rwkv_body,
        grid=(nh, nc),
        in_specs=[fwd] * 5 + [bwd2, bwd, bwd, bwd, bwd2, par, par],
        out_specs=[fwd, bwd],
        out_shape=[jax.ShapeDtypeStruct((t, d), F32)] * 2,
        scratch_shapes=[pltpu.VMEM((2, LANES, LANES), F32)],
        compiler_params=pltpu.CompilerParams(
            dimension_semantics=("arbitrary", "arbitrary"), vmem_limit_bytes=VMEM_LIMIT),
        name="rwkv7_scan",
    )(lw, r, k, v, a, lw, r, k, v, a, k_k.reshape(1, d), k_a.reshape(1, d))


MIX_TM = 256


def _mix_body(cur_ref, prev_ref, next_ref, mu_ref, *o_refs, nct, nt):
    i = pl.program_id(0)
    is_ctx = i < nct
    cur = cur_ref[...]
    tm, d = cur.shape
    qd = d // 4
    above = jnp.where(is_ctx | (i == nct), 0.0, prev_ref[...])
    below = jnp.where(is_ctx | (i == nt - 1), 0.0, next_ref[...])
    ext = jnp.concatenate([above, cur, below], axis=0)
    n = ext.shape[0]
    colid = lax.broadcasted_iota(jnp.int32, (n, 2 * qd), 0) % GRID_W
    left = jnp.where(is_ctx | (colid != 0), pltpu.roll(ext[:, :2 * qd], 1, 0), 0.0)
    colid3 = lax.broadcasted_iota(jnp.int32, (n, 3 * qd), 0) % GRID_W
    right = jnp.where(is_ctx | (colid3 != GRID_W - 1), pltpu.roll(ext[:, qd:], n - 1, 0), 0.0)
    mid = slice(GRID_W, GRID_W + tm)
    sh = jnp.concatenate([
        left[mid, :qd],
        jnp.where(is_ctx, left[mid, qd:], right[mid, :qd]),
        jnp.where(is_ctx, right[mid, qd:2 * qd], ext[0:tm, 2 * qd:3 * qd]),
        jnp.where(is_ctx, right[mid, 2 * qd:], ext[2 * GRID_W:2 * GRID_W + tm, 3 * qd:])], axis=1)
    dx = sh - cur
    for b, o_ref in enumerate(o_refs):
        o_ref[...] = (cur + dx * mu_ref[b:b + 1, :]).astype(o_ref.dtype)


def rwkv_mix(h, mu, n_ctx):
    t, d = h.shape
    tm = MIX_TM
    assert t % tm == 0 and n_ctx % tm == 0 and tm % GRID_W == 0
    r = tm // GRID_W
    nt, nu = t // tm, t // GRID_W
    nb = mu.shape[0]
    return pl.pallas_call(
        functools.partial(_mix_body, nct=n_ctx // tm, nt=nt),
        grid=(nt,),
        in_specs=[pl.BlockSpec((tm, d), lambda i: (i, 0)),
                  pl.BlockSpec((GRID_W, d), lambda i: (jnp.maximum(i * r - 1, 0), 0)),
                  pl.BlockSpec((GRID_W, d), lambda i: (jnp.minimum((i + 1) * r, nu - 1), 0)),
                  pl.BlockSpec((nb, d), lambda i: (0, 0))],
        out_specs=[pl.BlockSpec((tm, d), lambda i: (i, 0))] * nb,
        out_shape=[jax.ShapeDtypeStruct((t, d), BF16)] * nb,
        compiler_params=pltpu.CompilerParams(dimension_semantics=("arbitrary",), vmem_limit_bytes=VMEM_LIMIT),
        name="rwkv_mix",
    )(h, h, h, mu)


def _rwkv_out_body(yf_ref, yb_ref, r_ref, k_ref, v_ref, g_ref, af_ref, ab_ref, p_ref, e_ref, et_ref, w_ref, o_ref):
    e, et = e_ref[...], et_ref[...]
    head_sum = lambda x: _dot_01_rhs(_dot_01_rhs(x, e), et)
    k_a, r_k, gain, bias = (p_ref[n:n + 1, :] for n in range(4))
    y = yf_ref[...] + yb_ref[...]
    yc = y - head_sum(y) * (1.0 / RWKV_N)
    var = head_sum(yc * yc) * (1.0 / RWKV_N)
    yn = yc * lax.rsqrt(var + RWKV_GN_EPS) * gain + bias
    kbar = k_ref[...] * (1.0 + (0.5 * (af_ref[...] + ab_ref[...]) - 1.0) * k_a)
    bonus = head_sum(r_ref[...] * kbar * r_k) * v_ref[...]
    z = (yn + bonus) * g_ref[...]
    o_ref[...] = _d(z.astype(BF16), w_ref[...], NN)


def rwkv_out(yf, yb, r, k, v, g, a, k_a, r_k, lnx_g, lnx_b, w_out):
    t, d = yf.shape
    tm = LN_TM
    nh = d // RWKV_N
    row = pl.BlockSpec((tm, d), lambda i: (i, 0))
    const = lambda shape: pl.BlockSpec(shape, lambda i: (0, 0))
    e = (jnp.arange(d)[:, None] // RWKV_N == jnp.arange(LANES)[None, :]).astype(BF16)
    params = jnp.stack([k_a, r_k, lnx_g, lnx_b])
    return pl.pallas_call(
        _rwkv_out_body,
        grid=(t // tm,),
        in_specs=[row] * 6 + [row, pl.BlockSpec((tm, d), lambda i: (i, 1)),
                  const((4, d)), const((d, LANES)), const((LANES, d)), const((d, d))],
        out_specs=row,
        out_shape=jax.ShapeDtypeStruct((t, d), F32),
        compiler_params=pltpu.CompilerParams(dimension_semantics=("arbitrary",), vmem_limit_bytes=VMEM_LIMIT),
        name="rwkv_out",
    )(yf, yb, r, k, v, g, a, a, params, e, e.T, w_out.astype(BF16))


def _peer_stats(hx_ref, wk_ref, sc_ref, hxb_ref, n_ref, f0_ref, r1_ref, e1_ref):
    tm = hx_ref.shape[0]
    hxb_ref[...] = hx_ref[...].T.astype(BF16)
    rows_per = 4 * LANES

    def scores(c, carry):
        rows = pl.ds(pl.multiple_of(c * rows_per, rows_per), rows_per)
        sc_ref[rows, :] = _d(wk_ref[rows, :], hxb_ref[...], NN)
        return carry

    lax.fori_loop(0, wk_ref.shape[0] // rows_per, scores, 0)

    def block(tb, carry):
        lanes = pl.ds(pl.multiple_of(tb * LANES, LANES), LANES)
        neg = jnp.full((N_KEYS, LANES), -jnp.inf, F32)
        head_row = lax.broadcasted_iota(jnp.int32, (PEER_HEADS, LANES), 0)

        def extract(h, tops):
            tops = [list(t) for t in tops]
            rows = [pl.ds(pl.multiple_of((2 * h + p) * N_KEYS, N_KEYS), N_KEYS) for p in range(2)]
            cur = [sc_ref[rows[p], lanes] for p in range(2)]
            rank = jnp.full((N_KEYS, LANES), float(PEER_TOPK), F32)
            for a in range(PEER_TOPK):
                m = [jnp.max(c, axis=0, keepdims=True) for c in cur]
                hit = [c >= mm for c, mm in zip(cur, m)]
                cur = [jnp.where(ht, neg, c) for ht, c in zip(hit, cur)]
                rank = jnp.where(hit[1], float(a), rank)
                for p in range(2):
                    tops[p][a] = jnp.where(head_row == h, m[p], tops[p][a])
            r1_ref[h, :, lanes] = rank.astype(BF16)
            return tuple(tuple(t) for t in tops)

        zero = jnp.zeros((PEER_HEADS, LANES), F32)
        top0, top1 = lax.fori_loop(0, PEER_HEADS, extract, ((zero,) * PEER_TOPK,) * 2)
        cands = [top0[a] + top1[b]
                 for a in range(PEER_TOPK) for b in range(PEER_TOPK) if (a + 1) * (b + 1) <= PEER_TOPK]
        c_max = cands[0]
        z = jnp.zeros_like(c_max)
        tau = c_max
        for a in range(PEER_TOPK):
            tau = functools.reduce(jnp.maximum, cands)
            z = z + jnp.exp(tau - c_max)
            cands = [jnp.where(cd >= tau, -jnp.inf, cd) for cd in cands]
        inv_z = 1.0 / z

        def factors(h, carry):
            row_of = lambda x: jnp.max(jnp.where(head_row == h, x, -jnp.inf), axis=0, keepdims=True)
            s0 = sc_ref[pl.ds(pl.multiple_of(2 * h * N_KEYS, N_KEYS), N_KEYS), lanes]
            s1 = sc_ref[pl.ds(pl.multiple_of((2 * h + 1) * N_KEYS, N_KEYS), N_KEYS), lanes]
            tau_h = row_of(tau)
            n = jnp.zeros((N_KEYS, LANES), F32)
            for b in range(PEER_TOPK):
                n = jnp.where(s0 + row_of(top1[b]) >= tau_h, float(b + 1), n)
            n_ref[h, :, lanes] = n
            f0_ref[h, :, lanes] = jnp.exp(s0 - row_of(top0[0])) * row_of(inv_z)
            e1_ref[h, :, lanes] = jnp.exp(s1 - row_of(top1[0])).astype(BF16)
            return carry

        lax.fori_loop(0, PEER_HEADS, factors, 0)
        return carry

    lax.fori_loop(0, tm // LANES, block, 0)


PEER_I_GROUP = 4
PEER_STREAMS = 2


def _peer_body(hx_ref, wk_ref, *rest):
    u_refs, vt_refs = rest[:PEER_STREAMS], rest[PEER_STREAMS:2 * PEER_STREAMS]
    o_ref, sc_ref, hxb_ref, n_ref, f0_ref, r1_ref, e1_ref, w_ref, acc_ref = rest[2 * PEER_STREAMS:]
    e = pl.program_id(1)
    tm = hx_ref.shape[0]
    nsb = tm // PEER_SB

    @pl.when(e == 0)
    def _():
        _peer_stats(hx_ref, wk_ref, sc_ref, hxb_ref, n_ref, f0_ref, r1_ref, e1_ref)
        acc_ref[...] = jnp.zeros(acc_ref.shape, F32)

    i_rows = pl.ds(pl.multiple_of(e * PEER_I_BLOCK, PEER_I_BLOCK), PEER_I_BLOCK)

    def activations(sb):
        parts = []
        for u_ref in u_refs:
            act = _d(u_ref[...], hxb_ref[:, sb * PEER_SB:(sb + 1) * PEER_SB], NN)
            act = act.astype(BF16)
            parts.append(0.5 * act * (1.0 + lax.erf(act * (2.0 ** -0.5))))
        return parts

    def gates(sb, act):
        for hb in range(PEER_SB // LANES):
            lanes = slice(sb * PEER_SB + hb * LANES, sb * PEER_SB + (hb + 1) * LANES)
            sub = slice(hb * LANES, (hb + 1) * LANES)
            n8 = [n_ref[h, i_rows, lanes] for h in range(PEER_HEADS)]
            f8 = [f0_ref[h, i_rows, lanes] for h in range(PEER_HEADS)]
            bcast = lambda x, ii: jnp.broadcast_to(x[ii:ii + 1], (N_KEYS, LANES)).astype(BF16)
            for ig in range(0, PEER_I_BLOCK, PEER_I_GROUP):
                g = [jnp.zeros((N_KEYS, LANES), BF16) for _ in range(PEER_I_GROUP)]
                for h in range(PEER_HEADS):
                    r1 = r1_ref[h, :, lanes]
                    e1 = e1_ref[h, :, lanes]
                    for k in range(PEER_I_GROUP):
                        ii = ig + k
                        g[k] = g[k] + jnp.where(r1 < bcast(n8[h], ii), e1 * bcast(f8[h], ii), jnp.zeros_like(e1))
                for k in range(PEER_I_GROUP):
                    rows = slice((ig + k) * N_KEYS, (ig + k + 1) * N_KEYS)
                    part, off = divmod((ig + k) * N_KEYS, PEER_TE // PEER_STREAMS)
                    w_ref[sb, rows, sub] = g[k] * act[part][off:off + N_KEYS, sub]

    def accumulate(sb):
        cols = slice(sb * PEER_SB, (sb + 1) * PEER_SB)
        dr = acc_ref.shape[0] // PEER_STREAMS
        for k, vt_ref in enumerate(vt_refs):
            acc_ref[k * dr:(k + 1) * dr, cols] += _d(vt_ref[...], w_ref[sb], NN)

    act = activations(0)
    for sb in range(nsb):
        nxt = activations(sb + 1) if sb + 1 < nsb else None
        gates(sb, act)
        accumulate(sb)
        act = nxt

    @pl.when(e == pl.num_programs(1) - 1)
    def _():
        o_ref[...] = acc_ref[...].T


def _fold_body(k_ref, w_ref, o_ref):
    o_ref[...] = _mdot(k_ref[...], w_ref[...], NN, 6)


def peer_fold_keys(wq, keys):
    d = wq.shape[0]
    nhp, nk, dk = keys.shape
    wqt = wq.T.reshape(nhp, dk, d)
    return pl.pallas_call(
        _fold_body,
        grid=(nhp,),
        in_specs=[pl.BlockSpec((None, nk, dk), lambda i: (i, 0, 0)),
                  pl.BlockSpec((None, dk, d), lambda i: (i, 0, 0))],
        out_specs=pl.BlockSpec((nk, d), lambda i: (i, 0)),
        out_shape=jax.ShapeDtypeStruct((nhp * nk, d), F32),
        compiler_params=pltpu.CompilerParams(dimension_semantics=("arbitrary",), vmem_limit_bytes=VMEM_LIMIT),
        name="peer_fold_keys",
    )(keys, wqt)


def peer(hx, wk, u_bf, vt_bf):
    t, d = hx.shape
    tm = PEER_TM
    assert t % tm == 0
    ne = u_bf.shape[0] // PEER_TE
    h = PEER_HEADS
    ns = PEER_STREAMS
    return pl.pallas_call(
        _peer_body,
        grid=(t // tm, ne),
        in_specs=[pl.BlockSpec((tm, d), lambda i, e: (i, 0), pipeline_mode=pl.Buffered(1)),
                  pl.BlockSpec(wk.shape, lambda i, e: (0, 0), pipeline_mode=pl.Buffered(1)),
                  *[pl.BlockSpec((PEER_TE // ns, d), functools.partial(lambda i, e, k: (e * ns + k, 0), k=k))
                    for k in range(ns)],
                  *[pl.BlockSpec((None, d // ns, PEER_TE), functools.partial(lambda i, e, k: (e, k, 0), k=k))
                    for k in range(ns)]],
        out_specs=pl.BlockSpec((tm, d), lambda i, e: (i, 0)),
        out_shape=jax.ShapeDtypeStruct((t, d), F32),
        scratch_shapes=[pltpu.VMEM((2 * h * N_KEYS, tm), F32),
                        pltpu.VMEM((d, tm), BF16),
                        pltpu.VMEM((h, N_KEYS, tm), F32), pltpu.VMEM((h, N_KEYS, tm), F32),
                        pltpu.VMEM((h, N_KEYS, tm), BF16), pltpu.VMEM((h, N_KEYS, tm), BF16),
                        pltpu.VMEM((tm // PEER_SB, PEER_TE, PEER_SB), BF16), pltpu.VMEM((d, tm), F32)],
        compiler_params=pltpu.CompilerParams(
            dimension_semantics=("arbitrary", "arbitrary"), vmem_limit_bytes=VMEM_LIMIT),
        name="peer_dense",
    )(hx, wk.astype(BF16), *([u_bf] * ns), *([vt_bf] * ns))


LN_TM = 256


def _ln_mod_body(xs_ref, y_ref, mg_ref, mn_ref, lng_ref, lnb_ref, xo_ref, ho_ref, *, gate_col, mod_col, nct, nt):
    i = pl.program_id(0)
    d = xs_ref.shape[1]
    is_ctx = i < nct

    def pick(ref, col):
        return jnp.where(is_ctx, ref[1:2, col * d:(col + 1) * d], ref[0:1, col * d:(col + 1) * d])

    z = DN_ALPHA * xs_ref[...] + pick(mg_ref, gate_col) * y_ref[...]
    mu = jnp.mean(z, axis=-1, keepdims=True)
    zc = z - mu
    var = jnp.mean(zc * zc, axis=-1, keepdims=True)
    xn = zc * lax.rsqrt(var + LN_EPS) * lng_ref[...] + lnb_ref[...]
    xo_ref[...] = xn
    h = xn * (1.0 + pick(mn_ref, mod_col + 1)) + pick(mn_ref, mod_col)
    ho_ref[...] = jnp.where(i < nt, h, 0.0)


def ln_mod(xs, y, mod_gate, gate_col, mod_next, mod_col, ln_g, ln_b, n_ctx, pad_to):
    t, d = xs.shape
    tm = LN_TM
    assert t % tm == 0 and n_ctx % tm == 0 and pad_to % tm == 0 and pad_to >= t
    nt = t // tm
    row = lambda i: (jnp.minimum(i, nt - 1), 0)
    full = lambda a: pl.BlockSpec(a.shape, lambda i: (0, 0))
    return pl.pallas_call(
        functools.partial(_ln_mod_body, gate_col=gate_col, mod_col=mod_col, nct=n_ctx // tm, nt=nt),
        grid=(pad_to // tm,),
        in_specs=[pl.BlockSpec((tm, d), row), pl.BlockSpec((tm, d), row), full(mod_gate), full(mod_next),
                  pl.BlockSpec((1, d), lambda i: (0, 0)), pl.BlockSpec((1, d), lambda i: (0, 0))],
        out_specs=[pl.BlockSpec((tm, d), row), pl.BlockSpec((tm, d), lambda i: (i, 0))],
        out_shape=[jax.ShapeDtypeStruct((t, d), F32), jax.ShapeDtypeStruct((pad_to, d), F32)],
        compiler_params=pltpu.CompilerParams(dimension_semantics=("arbitrary",), vmem_limit_bytes=VMEM_LIMIT),
        name="ln_mod",
    )(xs, y, mod_gate, mod_next, ln_g.reshape(1, d), ln_b.reshape(1, d))


def _ln(x, g, b):
    mu = jnp.mean(x, -1, keepdims=True)
    xc = x - mu
    var = jnp.mean(xc * xc, -1, keepdims=True)
    return xc * lax.rsqrt(var + LN_EPS) * g + b


def _head_norm(h, nheads, eps):
    t = h.shape[0]
    hh = h.reshape(t, nheads, -1)
    mu = jnp.mean(hh, -1, keepdims=True)
    xc = hh - mu
    var = jnp.mean(xc * xc, -1, keepdims=True)
    return (xc * lax.rsqrt(var + eps)).reshape(t, -1)


def _pad_cols(w, n):
    return jnp.pad(w, ((0, 0), (0, n - w.shape[1])))


def _shift_rows(u, k):
    if k > 0:
        return jnp.concatenate([u[k:], jnp.zeros_like(u[:k])], axis=0)
    return jnp.concatenate([jnp.zeros_like(u[:-k]), u[:k]], axis=0)


def _grid_conv(u, w, b):
    s = u.shape[0]
    colid = (jnp.arange(s) % GRID_W)[:, None]
    out = jnp.zeros_like(u) + b
    for di in range(3):
        for dj in range(3):
            off = (di - 1) * GRID_W + (dj - 1)
            sh = _shift_rows(u, off) if off else u
            if dj == 0:
                sh = jnp.where(colid == 0, 0.0, sh)
            elif dj == 2:
                sh = jnp.where(colid == GRID_W - 1, 0.0, sh)
            out = out + sh * w[di, dj]
    return out


def _seq_conv(u, w, b):
    wc = w[1]
    return _shift_rows(u, -1) * wc[0] + u * wc[1] + _shift_rows(u, 1) * wc[2] + b


def _qshift(u):
    q = u.shape[1] // 4
    colid = (jnp.arange(u.shape[0]) % GRID_W)[:, None]
    left = jnp.where(colid == 0, 0.0, _shift_rows(u[:, :q], -1))
    right = jnp.where(colid == GRID_W - 1, 0.0, _shift_rows(u[:, q:2 * q], 1))
    up = _shift_rows(u[:, 2 * q:3 * q], -GRID_W)
    down = _shift_rows(u[:, 3 * q:], GRID_W)
    return jnp.concatenate([left, right, up, down], axis=1)


def _shift_seq(u):
    h = u.shape[1] // 2
    return jnp.concatenate([_shift_rows(u[:, :h], -1), _shift_rows(u[:, h:], 1)], axis=1)


def _mlstm_layer(h, n_ctx, w_in, b_in, conv_w, conv_b, hn_g, w_out):
    d = D_MODEL
    t = h.shape[0]
    qk_pre = matmul(h, w_in[:, :2 * d], b_in[:2 * d])
    v = matmul(h, w_in[:, 2 * d:3 * d], b_in[2 * d:3 * d], out_dtype=BF16)
    o = matmul(h, w_in[:, 3 * d:4 * d], b_in[3 * d:4 * d], act="sigmoid", out_dtype=BF16)
    g = matmul(h, _pad_cols(w_in[:, 4 * d:], LANES), jnp.pad(b_in[4 * d:], (0, LANES - 4 * MLSTM_HEADS)))
    g = g[:, :4 * MLSTM_HEADS].reshape(t, 4, MLSTM_HEADS)
    g = jnp.concatenate([g[:, :2], jax.nn.log_sigmoid(g[:, 2:])], axis=1)
    gh = jnp.transpose(g, (2, 0, 1))
    gcol = jnp.pad(gh, ((0, 0), (0, 0), (0, LANES - 4)))
    grow = jnp.pad(jnp.transpose(gh, (0, 2, 1)), ((0, 0), (0, 4), (0, 0)))
    qk = mlstm_conv(qk_pre, conv_w, conv_b, n_ctx)
    hf, hb = mlstm_scan(qk, v, gcol, grow, n_ctx)
    return mlstm_out(hf, hb, o, hn_g, w_out)


def _rwkv_layer(h, n_ctx, mu, w_rkv, w0, w1, w2, a0, a1, a2, g1, g2, k_k, k_a, r_k, lnx_g, lnx_b, w_out):
    d = D_MODEL
    xm = rwkv_mix(h, mu, n_ctx)
    r = matmul(xm[0], w_rkv[0])
    k = matmul(xm[1], w_rkv[1])
    v = matmul(xm[2], w_rkv[2])

    def lora_pair(x, w_in, w_mid, bias, act_mid, act_out):
        rank = w_in.shape[-1]
        w_a = _pad_cols(jnp.concatenate([w_in[0], w_in[1]], axis=1), LANES)
        zpad = jnp.zeros((rank, d), F32)
        w_b = jnp.concatenate([jnp.concatenate([w_mid[0], zpad], axis=1),
                               jnp.concatenate([zpad, w_mid[1]], axis=1)], axis=0)
        w_b = jnp.pad(w_b, ((0, LANES - 2 * rank), (0, 0)))
        mid = matmul(x, w_a, act=act_mid, out_dtype=BF16)
        return matmul(mid, w_b, jnp.concatenate([bias[0], bias[1]]), act=act_out)

    lw = lora_pair(xm[3], w1, w2, w0, "tanh", "logdecay")
    a = lora_pair(xm[4], a1, a2, a0, None, "sigmoid")
    gpad = 2 * LANES
    gg = matmul(xm[5], _pad_cols(g1, gpad), act="sigmoid", out_dtype=BF16)
    g = matmul(gg, jnp.pad(g2, ((0, gpad - g1.shape[1]), (0, 0))))
    yf, yb = rwkv_scan(r, k, v, lw, a, k_k, k_a, n_ctx)
    return rwkv_out(yf, yb, r, k, v, g, a, k_a, r_k, lnx_g, lnx_b, w_out)


def _forward(x, c, ctx, c_ctx, ada_w, ada_b, ln_g, ln_b,
             ml_w_in, ml_b_in, ml_conv_w, ml_conv_b, ml_hn_g, ml_w_out,
             rw_mu, rw_w_rkv, rw_w0, rw_w1, rw_w2, rw_a0, rw_a1, rw_a2, rw_g1, rw_g2,
             rw_k_k, rw_k_a, rw_r_k, rw_lnx_g, rw_lnx_b, rw_w_out,
             pk_wq, pk_keys, pk_u, pk_v):
    d = D_MODEL
    n_ctx = ctx.shape[1]
    n_lat = x.shape[1]
    xs = jnp.concatenate([ctx[0], x[0]], axis=0)
    t = xs.shape[0]
    t_pad = -(-t // PEER_TM) * PEER_TM
    s_in = jnp.zeros((8, d), F32).at[0].set(jax.nn.silu(c[0])).at[1].set(jax.nn.silu(c_ctx))
    depth = ada_w.shape[0]
    mods = [matmul(s_in, ada_w[i], ada_b[i], passes=3) for i in range(depth)]
    is_ctx = (jnp.arange(t) < n_ctx)[:, None]
    m0 = [jnp.where(is_ctx, mods[0][1, n * d:(n + 1) * d], mods[0][0, n * d:(n + 1) * d]) for n in range(2)]
    h = xs * (1.0 + m0[1]) + m0[0]
    for i in range(depth):
        j = i // 2
        if i % 2 == 0:
            y = _mlstm_layer(h, n_ctx, ml_w_in[j], ml_b_in[j], ml_conv_w[j], ml_conv_b[j],
                             ml_hn_g[j], ml_w_out[j])
        else:
            y = _rwkv_layer(h, n_ctx, rw_mu[j], rw_w_rkv[j], rw_w0[j], rw_w1[j], rw_w2[j],
                            rw_a0[j], rw_a1[j], rw_a2[j], rw_g1[j], rw_g2[j], rw_k_k[j],
                            rw_k_a[j], rw_r_k[j], rw_lnx_g[j], rw_lnx_b[j], rw_w_out[j])
        xs, h = ln_mod(xs, y, mods[i], 2, mods[i], 3, ln_g[i, 0], ln_b[i, 0], n_ctx, t_pad)
        wk = peer_fold_keys(pk_wq[i], pk_keys[i].reshape(2 * PEER_HEADS, N_KEYS, PEER_DQ // 2))
        vt = jnp.swapaxes(pk_v[i].astype(BF16).reshape(-1, PEER_TE, d), 1, 2)
        y = peer(h, wk, pk_u[i].astype(BF16), vt)
        xs, h = ln_mod(xs, y, mods[i], 5, mods[min(i + 1, depth - 1)], 0, ln_g[i, 1], ln_b[i, 1], n_ctx, t)
    return xs[n_ctx:][None]


def kernel(x, c, ctx, c_ctx, ada_w, ada_b, ln_g, ln_b, ml_w_in, ml_b_in, ml_conv_w, ml_conv_b, ml_hn_g, ml_w_out, rw_mu, rw_w_rkv, rw_w0, rw_w1, rw_w2, rw_a0, rw_a1, rw_a2, rw_g1, rw_g2, rw_k_k, rw_k_a, rw_r_k, rw_lnx_g, rw_lnx_b, rw_w_out, pk_wq, pk_keys, pk_u, pk_v):
    return _forward(x, c, ctx, c_ctx, ada_w, ada_b, ln_g, ln_b,
                    ml_w_in, ml_b_in, ml_conv_w, ml_conv_b, ml_hn_g, ml_w_out,
                    rw_mu, rw_w_rkv, rw_w0, rw_w1, rw_w2, rw_a0, rw_a1, rw_a2, rw_g1, rw_g2,
                    rw_k_k, rw_k_a, rw_r_k, rw_lnx_g, rw_lnx_b, rw_w_out,
                    pk_wq, pk_keys, pk_u, pk_v)
```

```python
import functools

import jax
import jax.numpy as jnp
from jax import lax
from jax.experimental import pallas as pl
from jax.experimental.pallas import tpu as pltpu

F32 = jnp.float32
BF16 = jnp.bfloat16

D_MODEL = 1024
DEPTH = 4
GRID_W = 64
N_MOD = 6
DN_ALPHA = (2.0 * DEPTH) ** 0.25
LN_EPS = 1e-5

MLSTM_HEADS = 4
MLSTM_DH = D_MODEL // MLSTM_HEADS
MLSTM_CHUNK = 128
M_INIT = -1e30

RWKV_N = 64
RWKV_HEADS = D_MODEL // RWKV_N
RWKV_CHUNK = 64
RWKV_GN_EPS = 64e-5

N_KEYS = 128
PEER_HEADS = 8
PEER_DQ = 256
PEER_TOPK = 16
PEER_I_BLOCK = 8
PEER_TE = PEER_I_BLOCK * N_KEYS
PEER_TM = 768
PEER_SB = 256

LANES = 128
VMEM_LIMIT = 62 * 1024 * 1024

NN = ((1,), (0,))
NT = ((1,), (1,))
TN = ((0,), (0,))


def _split(x, n):
    parts = []
    r = x.astype(F32)
    for i in range(n):
        p = r.astype(BF16)
        parts.append(p)
        if i + 1 < n:
            r = r - p.astype(F32)
    return parts


def _d(a, b, dims):
    return lax.dot_general(a, b, (dims, ((), ())), preferred_element_type=F32)


def _mdot(a, b, dims, passes):
    if passes == 1:
        return _d(a.astype(BF16), b.astype(BF16), dims)
    if passes == 3:
        a0, a1 = _split(a, 2)
        b0, b1 = _split(b, 2)
        return (_d(a0, b1, dims) + _d(a1, b0, dims)) + _d(a0, b0, dims)
    a0, a1, a2 = _split(a, 3)
    b0, b1, b2 = _split(b, 3)
    lo = (_d(a0, b2, dims) + _d(a2, b0, dims)) + _d(a1, b1, dims)
    mid = _d(a0, b1, dims) + _d(a1, b0, dims)
    return (lo + mid) + _d(a0, b0, dims)


def _dot_01_lhs(m01, x):
    mb = m01.astype(BF16)
    x0, x1, x2 = _split(x, 3)
    return (_d(mb, x2, NN) + _d(mb, x1, NN)) + _d(mb, x0, NN)


def _dot_01_rhs(x, m01):
    mb = m01.astype(BF16)
    x0, x1, x2 = _split(x, 3)
    return (_d(x2, mb, NN) + _d(x1, mb, NN)) + _d(x0, mb, NN)


def _pick(n, cands):
    for c in cands:
        if n % c == 0:
            return c
    raise ValueError(f"no tile for {n}")


_ACTS = {None: lambda x: x, "sigmoid": jax.nn.sigmoid, "tanh": jnp.tanh,
         "logdecay": lambda x: -(2.718281828459045 ** -0.5) * jax.nn.sigmoid(x)}


def _mm_body(x_ref, w_ref, b_ref, o_ref, *, passes, act):
    o_ref[...] = _ACTS[act](_mdot(x_ref[...], w_ref[...], NN, passes) + b_ref[...]).astype(o_ref.dtype)


def matmul(x, w, b=None, *, passes=1, act=None, out_dtype=F32):
    m, k = x.shape
    n = w.shape[1]
    assert n % LANES == 0 and w.shape[0] == k
    tm = m if m <= 1024 else _pick(m, (640, 512, 384, 256, 128))
    tn = _pick(n, (1024, 768, 640, 512, 384, 256, 128))
    if b is None:
        b = jnp.zeros((n,), F32)
    if passes == 1:
        w = w.astype(BF16)
    return pl.pallas_call(
        functools.partial(_mm_body, passes=passes, act=act),
        grid=(m // tm, n // tn),
        in_specs=[pl.BlockSpec((tm, k), lambda i, j: (i, 0)),
                  pl.BlockSpec((k, tn), lambda i, j: (0, j)),
                  pl.BlockSpec((1, tn), lambda i, j: (0, j))],
        out_specs=pl.BlockSpec((tm, tn), lambda i, j: (i, j)),
        out_shape=jax.ShapeDtypeStruct((m, n), out_dtype),
        compiler_params=pltpu.CompilerParams(
            dimension_semantics=("arbitrary", "arbitrary"), vmem_limit_bytes=VMEM_LIMIT),
        name="proj_matmul",
    )(x, w, b.reshape(1, n).astype(F32))


CONV_TM = 256
CONV_CB = 512


def _conv_body(cur_ref, prev_ref, next_ref, w_ref, b_ref, o_ref, *, nct, nt, q_blocks, q_scale):
    i = pl.program_id(0)
    j = pl.program_id(1)
    is_ctx = i < nct
    tm = cur_ref.shape[0]
    above = jnp.where(is_ctx | (i == nct), 0.0, prev_ref[...])
    below = jnp.where(is_ctx | (i == nt - 1), 0.0, next_ref[...])
    ext = jnp.concatenate([above, cur_ref[...], below], axis=0)
    n = ext.shape[0]
    colid = lax.broadcasted_iota(jnp.int32, ext.shape, 0) % GRID_W
    left = jnp.where(is_ctx | (colid != 0), pltpu.roll(ext, 1, 0), 0.0)
    right = jnp.where(is_ctx | (colid != GRID_W - 1), pltpu.roll(ext, n - 1, 0), 0.0)
    w = w_ref[...]
    acc = jnp.zeros((tm, ext.shape[1]), F32) + b_ref[...]
    for di in range(3):
        rows = slice(di * GRID_W, di * GRID_W + tm)
        tap = left[rows] * w[3 * di:3 * di + 1] + ext[rows] * w[3 * di + 1:3 * di + 2] \
            + right[rows] * w[3 * di + 2:3 * di + 3]
        acc = acc + (tap if di == 1 else jnp.where(is_ctx, 0.0, tap))
    y = acc * jax.nn.sigmoid(acc)
    o_ref[...] = (y * jnp.where(j < q_blocks, q_scale, 1.0)).astype(o_ref.dtype)


def mlstm_conv(qk_pre, conv_w, conv_b, n_ctx):
    t, c = qk_pre.shape
    tm, cb = CONV_TM, CONV_CB
    assert t % tm == 0 and n_ctx % tm == 0 and tm % GRID_W == 0 and c % (2 * cb) == 0
    r = tm // GRID_W
    nt, nu = t // tm, t // GRID_W
    return pl.pallas_call(
        functools.partial(_conv_body, nct=n_ctx // tm, nt=nt, q_blocks=c // (2 * cb), q_scale=MLSTM_DH ** -0.5),
        grid=(nt, c // cb),
        in_specs=[pl.BlockSpec((tm, cb), lambda i, j: (i, j)),
                  pl.BlockSpec((GRID_W, cb), lambda i, j: (jnp.maximum(i * r - 1, 0), j)),
                  pl.BlockSpec((GRID_W, cb), lambda i, j: (jnp.minimum((i + 1) * r, nu - 1), j)),
                  pl.BlockSpec((9, cb), lambda i, j: (0, j)),
                  pl.BlockSpec((1, cb), lambda i, j: (0, j))],
        out_specs=pl.BlockSpec((tm, cb), lambda i, j: (i, j)),
        out_shape=jax.ShapeDtypeStruct((t, c), BF16),
        compiler_params=pltpu.CompilerParams(
            dimension_semantics=("arbitrary", "arbitrary"), vmem_limit_bytes=VMEM_LIMIT),
        name="mlstm_conv",
    )(qk_pre, qk_pre, qk_pre, conv_w.reshape(9, c), conv_b.reshape(1, c))


def _mlstm_out_body(hf_ref, hb_ref, o_ref, g_ref, w_ref, y_ref):
    h = hf_ref[...] + hb_ref[...]
    parts = []
    for a in range(MLSTM_HEADS):
        x = h[:, a * MLSTM_DH:(a + 1) * MLSTM_DH]
        mu = jnp.mean(x, axis=-1, keepdims=True)
        xc = x - mu
        var = jnp.mean(xc * xc, axis=-1, keepdims=True)
        parts.append(xc * lax.rsqrt(var + LN_EPS))
    hn = jnp.concatenate(parts, axis=1)
    z = o_ref[...].astype(F32) * hn * g_ref[...]
    y_ref[...] = _d(z.astype(BF16), w_ref[...], NN)


def mlstm_out(hf, hb, o, hn_g, w_out):
    t, d = hf.shape
    tm = LN_TM
    row = pl.BlockSpec((tm, d), lambda i: (i, 0))
    return pl.pallas_call(
        _mlstm_out_body,
        grid=(t // tm,),
        in_specs=[row, row, row, pl.BlockSpec((1, d), lambda i: (0, 0)), pl.BlockSpec((d, d), lambda i: (0, 0))],
        out_specs=row,
        out_shape=jax.ShapeDtypeStruct((t, d), F32),
        compiler_params=pltpu.CompilerParams(dimension_semantics=("arbitrary",), vmem_limit_bytes=VMEM_LIMIT),
        name="mlstm_out",
    )(hf, hb, o, hn_g.reshape(1, d), w_out.astype(BF16))


def _mlstm_body(qf, kf, vf, gcf, grf, qb, kb, vb, gcb, grb, hf_ref, hb_ref, c_ref, n_ref, m_ref):
    @pl.when(pl.program_id(1) == 0)
    def _():
        c_ref[...] = jnp.zeros(c_ref.shape, F32)
        n_ref[...] = jnp.zeros(n_ref.shape, F32)
        m_ref[...] = jnp.full(m_ref.shape, M_INIT, F32)

    ds = (0, 1)
    each = lambda f, *ls: [f(*xs) for xs in zip(*ls)]
    q, k, v = [qf[...], qb[...]], [kf[...], kb[...]], [vf[...], vb[...]]
    gc, gr = [gcf[...], gcb[...]], [grf[...], grb[...]]
    L = q[0].shape[0]
    row = lax.broadcasted_iota(jnp.int32, (L, L), 0)
    col = lax.broadcasted_iota(jnp.int32, (L, L), 1)
    seen = (col <= row, col >= row)
    tri = [jnp.where(m, 1.0, 0.0) for m in seen]
    b_col = [_dot_01_lhs(tri[d], gc[d])[:, 2 + d:3 + d] for d in ds]
    b_row = [_dot_01_rhs(gr[d], tri[1 - d])[2 + d:3 + d, :] for d in ds]
    ig_col = [gc[d][:, d:d + 1] for d in ds]
    ig_row = [gr[d][d:d + 1, :] for d in ds]
    m_st = [m_ref[d, 0:1, 0:1] for d in ds]
    c_st = [c_ref[d] for d in ds]
    n_st = [n_ref[d] for d in ds]

    dlog = [jnp.where(seen[d], b_col[d] - b_row[d] + ig_row[d], -jnp.inf) for d in ds]
    m_inter = each(jnp.add, b_col, m_st)
    m_t = each(lambda mi, dl: jnp.maximum(mi, jnp.max(dl, axis=1, keepdims=True)), m_inter, dlog)
    qb16, kb16, vb16 = q, k, v
    q, k = (each(lambda x: x.astype(F32), a) for a in (q, k))
    qk = each(lambda a, b: _d(a, b, NT), qb16, kb16)
    s = each(lambda x, dl, mt: x * jnp.exp(dl - mt), qk, dlog, m_t)
    dec = each(lambda mi, mt: jnp.exp(mi - mt), m_inter, m_t)
    sv = each(lambda a, b: _d(a.astype(BF16), b, NN), s, vb16)
    qc = each(lambda a, b: _d(a, b.astype(BF16), NN), qb16, c_st)
    num = each(lambda a, dc, b: a + dc * b, sv, dec, qc)
    den = each(lambda x, dc, qq, nn: jnp.sum(x, axis=1, keepdims=True) + dc * jnp.sum(qq * nn, axis=1, keepdims=True),
               s, dec, q, n_st)
    h = each(lambda nu, de, mt: nu / jnp.maximum(jnp.abs(de), jnp.exp(-mt)), num, den, m_t)
    hf_ref[...] = h[0]
    hb_ref[...] = h[1]

    b_last = [b_col[0][L - 1:L, :], b_col[1][0:1, :]]
    w_c = each(lambda bl, bc, ic: bl - bc + ic, b_last, b_col, ig_col)
    m_new = each(lambda bl, ms, w: jnp.maximum(bl + ms, jnp.max(w, axis=0, keepdims=True)), b_last, m_st, w_c)
    a_c = each(lambda w, mn: jnp.exp(w - mn), w_c, m_new)
    g_prev = each(lambda bl, ms, mn: jnp.exp(bl + ms - mn), b_last, m_st, m_new)
    ak = each(jnp.multiply, a_c, k)
    kv = each(lambda a, b: _d(a.T.astype(BF16), b, NN), ak, vb16)
    for d in ds:
        c_ref[d] = g_prev[d] * c_st[d] + kv[d]
        n_ref[d] = g_prev[d] * n_st[d] + jnp.sum(ak[d], axis=0, keepdims=True)
        m_ref[d] = jnp.broadcast_to(m_new[d], m_ref.shape[1:])


def _bwd_chunk(c, nc0, nc):
    return jnp.where(c < nc0, nc0 - 1 - c, nc - 1 - (c - nc0))


def mlstm_scan(qk, v, gcol, grow, n_ctx):
    t = v.shape[0]
    L, dh = MLSTM_CHUNK, MLSTM_DH
    nc, nc0 = t // L, n_ctx // L
    fwd = lambda h, c: (c, h)
    bwd = lambda h, c: (_bwd_chunk(c, nc0, nc), h)
    fwd_k = lambda h, c: (c, MLSTM_HEADS + h)
    bwd_k = lambda h, c: (_bwd_chunk(c, nc0, nc), MLSTM_HEADS + h)
    qkv = lambda im: pl.BlockSpec((L, dh), im)
    gc_spec = lambda f: pl.BlockSpec((None, L, LANES), lambda h, c: (h, f(c), 0))
    gr_spec = lambda f: pl.BlockSpec((None, 8, L), lambda h, c: (h, 0, f(c)))
    idf = lambda c: c
    idb = lambda c: _bwd_chunk(c, nc0, nc)
    return pl.pallas_call(
        _mlstm_body,
        grid=(MLSTM_HEADS, nc),
        in_specs=[qkv(fwd), qkv(fwd_k), qkv(fwd), gc_spec(idf), gr_spec(idf),
                  qkv(bwd), qkv(bwd_k), qkv(bwd), gc_spec(idb), gr_spec(idb)],
        out_specs=[qkv(fwd), qkv(bwd)],
        out_shape=[jax.ShapeDtypeStruct((t, D_MODEL), F32)] * 2,
        scratch_shapes=[pltpu.VMEM((2, dh, dh), F32), pltpu.VMEM((2, 1, dh), F32),
                        pltpu.VMEM((2, 8, LANES), F32)],
        compiler_params=pltpu.CompilerParams(
            dimension_semantics=("arbitrary", "arbitrary"), vmem_limit_bytes=VMEM_LIMIT),
        name="mlstm_scan",
    )(qk, qk, v, gcol, grow, qk, qk, v, gcol, grow)


RW_PASSES = 1
RW_CHUNKS_PER_STEP = 4


def _stack2(x, lane_head):
    return jnp.concatenate([jnp.where(lane_head == 0, x, 0.0), jnp.where(lane_head == 1, x, 0.0)], axis=0)


def _unstack2(x):
    L = x.shape[0] // 2
    return x[:L] + x[L:]


def _rwkv_chunks(chains, k_k, k_a):
    L = chains[0][0].shape[0]
    n2 = 2 * L
    p = RW_PASSES
    ds = [c[5] for c in chains]
    each = lambda f, *ls: [f(*xs) for xs in zip(*ls)]
    dot = lambda dims: (lambda a, b: _mdot(a, b, dims, p))
    lw, r, k_raw, v, a = ([c[i] for c in chains] for i in range(5))

    lane_r = lax.broadcasted_iota(jnp.int32, (LANES, LANES), 0) // RWKV_N
    lane_c = lax.broadcasted_iota(jnp.int32, (LANES, LANES), 1) // RWKV_N
    head_ones = jnp.where(lane_r == lane_c, 1.0, 0.0)
    kkr = each(jnp.multiply, k_raw, k_k)
    ss = each(lambda x: _dot_01_rhs(x * x, head_ones), kkr)
    kap = each(lambda x, q: x / jnp.maximum(jnp.sqrt(q), 1e-12), kkr, ss)
    alp = each(jnp.multiply, kap, a)
    k = each(lambda x, y, z: x * (1.0 + (y - 1.0) * z), k_raw, a, k_a)

    row = lax.broadcasted_iota(jnp.int32, (L, L), 0)
    col = lax.broadcasted_iota(jnp.int32, (L, L), 1)
    tris = (jnp.where(col <= row, 1.0, 0.0), jnp.where(col >= row, 1.0, 0.0))
    lp = [_dot_01_lhs(tris[d], x) for d, x in zip(ds, lw)]
    lp_end = [x[0:1, :] if d else x[L - 1:L, :] for d, x in zip(ds, lp)]
    e_neg = each(lambda x: jnp.exp(-x), lp)
    e_end = each(lambda x, xe: jnp.exp(xe - x), lp, lp_end)
    kap_t = each(lambda x, y, z: x * jnp.exp(y - z), kap, lp, lw)
    r_t = each(lambda x, y: x * jnp.exp(y), r, lp)
    k_h = each(jnp.multiply, k, e_neg)
    a_h = each(jnp.multiply, alp, e_neg)
    k_e = each(jnp.multiply, k, e_end)
    a_e = each(jnp.multiply, alp, e_end)

    lane_head = lax.broadcasted_iota(jnp.int32, (L, LANES), 1) // RWKV_N
    st = lambda x: _stack2(x, lane_head)
    kap_s, r_s, v_s, k_s, a_s = (each(st, x) for x in (kap_t, r_t, v, k_e, a_e))
    rhs_k = each(lambda x: jnp.concatenate([x, x], axis=0), k_h)
    rhs_a = each(lambda x: jnp.concatenate([x, x], axis=0), a_h)

    row2 = lax.broadcasted_iota(jnp.int32, (n2, n2), 0)
    col2 = lax.broadcasted_iota(jnp.int32, (n2, n2), 1)
    same_head = (row2 // L) == (col2 // L)
    strict = (same_head & (col2 < row2), same_head & (col2 > row2))
    incl = (same_head & (col2 <= row2), same_head & (col2 >= row2))
    zero = jnp.zeros((n2, n2), F32)
    masked = lambda masks: (lambda d, x: jnp.where(masks[d], x, zero))
    n_ka = each(masked(strict), ds, each(dot(NT), kap_s, rhs_a))
    m_kk = each(masked(strict), ds, each(dot(NT), kap_s, rhs_k))
    a_rk = each(masked(incl), ds, each(dot(NT), r_s, rhs_k))
    a_ra = each(masked(incl), ds, each(dot(NT), r_s, rhs_a))

    b16 = (row2 // 16) == (col2 // 16)
    b32 = (row2 // 32) == (col2 // 32)
    eye = jnp.where(row2 == col2, 1.0, 0.0)
    n16 = each(lambda x: jnp.where(b16, x, zero), n_ka)
    n_2 = each(dot(NN), n16, n16)
    n_4 = each(dot(NN), n_2, n_2)
    n_8 = each(dot(NN), n_4, n_4)
    inv = each(lambda x: eye - x, n16)
    for pw in (n_2, n_4, n_8):
        inv = each(jnp.add, inv, each(dot(NN), inv, pw))
    for sel in (lambda x: jnp.where(b32 & ~b16, x, zero), lambda x: jnp.where(b32, zero, x)):
        t1 = each(dot(NN), inv, each(sel, n_ka))
        inv = each(jnp.subtract, inv, each(dot(NN), t1, inv))

    mv = each(dot(NN), m_kk, v_s)
    w1u0 = each(dot(NN), inv, each(lambda x, y: jnp.concatenate([x, y], axis=1), kap_s, mv))
    ar = each(dot(NN), a_ra, w1u0)
    av = each(dot(NN), a_rk, v_s)
    r2 = each(lambda x, y: _unstack2(x - y[:, :LANES]), r_s, ar)
    y0 = each(lambda x, y: _unstack2(x - y[:, LANES:]), av, ar)
    w1_t = each(lambda x: x[:, :LANES].T, w1u0)
    u0_t = each(lambda x: x[:, LANES:].T, w1u0)
    v_t = each(lambda x: x.T, v_s)
    wa = each(dot(NN), w1_t, a_s)
    vk = each(dot(NN), v_t, k_s)
    ua = each(dot(NN), u0_t, a_s)
    g = each(lambda xe, x: jnp.where(row2 == col2, jnp.exp(xe), zero) - x, lp_end, wa)
    b = each(jnp.subtract, vk, ua)
    return list(zip(y0, r2, g, b))


RW_PAIRS_PER_STEP = 2


def _rwkv_body(lwf, rf, kf, vf, af, lwb, rb, kb, vb, ab, kk_ref, ka_ref, yf_ref, yb_ref, s_ref):
    @pl.when(pl.program_id(1) == 0)
    def _():
        s_ref[...] = jnp.zeros(s_ref.shape, F32)

    L = RWKV_CHUNK
    n = RW_CHUNKS_PER_STEP
    rows = lambda j: slice(j * L, (j + 1) * L)
    lanes = lambda q: slice(q * LANES, (q + 1) * LANES)
    refs = ((lwf, rf, kf, vf, af), (lwb, rb, kb, vb, ab))
    y_refs = (yf_ref, yb_ref)
    visit = [(d, j if d == 0 else n - 1 - j) for j in range(n) for d in (0, 1)]
    pairs = range(RW_PAIRS_PER_STEP)
    pre = _rwkv_chunks([tuple(ref[rows(j), lanes(q)] for ref in refs[d]) + (d,) for d, j in visit for q in pairs],
                       [kk_ref[:, lanes(q)] for _ in visit for q in pairs],
                       [ka_ref[:, lanes(q)] for _ in visit for q in pairs])
    s = [[s_ref[q, 0], s_ref[q, 1]] for q in pairs]
    chain = iter(pre)
    for d, j in visit:
        for q in pairs:
            y0, r2, g, b = next(chain)
            y_refs[d][rows(j), lanes(q)] = y0 + _mdot(r2, s[q][d], NT, RW_PASSES)
            s[q][d] = _mdot(s[q][d], g, NN, RW_PASSES) + b
    for q in range(RW_PAIRS_PER_STEP):
        s_ref[q, 0] = s[q][0]
        s_ref[q, 1] = s[q][1]


def rwkv_scan(r, k, v, lw, a, k_k, k_a, n_ctx):
    t, d = r.shape
    L = RWKV_CHUNK
    blk = RW_CHUNKS_PER_STEP * L
    w = RW_PAIRS_PER_STEP * LANES
    assert 2 * L == LANES and t % blk == 0 and n_ctx % blk == 0 and d % w == 0
    nc, nc0 = t // blk, n_ctx // blk
    nh = d // w
    fwd = pl.BlockSpec((blk, w), lambda h, c: (c, h))
    bwd = pl.BlockSpec((blk, w), lambda h, c: (_bwd_chunk(c, nc0, nc), h))
    bwd2 = pl.BlockSpec((blk, w), lambda h, c: (_bwd_chunk(c, nc0, nc), nh + h))
    par = pl.BlockSpec((1, w), lambda h, c: (0, h))
    return pl.pallas_call(
        _rwkv_body,
        grid=(nh, nc),
        in_specs=[fwd] * 5 + [bwd2, bwd, bwd, bwd, bwd2, par, par],
        out_specs=[fwd, bwd],
        out_shape=[jax.ShapeDtypeStruct((t, d), F32)] * 2,
        scratch_shapes=[pltpu.VMEM((RW_PAIRS_PER_STEP, 2, LANES, LANES), F32)],
        compiler_params=pltpu.CompilerParams(
            dimension_semantics=("arbitrary", "arbitrary"), vmem_limit_bytes=VMEM_LIMIT),
        name="rwkv7_scan",
    )(lw, r, k, v, a, lw, r, k, v, a, k_k.reshape(1, d), k_a.reshape(1, d))


MIX_TM = 256


def _mix_body(cur_ref, prev_ref, next_ref, mu_ref, *o_refs, nct, nt):
    i = pl.program_id(0)
    is_ctx = i < nct
    cur = cur_ref[...]
    tm, d = cur.shape
    qd = d // 4
    above = jnp.where(is_ctx | (i == nct), 0.0, prev_ref[...])
    below = jnp.where(is_ctx | (i == nt - 1), 0.0, next_ref[...])
    ext = jnp.concatenate([above, cur, below], axis=0)
    n = ext.shape[0]
    colid = lax.broadcasted_iota(jnp.int32, (n, 2 * qd), 0) % GRID_W
    left = jnp.where(is_ctx | (colid != 0), pltpu.roll(ext[:, :2 * qd], 1, 0), 0.0)
    colid3 = lax.broadcasted_iota(jnp.int32, (n, 3 * qd), 0) % GRID_W
    right = jnp.where(is_ctx | (colid3 != GRID_W - 1), pltpu.roll(ext[:, qd:], n - 1, 0), 0.0)
    mid = slice(GRID_W, GRID_W + tm)
    sh = jnp.concatenate([
        left[mid, :qd],
        jnp.where(is_ctx, left[mid, qd:], right[mid, :qd]),
        jnp.where(is_ctx, right[mid, qd:2 * qd], ext[0:tm, 2 * qd:3 * qd]),
        jnp.where(is_ctx, right[mid, 2 * qd:], ext[2 * GRID_W:2 * GRID_W + tm, 3 * qd:])], axis=1)
    dx = sh - cur
    for b, o_ref in enumerate(o_refs):
        o_ref[...] = (cur + dx * mu_ref[b:b + 1, :]).astype(o_ref.dtype)


def rwkv_mix(h, mu, n_ctx):
    t, d = h.shape
    tm = MIX_TM
    assert t % tm == 0 and n_ctx % tm == 0 and tm % GRID_W == 0
    r = tm // GRID_W
    nt, nu = t // tm, t // GRID_W
    nb = mu.shape[0]
    return pl.pallas_call(
        functools.partial(_mix_body, nct=n_ctx // tm, nt=nt),
        grid=(nt,),
        in_specs=[pl.BlockSpec((tm, d), lambda i: (i, 0)),
                  pl.BlockSpec((GRID_W, d), lambda i: (jnp.maximum(i * r - 1, 0), 0)),
                  pl.BlockSpec((GRID_W, d), lambda i: (jnp.minimum((i + 1) * r, nu - 1), 0)),
                  pl.BlockSpec((nb, d), lambda i: (0, 0))],
        out_specs=[pl.BlockSpec((tm, d), lambda i: (i, 0))] * nb,
        out_shape=[jax.ShapeDtypeStruct((t, d), BF16)] * nb,
        compiler_params=pltpu.CompilerParams(dimension_semantics=("arbitrary",), vmem_limit_bytes=VMEM_LIMIT),
        name="rwkv_mix",
    )(h, h, h, mu)


def _rwkv_out_body(yf_ref, yb_ref, r_ref, k_ref, v_ref, g_ref, af_ref, ab_ref, p_ref, e_ref, et_ref, w_ref, o_ref):
    e, et = e_ref[...], et_ref[...]
    head_sum = lambda x: _dot_01_rhs(_dot_01_rhs(x, e), et)
    k_a, r_k, gain, bias = (p_ref[n:n + 1, :] for n in range(4))
    y = yf_ref[...] + yb_ref[...]
    yc = y - head_sum(y) * (1.0 / RWKV_N)
    var = head_sum(yc * yc) * (1.0 / RWKV_N)
    yn = yc * lax.rsqrt(var + RWKV_GN_EPS) * gain + bias
    kbar = k_ref[...] * (1.0 + (0.5 * (af_ref[...] + ab_ref[...]) - 1.0) * k_a)
    bonus = head_sum(r_ref[...] * kbar * r_k) * v_ref[...]
    z = (yn + bonus) * g_ref[...]
    o_ref[...] = _d(z.astype(BF16), w_ref[...], NN)


def rwkv_out(yf, yb, r, k, v, g, a, k_a, r_k, lnx_g, lnx_b, w_out):
    t, d = yf.shape
    tm = LN_TM
    nh = d // RWKV_N
    row = pl.BlockSpec((tm, d), lambda i: (i, 0))
    const = lambda shape: pl.BlockSpec(shape, lambda i: (0, 0))
    e = (jnp.arange(d)[:, None] // RWKV_N == jnp.arange(LANES)[None, :]).astype(BF16)
    params = jnp.stack([k_a, r_k, lnx_g, lnx_b])
    return pl.pallas_call(
        _rwkv_out_body,
        grid=(t // tm,),
        in_specs=[row] * 6 + [row, pl.BlockSpec((tm, d), lambda i: (i, 1)),
                  const((4, d)), const((d, LANES)), const((LANES, d)), const((d, d))],
        out_specs=row,
        out_shape=jax.ShapeDtypeStruct((t, d), F32),
        compiler_params=pltpu.CompilerParams(dimension_semantics=("arbitrary",), vmem_limit_bytes=VMEM_LIMIT),
        name="rwkv_out",
    )(yf, yb, r, k, v, g, a, a, params, e, e.T, w_out.astype(BF16))


def _peer_stats(hx_ref, wk_ref, sc_ref, hxb_ref, n_ref, f0_ref, r1_ref, e1_ref):
    tm = hx_ref.shape[0]
    hxb_ref[...] = hx_ref[...].astype(F32).T.astype(BF16)
    rows_per = 4 * LANES

    def scores(c, carry):
        rows = pl.ds(pl.multiple_of(c * rows_per, rows_per), rows_per)
        sc_ref[rows, :] = _d(wk_ref[rows, :], hxb_ref[...], NN)
        return carry

    lax.fori_loop(0, wk_ref.shape[0] // rows_per, scores, 0)

    def block(tb, carry):
        lanes = pl.ds(pl.multiple_of(tb * LANES, LANES), LANES)
        neg = jnp.full((N_KEYS, LANES), -jnp.inf, F32)
        head_row = lax.broadcasted_iota(jnp.int32, (PEER_HEADS, LANES), 0)

        def extract(h, tops):
            tops = [list(t) for t in tops]
            rows = [pl.ds(pl.multiple_of((2 * h + p) * N_KEYS, N_KEYS), N_KEYS) for p in range(2)]
            cur = [sc_ref[rows[p], lanes] for p in range(2)]
            rank = jnp.full((N_KEYS, LANES), float(PEER_TOPK), F32)
            for a in range(PEER_TOPK):
                m = [jnp.max(c, axis=0, keepdims=True) for c in cur]
                hit = [c >= mm for c, mm in zip(cur, m)]
                cur = [jnp.where(ht, neg, c) for ht, c in zip(hit, cur)]
                rank = jnp.where(hit[1], float(a), rank)
                for p in range(2):
                    tops[p][a] = jnp.where(head_row == h, m[p], tops[p][a])
            r1_ref[h, :, lanes] = rank.astype(BF16)
            return tuple(tuple(t) for t in tops)

        zero = jnp.zeros((PEER_HEADS, LANES), F32)
        top0, top1 = lax.fori_loop(0, PEER_HEADS, extract, ((zero,) * PEER_TOPK,) * 2)
        cands = [top0[a] + top1[b]
                 for a in range(PEER_TOPK) for b in range(PEER_TOPK) if (a + 1) * (b + 1) <= PEER_TOPK]
        c_max = cands[0]
        z = jnp.zeros_like(c_max)
        tau = c_max
        for a in range(PEER_TOPK):
            tau = functools.reduce(jnp.maximum, cands)
            z = z + jnp.exp(tau - c_max)
            cands = [jnp.where(cd >= tau, -jnp.inf, cd) for cd in cands]
        inv_z = 1.0 / z

        def factors(h, carry):
            row_of = lambda x: jnp.max(jnp.where(head_row == h, x, -jnp.inf), axis=0, keepdims=True)
            s0 = sc_ref[pl.ds(pl.multiple_of(2 * h * N_KEYS, N_KEYS), N_KEYS), lanes]
            s1 = sc_ref[pl.ds(pl.multiple_of((2 * h + 1) * N_KEYS, N_KEYS), N_KEYS), lanes]
            tau_h = row_of(tau)
            n = jnp.zeros((N_KEYS, LANES), F32)
            for b in range(PEER_TOPK):
                n = jnp.where(s0 + row_of(top1[b]) >= tau_h, float(b + 1), n)
            n_ref[h, :, lanes] = n
            f0_ref[h, :, lanes] = jnp.exp(s0 - row_of(top0[0])) * row_of(inv_z)
            e1_ref[h, :, lanes] = jnp.exp(s1 - row_of(top1[0])).astype(BF16)
            return carry

        lax.fori_loop(0, PEER_HEADS, factors, 0)
        return carry

    lax.fori_loop(0, tm // LANES, block, 0)


PEER_I_GROUP = 4
PEER_STREAMS = 2


def _peer_body(hx_ref, wk_ref, *rest):
    u_refs, vt_refs = rest[:PEER_STREAMS], rest[PEER_STREAMS:2 * PEER_STREAMS]
    o_ref, sc_ref, hxb_ref, n_ref, f0_ref, r1_ref, e1_ref, w_ref, acc_ref = rest[2 * PEER_STREAMS:]
    e = pl.program_id(1)
    tm = hx_ref.shape[0]
    nsb = tm // PEER_SB

    @pl.when(e == 0)
    def _():
        _peer_stats(hx_ref, wk_ref, sc_ref, hxb_ref, n_ref, f0_ref, r1_ref, e1_ref)
        acc_ref[...] = jnp.zeros(acc_ref.shape, F32)

    i_rows = pl.ds(pl.multiple_of(e * PEER_I_BLOCK, PEER_I_BLOCK), PEER_I_BLOCK)

    def activations(sb):
        parts = []
        for u_ref in u_refs:
            act = _d(u_ref[...], hxb_ref[:, sb * PEER_SB:(sb + 1) * PEER_SB], NN)
            act = act.astype(BF16)
            parts.append(0.5 * act * (1.0 + lax.erf(act * (2.0 ** -0.5))))
        return parts

    def gates(sb, act):
        for hb in range(PEER_SB // LANES):
            lanes = slice(sb * PEER_SB + hb * LANES, sb * PEER_SB + (hb + 1) * LANES)
            sub = slice(hb * LANES, (hb + 1) * LANES)
            n8 = [n_ref[h, i_rows, lanes] for h in range(PEER_HEADS)]
            f8 = [f0_ref[h, i_rows, lanes] for h in range(PEER_HEADS)]
            bcast = lambda x, ii: jnp.broadcast_to(x[ii:ii + 1], (N_KEYS, LANES)).astype(BF16)
            for ig in range(0, PEER_I_BLOCK, PEER_I_GROUP):
                g = [jnp.zeros((N_KEYS, LANES), BF16) for _ in range(PEER_I_GROUP)]
                for h in range(PEER_HEADS):
                    r1 = r1_ref[h, :, lanes]
                    e1 = e1_ref[h, :, lanes]
                    for k in range(PEER_I_GROUP):
                        ii = ig + k
                        g[k] = g[k] + jnp.where(r1 < bcast(n8[h], ii), e1 * bcast(f8[h], ii), jnp.zeros_like(e1))
                for k in range(PEER_I_GROUP):
                    rows = slice((ig + k) * N_KEYS, (ig + k + 1) * N_KEYS)
                    part, off = divmod((ig + k) * N_KEYS, PEER_TE // PEER_STREAMS)
                    w_ref[sb, rows, sub] = g[k] * act[part][off:off + N_KEYS, sub]

    def accumulate(sb):
        cols = slice(sb * PEER_SB, (sb + 1) * PEER_SB)
        dr = acc_ref.shape[0] // PEER_STREAMS
        for k, vt_ref in enumerate(vt_refs):
            acc_ref[k * dr:(k + 1) * dr, cols] += _d(vt_ref[...], w_ref[sb], NN)

    act = activations(0)
    for sb in range(nsb):
        nxt = activations(sb + 1) if sb + 1 < nsb else None
        gates(sb, act)
        accumulate(sb)
        act = nxt

    @pl.when(e == pl.num_programs(1) - 1)
    def _():
        o_ref[...] = acc_ref[...].T


def _fold_body(k_ref, w_ref, o_ref):
    o_ref[...] = _mdot(k_ref[...], w_ref[...], NN, 6)


def peer_fold_keys(wq, keys):
    d = wq.shape[0]
    nhp, nk, dk = keys.shape
    wqt = wq.T.reshape(nhp, dk, d)
    return pl.pallas_call(
        _fold_body,
        grid=(nhp,),
        in_specs=[pl.BlockSpec((None, nk, dk), lambda i: (i, 0, 0)),
                  pl.BlockSpec((None, dk, d), lambda i: (i, 0, 0))],
        out_specs=pl.BlockSpec((nk, d), lambda i: (i, 0)),
        out_shape=jax.ShapeDtypeStruct((nhp * nk, d), F32),
        compiler_params=pltpu.CompilerParams(dimension_semantics=("arbitrary",), vmem_limit_bytes=VMEM_LIMIT),
        name="peer_fold_keys",
    )(keys, wqt)


def peer(hx, wk, u_bf, vt_bf):
    t, d = hx.shape
    tm = PEER_TM
    assert t % tm == 0
    ne = u_bf.shape[0] // PEER_TE
    h = PEER_HEADS
    ns = PEER_STREAMS
    return pl.pallas_call(
        _peer_body,
        grid=(t // tm, ne),
        in_specs=[pl.BlockSpec((tm, d), lambda i, e: (i, 0), pipeline_mode=pl.Buffered(1)),
                  pl.BlockSpec(wk.shape, lambda i, e: (0, 0), pipeline_mode=pl.Buffered(1)),
                  *[pl.BlockSpec((PEER_TE // ns, d), functools.partial(lambda i, e, k: (e * ns + k, 0), k=k))
                    for k in range(ns)],
                  *[pl.BlockSpec((None, d // ns, PEER_TE), functools.partial(lambda i, e, k: (e, k, 0), k=k))
                    for k in range(ns)]],
        out_specs=pl.BlockSpec((tm, d), lambda i, e: (i, 0)),
        out_shape=jax.ShapeDtypeStruct((t, d), F32),
        scratch_shapes=[pltpu.VMEM((2 * h * N_KEYS, tm), F32),
                        pltpu.VMEM((d, tm), BF16),
                        pltpu.VMEM((h, N_KEYS, tm), F32), pltpu.VMEM((h, N_KEYS, tm), F32),
                        pltpu.VMEM((h, N_KEYS, tm), BF16), pltpu.VMEM((h, N_KEYS, tm), BF16),
                        pltpu.VMEM((tm // PEER_SB, PEER_TE, PEER_SB), BF16), pltpu.VMEM((d, tm), F32)],
        compiler_params=pltpu.CompilerParams(
            dimension_semantics=("arbitrary", "arbitrary"), vmem_limit_bytes=VMEM_LIMIT),
        name="peer_dense",
    )(hx, wk.astype(BF16), *([u_bf] * ns), *([vt_bf] * ns))


LN_TM = 256


def _ln_mod_body(xs_ref, y_ref, mg_ref, mn_ref, lng_ref, lnb_ref, xo_ref, ho_ref, *, gate_col, mod_col, nct, nt):
    i = pl.program_id(0)
    d = xs_ref.shape[1]
    is_ctx = i < nct

    def pick(ref, col):
        return jnp.where(is_ctx, ref[1:2, col * d:(col + 1) * d], ref[0:1, col * d:(col + 1) * d])

    z = DN_ALPHA * xs_ref[...] + pick(mg_ref, gate_col) * y_ref[...]
    mu = jnp.mean(z, axis=-1, keepdims=True)
    zc = z - mu
    var = jnp.mean(zc * zc, axis=-1, keepdims=True)
    xn = zc * lax.rsqrt(var + LN_EPS) * lng_ref[...] + lnb_ref[...]
    xo_ref[...] = xn
    h = xn * (1.0 + pick(mn_ref, mod_col + 1)) + pick(mn_ref, mod_col)
    ho_ref[...] = jnp.where(i < nt, h, 0.0).astype(ho_ref.dtype)


def ln_mod(xs, y, mod_gate, gate_col, mod_next, mod_col, ln_g, ln_b, n_ctx, pad_to, h_dtype):
    t, d = xs.shape
    tm = LN_TM
    assert t % tm == 0 and n_ctx % tm == 0 and pad_to % tm == 0 and pad_to >= t
    nt = t // tm
    row = lambda i: (jnp.minimum(i, nt - 1), 0)
    full = lambda a: pl.BlockSpec(a.shape, lambda i: (0, 0))
    return pl.pallas_call(
        functools.partial(_ln_mod_body, gate_col=gate_col, mod_col=mod_col, nct=n_ctx // tm, nt=nt),
        grid=(pad_to // tm,),
        in_specs=[pl.BlockSpec((tm, d), row), pl.BlockSpec((tm, d), row), full(mod_gate), full(mod_next),
                  pl.BlockSpec((1, d), lambda i: (0, 0)), pl.BlockSpec((1, d), lambda i: (0, 0))],
        out_specs=[pl.BlockSpec((tm, d), row), pl.BlockSpec((tm, d), lambda i: (i, 0))],
        out_shape=[jax.ShapeDtypeStruct((t, d), F32), jax.ShapeDtypeStruct((pad_to, d), h_dtype)],
        compiler_params=pltpu.CompilerParams(dimension_semantics=("arbitrary",), vmem_limit_bytes=VMEM_LIMIT),
        name="ln_mod",
    )(xs, y, mod_gate, mod_next, ln_g.reshape(1, d), ln_b.reshape(1, d))


def _ln(x, g, b):
    mu = jnp.mean(x, -1, keepdims=True)
    xc = x - mu
    var = jnp.mean(xc * xc, -1, keepdims=True)
    return xc * lax.rsqrt(var + LN_EPS) * g + b


def _head_norm(h, nheads, eps):
    t = h.shape[0]
    hh = h.reshape(t, nheads, -1)
    mu = jnp.mean(hh, -1, keepdims=True)
    xc = hh - mu
    var = jnp.mean(xc * xc, -1, keepdims=True)
    return (xc * lax.rsqrt(var + eps)).reshape(t, -1)


def _pad_cols(w, n):
    return jnp.pad(w, ((0, 0), (0, n - w.shape[1])))


def _shift_rows(u, k):
    if k > 0:
        return jnp.concatenate([u[k:], jnp.zeros_like(u[:k])], axis=0)
    return jnp.concatenate([jnp.zeros_like(u[:-k]), u[:k]], axis=0)


def _grid_conv(u, w, b):
    s = u.shape[0]
    colid = (jnp.arange(s) % GRID_W)[:, None]
    out = jnp.zeros_like(u) + b
    for di in range(3):
        for dj in range(3):
            off = (di - 1) * GRID_W + (dj - 1)
            sh = _shift_rows(u, off) if off else u
            if dj == 0:
                sh = jnp.where(colid == 0, 0.0, sh)
            elif dj == 2:
                sh = jnp.where(colid == GRID_W - 1, 0.0, sh)
            out = out + sh * w[di, dj]
    return out


def _seq_conv(u, w, b):
    wc = w[1]
    return _shift_rows(u, -1) * wc[0] + u * wc[1] + _shift_rows(u, 1) * wc[2] + b


def _qshift(u):
    q = u.shape[1] // 4
    colid = (jnp.arange(u.shape[0]) % GRID_W)[:, None]
    left = jnp.where(colid == 0, 0.0, _shift_rows(u[:, :q], -1))
    right = jnp.where(colid == GRID_W - 1, 0.0, _shift_rows(u[:, q:2 * q], 1))
    up = _shift_rows(u[:, 2 * q:3 * q], -GRID_W)
    down = _shift_rows(u[:, 3 * q:], GRID_W)
    return jnp.concatenate([left, right, up, down], axis=1)


def _shift_seq(u):
    h = u.shape[1] // 2
    return jnp.concatenate([_shift_rows(u[:, :h], -1), _shift_rows(u[:, h:], 1)], axis=1)


def _mlstm_layer(h, n_ctx, w_in, b_in, conv_w, conv_b, hn_g, w_out):
    d = D_MODEL
    t = h.shape[0]
    qk_pre = matmul(h, w_in[:, :2 * d], b_in[:2 * d])
    v = matmul(h, w_in[:, 2 * d:3 * d], b_in[2 * d:3 * d], out_dtype=BF16)
    o = matmul(h, w_in[:, 3 * d:4 * d], b_in[3 * d:4 * d], act="sigmoid", out_dtype=BF16)
    g = matmul(h, _pad_cols(w_in[:, 4 * d:], LANES), jnp.pad(b_in[4 * d:], (0, LANES - 4 * MLSTM_HEADS)))
    g = g[:, :4 * MLSTM_HEADS].reshape(t, 4, MLSTM_HEADS)
    g = jnp.concatenate([g[:, :2], jax.nn.log_sigmoid(g[:, 2:])], axis=1)
    gh = jnp.transpose(g, (2, 0, 1))
    gcol = jnp.pad(gh, ((0, 0), (0, 0), (0, LANES - 4)))
    grow = jnp.pad(jnp.transpose(gh, (0, 2, 1)), ((0, 0), (0, 4), (0, 0)))
    qk = mlstm_conv(qk_pre, conv_w, conv_b, n_ctx)
    hf, hb = mlstm_scan(qk, v, gcol, grow, n_ctx)
    return mlstm_out(hf, hb, o, hn_g, w_out)


def _rwkv_layer(h, n_ctx, mu, w_rkv, w0, w1, w2, a0, a1, a2, g1, g2, k_k, k_a, r_k, lnx_g, lnx_b, w_out):
    d = D_MODEL
    xm = rwkv_mix(h, mu, n_ctx)
    r = matmul(xm[0], w_rkv[0])
    k = matmul(xm[1], w_rkv[1])
    v = matmul(xm[2], w_rkv[2])

    def lora_pair(x, w_in, w_mid, bias, act_mid, act_out):
        rank = w_in.shape[-1]
        w_a = _pad_cols(jnp.concatenate([w_in[0], w_in[1]], axis=1), LANES)
        zpad = jnp.zeros((rank, d), F32)
        w_b = jnp.concatenate([jnp.concatenate([w_mid[0], zpad], axis=1),
                               jnp.concatenate([zpad, w_mid[1]], axis=1)], axis=0)
        w_b = jnp.pad(w_b, ((0, LANES - 2 * rank), (0, 0)))
        mid = matmul(x, w_a, act=act_mid, out_dtype=BF16)
        return matmul(mid, w_b, jnp.concatenate([bias[0], bias[1]]), act=act_out)

    lw = lora_pair(xm[3], w1, w2, w0, "tanh", "logdecay")
    a = lora_pair(xm[4], a1, a2, a0, None, "sigmoid")
    gpad = 2 * LANES
    gg = matmul(xm[5], _pad_cols(g1, gpad), act="sigmoid", out_dtype=BF16)
    g = matmul(gg, jnp.pad(g2, ((0, gpad - g1.shape[1]), (0, 0))))
    yf, yb = rwkv_scan(r, k, v, lw, a, k_k, k_a, n_ctx)
    return rwkv_out(yf, yb, r, k, v, g, a, k_a, r_k, lnx_g, lnx_b, w_out)


def _forward(x, c, ctx, c_ctx, ada_w, ada_b, ln_g, ln_b,
             ml_w_in, ml_b_in, ml_conv_w, ml_conv_b, ml_hn_g, ml_w_out,
             rw_mu, rw_w_rkv, rw_w0, rw_w1, rw_w2, rw_a0, rw_a1, rw_a2, rw_g1, rw_g2,
             rw_k_k, rw_k_a, rw_r_k, rw_lnx_g, rw_lnx_b, rw_w_out,
             pk_wq, pk_keys, pk_u, pk_v):
    d = D_MODEL
    n_ctx = ctx.shape[1]
    n_lat = x.shape[1]
    xs = jnp.concatenate([ctx[0], x[0]], axis=0)
    t = xs.shape[0]
    t_pad = -(-t // PEER_TM) * PEER_TM
    s_in = jnp.zeros((8, d), F32).at[0].set(jax.nn.silu(c[0])).at[1].set(jax.nn.silu(c_ctx))
    depth = ada_w.shape[0]
    mods = [matmul(s_in, ada_w[i], ada_b[i], passes=3) for i in range(depth)]
    is_ctx = (jnp.arange(t) < n_ctx)[:, None]
    m0 = [jnp.where(is_ctx, mods[0][1, n * d:(n + 1) * d], mods[0][0, n * d:(n + 1) * d]) for n in range(2)]
    mixer_dtype = lambda i: BF16 if i % 2 == 0 else F32
    h = (xs * (1.0 + m0[1]) + m0[0]).astype(mixer_dtype(0))
    for i in range(depth):
        j = i // 2
        if i % 2 == 0:
            y = _mlstm_layer(h, n_ctx, ml_w_in[j], ml_b_in[j], ml_conv_w[j], ml_conv_b[j],
                             ml_hn_g[j], ml_w_out[j])
        else:
            y = _rwkv_layer(h, n_ctx, rw_mu[j], rw_w_rkv[j], rw_w0[j], rw_w1[j], rw_w2[j],
                            rw_a0[j], rw_a1[j], rw_a2[j], rw_g1[j], rw_g2[j], rw_k_k[j],
                            rw_k_a[j], rw_r_k[j], rw_lnx_g[j], rw_lnx_b[j], rw_w_out[j])
        xs, h = ln_mod(xs, y, mods[i], 2, mods[i], 3, ln_g[i, 0], ln_b[i, 0], n_ctx, t_pad, BF16)
        wk = peer_fold_keys(pk_wq[i], pk_keys[i].reshape(2 * PEER_HEADS, N_KEYS, PEER_DQ // 2))
        vt = jnp.swapaxes(pk_v[i].astype(BF16).reshape(-1, PEER_TE, d), 1, 2)
        y = peer(h, wk, pk_u[i].astype(BF16), vt)
        xs, h = ln_mod(xs, y, mods[i], 5, mods[min(i + 1, depth - 1)], 0, ln_g[i, 1], ln_b[i, 1], n_ctx, t,
                       mixer_dtype(i + 1))
    return xs[n_ctx:][None]


def kernel(x, c, ctx, c_ctx, ada_w, ada_b, ln_g, ln_b, ml_w_in, ml_b_in, ml_conv_w, ml_conv_b, ml_hn_g, ml_w_out, rw_mu, rw_w_rkv, rw_w0, rw_w1, rw_w2, rw_a0, rw_a1, rw_a2, rw_g1, rw_g2, rw_k_k, rw_k_a, rw_r_k, rw_lnx_g, rw_lnx_b, rw_w_out, pk_wq, pk_keys, pk_u, pk_v):
    return _forward(x, c, ctx, c_ctx, ada_w, ada_b, ln_g, ln_b,
                    ml_w_in, ml_b_in, ml_conv_w, ml_conv_b, ml_hn_g, ml_w_out,
                    rw_mu, rw_w_rkv, rw_w0, rw_w1, rw_w2, rw_a0, rw_a1, rw_a2, rw_g1, rw_g2,
                    rw_k_k, rw_k_a, rw_r_k, rw_lnx_g, rw_lnx_b, rw_w_out,
                    pk_wq, pk_keys, pk_u, pk_v)
```

```python
import functools

import jax
import jax.numpy as jnp
from jax import lax
from jax.experimental import pallas as pl
from jax.experimental.pallas import tpu as pltpu

F32 = jnp.float32
BF16 = jnp.bfloat16

D_MODEL = 1024
DEPTH = 4
GRID_W = 64
DN_ALPHA = (2.0 * DEPTH) ** 0.25
LN_EPS = 1e-5

MLSTM_HEADS = 4
MLSTM_DH = D_MODEL // MLSTM_HEADS
MLSTM_CHUNK = 128
M_INIT = -1e30

RWKV_N = 64
RWKV_HEADS = D_MODEL // RWKV_N
RWKV_CHUNK = 64
RWKV_GN_EPS = 64e-5

N_KEYS = 128
PEER_HEADS = 8
PEER_DQ = 256
PEER_TOPK = 16
PEER_I_BLOCK = 8
PEER_TE = PEER_I_BLOCK * N_KEYS
PEER_TM = 768
PEER_SB = 256

LANES = 128
VMEM_LIMIT = 62 * 1024 * 1024

NN = ((1,), (0,))
NT = ((1,), (1,))


def _split(x, n):
    parts = []
    r = x.astype(F32)
    for i in range(n):
        p = r.astype(BF16)
        parts.append(p)
        if i + 1 < n:
            r = r - p.astype(F32)
    return parts


def _d(a, b, dims):
    return lax.dot_general(a, b, (dims, ((), ())), preferred_element_type=F32)


def _mdot(a, b, dims, passes):
    if passes == 1:
        return _d(a.astype(BF16), b.astype(BF16), dims)
    if passes == 3:
        a0, a1 = _split(a, 2)
        b0, b1 = _split(b, 2)
        return (_d(a0, b1, dims) + _d(a1, b0, dims)) + _d(a0, b0, dims)
    a0, a1, a2 = _split(a, 3)
    b0, b1, b2 = _split(b, 3)
    lo = (_d(a0, b2, dims) + _d(a2, b0, dims)) + _d(a1, b1, dims)
    mid = _d(a0, b1, dims) + _d(a1, b0, dims)
    return (lo + mid) + _d(a0, b0, dims)


def _dot_01_lhs(m01, x):
    mb = m01.astype(BF16)
    x0, x1, x2 = _split(x, 3)
    return (_d(mb, x2, NN) + _d(mb, x1, NN)) + _d(mb, x0, NN)


def _dot_01_rhs(x, m01):
    mb = m01.astype(BF16)
    x0, x1, x2 = _split(x, 3)
    return (_d(x2, mb, NN) + _d(x1, mb, NN)) + _d(x0, mb, NN)


def _pick(n, cands):
    for c in cands:
        if n % c == 0:
            return c
    raise ValueError(f"no tile for {n}")


_ACTS = {None: lambda x: x, "sigmoid": jax.nn.sigmoid, "tanh": jnp.tanh,
         "logdecay": lambda x: -(2.718281828459045 ** -0.5) * jax.nn.sigmoid(x)}


def _mm_body(x_ref, w_ref, b_ref, o_ref, *, passes, act):
    o_ref[...] = _ACTS[act](_mdot(x_ref[...], w_ref[...], NN, passes) + b_ref[...]).astype(o_ref.dtype)


def matmul(x, w, b=None, *, passes=1, act=None, out_dtype=F32):
    m, k = x.shape
    n = w.shape[1]
    assert n % LANES == 0 and w.shape[0] == k
    tm = m if m <= 1024 else _pick(m, (640, 512, 384, 256, 128))
    tn = _pick(n, (1024, 768, 640, 512, 384, 256, 128))
    if b is None:
        b = jnp.zeros((n,), F32)
    if passes == 1:
        w = w.astype(BF16)
    return pl.pallas_call(
        functools.partial(_mm_body, passes=passes, act=act),
        grid=(m // tm, n // tn),
        in_specs=[pl.BlockSpec((tm, k), lambda i, j: (i, 0)),
                  pl.BlockSpec((k, tn), lambda i, j: (0, j)),
                  pl.BlockSpec((1, tn), lambda i, j: (0, j))],
        out_specs=pl.BlockSpec((tm, tn), lambda i, j: (i, j)),
        out_shape=jax.ShapeDtypeStruct((m, n), out_dtype),
        compiler_params=pltpu.CompilerParams(
            dimension_semantics=("arbitrary", "arbitrary"), vmem_limit_bytes=VMEM_LIMIT),
        name="proj_matmul",
    )(x, w, b.reshape(1, n).astype(F32))


CONV_TM = 256
CONV_CB = 512


def _conv_body(cur_ref, prev_ref, next_ref, w_ref, b_ref, o_ref, *, nct, nt, q_blocks, q_scale):
    i = pl.program_id(0)
    j = pl.program_id(1)
    is_ctx = i < nct
    tm = cur_ref.shape[0]
    above = jnp.where(is_ctx | (i == nct), 0.0, prev_ref[...])
    below = jnp.where(is_ctx | (i == nt - 1), 0.0, next_ref[...])
    ext = jnp.concatenate([above, cur_ref[...], below], axis=0)
    n = ext.shape[0]
    colid = lax.broadcasted_iota(jnp.int32, ext.shape, 0) % GRID_W
    left = jnp.where(is_ctx | (colid != 0), pltpu.roll(ext, 1, 0), 0.0)
    right = jnp.where(is_ctx | (colid != GRID_W - 1), pltpu.roll(ext, n - 1, 0), 0.0)
    w = w_ref[...]
    acc = jnp.zeros((tm, ext.shape[1]), F32) + b_ref[...]
    for di in range(3):
        rows = slice(di * GRID_W, di * GRID_W + tm)
        tap = left[rows] * w[3 * di:3 * di + 1] + ext[rows] * w[3 * di + 1:3 * di + 2] \
            + right[rows] * w[3 * di + 2:3 * di + 3]
        acc = acc + (tap if di == 1 else jnp.where(is_ctx, 0.0, tap))
    y = acc * jax.nn.sigmoid(acc)
    o_ref[...] = (y * jnp.where(j < q_blocks, q_scale, 1.0)).astype(o_ref.dtype)


def mlstm_conv(qk_pre, conv_w, conv_b, n_ctx):
    t, c = qk_pre.shape
    tm, cb = CONV_TM, CONV_CB
    assert t % tm == 0 and n_ctx % tm == 0 and tm % GRID_W == 0 and c % (2 * cb) == 0
    r = tm // GRID_W
    nt, nu = t // tm, t // GRID_W
    return pl.pallas_call(
        functools.partial(_conv_body, nct=n_ctx // tm, nt=nt, q_blocks=c // (2 * cb), q_scale=MLSTM_DH ** -0.5),
        grid=(nt, c // cb),
        in_specs=[pl.BlockSpec((tm, cb), lambda i, j: (i, j)),
                  pl.BlockSpec((GRID_W, cb), lambda i, j: (jnp.maximum(i * r - 1, 0), j)),
                  pl.BlockSpec((GRID_W, cb), lambda i, j: (jnp.minimum((i + 1) * r, nu - 1), j)),
                  pl.BlockSpec((9, cb), lambda i, j: (0, j)),
                  pl.BlockSpec((1, cb), lambda i, j: (0, j))],
        out_specs=pl.BlockSpec((tm, cb), lambda i, j: (i, j)),
        out_shape=jax.ShapeDtypeStruct((t, c), BF16),
        compiler_params=pltpu.CompilerParams(
            dimension_semantics=("arbitrary", "arbitrary"), vmem_limit_bytes=VMEM_LIMIT),
        name="mlstm_conv",
    )(qk_pre, qk_pre, qk_pre, conv_w.reshape(9, c), conv_b.reshape(1, c))


def _mlstm_out_body(hf_ref, hb_ref, o_ref, g_ref, w_ref, y_ref):
    h = hf_ref[...] + hb_ref[...]
    parts = []
    for a in range(MLSTM_HEADS):
        x = h[:, a * MLSTM_DH:(a + 1) * MLSTM_DH]
        mu = jnp.mean(x, axis=-1, keepdims=True)
        xc = x - mu
        var = jnp.mean(xc * xc, axis=-1, keepdims=True)
        parts.append(xc * lax.rsqrt(var + LN_EPS))
    hn = jnp.concatenate(parts, axis=1)
    z = o_ref[...].astype(F32) * hn * g_ref[...]
    y_ref[...] = _d(z.astype(BF16), w_ref[...], NN)


def mlstm_out(hf, hb, o, hn_g, w_out):
    t, d = hf.shape
    tm = LN_TM
    row = pl.BlockSpec((tm, d), lambda i: (i, 0))
    return pl.pallas_call(
        _mlstm_out_body,
        grid=(t // tm,),
        in_specs=[row, row, row, pl.BlockSpec((1, d), lambda i: (0, 0)), pl.BlockSpec((d, d), lambda i: (0, 0))],
        out_specs=row,
        out_shape=jax.ShapeDtypeStruct((t, d), F32),
        compiler_params=pltpu.CompilerParams(dimension_semantics=("arbitrary",), vmem_limit_bytes=VMEM_LIMIT),
        name="mlstm_out",
    )(hf, hb, o, hn_g.reshape(1, d), w_out.astype(BF16))


ML_HEADS_PER_STEP = 4


def _mlstm_body(qf, kf, vf, gcf, grf, qb, kb, vb, gcb, grb, hf_ref, hb_ref, c_ref, n_ref, m_ref):
    @pl.when(pl.program_id(1) == 0)
    def _():
        c_ref[...] = jnp.zeros(c_ref.shape, F32)
        n_ref[...] = jnp.zeros(n_ref.shape, F32)
        m_ref[...] = jnp.full(m_ref.shape, M_INIT, F32)

    dh = MLSTM_DH
    cs = [(a, d) for a in range(ML_HEADS_PER_STEP) for d in (0, 1)]
    each = lambda f, *ls: [f(*xs) for xs in zip(*ls)]
    cols = lambda a: slice(a * dh, (a + 1) * dh)
    qs, ks, vs, gcs, grs = (qf, qb), (kf, kb), (vf, vb), (gcf, gcb), (grf, grb)
    qb16 = [qs[d][:, cols(a)] for a, d in cs]
    kb16 = [ks[d][:, cols(a)] for a, d in cs]
    vb16 = [vs[d][:, cols(a)] for a, d in cs]
    gc = [gcs[d][a] for a, d in cs]
    gr = [grs[d][a] for a, d in cs]
    q, k = (each(lambda x: x.astype(F32), x16) for x16 in (qb16, kb16))
    L = q[0].shape[0]
    row = lax.broadcasted_iota(jnp.int32, (L, L), 0)
    col = lax.broadcasted_iota(jnp.int32, (L, L), 1)
    seen = (col <= row, col >= row)
    tri = [jnp.where(m, 1.0, 0.0) for m in seen]
    b_col = [_dot_01_lhs(tri[d], g)[:, 2 + d:3 + d] for (a, d), g in zip(cs, gc)]
    b_row = [_dot_01_rhs(g, tri[1 - d])[2 + d:3 + d, :] for (a, d), g in zip(cs, gr)]
    ig_col = [g[:, d:d + 1] for (a, d), g in zip(cs, gc)]
    ig_row = [g[d:d + 1, :] for (a, d), g in zip(cs, gr)]
    m_st = [m_ref[a, d, 0:1, 0:1] for a, d in cs]
    c_st = [c_ref[a, d] for a, d in cs]
    n_st = [n_ref[a, d] for a, d in cs]

    dlog = [jnp.where(seen[d], bc - br + ir, -jnp.inf) for (a, d), bc, br, ir in zip(cs, b_col, b_row, ig_row)]
    m_inter = each(jnp.add, b_col, m_st)
    m_t = each(lambda mi, dl: jnp.maximum(mi, jnp.max(dl, axis=1, keepdims=True)), m_inter, dlog)
    qk = each(lambda x, y: _d(x, y, NT), qb16, kb16)
    s = each(lambda x, dl, mt: x * jnp.exp(dl - mt), qk, dlog, m_t)
    dec = each(lambda mi, mt: jnp.exp(mi - mt), m_inter, m_t)
    sv = each(lambda x, y: _d(x.astype(BF16), y, NN), s, vb16)
    qc = each(lambda x, y: _d(x, y.astype(BF16), NN), qb16, c_st)
    num = each(lambda x, dc, y: x + dc * y, sv, dec, qc)
    den = each(lambda x, dc, qq, nn: jnp.sum(x, axis=1, keepdims=True) + dc * jnp.sum(qq * nn, axis=1, keepdims=True),
               s, dec, q, n_st)
    h = each(lambda nu, de, mt: nu / jnp.maximum(jnp.abs(de), jnp.exp(-mt)), num, den, m_t)
    h_refs = (hf_ref, hb_ref)
    for (a, d), x in zip(cs, h):
        h_refs[d][:, cols(a)] = x

    b_last = [bc[0:1, :] if d else bc[L - 1:L, :] for (a, d), bc in zip(cs, b_col)]
    w_c = each(lambda bl, bc, ic: bl - bc + ic, b_last, b_col, ig_col)
    m_new = each(lambda bl, ms, w: jnp.maximum(bl + ms, jnp.max(w, axis=0, keepdims=True)), b_last, m_st, w_c)
    a_c = each(lambda w, mn: jnp.exp(w - mn), w_c, m_new)
    g_prev = each(lambda bl, ms, mn: jnp.exp(bl + ms - mn), b_last, m_st, m_new)
    ak = each(jnp.multiply, a_c, k)
    kv = each(lambda x, y: _d(x.T.astype(BF16), y, NN), ak, vb16)
    for i, (a, d) in enumerate(cs):
        c_ref[a, d] = g_prev[i] * c_st[i] + kv[i]
        n_ref[a, d] = g_prev[i] * n_st[i] + jnp.sum(ak[i], axis=0, keepdims=True)
        m_ref[a, d] = jnp.broadcast_to(m_new[i], m_ref.shape[2:])


def _bwd_chunk(c, nc0, nc):
    return jnp.where(c < nc0, nc0 - 1 - c, nc - 1 - (c - nc0))


def mlstm_scan(qk, v, gcol, grow, n_ctx):
    t = v.shape[0]
    L, dh, hs = MLSTM_CHUNK, MLSTM_DH, ML_HEADS_PER_STEP
    ng = MLSTM_HEADS // hs
    nc, nc0 = t // L, n_ctx // L
    idf = lambda c: c
    idb = lambda c: _bwd_chunk(c, nc0, nc)
    qkv = lambda f, off: pl.BlockSpec((L, hs * dh), lambda h, c: (f(c), off + h))
    gc_spec = lambda f: pl.BlockSpec((hs, L, LANES), lambda h, c: (h, f(c), 0))
    gr_spec = lambda f: pl.BlockSpec((hs, 8, L), lambda h, c: (h, 0, f(c)))
    return pl.pallas_call(
        _mlstm_body,
        grid=(ng, nc),
        in_specs=[qkv(idf, 0), qkv(idf, ng), qkv(idf, 0), gc_spec(idf), gr_spec(idf),
                  qkv(idb, 0), qkv(idb, ng), qkv(idb, 0), gc_spec(idb), gr_spec(idb)],
        out_specs=[qkv(idf, 0), qkv(idb, 0)],
        out_shape=[jax.ShapeDtypeStruct((t, D_MODEL), F32)] * 2,
        scratch_shapes=[pltpu.VMEM((hs, 2, dh, dh), F32), pltpu.VMEM((hs, 2, 1, dh), F32),
                        pltpu.VMEM((hs, 2, 8, LANES), F32)],
        compiler_params=pltpu.CompilerParams(
            dimension_semantics=("arbitrary", "arbitrary"), vmem_limit_bytes=VMEM_LIMIT),
        name="mlstm_scan",
    )(qk, qk, v, gcol, grow, qk, qk, v, gcol, grow)


RW_PASSES = 1
RW_CHUNKS_PER_STEP = 4


def _stack2(x, lane_head):
    return jnp.concatenate([jnp.where(lane_head == 0, x, 0.0), jnp.where(lane_head == 1, x, 0.0)], axis=0)


def _unstack2(x):
    L = x.shape[0] // 2
    return x[:L] + x[L:]


def _rwkv_chunks(chains, k_k, k_a):
    L = chains[0][0].shape[0]
    n2 = 2 * L
    p = RW_PASSES
    ds = [c[5] for c in chains]
    each = lambda f, *ls: [f(*xs) for xs in zip(*ls)]
    dot = lambda dims: (lambda a, b: _mdot(a, b, dims, p))
    lw, r, k_raw, v, a = ([c[i] for c in chains] for i in range(5))

    lane_r = lax.broadcasted_iota(jnp.int32, (LANES, LANES), 0) // RWKV_N
    lane_c = lax.broadcasted_iota(jnp.int32, (LANES, LANES), 1) // RWKV_N
    head_ones = jnp.where(lane_r == lane_c, 1.0, 0.0)
    kkr = each(jnp.multiply, k_raw, k_k)
    ss = each(lambda x: _dot_01_rhs(x * x, head_ones), kkr)
    kap = each(lambda x, q: x / jnp.maximum(jnp.sqrt(q), 1e-12), kkr, ss)
    alp = each(jnp.multiply, kap, a)
    k = each(lambda x, y, z: x * (1.0 + (y - 1.0) * z), k_raw, a, k_a)

    row = lax.broadcasted_iota(jnp.int32, (L, L), 0)
    col = lax.broadcasted_iota(jnp.int32, (L, L), 1)
    tris = (jnp.where(col <= row, 1.0, 0.0), jnp.where(col >= row, 1.0, 0.0))
    lp = [_dot_01_lhs(tris[d], x) for d, x in zip(ds, lw)]
    lp_end = [x[0:1, :] if d else x[L - 1:L, :] for d, x in zip(ds, lp)]
    e_neg = each(lambda x: jnp.exp(-x), lp)
    e_end = each(lambda x, xe: jnp.exp(xe - x), lp, lp_end)
    kap_t = each(lambda x, y, z: x * jnp.exp(y - z), kap, lp, lw)
    r_t = each(lambda x, y: x * jnp.exp(y), r, lp)
    k_h = each(jnp.multiply, k, e_neg)
    a_h = each(jnp.multiply, alp, e_neg)
    k_e = each(jnp.multiply, k, e_end)
    a_e = each(jnp.multiply, alp, e_end)

    lane_head = lax.broadcasted_iota(jnp.int32, (L, LANES), 1) // RWKV_N
    st = lambda x: _stack2(x, lane_head)
    kap_s, r_s, v_s, k_s, a_s = (each(st, x) for x in (kap_t, r_t, v, k_e, a_e))
    rhs_k = each(lambda x: jnp.concatenate([x, x], axis=0), k_h)
    rhs_a = each(lambda x: jnp.concatenate([x, x], axis=0), a_h)

    row2 = lax.broadcasted_iota(jnp.int32, (n2, n2), 0)
    col2 = lax.broadcasted_iota(jnp.int32, (n2, n2), 1)
    same_head = (row2 // L) == (col2 // L)
    strict = (same_head & (col2 < row2), same_head & (col2 > row2))
    incl = (same_head & (col2 <= row2), same_head & (col2 >= row2))
    zero = jnp.zeros((n2, n2), F32)
    masked = lambda masks: (lambda d, x: jnp.where(masks[d], x, zero))
    n_ka = each(masked(strict), ds, each(dot(NT), kap_s, rhs_a))
    m_kk = each(masked(strict), ds, each(dot(NT), kap_s, rhs_k))
    a_rk = each(masked(incl), ds, each(dot(NT), r_s, rhs_k))
    a_ra = each(masked(incl), ds, each(dot(NT), r_s, rhs_a))

    b16 = (row2 // 16) == (col2 // 16)
    b32 = (row2 // 32) == (col2 // 32)
    eye = jnp.where(row2 == col2, 1.0, 0.0)
    n16 = each(lambda x: jnp.where(b16, x, zero), n_ka)
    n_2 = each(dot(NN), n16, n16)
    n_4 = each(dot(NN), n_2, n_2)
    n_8 = each(dot(NN), n_4, n_4)
    inv = each(lambda x: eye - x, n16)
    for pw in (n_2, n_4, n_8):
        inv = each(jnp.add, inv, each(dot(NN), inv, pw))
    for sel in (lambda x: jnp.where(b32 & ~b16, x, zero), lambda x: jnp.where(b32, zero, x)):
        t1 = each(dot(NN), inv, each(sel, n_ka))
        inv = each(jnp.subtract, inv, each(dot(NN), t1, inv))

    mv = each(dot(NN), m_kk, v_s)
    w1u0 = each(dot(NN), inv, each(lambda x, y: jnp.concatenate([x, y], axis=1), kap_s, mv))
    ar = each(dot(NN), a_ra, w1u0)
    av = each(dot(NN), a_rk, v_s)
    r2 = each(lambda x, y: _unstack2(x - y[:, :LANES]), r_s, ar)
    y0 = each(lambda x, y: _unstack2(x - y[:, LANES:]), av, ar)
    w1_t = each(lambda x: x[:, :LANES].T, w1u0)
    u0_t = each(lambda x: x[:, LANES:].T, w1u0)
    v_t = each(lambda x: x.T, v_s)
    wa = each(dot(NN), w1_t, a_s)
    vk = each(dot(NN), v_t, k_s)
    ua = each(dot(NN), u0_t, a_s)
    g = each(lambda xe, x: jnp.where(row2 == col2, jnp.exp(xe), zero) - x, lp_end, wa)
    b = each(jnp.subtract, vk, ua)
    return list(zip(y0, r2, g, b))


RW_PAIRS_PER_STEP = 2


def _rwkv_body(lwf, rf, kf, vf, af, lwb, rb, kb, vb, ab, kk_ref, ka_ref, yf_ref, yb_ref, s_ref):
    @pl.when(pl.program_id(1) == 0)
    def _():
        s_ref[...] = jnp.zeros(s_ref.shape, F32)

    L = RWKV_CHUNK
    n = RW_CHUNKS_PER_STEP
    rows = lambda j: slice(j * L, (j + 1) * L)
    lanes = lambda q: slice(q * LANES, (q + 1) * LANES)
    refs = ((lwf, rf, kf, vf, af), (lwb, rb, kb, vb, ab))
    y_refs = (yf_ref, yb_ref)
    visit = [(d, j if d == 0 else n - 1 - j) for j in range(n) for d in (0, 1)]
    pairs = range(RW_PAIRS_PER_STEP)
    pre = _rwkv_chunks([tuple(ref[rows(j), lanes(q)] for ref in refs[d]) + (d,) for d, j in visit for q in pairs],
                       [kk_ref[:, lanes(q)] for _ in visit for q in pairs],
                       [ka_ref[:, lanes(q)] for _ in visit for q in pairs])
    s = [[s_ref[q, 0], s_ref[q, 1]] for q in pairs]
    chain = iter(pre)
    for d, j in visit:
        for q in pairs:
            y0, r2, g, b = next(chain)
            y_refs[d][rows(j), lanes(q)] = y0 + _mdot(r2, s[q][d], NT, RW_PASSES)
            s[q][d] = _mdot(s[q][d], g, NN, RW_PASSES) + b
    for q in range(RW_PAIRS_PER_STEP):
        s_ref[q, 0] = s[q][0]
        s_ref[q, 1] = s[q][1]


def rwkv_scan(r, k, v, lw, a, k_k, k_a, n_ctx):
    t, d = r.shape
    L = RWKV_CHUNK
    blk = RW_CHUNKS_PER_STEP * L
    w = RW_PAIRS_PER_STEP * LANES
    assert 2 * L == LANES and t % blk == 0 and n_ctx % blk == 0 and d % w == 0
    nc, nc0 = t // blk, n_ctx // blk
    nh = d // w
    fwd = pl.BlockSpec((blk, w), lambda h, c: (c, h))
    bwd = pl.BlockSpec((blk, w), lambda h, c: (_bwd_chunk(c, nc0, nc), h))
    bwd2 = pl.BlockSpec((blk, w), lambda h, c: (_bwd_chunk(c, nc0, nc), nh + h))
    par = pl.BlockSpec((1, w), lambda h, c: (0, h))
    return pl.pallas_call(
        _rwkv_body,
        grid=(nh, nc),
        in_specs=[fwd] * 5 + [bwd2, bwd, bwd, bwd, bwd2, par, par],
        out_specs=[fwd, bwd],
        out_shape=[jax.ShapeDtypeStruct((t, d), F32)] * 2,
        scratch_shapes=[pltpu.VMEM((RW_PAIRS_PER_STEP, 2, LANES, LANES), F32)],
        compiler_params=pltpu.CompilerParams(
            dimension_semantics=("arbitrary", "arbitrary"), vmem_limit_bytes=VMEM_LIMIT),
        name="rwkv7_scan",
    )(lw, r, k, v, a, lw, r, k, v, a, k_k.reshape(1, d), k_a.reshape(1, d))


MIX_TM = 256


def _mix_body(cur_ref, prev_ref, next_ref, mu_ref, *o_refs, nct, nt):
    i = pl.program_id(0)
    is_ctx = i < nct
    cur = cur_ref[...]
    tm, d = cur.shape
    qd = d // 4
    above = jnp.where(is_ctx | (i == nct), 0.0, prev_ref[...])
    below = jnp.where(is_ctx | (i == nt - 1), 0.0, next_ref[...])
    ext = jnp.concatenate([above, cur, below], axis=0)
    n = ext.shape[0]
    colid = lax.broadcasted_iota(jnp.int32, (n, 2 * qd), 0) % GRID_W
    left = jnp.where(is_ctx | (colid != 0), pltpu.roll(ext[:, :2 * qd], 1, 0), 0.0)
    colid3 = lax.broadcasted_iota(jnp.int32, (n, 3 * qd), 0) % GRID_W
    right = jnp.where(is_ctx | (colid3 != GRID_W - 1), pltpu.roll(ext[:, qd:], n - 1, 0), 0.0)
    mid = slice(GRID_W, GRID_W + tm)
    sh = jnp.concatenate([
        left[mid, :qd],
        jnp.where(is_ctx, left[mid, qd:], right[mid, :qd]),
        jnp.where(is_ctx, right[mid, qd:2 * qd], ext[0:tm, 2 * qd:3 * qd]),
        jnp.where(is_ctx, right[mid, 2 * qd:], ext[2 * GRID_W:2 * GRID_W + tm, 3 * qd:])], axis=1)
    dx = sh - cur
    for b, o_ref in enumerate(o_refs):
        o_ref[...] = (cur + dx * mu_ref[b:b + 1, :]).astype(o_ref.dtype)


def rwkv_mix(h, mu, n_ctx):
    t, d = h.shape
    tm = MIX_TM
    assert t % tm == 0 and n_ctx % tm == 0 and tm % GRID_W == 0
    r = tm // GRID_W
    nt, nu = t // tm, t // GRID_W
    nb = mu.shape[0]
    return pl.pallas_call(
        functools.partial(_mix_body, nct=n_ctx // tm, nt=nt),
        grid=(nt,),
        in_specs=[pl.BlockSpec((tm, d), lambda i: (i, 0)),
                  pl.BlockSpec((GRID_W, d), lambda i: (jnp.maximum(i * r - 1, 0), 0)),
                  pl.BlockSpec((GRID_W, d), lambda i: (jnp.minimum((i + 1) * r, nu - 1), 0)),
                  pl.BlockSpec((nb, d), lambda i: (0, 0))],
        out_specs=[pl.BlockSpec((tm, d), lambda i: (i, 0))] * nb,
        out_shape=[jax.ShapeDtypeStruct((t, d), BF16)] * nb,
        compiler_params=pltpu.CompilerParams(dimension_semantics=("arbitrary",), vmem_limit_bytes=VMEM_LIMIT),
        name="rwkv_mix",
    )(h, h, h, mu)


def _rwkv_out_body(yf_ref, yb_ref, r_ref, k_ref, v_ref, g_ref, af_ref, ab_ref, p_ref, e_ref, et_ref, w_ref, o_ref):
    e, et = e_ref[...], et_ref[...]
    head_sum = lambda x: _dot_01_rhs(_dot_01_rhs(x, e), et)
    k_a, r_k, gain, bias = (p_ref[n:n + 1, :] for n in range(4))
    y = yf_ref[...] + yb_ref[...]
    yc = y - head_sum(y) * (1.0 / RWKV_N)
    var = head_sum(yc * yc) * (1.0 / RWKV_N)
    yn = yc * lax.rsqrt(var + RWKV_GN_EPS) * gain + bias
    kbar = k_ref[...] * (1.0 + (0.5 * (af_ref[...] + ab_ref[...]) - 1.0) * k_a)
    bonus = head_sum(r_ref[...] * kbar * r_k) * v_ref[...]
    z = (yn + bonus) * g_ref[...]
    o_ref[...] = _d(z.astype(BF16), w_ref[...], NN)


def rwkv_out(yf, yb, r, k, v, g, a, k_a, r_k, lnx_g, lnx_b, w_out):
    t, d = yf.shape
    tm = LN_TM
    nh = d // RWKV_N
    row = pl.BlockSpec((tm, d), lambda i: (i, 0))
    const = lambda shape: pl.BlockSpec(shape, lambda i: (0, 0))
    e = (jnp.arange(d)[:, None] // RWKV_N == jnp.arange(LANES)[None, :]).astype(BF16)
    params = jnp.stack([k_a, r_k, lnx_g, lnx_b])
    return pl.pallas_call(
        _rwkv_out_body,
        grid=(t // tm,),
        in_specs=[row] * 6 + [row, pl.BlockSpec((tm, d), lambda i: (i, 1)),
                  const((4, d)), const((d, LANES)), const((LANES, d)), const((d, d))],
        out_specs=row,
        out_shape=jax.ShapeDtypeStruct((t, d), F32),
        compiler_params=pltpu.CompilerParams(dimension_semantics=("arbitrary",), vmem_limit_bytes=VMEM_LIMIT),
        name="rwkv_out",
    )(yf, yb, r, k, v, g, a, a, params, e, e.T, w_out.astype(BF16))


def _peer_stats(hx_ref, wk_ref, sc_ref, hxb_ref, n_ref, f0_ref, r1_ref, e1_ref):
    tm = hx_ref.shape[0]
    hxb_ref[...] = hx_ref[...].astype(F32).T.astype(BF16)
    rows_per = 4 * LANES

    def scores(c, carry):
        rows = pl.ds(pl.multiple_of(c * rows_per, rows_per), rows_per)
        sc_ref[rows, :] = _d(wk_ref[rows, :], hxb_ref[...], NN)
        return carry

    lax.fori_loop(0, wk_ref.shape[0] // rows_per, scores, 0)

    def block(tb, carry):
        lanes = pl.ds(pl.multiple_of(tb * LANES, LANES), LANES)
        neg = jnp.full((N_KEYS, LANES), -jnp.inf, F32)
        head_row = lax.broadcasted_iota(jnp.int32, (PEER_HEADS, LANES), 0)

        def extract(h, tops):
            tops = [list(t) for t in tops]
            rows = [pl.ds(pl.multiple_of((2 * h + p) * N_KEYS, N_KEYS), N_KEYS) for p in range(2)]
            cur = [sc_ref[rows[p], lanes] for p in range(2)]
            rank = jnp.full((N_KEYS, LANES), float(PEER_TOPK), F32)
            for a in range(PEER_TOPK):
                m = [jnp.max(c, axis=0, keepdims=True) for c in cur]
                hit = [c >= mm for c, mm in zip(cur, m)]
                cur = [jnp.where(ht, neg, c) for ht, c in zip(hit, cur)]
                rank = jnp.where(hit[1], float(a), rank)
                for p in range(2):
                    tops[p][a] = jnp.where(head_row == h, m[p], tops[p][a])
            r1_ref[h, :, lanes] = rank.astype(BF16)
            return tuple(tuple(t) for t in tops)

        zero = jnp.zeros((PEER_HEADS, LANES), F32)
        top0, top1 = lax.fori_loop(0, PEER_HEADS, extract, ((zero,) * PEER_TOPK,) * 2)
        cands = [top0[a] + top1[b]
                 for a in range(PEER_TOPK) for b in range(PEER_TOPK) if (a + 1) * (b + 1) <= PEER_TOPK]
        c_max = cands[0]
        z = jnp.zeros_like(c_max)
        tau = c_max
        for a in range(PEER_TOPK):
            tau = functools.reduce(jnp.maximum, cands)
            z = z + jnp.exp(tau - c_max)
            cands = [jnp.where(cd >= tau, -jnp.inf, cd) for cd in cands]
        inv_z = 1.0 / z

        def factors(h, carry):
            row_of = lambda x: jnp.max(jnp.where(head_row == h, x, -jnp.inf), axis=0, keepdims=True)
            s0 = sc_ref[pl.ds(pl.multiple_of(2 * h * N_KEYS, N_KEYS), N_KEYS), lanes]
            s1 = sc_ref[pl.ds(pl.multiple_of((2 * h + 1) * N_KEYS, N_KEYS), N_KEYS), lanes]
            tau_h = row_of(tau)
            n = jnp.zeros((N_KEYS, LANES), F32)
            for b in range(PEER_TOPK):
                n = jnp.where(s0 + row_of(top1[b]) >= tau_h, float(b + 1), n)
            n_ref[h, :, lanes] = n
            f0_ref[h, :, lanes] = jnp.exp(s0 - row_of(top0[0])) * row_of(inv_z)
            e1_ref[h, :, lanes] = jnp.exp(s1 - row_of(top1[0])).astype(BF16)
            return carry

        lax.fori_loop(0, PEER_HEADS, factors, 0)
        return carry

    lax.fori_loop(0, tm // LANES, block, 0)


PEER_I_GROUP = 4
PEER_STREAMS = 2


def _peer_body(hx_ref, wk_ref, *rest):
    u_refs, vt_refs = rest[:PEER_STREAMS], rest[PEER_STREAMS:2 * PEER_STREAMS]
    o_ref, sc_ref, hxb_ref, n_ref, f0_ref, r1_ref, e1_ref, w_ref, acc_ref = rest[2 * PEER_STREAMS:]
    e = pl.program_id(1)
    tm = hx_ref.shape[0]
    nsb = tm // PEER_SB

    @pl.when(e == 0)
    def _():
        _peer_stats(hx_ref, wk_ref, sc_ref, hxb_ref, n_ref, f0_ref, r1_ref, e1_ref)
        acc_ref[...] = jnp.zeros(acc_ref.shape, F32)

    i_rows = pl.ds(pl.multiple_of(e * PEER_I_BLOCK, PEER_I_BLOCK), PEER_I_BLOCK)

    def activations(sb):
        parts = []
        for u_ref in u_refs:
            act = _d(u_ref[...], hxb_ref[:, sb * PEER_SB:(sb + 1) * PEER_SB], NN)
            act = act.astype(BF16)
            parts.append(0.5 * act * (1.0 + lax.erf(act * (2.0 ** -0.5))))
        return parts

    def gates(sb, act):
        for hb in range(PEER_SB // LANES):
            lanes = slice(sb * PEER_SB + hb * LANES, sb * PEER_SB + (hb + 1) * LANES)
            sub = slice(hb * LANES, (hb + 1) * LANES)
            n8 = [n_ref[h, i_rows, lanes] for h in range(PEER_HEADS)]
            f8 = [f0_ref[h, i_rows, lanes] for h in range(PEER_HEADS)]
            bcast = lambda x, ii: jnp.broadcast_to(x[ii:ii + 1], (N_KEYS, LANES)).astype(BF16)
            for ig in range(0, PEER_I_BLOCK, PEER_I_GROUP):
                g = [jnp.zeros((N_KEYS, LANES), BF16) for _ in range(PEER_I_GROUP)]
                for h in range(PEER_HEADS):
                    r1 = r1_ref[h, :, lanes]
                    e1 = e1_ref[h, :, lanes]
                    for k in range(PEER_I_GROUP):
                        ii = ig + k
                        g[k] = g[k] + jnp.where(r1 < bcast(n8[h], ii), e1 * bcast(f8[h], ii), jnp.zeros_like(e1))
                for k in range(PEER_I_GROUP):
                    rows = slice((ig + k) * N_KEYS, (ig + k + 1) * N_KEYS)
                    part, off = divmod((ig + k) * N_KEYS, PEER_TE // PEER_STREAMS)
                    w_ref[sb, rows, sub] = g[k] * act[part][off:off + N_KEYS, sub]

    def accumulate(sb):
        cols = slice(sb * PEER_SB, (sb + 1) * PEER_SB)
        dr = acc_ref.shape[0] // PEER_STREAMS
        for k, vt_ref in enumerate(vt_refs):
            acc_ref[k * dr:(k + 1) * dr, cols] += _d(vt_ref[...], w_ref[sb], NN)

    act = activations(0)
    for sb in range(nsb):
        nxt = activations(sb + 1) if sb + 1 < nsb else None
        gates(sb, act)
        accumulate(sb)
        act = nxt

    @pl.when(e == pl.num_programs(1) - 1)
    def _():
        o_ref[...] = acc_ref[...].T


def _fold_body(k_ref, w_ref, o_ref):
    o_ref[...] = _mdot(k_ref[...], w_ref[...], NN, 6)


def peer_fold_keys(wq, keys):
    d = wq.shape[0]
    nhp, nk, dk = keys.shape
    wqt = wq.T.reshape(nhp, dk, d)
    return pl.pallas_call(
        _fold_body,
        grid=(nhp,),
        in_specs=[pl.BlockSpec((None, nk, dk), lambda i: (i, 0, 0)),
                  pl.BlockSpec((None, dk, d), lambda i: (i, 0, 0))],
        out_specs=pl.BlockSpec((nk, d), lambda i: (i, 0)),
        out_shape=jax.ShapeDtypeStruct((nhp * nk, d), F32),
        compiler_params=pltpu.CompilerParams(dimension_semantics=("arbitrary",), vmem_limit_bytes=VMEM_LIMIT),
        name="peer_fold_keys",
    )(keys, wqt)


def peer(hx, wk, u_bf, vt_bf):
    t, d = hx.shape
    tm = PEER_TM
    assert t % tm == 0
    ne = u_bf.shape[0] // PEER_TE
    h = PEER_HEADS
    ns = PEER_STREAMS
    return pl.pallas_call(
        _peer_body,
        grid=(t // tm, ne),
        in_specs=[pl.BlockSpec((tm, d), lambda i, e: (i, 0), pipeline_mode=pl.Buffered(1)),
                  pl.BlockSpec(wk.shape, lambda i, e: (0, 0), pipeline_mode=pl.Buffered(1)),
                  *[pl.BlockSpec((PEER_TE // ns, d), functools.partial(lambda i, e, k: (e * ns + k, 0), k=k))
                    for k in range(ns)],
                  *[pl.BlockSpec((None, d // ns, PEER_TE), functools.partial(lambda i, e, k: (e, k, 0), k=k))
                    for k in range(ns)]],
        out_specs=pl.BlockSpec((tm, d), lambda i, e: (i, 0)),
        out_shape=jax.ShapeDtypeStruct((t, d), F32),
        scratch_shapes=[pltpu.VMEM((2 * h * N_KEYS, tm), F32),
                        pltpu.VMEM((d, tm), BF16),
                        pltpu.VMEM((h, N_KEYS, tm), F32), pltpu.VMEM((h, N_KEYS, tm), F32),
                        pltpu.VMEM((h, N_KEYS, tm), BF16), pltpu.VMEM((h, N_KEYS, tm), BF16),
                        pltpu.VMEM((tm // PEER_SB, PEER_TE, PEER_SB), BF16), pltpu.VMEM((d, tm), F32)],
        compiler_params=pltpu.CompilerParams(
            dimension_semantics=("arbitrary", "arbitrary"), vmem_limit_bytes=VMEM_LIMIT),
        name="peer_dense",
    )(hx, wk.astype(BF16), *([u_bf] * ns), *([vt_bf] * ns))


LN_TM = 256


def _ln_mod_body(xs_ref, y_ref, mg_ref, mn_ref, lng_ref, lnb_ref, xo_ref, ho_ref, *, gate_col, mod_col, nct, nt):
    i = pl.program_id(0)
    d = xs_ref.shape[1]
    is_ctx = i < nct

    def pick(ref, col):
        return jnp.where(is_ctx, ref[1:2, col * d:(col + 1) * d], ref[0:1, col * d:(col + 1) * d])

    z = DN_ALPHA * xs_ref[...] + pick(mg_ref, gate_col) * y_ref[...]
    mu = jnp.mean(z, axis=-1, keepdims=True)
    zc = z - mu
    var = jnp.mean(zc * zc, axis=-1, keepdims=True)
    xn = zc * lax.rsqrt(var + LN_EPS) * lng_ref[...] + lnb_ref[...]
    xo_ref[...] = xn
    h = xn * (1.0 + pick(mn_ref, mod_col + 1)) + pick(mn_ref, mod_col)
    ho_ref[...] = jnp.where(i < nt, h, 0.0).astype(ho_ref.dtype)


def ln_mod(xs, y, mod_gate, gate_col, mod_next, mod_col, ln_g, ln_b, n_ctx, pad_to, h_dtype):
    t, d = xs.shape
    tm = LN_TM
    assert t % tm == 0 and n_ctx % tm == 0 and pad_to % tm == 0 and pad_to >= t
    nt = t // tm
    row = lambda i: (jnp.minimum(i, nt - 1), 0)
    full = lambda a: pl.BlockSpec(a.shape, lambda i: (0, 0))
    return pl.pallas_call(
        functools.partial(_ln_mod_body, gate_col=gate_col, mod_col=mod_col, nct=n_ctx // tm, nt=nt),
        grid=(pad_to // tm,),
        in_specs=[pl.BlockSpec((tm, d), row), pl.BlockSpec((tm, d), row), full(mod_gate), full(mod_next),
                  pl.BlockSpec((1, d), lambda i: (0, 0)), pl.BlockSpec((1, d), lambda i: (0, 0))],
        out_specs=[pl.BlockSpec((tm, d), row), pl.BlockSpec((tm, d), lambda i: (i, 0))],
        out_shape=[jax.ShapeDtypeStruct((t, d), F32), jax.ShapeDtypeStruct((pad_to, d), h_dtype)],
        compiler_params=pltpu.CompilerParams(dimension_semantics=("arbitrary",), vmem_limit_bytes=VMEM_LIMIT),
        name="ln_mod",
    )(xs, y, mod_gate, mod_next, ln_g.reshape(1, d), ln_b.reshape(1, d))


def _pad_cols(w, n):
    return jnp.pad(w, ((0, 0), (0, n - w.shape[1])))


def _mlstm_layer(h, n_ctx, w_in, b_in, conv_w, conv_b, hn_g, w_out):
    d = D_MODEL
    t = h.shape[0]
    qk_pre = matmul(h, w_in[:, :2 * d], b_in[:2 * d])
    v = matmul(h, w_in[:, 2 * d:3 * d], b_in[2 * d:3 * d], out_dtype=BF16)
    o = matmul(h, w_in[:, 3 * d:4 * d], b_in[3 * d:4 * d], act="sigmoid", out_dtype=BF16)
    g = matmul(h, _pad_cols(w_in[:, 4 * d:], LANES), jnp.pad(b_in[4 * d:], (0, LANES - 4 * MLSTM_HEADS)))
    g = g[:, :4 * MLSTM_HEADS].reshape(t, 4, MLSTM_HEADS)
    g = jnp.concatenate([g[:, :2], jax.nn.log_sigmoid(g[:, 2:])], axis=1)
    gh = jnp.transpose(g, (2, 0, 1))
    gcol = jnp.pad(gh, ((0, 0), (0, 0), (0, LANES - 4)))
    grow = jnp.pad(jnp.transpose(gh, (0, 2, 1)), ((0, 0), (0, 4), (0, 0)))
    qk = mlstm_conv(qk_pre, conv_w, conv_b, n_ctx)
    hf, hb = mlstm_scan(qk, v, gcol, grow, n_ctx)
    return mlstm_out(hf, hb, o, hn_g, w_out)


def _rwkv_layer(h, n_ctx, mu, w_rkv, w0, w1, w2, a0, a1, a2, g1, g2, k_k, k_a, r_k, lnx_g, lnx_b, w_out):
    d = D_MODEL
    xm = rwkv_mix(h, mu, n_ctx)
    r = matmul(xm[0], w_rkv[0])
    k = matmul(xm[1], w_rkv[1])
    v = matmul(xm[2], w_rkv[2])

    def lora_pair(x, w_in, w_mid, bias, act_mid, act_out):
        rank = w_in.shape[-1]
        w_a = _pad_cols(jnp.concatenate([w_in[0], w_in[1]], axis=1), LANES)
        zpad = jnp.zeros((rank, d), F32)
        w_b = jnp.concatenate([jnp.concatenate([w_mid[0], zpad], axis=1),
                               jnp.concatenate([zpad, w_mid[1]], axis=1)], axis=0)
        w_b = jnp.pad(w_b, ((0, LANES - 2 * rank), (0, 0)))
        mid = matmul(x, w_a, act=act_mid, out_dtype=BF16)
        return matmul(mid, w_b, jnp.concatenate([bias[0], bias[1]]), act=act_out)

    lw = lora_pair(xm[3], w1, w2, w0, "tanh", "logdecay")
    a = lora_pair(xm[4], a1, a2, a0, None, "sigmoid")
    gpad = 2 * LANES
    gg = matmul(xm[5], _pad_cols(g1, gpad), act="sigmoid", out_dtype=BF16)
    g = matmul(gg, jnp.pad(g2, ((0, gpad - g1.shape[1]), (0, 0))))
    yf, yb = rwkv_scan(r, k, v, lw, a, k_k, k_a, n_ctx)
    return rwkv_out(yf, yb, r, k, v, g, a, k_a, r_k, lnx_g, lnx_b, w_out)


def _forward(x, c, ctx, c_ctx, ada_w, ada_b, ln_g, ln_b,
             ml_w_in, ml_b_in, ml_conv_w, ml_conv_b, ml_hn_g, ml_w_out,
             rw_mu, rw_w_rkv, rw_w0, rw_w1, rw_w2, rw_a0, rw_a1, rw_a2, rw_g1, rw_g2,
             rw_k_k, rw_k_a, rw_r_k, rw_lnx_g, rw_lnx_b, rw_w_out,
             pk_wq, pk_keys, pk_u, pk_v):
    d = D_MODEL
    n_ctx = ctx.shape[1]
    xs = jnp.concatenate([ctx[0], x[0]], axis=0)
    t = xs.shape[0]
    t_pad = -(-t // PEER_TM) * PEER_TM
    s_in = jnp.zeros((8, d), F32).at[0].set(jax.nn.silu(c[0])).at[1].set(jax.nn.silu(c_ctx))
    depth = ada_w.shape[0]
    mods = [matmul(s_in, ada_w[i], ada_b[i], passes=3) for i in range(depth)]
    is_ctx = (jnp.arange(t) < n_ctx)[:, None]
    m0 = [jnp.where(is_ctx, mods[0][1, n * d:(n + 1) * d], mods[0][0, n * d:(n + 1) * d]) for n in range(2)]
    mixer_dtype = lambda i: BF16 if i % 2 == 0 else F32
    h = (xs * (1.0 + m0[1]) + m0[0]).astype(mixer_dtype(0))
    for i in range(depth):
        j = i // 2
        if i % 2 == 0:
            y = _mlstm_layer(h, n_ctx, ml_w_in[j], ml_b_in[j], ml_conv_w[j], ml_conv_b[j],
                             ml_hn_g[j], ml_w_out[j])
        else:
            y = _rwkv_layer(h, n_ctx, rw_mu[j], rw_w_rkv[j], rw_w0[j], rw_w1[j], rw_w2[j],
                            rw_a0[j], rw_a1[j], rw_a2[j], rw_g1[j], rw_g2[j], rw_k_k[j],
                            rw_k_a[j], rw_r_k[j], rw_lnx_g[j], rw_lnx_b[j], rw_w_out[j])
        xs, h = ln_mod(xs, y, mods[i], 2, mods[i], 3, ln_g[i, 0], ln_b[i, 0], n_ctx, t_pad, BF16)
        wk = peer_fold_keys(pk_wq[i], pk_keys[i].reshape(2 * PEER_HEADS, N_KEYS, PEER_DQ // 2))
        vt = jnp.swapaxes(pk_v[i].astype(BF16).reshape(-1, PEER_TE, d), 1, 2)
        y = peer(h, wk, pk_u[i].astype(BF16), vt)
        xs, h = ln_mod(xs, y, mods[i], 5, mods[min(i + 1, depth - 1)], 0, ln_g[i, 1], ln_b[i, 1], n_ctx, t,
                       mixer_dtype(i + 1))
    return xs[n_ctx:][None]


def kernel(x, c, ctx, c_ctx, ada_w, ada_b, ln_g, ln_b, ml_w_in, ml_b_in, ml_conv_w, ml_conv_b, ml_hn_g, ml_w_out, rw_mu, rw_w_rkv, rw_w0, rw_w1, rw_w2, rw_a0, rw_a1, rw_a2, rw_g1, rw_g2, rw_k_k, rw_k_a, rw_r_k, rw_lnx_g, rw_lnx_b, rw_w_out, pk_wq, pk_keys, pk_u, pk_v):
    return _forward(x, c, ctx, c_ctx, ada_w, ada_b, ln_g, ln_b,
                    ml_w_in, ml_b_in, ml_conv_w, ml_conv_b, ml_hn_g, ml_w_out,
                    rw_mu, rw_w_rkv, rw_w0, rw_w1, rw_w2, rw_a0, rw_a1, rw_a2, rw_g1, rw_g2,
                    rw_k_k, rw_k_a, rw_r_k, rw_lnx_g, rw_lnx_b, rw_w_out,
                    pk_wq, pk_keys, pk_u, pk_v)
```

```python
import functools

import jax
import jax.numpy as jnp
from jax import lax
from jax.experimental import pallas as pl
from jax.experimental.pallas import tpu as pltpu

F32 = jnp.float32
BF16 = jnp.bfloat16

D_MODEL = 1024
DEPTH = 4
GRID_W = 64
DN_ALPHA = (2.0 * DEPTH) ** 0.25
LN_EPS = 1e-5

MLSTM_HEADS = 4
MLSTM_DH = D_MODEL // MLSTM_HEADS
MLSTM_CHUNK = 128
M_INIT = -1e30

RWKV_N = 64
RWKV_HEADS = D_MODEL // RWKV_N
RWKV_CHUNK = 64
RWKV_GN_EPS = 64e-5

N_KEYS = 128
PEER_HEADS = 8
PEER_DQ = 256
PEER_TOPK = 16
PEER_I_BLOCK = 16
PEER_TE = PEER_I_BLOCK * N_KEYS
PEER_TM = 768
PEER_SB = 256

LANES = 128
VMEM_LIMIT = 62 * 1024 * 1024

NN = ((1,), (0,))
NT = ((1,), (1,))


def _split(x, n):
    parts = []
    r = x.astype(F32)
    for i in range(n):
        p = r.astype(BF16)
        parts.append(p)
        if i + 1 < n:
            r = r - p.astype(F32)
    return parts


def _d(a, b, dims):
    return lax.dot_general(a, b, (dims, ((), ())), preferred_element_type=F32)


def _mdot(a, b, dims, passes):
    if passes == 1:
        return _d(a.astype(BF16), b.astype(BF16), dims)
    if passes == 3:
        a0, a1 = _split(a, 2)
        b0, b1 = _split(b, 2)
        return (_d(a0, b1, dims) + _d(a1, b0, dims)) + _d(a0, b0, dims)
    a0, a1, a2 = _split(a, 3)
    b0, b1, b2 = _split(b, 3)
    lo = (_d(a0, b2, dims) + _d(a2, b0, dims)) + _d(a1, b1, dims)
    mid = _d(a0, b1, dims) + _d(a1, b0, dims)
    return (lo + mid) + _d(a0, b0, dims)


def _dot_01_lhs(m01, x):
    mb = m01.astype(BF16)
    x0, x1, x2 = _split(x, 3)
    return (_d(mb, x2, NN) + _d(mb, x1, NN)) + _d(mb, x0, NN)


def _dot_01_rhs(x, m01):
    mb = m01.astype(BF16)
    x0, x1, x2 = _split(x, 3)
    return (_d(x2, mb, NN) + _d(x1, mb, NN)) + _d(x0, mb, NN)


def _pick(n, cands):
    for c in cands:
        if n % c == 0:
            return c
    raise ValueError(f"no tile for {n}")


_ACTS = {None: lambda x: x, "sigmoid": jax.nn.sigmoid, "tanh": jnp.tanh,
         "logdecay": lambda x: -(2.718281828459045 ** -0.5) * jax.nn.sigmoid(x)}


def _mm_body(x_ref, w_ref, b_ref, o_ref, *, passes, act):
    o_ref[...] = _ACTS[act](_mdot(x_ref[...], w_ref[...], NN, passes) + b_ref[...]).astype(o_ref.dtype)


def matmul(x, w, b=None, *, passes=1, act=None, out_dtype=F32):
    m, k = x.shape
    n = w.shape[1]
    assert n % LANES == 0 and w.shape[0] == k
    tm = m if m <= 1024 else _pick(m, (640, 512, 384, 256, 128))
    tn = _pick(n, (1024, 768, 640, 512, 384, 256, 128))
    if b is None:
        b = jnp.zeros((n,), F32)
    if passes == 1:
        w = w.astype(BF16)
    return pl.pallas_call(
        functools.partial(_mm_body, passes=passes, act=act),
        grid=(m // tm, n // tn),
        in_specs=[pl.BlockSpec((tm, k), lambda i, j: (i, 0)),
                  pl.BlockSpec((k, tn), lambda i, j: (0, j)),
                  pl.BlockSpec((1, tn), lambda i, j: (0, j))],
        out_specs=pl.BlockSpec((tm, tn), lambda i, j: (i, j)),
        out_shape=jax.ShapeDtypeStruct((m, n), out_dtype),
        compiler_params=pltpu.CompilerParams(
            dimension_semantics=("arbitrary", "arbitrary"), vmem_limit_bytes=VMEM_LIMIT),
        name="proj_matmul",
    )(x, w, b.reshape(1, n).astype(F32))


CONV_TM = 256
CONV_CB = 512


def _conv_body(cur_ref, prev_ref, next_ref, w_ref, b_ref, o_ref, *, nct, nt, q_blocks, q_scale):
    i = pl.program_id(0)
    j = pl.program_id(1)
    is_ctx = i < nct
    tm = cur_ref.shape[0]
    above = jnp.where(is_ctx | (i == nct), 0.0, prev_ref[...])
    below = jnp.where(is_ctx | (i == nt - 1), 0.0, next_ref[...])
    ext = jnp.concatenate([above, cur_ref[...], below], axis=0)
    n = ext.shape[0]
    colid = lax.broadcasted_iota(jnp.int32, ext.shape, 0) % GRID_W
    left = jnp.where(is_ctx | (colid != 0), pltpu.roll(ext, 1, 0), 0.0)
    right = jnp.where(is_ctx | (colid != GRID_W - 1), pltpu.roll(ext, n - 1, 0), 0.0)
    w = w_ref[...]
    acc = jnp.zeros((tm, ext.shape[1]), F32) + b_ref[...]
    for di in range(3):
        rows = slice(di * GRID_W, di * GRID_W + tm)
        tap = left[rows] * w[3 * di:3 * di + 1] + ext[rows] * w[3 * di + 1:3 * di + 2] \
            + right[rows] * w[3 * di + 2:3 * di + 3]
        acc = acc + (tap if di == 1 else jnp.where(is_ctx, 0.0, tap))
    y = acc * jax.nn.sigmoid(acc)
    o_ref[...] = (y * jnp.where(j < q_blocks, q_scale, 1.0)).astype(o_ref.dtype)


def mlstm_conv(qk_pre, conv_w, conv_b, n_ctx):
    t, c = qk_pre.shape
    tm, cb = CONV_TM, CONV_CB
    assert t % tm == 0 and n_ctx % tm == 0 and tm % GRID_W == 0 and c % (2 * cb) == 0
    r = tm // GRID_W
    nt, nu = t // tm, t // GRID_W
    return pl.pallas_call(
        functools.partial(_conv_body, nct=n_ctx // tm, nt=nt, q_blocks=c // (2 * cb), q_scale=MLSTM_DH ** -0.5),
        grid=(nt, c // cb),
        in_specs=[pl.BlockSpec((tm, cb), lambda i, j: (i, j)),
                  pl.BlockSpec((GRID_W, cb), lambda i, j: (jnp.maximum(i * r - 1, 0), j)),
                  pl.BlockSpec((GRID_W, cb), lambda i, j: (jnp.minimum((i + 1) * r, nu - 1), j)),
                  pl.BlockSpec((9, cb), lambda i, j: (0, j)),
                  pl.BlockSpec((1, cb), lambda i, j: (0, j))],
        out_specs=pl.BlockSpec((tm, cb), lambda i, j: (i, j)),
        out_shape=jax.ShapeDtypeStruct((t, c), BF16),
        compiler_params=pltpu.CompilerParams(
            dimension_semantics=("arbitrary", "arbitrary"), vmem_limit_bytes=VMEM_LIMIT),
        name="mlstm_conv",
    )(qk_pre, qk_pre, qk_pre, conv_w.reshape(9, c), conv_b.reshape(1, c))


def _mlstm_out_body(hf_ref, hb_ref, o_ref, g_ref, w_ref, y_ref):
    h = hf_ref[...] + hb_ref[...]
    parts = []
    for a in range(MLSTM_HEADS):
        x = h[:, a * MLSTM_DH:(a + 1) * MLSTM_DH]
        mu = jnp.mean(x, axis=-1, keepdims=True)
        xc = x - mu
        var = jnp.mean(xc * xc, axis=-1, keepdims=True)
        parts.append(xc * lax.rsqrt(var + LN_EPS))
    hn = jnp.concatenate(parts, axis=1)
    z = o_ref[...].astype(F32) * hn * g_ref[...]
    y_ref[...] = _d(z.astype(BF16), w_ref[...], NN)


def mlstm_out(hf, hb, o, hn_g, w_out):
    t, d = hf.shape
    tm = LN_TM
    row = pl.BlockSpec((tm, d), lambda i: (i, 0))
    return pl.pallas_call(
        _mlstm_out_body,
        grid=(t // tm,),
        in_specs=[row, row, row, pl.BlockSpec((1, d), lambda i: (0, 0)), pl.BlockSpec((d, d), lambda i: (0, 0))],
        out_specs=row,
        out_shape=jax.ShapeDtypeStruct((t, d), F32),
        compiler_params=pltpu.CompilerParams(dimension_semantics=("arbitrary",), vmem_limit_bytes=VMEM_LIMIT),
        name="mlstm_out",
    )(hf, hb, o, hn_g.reshape(1, d), w_out.astype(BF16))


ML_HEADS_PER_STEP = 4


def _mlstm_body(qf, kf, vf, gcf, grf, qb, kb, vb, gcb, grb, hf_ref, hb_ref, c_ref, n_ref, m_ref):
    @pl.when(pl.program_id(1) == 0)
    def _():
        c_ref[...] = jnp.zeros(c_ref.shape, F32)
        n_ref[...] = jnp.zeros(n_ref.shape, F32)
        m_ref[...] = jnp.full(m_ref.shape, M_INIT, F32)

    dh = MLSTM_DH
    cs = [(a, d) for a in range(ML_HEADS_PER_STEP) for d in (0, 1)]
    each = lambda f, *ls: [f(*xs) for xs in zip(*ls)]
    cols = lambda a: slice(a * dh, (a + 1) * dh)
    qs, ks, vs, gcs, grs = (qf, qb), (kf, kb), (vf, vb), (gcf, gcb), (grf, grb)
    qb16 = [qs[d][:, cols(a)] for a, d in cs]
    kb16 = [ks[d][:, cols(a)] for a, d in cs]
    vb16 = [vs[d][:, cols(a)] for a, d in cs]
    gc = [gcs[d][a] for a, d in cs]
    gr = [grs[d][a] for a, d in cs]
    q, k = (each(lambda x: x.astype(F32), x16) for x16 in (qb16, kb16))
    L = q[0].shape[0]
    row = lax.broadcasted_iota(jnp.int32, (L, L), 0)
    col = lax.broadcasted_iota(jnp.int32, (L, L), 1)
    seen = (col <= row, col >= row)
    tri = [jnp.where(m, 1.0, 0.0) for m in seen]
    b_col = [_dot_01_lhs(tri[d], g)[:, 2 + d:3 + d] for (a, d), g in zip(cs, gc)]
    b_row = [_dot_01_rhs(g, tri[1 - d])[2 + d:3 + d, :] for (a, d), g in zip(cs, gr)]
    ig_col = [g[:, d:d + 1] for (a, d), g in zip(cs, gc)]
    ig_row = [g[d:d + 1, :] for (a, d), g in zip(cs, gr)]
    m_st = [m_ref[a, d, 0:1, 0:1] for a, d in cs]
    c_st = [c_ref[a, d] for a, d in cs]
    n_st = [n_ref[a, d] for a, d in cs]

    dlog = [jnp.where(seen[d], bc - br + ir, -jnp.inf) for (a, d), bc, br, ir in zip(cs, b_col, b_row, ig_row)]
    m_inter = each(jnp.add, b_col, m_st)
    m_t = each(lambda mi, dl: jnp.maximum(mi, jnp.max(dl, axis=1, keepdims=True)), m_inter, dlog)
    qk = each(lambda x, y: _d(x, y, NT), qb16, kb16)
    s = each(lambda x, dl, mt: x * jnp.exp(dl - mt), qk, dlog, m_t)
    dec = each(lambda mi, mt: jnp.exp(mi - mt), m_inter, m_t)
    sv = each(lambda x, y: _d(x.astype(BF16), y, NN), s, vb16)
    qc = each(lambda x, y: _d(x, y.astype(BF16), NN), qb16, c_st)
    num = each(lambda x, dc, y: x + dc * y, sv, dec, qc)
    den = each(lambda x, dc, qq, nn: jnp.sum(x, axis=1, keepdims=True) + dc * jnp.sum(qq * nn, axis=1, keepdims=True),
               s, dec, q, n_st)
    h = each(lambda nu, de, mt: nu / jnp.maximum(jnp.abs(de), jnp.exp(-mt)), num, den, m_t)
    h_refs = (hf_ref, hb_ref)
    for (a, d), x in zip(cs, h):
        h_refs[d][:, cols(a)] = x

    b_last = [bc[0:1, :] if d else bc[L - 1:L, :] for (a, d), bc in zip(cs, b_col)]
    w_c = each(lambda bl, bc, ic: bl - bc + ic, b_last, b_col, ig_col)
    m_new = each(lambda bl, ms, w: jnp.maximum(bl + ms, jnp.max(w, axis=0, keepdims=True)), b_last, m_st, w_c)
    a_c = each(lambda w, mn: jnp.exp(w - mn), w_c, m_new)
    g_prev = each(lambda bl, ms, mn: jnp.exp(bl + ms - mn), b_last, m_st, m_new)
    ak = each(jnp.multiply, a_c, k)
    kv = each(lambda x, y: _d(x.T.astype(BF16), y, NN), ak, vb16)
    for i, (a, d) in enumerate(cs):
        c_ref[a, d] = g_prev[i] * c_st[i] + kv[i]
        n_ref[a, d] = g_prev[i] * n_st[i] + jnp.sum(ak[i], axis=0, keepdims=True)
        m_ref[a, d] = jnp.broadcast_to(m_new[i], m_ref.shape[2:])


def _bwd_chunk(c, nc0, nc):
    return jnp.where(c < nc0, nc0 - 1 - c, nc - 1 - (c - nc0))


def mlstm_scan(qk, v, gcol, grow, n_ctx):
    t = v.shape[0]
    L, dh, hs = MLSTM_CHUNK, MLSTM_DH, ML_HEADS_PER_STEP
    ng = MLSTM_HEADS // hs
    nc, nc0 = t // L, n_ctx // L
    idf = lambda c: c
    idb = lambda c: _bwd_chunk(c, nc0, nc)
    qkv = lambda f, off: pl.BlockSpec((L, hs * dh), lambda h, c: (f(c), off + h))
    gc_spec = lambda f: pl.BlockSpec((hs, L, LANES), lambda h, c: (h, f(c), 0))
    gr_spec = lambda f: pl.BlockSpec((hs, 8, L), lambda h, c: (h, 0, f(c)))
    return pl.pallas_call(
        _mlstm_body,
        grid=(ng, nc),
        in_specs=[qkv(idf, 0), qkv(idf, ng), qkv(idf, 0), gc_spec(idf), gr_spec(idf),
                  qkv(idb, 0), qkv(idb, ng), qkv(idb, 0), gc_spec(idb), gr_spec(idb)],
        out_specs=[qkv(idf, 0), qkv(idb, 0)],
        out_shape=[jax.ShapeDtypeStruct((t, D_MODEL), F32)] * 2,
        scratch_shapes=[pltpu.VMEM((hs, 2, dh, dh), F32), pltpu.VMEM((hs, 2, 1, dh), F32),
                        pltpu.VMEM((hs, 2, 8, LANES), F32)],
        compiler_params=pltpu.CompilerParams(
            dimension_semantics=("arbitrary", "arbitrary"), vmem_limit_bytes=VMEM_LIMIT),
        name="mlstm_scan",
    )(qk, qk, v, gcol, grow, qk, qk, v, gcol, grow)


RW_PASSES = 1
RW_CHUNKS_PER_STEP = 4


def _stack2(x, lane_head):
    return jnp.concatenate([jnp.where(lane_head == 0, x, 0.0), jnp.where(lane_head == 1, x, 0.0)], axis=0)


def _unstack2(x):
    L = x.shape[0] // 2
    return x[:L] + x[L:]


def _rwkv_chunks(chains, k_k, k_a):
    L = chains[0][0].shape[0]
    n2 = 2 * L
    p = RW_PASSES
    ds = [c[5] for c in chains]
    each = lambda f, *ls: [f(*xs) for xs in zip(*ls)]
    dot = lambda dims: (lambda a, b: _mdot(a, b, dims, p))
    lw, r, k_raw, v, a = ([c[i] for c in chains] for i in range(5))

    lane_r = lax.broadcasted_iota(jnp.int32, (LANES, LANES), 0) // RWKV_N
    lane_c = lax.broadcasted_iota(jnp.int32, (LANES, LANES), 1) // RWKV_N
    head_ones = jnp.where(lane_r == lane_c, 1.0, 0.0)
    kkr = each(jnp.multiply, k_raw, k_k)
    ss = each(lambda x: _dot_01_rhs(x * x, head_ones), kkr)
    kap = each(lambda x, q: x / jnp.maximum(jnp.sqrt(q), 1e-12), kkr, ss)
    alp = each(jnp.multiply, kap, a)
    k = each(lambda x, y, z: x * (1.0 + (y - 1.0) * z), k_raw, a, k_a)

    row = lax.broadcasted_iota(jnp.int32, (L, L), 0)
    col = lax.broadcasted_iota(jnp.int32, (L, L), 1)
    tris = (jnp.where(col <= row, 1.0, 0.0), jnp.where(col >= row, 1.0, 0.0))
    lp = [_dot_01_lhs(tris[d], x) for d, x in zip(ds, lw)]
    lp_end = [x[0:1, :] if d else x[L - 1:L, :] for d, x in zip(ds, lp)]
    e_neg = each(lambda x: jnp.exp(-x), lp)
    e_end = each(lambda x, xe: jnp.exp(xe - x), lp, lp_end)
    kap_t = each(lambda x, y, z: x * jnp.exp(y - z), kap, lp, lw)
    r_t = each(lambda x, y: x * jnp.exp(y), r, lp)
    k_h = each(jnp.multiply, k, e_neg)
    a_h = each(jnp.multiply, alp, e_neg)
    k_e = each(jnp.multiply, k, e_end)
    a_e = each(jnp.multiply, alp, e_end)

    lane_head = lax.broadcasted_iota(jnp.int32, (L, LANES), 1) // RWKV_N
    st = lambda x: _stack2(x, lane_head)
    kap_s, r_s, v_s, k_s, a_s = (each(st, x) for x in (kap_t, r_t, v, k_e, a_e))
    rhs_k = each(lambda x: jnp.concatenate([x, x], axis=0), k_h)
    rhs_a = each(lambda x: jnp.concatenate([x, x], axis=0), a_h)

    row2 = lax.broadcasted_iota(jnp.int32, (n2, n2), 0)
    col2 = lax.broadcasted_iota(jnp.int32, (n2, n2), 1)
    same_head = (row2 // L) == (col2 // L)
    strict = (same_head & (col2 < row2), same_head & (col2 > row2))
    incl = (same_head & (col2 <= row2), same_head & (col2 >= row2))
    zero = jnp.zeros((n2, n2), F32)
    masked = lambda masks: (lambda d, x: jnp.where(masks[d], x, zero))
    n_ka = each(masked(strict), ds, each(dot(NT), kap_s, rhs_a))
    m_kk = each(masked(strict), ds, each(dot(NT), kap_s, rhs_k))
    a_rk = each(masked(incl), ds, each(dot(NT), r_s, rhs_k))
    a_ra = each(masked(incl), ds, each(dot(NT), r_s, rhs_a))

    b16 = (row2 // 16) == (col2 // 16)
    b32 = (row2 // 32) == (col2 // 32)
    eye = jnp.where(row2 == col2, 1.0, 0.0)
    n16 = each(lambda x: jnp.where(b16, x, zero), n_ka)
    n_2 = each(dot(NN), n16, n16)
    n_4 = each(dot(NN), n_2, n_2)
    n_8 = each(dot(NN), n_4, n_4)
    inv = each(lambda x: eye - x, n16)
    for pw in (n_2, n_4, n_8):
        inv = each(jnp.add, inv, each(dot(NN), inv, pw))
    for sel in (lambda x: jnp.where(b32 & ~b16, x, zero), lambda x: jnp.where(b32, zero, x)):
        t1 = each(dot(NN), inv, each(sel, n_ka))
        inv = each(jnp.subtract, inv, each(dot(NN), t1, inv))

    mv = each(dot(NN), m_kk, v_s)
    w1u0 = each(dot(NN), inv, each(lambda x, y: jnp.concatenate([x, y], axis=1), kap_s, mv))
    ar = each(dot(NN), a_ra, w1u0)
    av = each(dot(NN), a_rk, v_s)
    r2 = each(lambda x, y: _unstack2(x - y[:, :LANES]), r_s, ar)
    y0 = each(lambda x, y: _unstack2(x - y[:, LANES:]), av, ar)
    w1_t = each(lambda x: x[:, :LANES].T, w1u0)
    u0_t = each(lambda x: x[:, LANES:].T, w1u0)
    v_t = each(lambda x: x.T, v_s)
    wa = each(dot(NN), w1_t, a_s)
    vk = each(dot(NN), v_t, k_s)
    ua = each(dot(NN), u0_t, a_s)
    g = each(lambda xe, x: jnp.where(row2 == col2, jnp.exp(xe), zero) - x, lp_end, wa)
    b = each(jnp.subtract, vk, ua)
    return list(zip(y0, r2, g, b))


RW_PAIRS_PER_STEP = 2


def _rwkv_body(lwf, rf, kf, vf, af, lwb, rb, kb, vb, ab, kk_ref, ka_ref, yf_ref, yb_ref, s_ref):
    @pl.when(pl.program_id(1) == 0)
    def _():
        s_ref[...] = jnp.zeros(s_ref.shape, F32)

    L = RWKV_CHUNK
    n = RW_CHUNKS_PER_STEP
    rows = lambda j: slice(j * L, (j + 1) * L)
    lanes = lambda q: slice(q * LANES, (q + 1) * LANES)
    refs = ((lwf, rf, kf, vf, af), (lwb, rb, kb, vb, ab))
    y_refs = (yf_ref, yb_ref)
    visit = [(d, j if d == 0 else n - 1 - j) for j in range(n) for d in (0, 1)]
    pairs = range(RW_PAIRS_PER_STEP)
    pre = _rwkv_chunks([tuple(ref[rows(j), lanes(q)] for ref in refs[d]) + (d,) for d, j in visit for q in pairs],
                       [kk_ref[:, lanes(q)] for _ in visit for q in pairs],
                       [ka_ref[:, lanes(q)] for _ in visit for q in pairs])
    s = [[s_ref[q, 0], s_ref[q, 1]] for q in pairs]
    chain = iter(pre)
    for d, j in visit:
        for q in pairs:
            y0, r2, g, b = next(chain)
            y_refs[d][rows(j), lanes(q)] = y0 + _mdot(r2, s[q][d], NT, RW_PASSES)
            s[q][d] = _mdot(s[q][d], g, NN, RW_PASSES) + b
    for q in range(RW_PAIRS_PER_STEP):
        s_ref[q, 0] = s[q][0]
        s_ref[q, 1] = s[q][1]


def rwkv_scan(r, k, v, lw, a, k_k, k_a, n_ctx):
    t, d = r.shape
    L = RWKV_CHUNK
    blk = RW_CHUNKS_PER_STEP * L
    w = RW_PAIRS_PER_STEP * LANES
    assert 2 * L == LANES and t % blk == 0 and n_ctx % blk == 0 and d % w == 0
    nc, nc0 = t // blk, n_ctx // blk
    nh = d // w
    fwd = pl.BlockSpec((blk, w), lambda h, c: (c, h))
    bwd = pl.BlockSpec((blk, w), lambda h, c: (_bwd_chunk(c, nc0, nc), h))
    bwd2 = pl.BlockSpec((blk, w), lambda h, c: (_bwd_chunk(c, nc0, nc), nh + h))
    par = pl.BlockSpec((1, w), lambda h, c: (0, h))
    return pl.pallas_call(
        _rwkv_body,
        grid=(nh, nc),
        in_specs=[fwd] * 5 + [bwd2, bwd, bwd, bwd, bwd2, par, par],
        out_specs=[fwd, bwd],
        out_shape=[jax.ShapeDtypeStruct((t, d), F32)] * 2,
        scratch_shapes=[pltpu.VMEM((RW_PAIRS_PER_STEP, 2, LANES, LANES), F32)],
        compiler_params=pltpu.CompilerParams(
            dimension_semantics=("arbitrary", "arbitrary"), vmem_limit_bytes=VMEM_LIMIT),
        name="rwkv7_scan",
    )(lw, r, k, v, a, lw, r, k, v, a, k_k.reshape(1, d), k_a.reshape(1, d))


MIX_TM = 256


def _mix_body(cur_ref, prev_ref, next_ref, mu_ref, *o_refs, nct, nt):
    i = pl.program_id(0)
    is_ctx = i < nct
    cur = cur_ref[...]
    tm, d = cur.shape
    qd = d // 4
    above = jnp.where(is_ctx | (i == nct), 0.0, prev_ref[...])
    below = jnp.where(is_ctx | (i == nt - 1), 0.0, next_ref[...])
    ext = jnp.concatenate([above, cur, below], axis=0)
    n = ext.shape[0]
    colid = lax.broadcasted_iota(jnp.int32, (n, 2 * qd), 0) % GRID_W
    left = jnp.where(is_ctx | (colid != 0), pltpu.roll(ext[:, :2 * qd], 1, 0), 0.0)
    colid3 = lax.broadcasted_iota(jnp.int32, (n, 3 * qd), 0) % GRID_W
    right = jnp.where(is_ctx | (colid3 != GRID_W - 1), pltpu.roll(ext[:, qd:], n - 1, 0), 0.0)
    mid = slice(GRID_W, GRID_W + tm)
    sh = jnp.concatenate([
        left[mid, :qd],
        jnp.where(is_ctx, left[mid, qd:], right[mid, :qd]),
        jnp.where(is_ctx, right[mid, qd:2 * qd], ext[0:tm, 2 * qd:3 * qd]),
        jnp.where(is_ctx, right[mid, 2 * qd:], ext[2 * GRID_W:2 * GRID_W + tm, 3 * qd:])], axis=1)
    dx = sh - cur
    for b, o_ref in enumerate(o_refs):
        o_ref[...] = (cur + dx * mu_ref[b:b + 1, :]).astype(o_ref.dtype)


def rwkv_mix(h, mu, n_ctx):
    t, d = h.shape
    tm = MIX_TM
    assert t % tm == 0 and n_ctx % tm == 0 and tm % GRID_W == 0
    r = tm // GRID_W
    nt, nu = t // tm, t // GRID_W
    nb = mu.shape[0]
    return pl.pallas_call(
        functools.partial(_mix_body, nct=n_ctx // tm, nt=nt),
        grid=(nt,),
        in_specs=[pl.BlockSpec((tm, d), lambda i: (i, 0)),
                  pl.BlockSpec((GRID_W, d), lambda i: (jnp.maximum(i * r - 1, 0), 0)),
                  pl.BlockSpec((GRID_W, d), lambda i: (jnp.minimum((i + 1) * r, nu - 1), 0)),
                  pl.BlockSpec((nb, d), lambda i: (0, 0))],
        out_specs=[pl.BlockSpec((tm, d), lambda i: (i, 0))] * nb,
        out_shape=[jax.ShapeDtypeStruct((t, d), BF16)] * nb,
        compiler_params=pltpu.CompilerParams(dimension_semantics=("arbitrary",), vmem_limit_bytes=VMEM_LIMIT),
        name="rwkv_mix",
    )(h, h, h, mu)


def _rwkv_out_body(yf_ref, yb_ref, r_ref, k_ref, v_ref, g_ref, af_ref, ab_ref, p_ref, e_ref, et_ref, w_ref, o_ref):
    e, et = e_ref[...], et_ref[...]
    head_sum = lambda x: _dot_01_rhs(_dot_01_rhs(x, e), et)
    k_a, r_k, gain, bias = (p_ref[n:n + 1, :] for n in range(4))
    y = yf_ref[...] + yb_ref[...]
    yc = y - head_sum(y) * (1.0 / RWKV_N)
    var = head_sum(yc * yc) * (1.0 / RWKV_N)
    yn = yc * lax.rsqrt(var + RWKV_GN_EPS) * gain + bias
    kbar = k_ref[...] * (1.0 + (0.5 * (af_ref[...] + ab_ref[...]) - 1.0) * k_a)
    bonus = head_sum(r_ref[...] * kbar * r_k) * v_ref[...]
    z = (yn + bonus) * g_ref[...]
    o_ref[...] = _d(z.astype(BF16), w_ref[...], NN)


def rwkv_out(yf, yb, r, k, v, g, a, k_a, r_k, lnx_g, lnx_b, w_out):
    t, d = yf.shape
    tm = LN_TM
    nh = d // RWKV_N
    row = pl.BlockSpec((tm, d), lambda i: (i, 0))
    const = lambda shape: pl.BlockSpec(shape, lambda i: (0, 0))
    e = (jnp.arange(d)[:, None] // RWKV_N == jnp.arange(LANES)[None, :]).astype(BF16)
    params = jnp.stack([k_a, r_k, lnx_g, lnx_b])
    return pl.pallas_call(
        _rwkv_out_body,
        grid=(t // tm,),
        in_specs=[row] * 6 + [row, pl.BlockSpec((tm, d), lambda i: (i, 1)),
                  const((4, d)), const((d, LANES)), const((LANES, d)), const((d, d))],
        out_specs=row,
        out_shape=jax.ShapeDtypeStruct((t, d), F32),
        compiler_params=pltpu.CompilerParams(dimension_semantics=("arbitrary",), vmem_limit_bytes=VMEM_LIMIT),
        name="rwkv_out",
    )(yf, yb, r, k, v, g, a, a, params, e, e.T, w_out.astype(BF16))


def _peer_stats(hx_ref, wk_ref, sc_ref, hxb_ref, n_ref, f0_ref, r1_ref, e1_ref):
    tm = hx_ref.shape[0]
    hxb_ref[...] = hx_ref[...].astype(F32).T.astype(BF16)
    rows_per = 4 * LANES

    def scores(c, carry):
        rows = pl.ds(pl.multiple_of(c * rows_per, rows_per), rows_per)
        sc_ref[rows, :] = _d(wk_ref[rows, :], hxb_ref[...], NN)
        return carry

    lax.fori_loop(0, wk_ref.shape[0] // rows_per, scores, 0)

    def block(tb, carry):
        lanes = pl.ds(pl.multiple_of(tb * LANES, LANES), LANES)
        neg = jnp.full((N_KEYS, LANES), -jnp.inf, F32)
        head_row = lax.broadcasted_iota(jnp.int32, (PEER_HEADS, LANES), 0)

        def extract(h, tops):
            tops = [list(t) for t in tops]
            rows = [pl.ds(pl.multiple_of((2 * h + p) * N_KEYS, N_KEYS), N_KEYS) for p in range(2)]
            cur = [sc_ref[rows[p], lanes] for p in range(2)]
            rank = jnp.full((N_KEYS, LANES), float(PEER_TOPK), F32)
            for a in range(PEER_TOPK):
                m = [jnp.max(c, axis=0, keepdims=True) for c in cur]
                hit = [c >= mm for c, mm in zip(cur, m)]
                cur = [jnp.where(ht, neg, c) for ht, c in zip(hit, cur)]
                rank = jnp.where(hit[1], float(a), rank)
                for p in range(2):
                    tops[p][a] = jnp.where(head_row == h, m[p], tops[p][a])
            r1_ref[h, :, lanes] = rank.astype(BF16)
            return tuple(tuple(t) for t in tops)

        zero = jnp.zeros((PEER_HEADS, LANES), F32)
        top0, top1 = lax.fori_loop(0, PEER_HEADS, extract, ((zero,) * PEER_TOPK,) * 2)
        cands = [top0[a] + top1[b]
                 for a in range(PEER_TOPK) for b in range(PEER_TOPK) if (a + 1) * (b + 1) <= PEER_TOPK]
        c_max = cands[0]
        z = jnp.zeros_like(c_max)
        tau = c_max
        for a in range(PEER_TOPK):
            tau = functools.reduce(jnp.maximum, cands)
            z = z + jnp.exp(tau - c_max)
            cands = [jnp.where(cd >= tau, -jnp.inf, cd) for cd in cands]
        inv_z = 1.0 / z

        def factors(h, carry):
            row_of = lambda x: jnp.max(jnp.where(head_row == h, x, -jnp.inf), axis=0, keepdims=True)
            s0 = sc_ref[pl.ds(pl.multiple_of(2 * h * N_KEYS, N_KEYS), N_KEYS), lanes]
            s1 = sc_ref[pl.ds(pl.multiple_of((2 * h + 1) * N_KEYS, N_KEYS), N_KEYS), lanes]
            tau_h = row_of(tau)
            n = jnp.zeros((N_KEYS, LANES), F32)
            for b in range(PEER_TOPK):
                n = jnp.where(s0 + row_of(top1[b]) >= tau_h, float(b + 1), n)
            n_ref[h, :, lanes] = n
            f0_ref[h, :, lanes] = jnp.exp(s0 - row_of(top0[0])) * row_of(inv_z)
            e1_ref[h, :, lanes] = jnp.exp(s1 - row_of(top1[0])).astype(BF16)
            return carry

        lax.fori_loop(0, PEER_HEADS, factors, 0)
        return carry

    lax.fori_loop(0, tm // LANES, block, 0)


PEER_I_GROUP = 4
PEER_STREAMS = 2


def _peer_body(hx_ref, wk_ref, *rest):
    u_refs, vt_refs = rest[:PEER_STREAMS], rest[PEER_STREAMS:2 * PEER_STREAMS]
    o_ref, sc_ref, hxb_ref, n_ref, f0_ref, r1_ref, e1_ref, w_ref, acc_ref = rest[2 * PEER_STREAMS:]
    e = pl.program_id(1)
    tm = hx_ref.shape[0]
    nsb = tm // PEER_SB

    @pl.when(e == 0)
    def _():
        _peer_stats(hx_ref, wk_ref, sc_ref, hxb_ref, n_ref, f0_ref, r1_ref, e1_ref)
        acc_ref[...] = jnp.zeros(acc_ref.shape, F32)

    i_rows = pl.ds(pl.multiple_of(e * PEER_I_BLOCK, PEER_I_BLOCK), PEER_I_BLOCK)

    def activations(sb):
        parts = []
        for u_ref in u_refs:
            act = _d(u_ref[...], hxb_ref[:, sb * PEER_SB:(sb + 1) * PEER_SB], NN)
            act = act.astype(BF16)
            parts.append(0.5 * act * (1.0 + lax.erf(act * (2.0 ** -0.5))))
        return parts

    def gates(sb, act):
        for hb in range(PEER_SB // LANES):
            lanes = slice(sb * PEER_SB + hb * LANES, sb * PEER_SB + (hb + 1) * LANES)
            sub = slice(hb * LANES, (hb + 1) * LANES)
            n8 = [n_ref[h, i_rows, lanes] for h in range(PEER_HEADS)]
            f8 = [f0_ref[h, i_rows, lanes] for h in range(PEER_HEADS)]
            bcast = lambda x, ii: jnp.broadcast_to(x[ii:ii + 1], (N_KEYS, LANES)).astype(BF16)
            for ig in range(0, PEER_I_BLOCK, PEER_I_GROUP):
                g = [jnp.zeros((N_KEYS, LANES), BF16) for _ in range(PEER_I_GROUP)]
                for h in range(PEER_HEADS):
                    r1 = r1_ref[h, :, lanes]
                    e1 = e1_ref[h, :, lanes]
                    for k in range(PEER_I_GROUP):
                        ii = ig + k
                        g[k] = g[k] + jnp.where(r1 < bcast(n8[h], ii), e1 * bcast(f8[h], ii), jnp.zeros_like(e1))
                for k in range(PEER_I_GROUP):
                    rows = slice((ig + k) * N_KEYS, (ig + k + 1) * N_KEYS)
                    part, off = divmod((ig + k) * N_KEYS, PEER_TE // PEER_STREAMS)
                    w_ref[sb, rows, sub] = g[k] * act[part][off:off + N_KEYS, sub]

    def accumulate(sb):
        cols = slice(sb * PEER_SB, (sb + 1) * PEER_SB)
        dr = acc_ref.shape[0] // PEER_STREAMS
        for k, vt_ref in enumerate(vt_refs):
            acc_ref[k * dr:(k + 1) * dr, cols] += _d(vt_ref[...], w_ref[sb], NN)

    act = activations(0)
    for sb in range(nsb):
        nxt = activations(sb + 1) if sb + 1 < nsb else None
        gates(sb, act)
        accumulate(sb)
        act = nxt

    @pl.when(e == pl.num_programs(1) - 1)
    def _():
        o_ref[...] = acc_ref[...].T


def _fold_body(k_ref, w_ref, o_ref):
    o_ref[...] = _mdot(k_ref[...], w_ref[...], NN, 6)


def peer_fold_keys(wq, keys):
    d = wq.shape[0]
    nhp, nk, dk = keys.shape
    wqt = wq.T.reshape(nhp, dk, d)
    return pl.pallas_call(
        _fold_body,
        grid=(nhp,),
        in_specs=[pl.BlockSpec((None, nk, dk), lambda i: (i, 0, 0)),
                  pl.BlockSpec((None, dk, d), lambda i: (i, 0, 0))],
        out_specs=pl.BlockSpec((nk, d), lambda i: (i, 0)),
        out_shape=jax.ShapeDtypeStruct((nhp * nk, d), F32),
        compiler_params=pltpu.CompilerParams(dimension_semantics=("arbitrary",), vmem_limit_bytes=VMEM_LIMIT),
        name="peer_fold_keys",
    )(keys, wqt)


def peer(hx, wk, u_bf, vt_bf):
    t, d = hx.shape
    tm = PEER_TM
    assert t % tm == 0
    ne = u_bf.shape[0] // PEER_TE
    h = PEER_HEADS
    ns = PEER_STREAMS
    return pl.pallas_call(
        _peer_body,
        grid=(t // tm, ne),
        in_specs=[pl.BlockSpec((tm, d), lambda i, e: (i, 0), pipeline_mode=pl.Buffered(1)),
                  pl.BlockSpec(wk.shape, lambda i, e: (0, 0), pipeline_mode=pl.Buffered(1)),
                  *[pl.BlockSpec((PEER_TE // ns, d), functools.partial(lambda i, e, k: (e * ns + k, 0), k=k))
                    for k in range(ns)],
                  *[pl.BlockSpec((None, d // ns, PEER_TE), functools.partial(lambda i, e, k: (e, k, 0), k=k))
                    for k in range(ns)]],
        out_specs=pl.BlockSpec((tm, d), lambda i, e: (i, 0)),
        out_shape=jax.ShapeDtypeStruct((t, d), F32),
        scratch_shapes=[pltpu.VMEM((2 * h * N_KEYS, tm), F32),
                        pltpu.VMEM((d, tm), BF16),
                        pltpu.VMEM((h, N_KEYS, tm), F32), pltpu.VMEM((h, N_KEYS, tm), F32),
                        pltpu.VMEM((h, N_KEYS, tm), BF16), pltpu.VMEM((h, N_KEYS, tm), BF16),
                        pltpu.VMEM((tm // PEER_SB, PEER_TE, PEER_SB), BF16), pltpu.VMEM((d, tm), F32)],
        compiler_params=pltpu.CompilerParams(
            dimension_semantics=("arbitrary", "arbitrary"), vmem_limit_bytes=VMEM_LIMIT),
        name="peer_dense",
    )(hx, wk.astype(BF16), *([u_bf] * ns), *([vt_bf] * ns))


LN_TM = 256


def _ln_mod_body(xs_ref, y_ref, mg_ref, mn_ref, lng_ref, lnb_ref, xo_ref, ho_ref, *, gate_col, mod_col, nct, nt):
    i = pl.program_id(0)
    d = xs_ref.shape[1]
    is_ctx = i < nct

    def pick(ref, col):
        return jnp.where(is_ctx, ref[1:2, col * d:(col + 1) * d], ref[0:1, col * d:(col + 1) * d])

    z = DN_ALPHA * xs_ref[...] + pick(mg_ref, gate_col) * y_ref[...]
    mu = jnp.mean(z, axis=-1, keepdims=True)
    zc = z - mu
    var = jnp.mean(zc * zc, axis=-1, keepdims=True)
    xn = zc * lax.rsqrt(var + LN_EPS) * lng_ref[...] + lnb_ref[...]
    xo_ref[...] = xn
    h = xn * (1.0 + pick(mn_ref, mod_col + 1)) + pick(mn_ref, mod_col)
    ho_ref[...] = jnp.where(i < nt, h, 0.0).astype(ho_ref.dtype)


def ln_mod(xs, y, mod_gate, gate_col, mod_next, mod_col, ln_g, ln_b, n_ctx, pad_to, h_dtype):
    t, d = xs.shape
    tm = LN_TM
    assert t % tm == 0 and n_ctx % tm == 0 and pad_to % tm == 0 and pad_to >= t
    nt = t // tm
    row = lambda i: (jnp.minimum(i, nt - 1), 0)
    full = lambda a: pl.BlockSpec(a.shape, lambda i: (0, 0))
    return pl.pallas_call(
        functools.partial(_ln_mod_body, gate_col=gate_col, mod_col=mod_col, nct=n_ctx // tm, nt=nt),
        grid=(pad_to // tm,),
        in_specs=[pl.BlockSpec((tm, d), row), pl.BlockSpec((tm, d), row), full(mod_gate), full(mod_next),
                  pl.BlockSpec((1, d), lambda i: (0, 0)), pl.BlockSpec((1, d), lambda i: (0, 0))],
        out_specs=[pl.BlockSpec((tm, d), row), pl.BlockSpec((tm, d), lambda i: (i, 0))],
        out_shape=[jax.ShapeDtypeStruct((t, d), F32), jax.ShapeDtypeStruct((pad_to, d), h_dtype)],
        compiler_params=pltpu.CompilerParams(dimension_semantics=("arbitrary",), vmem_limit_bytes=VMEM_LIMIT),
        name="ln_mod",
    )(xs, y, mod_gate, mod_next, ln_g.reshape(1, d), ln_b.reshape(1, d))


def _pad_cols(w, n):
    return jnp.pad(w, ((0, 0), (0, n - w.shape[1])))


def _mlstm_layer(h, n_ctx, w_in, b_in, conv_w, conv_b, hn_g, w_out):
    d = D_MODEL
    t = h.shape[0]
    qk_pre = matmul(h, w_in[:, :2 * d], b_in[:2 * d])
    v = matmul(h, w_in[:, 2 * d:3 * d], b_in[2 * d:3 * d], out_dtype=BF16)
    o = matmul(h, w_in[:, 3 * d:4 * d], b_in[3 * d:4 * d], act="sigmoid", out_dtype=BF16)
    g = matmul(h, _pad_cols(w_in[:, 4 * d:], LANES), jnp.pad(b_in[4 * d:], (0, LANES - 4 * MLSTM_HEADS)))
    g = g[:, :4 * MLSTM_HEADS].reshape(t, 4, MLSTM_HEADS)
    g = jnp.concatenate([g[:, :2], jax.nn.log_sigmoid(g[:, 2:])], axis=1)
    gh = jnp.transpose(g, (2, 0, 1))
    gcol = jnp.pad(gh, ((0, 0), (0, 0), (0, LANES - 4)))
    grow = jnp.pad(jnp.transpose(gh, (0, 2, 1)), ((0, 0), (0, 4), (0, 0)))
    qk = mlstm_conv(qk_pre, conv_w, conv_b, n_ctx)
    hf, hb = mlstm_scan(qk, v, gcol, grow, n_ctx)
    return mlstm_out(hf, hb, o, hn_g, w_out)


def _rwkv_layer(h, n_ctx, mu, w_rkv, w0, w1, w2, a0, a1, a2, g1, g2, k_k, k_a, r_k, lnx_g, lnx_b, w_out):
    d = D_MODEL
    xm = rwkv_mix(h, mu, n_ctx)
    r = matmul(xm[0], w_rkv[0])
    k = matmul(xm[1], w_rkv[1])
    v = matmul(xm[2], w_rkv[2])

    def lora_pair(x, w_in, w_mid, bias, act_mid, act_out):
        rank = w_in.shape[-1]
        w_a = _pad_cols(jnp.concatenate([w_in[0], w_in[1]], axis=1), LANES)
        zpad = jnp.zeros((rank, d), F32)
        w_b = jnp.concatenate([jnp.concatenate([w_mid[0], zpad], axis=1),
                               jnp.concatenate([zpad, w_mid[1]], axis=1)], axis=0)
        w_b = jnp.pad(w_b, ((0, LANES - 2 * rank), (0, 0)))
        mid = matmul(x, w_a, act=act_mid, out_dtype=BF16)
        return matmul(mid, w_b, jnp.concatenate([bias[0], bias[1]]), act=act_out)

    lw = lora_pair(xm[3], w1, w2, w0, "tanh", "logdecay")
    a = lora_pair(xm[4], a1, a2, a0, None, "sigmoid")
    gpad = 2 * LANES
    gg = matmul(xm[5], _pad_cols(g1, gpad), act="sigmoid", out_dtype=BF16)
    g = matmul(gg, jnp.pad(g2, ((0, gpad - g1.shape[1]), (0, 0))))
    yf, yb = rwkv_scan(r, k, v, lw, a, k_k, k_a, n_ctx)
    return rwkv_out(yf, yb, r, k, v, g, a, k_a, r_k, lnx_g, lnx_b, w_out)


def _forward(x, c, ctx, c_ctx, ada_w, ada_b, ln_g, ln_b,
             ml_w_in, ml_b_in, ml_conv_w, ml_conv_b, ml_hn_g, ml_w_out,
             rw_mu, rw_w_rkv, rw_w0, rw_w1, rw_w2, rw_a0, rw_a1, rw_a2, rw_g1, rw_g2,
             rw_k_k, rw_k_a, rw_r_k, rw_lnx_g, rw_lnx_b, rw_w_out,
             pk_wq, pk_keys, pk_u, pk_v):
    d = D_MODEL
    n_ctx = ctx.shape[1]
    xs = jnp.concatenate([ctx[0], x[0]], axis=0)
    t = xs.shape[0]
    t_pad = -(-t // PEER_TM) * PEER_TM
    s_in = jnp.zeros((8, d), F32).at[0].set(jax.nn.silu(c[0])).at[1].set(jax.nn.silu(c_ctx))
    depth = ada_w.shape[0]
    mods = [matmul(s_in, ada_w[i], ada_b[i], passes=3) for i in range(depth)]
    is_ctx = (jnp.arange(t) < n_ctx)[:, None]
    m0 = [jnp.where(is_ctx, mods[0][1, n * d:(n + 1) * d], mods[0][0, n * d:(n + 1) * d]) for n in range(2)]
    mixer_dtype = lambda i: BF16 if i % 2 == 0 else F32
    h = (xs * (1.0 + m0[1]) + m0[0]).astype(mixer_dtype(0))
    for i in range(depth):
        j = i // 2
        if i % 2 == 0:
            y = _mlstm_layer(h, n_ctx, ml_w_in[j], ml_b_in[j], ml_conv_w[j], ml_conv_b[j],
                             ml_hn_g[j], ml_w_out[j])
        else:
            y = _rwkv_layer(h, n_ctx, rw_mu[j], rw_w_rkv[j], rw_w0[j], rw_w1[j], rw_w2[j],
                            rw_a0[j], rw_a1[j], rw_a2[j], rw_g1[j], rw_g2[j], rw_k_k[j],
                            rw_k_a[j], rw_r_k[j], rw_lnx_g[j], rw_lnx_b[j], rw_w_out[j])
        xs, h = ln_mod(xs, y, mods[i], 2, mods[i], 3, ln_g[i, 0], ln_b[i, 0], n_ctx, t_pad, BF16)
        wk = peer_fold_keys(pk_wq[i], pk_keys[i].reshape(2 * PEER_HEADS, N_KEYS, PEER_DQ // 2))
        vt = jnp.swapaxes(pk_v[i].astype(BF16).reshape(-1, PEER_TE, d), 1, 2)
        y = peer(h, wk, pk_u[i].astype(BF16), vt)
        xs, h = ln_mod(xs, y, mods[i], 5, mods[min(i + 1, depth - 1)], 0, ln_g[i, 1], ln_b[i, 1], n_ctx, t,
                       mixer_dtype(i + 1))
    return xs[n_ctx:][None]


def kernel(x, c, ctx, c_ctx, ada_w, ada_b, ln_g, ln_b, ml_w_in, ml_b_in, ml_conv_w, ml_conv_b, ml_hn_g, ml_w_out, rw_mu, rw_w_rkv, rw_w0, rw_w1, rw_w2, rw_a0, rw_a1, rw_a2, rw_g1, rw_g2, rw_k_k, rw_k_a, rw_r_k, rw_lnx_g, rw_lnx_b, rw_w_out, pk_wq, pk_keys, pk_u, pk_v):
    return _forward(x, c, ctx, c_ctx, ada_w, ada_b, ln_g, ln_b,
                    ml_w_in, ml_b_in, ml_conv_w, ml_conv_b, ml_hn_g, ml_w_out,
                    rw_mu, rw_w_rkv, rw_w0, rw_w1, rw_w2, rw_a0, rw_a1, rw_a2, rw_g1, rw_g2,
                    rw_k_k, rw_k_a, rw_r_k, rw_lnx_g, rw_lnx_b, rw_w_out,
                    pk_wq, pk_keys, pk_u, pk_v)
```

```python
import functools

import jax
import jax.numpy as jnp
from jax import lax
from jax.experimental import pallas as pl
from jax.experimental.pallas import tpu as pltpu

F32 = jnp.float32
BF16 = jnp.bfloat16

D_MODEL = 1024
DEPTH = 4
GRID_W = 64
DN_ALPHA = (2.0 * DEPTH) ** 0.25
LN_EPS = 1e-5

MLSTM_HEADS = 4
MLSTM_DH = D_MODEL // MLSTM_HEADS
MLSTM_CHUNK = 128
M_INIT = -1e30

RWKV_N = 64
RWKV_HEADS = D_MODEL // RWKV_N
RWKV_CHUNK = 64
RWKV_GN_EPS = 64e-5

N_KEYS = 128
PEER_HEADS = 8
PEER_DQ = 256
PEER_TOPK = 16
PEER_I_BLOCK = 16
PEER_TE = PEER_I_BLOCK * N_KEYS
PEER_TM = 768
PEER_SB = 256

LANES = 128
VMEM_LIMIT = 62 * 1024 * 1024

NN = ((1,), (0,))
NT = ((1,), (1,))


def _split(x, n):
    parts = []
    r = x.astype(F32)
    for i in range(n):
        p = r.astype(BF16)
        parts.append(p)
        if i + 1 < n:
            r = r - p.astype(F32)
    return parts


def _d(a, b, dims):
    return lax.dot_general(a, b, (dims, ((), ())), preferred_element_type=F32)


def _mdot(a, b, dims, passes):
    if passes == 1:
        return _d(a.astype(BF16), b.astype(BF16), dims)
    if passes == 3:
        a0, a1 = _split(a, 2)
        b0, b1 = _split(b, 2)
        return (_d(a0, b1, dims) + _d(a1, b0, dims)) + _d(a0, b0, dims)
    a0, a1, a2 = _split(a, 3)
    b0, b1, b2 = _split(b, 3)
    lo = (_d(a0, b2, dims) + _d(a2, b0, dims)) + _d(a1, b1, dims)
    mid = _d(a0, b1, dims) + _d(a1, b0, dims)
    return (lo + mid) + _d(a0, b0, dims)


def _dot_01_lhs(m01, x):
    mb = m01.astype(BF16)
    x0, x1, x2 = _split(x, 3)
    return (_d(mb, x2, NN) + _d(mb, x1, NN)) + _d(mb, x0, NN)


def _dot_01_rhs(x, m01, pieces=3):
    mb = m01.astype(BF16)
    return functools.reduce(jnp.add, [_d(xp, mb, NN) for xp in reversed(_split(x, pieces))])


def _pick(n, cands):
    for c in cands:
        if n % c == 0:
            return c
    raise ValueError(f"no tile for {n}")


_ACTS = {None: lambda x: x, "sigmoid": jax.nn.sigmoid, "tanh": jnp.tanh,
         "logdecay": lambda x: -(2.718281828459045 ** -0.5) * jax.nn.sigmoid(x)}


def _mm_body(x_ref, w_ref, b_ref, o_ref, *, passes, act):
    o_ref[...] = _ACTS[act](_mdot(x_ref[...], w_ref[...], NN, passes) + b_ref[...]).astype(o_ref.dtype)


def matmul(x, w, b=None, *, passes=1, act=None, out_dtype=F32):
    m, k = x.shape
    n = w.shape[1]
    assert n % LANES == 0 and w.shape[0] == k
    tm = m if m <= 1024 else _pick(m, (640, 512, 384, 256, 128))
    tn = _pick(n, (1024, 768, 640, 512, 384, 256, 128))
    if b is None:
        b = jnp.zeros((n,), F32)
    if passes == 1:
        w = w.astype(BF16)
    return pl.pallas_call(
        functools.partial(_mm_body, passes=passes, act=act),
        grid=(m // tm, n // tn),
        in_specs=[pl.BlockSpec((tm, k), lambda i, j: (i, 0)),
                  pl.BlockSpec((k, tn), lambda i, j: (0, j)),
                  pl.BlockSpec((1, tn), lambda i, j: (0, j))],
        out_specs=pl.BlockSpec((tm, tn), lambda i, j: (i, j)),
        out_shape=jax.ShapeDtypeStruct((m, n), out_dtype),
        compiler_params=pltpu.CompilerParams(
            dimension_semantics=("arbitrary", "arbitrary"), vmem_limit_bytes=VMEM_LIMIT),
        name="proj_matmul",
    )(x, w, b.reshape(1, n).astype(F32))


CONV_TM = 256
CONV_CB = 512


def _conv_body(cur_ref, prev_ref, next_ref, w_ref, b_ref, o_ref, *, nct, nt, q_blocks, q_scale):
    i = pl.program_id(0)
    j = pl.program_id(1)
    is_ctx = i < nct
    tm = cur_ref.shape[0]
    above = jnp.where(is_ctx | (i == nct), 0.0, prev_ref[...])
    below = jnp.where(is_ctx | (i == nt - 1), 0.0, next_ref[...])
    ext = jnp.concatenate([above, cur_ref[...], below], axis=0)
    n = ext.shape[0]
    colid = lax.broadcasted_iota(jnp.int32, ext.shape, 0) % GRID_W
    left = jnp.where(is_ctx | (colid != 0), pltpu.roll(ext, 1, 0), 0.0)
    right = jnp.where(is_ctx | (colid != GRID_W - 1), pltpu.roll(ext, n - 1, 0), 0.0)
    w = w_ref[...]
    acc = jnp.zeros((tm, ext.shape[1]), F32) + b_ref[...]
    for di in range(3):
        rows = slice(di * GRID_W, di * GRID_W + tm)
        tap = left[rows] * w[3 * di:3 * di + 1] + ext[rows] * w[3 * di + 1:3 * di + 2] \
            + right[rows] * w[3 * di + 2:3 * di + 3]
        acc = acc + (tap if di == 1 else jnp.where(is_ctx, 0.0, tap))
    y = acc * jax.nn.sigmoid(acc)
    o_ref[...] = (y * jnp.where(j < q_blocks, q_scale, 1.0)).astype(o_ref.dtype)


def mlstm_conv(qk_pre, conv_w, conv_b, n_ctx):
    t, c = qk_pre.shape
    tm, cb = CONV_TM, CONV_CB
    assert t % tm == 0 and n_ctx % tm == 0 and tm % GRID_W == 0 and c % (2 * cb) == 0
    r = tm // GRID_W
    nt, nu = t // tm, t // GRID_W
    return pl.pallas_call(
        functools.partial(_conv_body, nct=n_ctx // tm, nt=nt, q_blocks=c // (2 * cb), q_scale=MLSTM_DH ** -0.5),
        grid=(nt, c // cb),
        in_specs=[pl.BlockSpec((tm, cb), lambda i, j: (i, j)),
                  pl.BlockSpec((GRID_W, cb), lambda i, j: (jnp.maximum(i * r - 1, 0), j)),
                  pl.BlockSpec((GRID_W, cb), lambda i, j: (jnp.minimum((i + 1) * r, nu - 1), j)),
                  pl.BlockSpec((9, cb), lambda i, j: (0, j)),
                  pl.BlockSpec((1, cb), lambda i, j: (0, j))],
        out_specs=pl.BlockSpec((tm, cb), lambda i, j: (i, j)),
        out_shape=jax.ShapeDtypeStruct((t, c), BF16),
        compiler_params=pltpu.CompilerParams(
            dimension_semantics=("arbitrary", "arbitrary"), vmem_limit_bytes=VMEM_LIMIT),
        name="mlstm_conv",
    )(qk_pre, qk_pre, qk_pre, conv_w.reshape(9, c), conv_b.reshape(1, c))


def _mlstm_out_body(hf_ref, hb_ref, o_ref, g_ref, w_ref, y_ref):
    h = hf_ref[...] + hb_ref[...]
    parts = []
    for a in range(MLSTM_HEADS):
        x = h[:, a * MLSTM_DH:(a + 1) * MLSTM_DH]
        mu = jnp.mean(x, axis=-1, keepdims=True)
        xc = x - mu
        var = jnp.mean(xc * xc, axis=-1, keepdims=True)
        parts.append(xc * lax.rsqrt(var + LN_EPS))
    hn = jnp.concatenate(parts, axis=1)
    z = o_ref[...].astype(F32) * hn * g_ref[...]
    y_ref[...] = _d(z.astype(BF16), w_ref[...], NN)


def mlstm_out(hf, hb, o, hn_g, w_out):
    t, d = hf.shape
    tm = LN_TM
    row = pl.BlockSpec((tm, d), lambda i: (i, 0))
    return pl.pallas_call(
        _mlstm_out_body,
        grid=(t // tm,),
        in_specs=[row, row, row, pl.BlockSpec((1, d), lambda i: (0, 0)), pl.BlockSpec((d, d), lambda i: (0, 0))],
        out_specs=row,
        out_shape=jax.ShapeDtypeStruct((t, d), F32),
        compiler_params=pltpu.CompilerParams(dimension_semantics=("arbitrary",), vmem_limit_bytes=VMEM_LIMIT),
        name="mlstm_out",
    )(hf, hb, o, hn_g.reshape(1, d), w_out.astype(BF16))


ML_HEADS_PER_STEP = 4


def _mlstm_body(qf, kf, vf, gcf, grf, qb, kb, vb, gcb, grb, hf_ref, hb_ref, c_ref, n_ref, m_ref):
    @pl.when(pl.program_id(1) == 0)
    def _():
        c_ref[...] = jnp.zeros(c_ref.shape, F32)
        n_ref[...] = jnp.zeros(n_ref.shape, F32)
        m_ref[...] = jnp.full(m_ref.shape, M_INIT, F32)

    dh = MLSTM_DH
    cs = [(a, d) for a in range(ML_HEADS_PER_STEP) for d in (0, 1)]
    each = lambda f, *ls: [f(*xs) for xs in zip(*ls)]
    cols = lambda a: slice(a * dh, (a + 1) * dh)
    qs, ks, vs, gcs, grs = (qf, qb), (kf, kb), (vf, vb), (gcf, gcb), (grf, grb)
    qb16 = [qs[d][:, cols(a)] for a, d in cs]
    kb16 = [ks[d][:, cols(a)] for a, d in cs]
    vb16 = [vs[d][:, cols(a)] for a, d in cs]
    gc = [gcs[d][a] for a, d in cs]
    gr = [grs[d][a] for a, d in cs]
    q, k = (each(lambda x: x.astype(F32), x16) for x16 in (qb16, kb16))
    L = q[0].shape[0]
    row = lax.broadcasted_iota(jnp.int32, (L, L), 0)
    col = lax.broadcasted_iota(jnp.int32, (L, L), 1)
    seen = (col <= row, col >= row)
    tri = [jnp.where(m, 1.0, 0.0) for m in seen]
    b_col = [_dot_01_lhs(tri[d], g)[:, 2 + d:3 + d] for (a, d), g in zip(cs, gc)]
    b_row = [_dot_01_rhs(g, tri[1 - d])[2 + d:3 + d, :] for (a, d), g in zip(cs, gr)]
    ig_col = [g[:, d:d + 1] for (a, d), g in zip(cs, gc)]
    ig_row = [g[d:d + 1, :] for (a, d), g in zip(cs, gr)]
    m_st = [m_ref[a, d, 0:1, 0:1] for a, d in cs]
    c_st = [c_ref[a, d] for a, d in cs]
    n_st = [n_ref[a, d] for a, d in cs]

    dlog = [jnp.where(seen[d], bc - br + ir, -jnp.inf) for (a, d), bc, br, ir in zip(cs, b_col, b_row, ig_row)]
    m_inter = each(jnp.add, b_col, m_st)
    m_t = each(lambda mi, dl: jnp.maximum(mi, jnp.max(dl, axis=1, keepdims=True)), m_inter, dlog)
    qk = each(lambda x, y: _d(x, y, NT), qb16, kb16)
    s = each(lambda x, dl, mt: x * jnp.exp(dl - mt), qk, dlog, m_t)
    dec = each(lambda mi, mt: jnp.exp(mi - mt), m_inter, m_t)
    sv = each(lambda x, y: _d(x.astype(BF16), y, NN), s, vb16)
    qc = each(lambda x, y: _d(x, y.astype(BF16), NN), qb16, c_st)
    num = each(lambda x, dc, y: x + dc * y, sv, dec, qc)
    den = each(lambda x, dc, qq, nn: jnp.sum(x, axis=1, keepdims=True) + dc * jnp.sum(qq * nn, axis=1, keepdims=True),
               s, dec, q, n_st)
    h = each(lambda nu, de, mt: nu / jnp.maximum(jnp.abs(de), jnp.exp(-mt)), num, den, m_t)
    h_refs = (hf_ref, hb_ref)
    for (a, d), x in zip(cs, h):
        h_refs[d][:, cols(a)] = x

    b_last = [bc[0:1, :] if d else bc[L - 1:L, :] for (a, d), bc in zip(cs, b_col)]
    w_c = each(lambda bl, bc, ic: bl - bc + ic, b_last, b_col, ig_col)
    m_new = each(lambda bl, ms, w: jnp.maximum(bl + ms, jnp.max(w, axis=0, keepdims=True)), b_last, m_st, w_c)
    a_c = each(lambda w, mn: jnp.exp(w - mn), w_c, m_new)
    g_prev = each(lambda bl, ms, mn: jnp.exp(bl + ms - mn), b_last, m_st, m_new)
    ak = each(jnp.multiply, a_c, k)
    kv = each(lambda x, y: _d(x.T.astype(BF16), y, NN), ak, vb16)
    for i, (a, d) in enumerate(cs):
        c_ref[a, d] = g_prev[i] * c_st[i] + kv[i]
        n_ref[a, d] = g_prev[i] * n_st[i] + jnp.sum(ak[i], axis=0, keepdims=True)
        m_ref[a, d] = jnp.broadcast_to(m_new[i], m_ref.shape[2:])


def _bwd_chunk(c, nc0, nc):
    return jnp.where(c < nc0, nc0 - 1 - c, nc - 1 - (c - nc0))


def mlstm_scan(qk, v, gcol, grow, n_ctx):
    t = v.shape[0]
    L, dh, hs = MLSTM_CHUNK, MLSTM_DH, ML_HEADS_PER_STEP
    ng = MLSTM_HEADS // hs
    nc, nc0 = t // L, n_ctx // L
    idf = lambda c: c
    idb = lambda c: _bwd_chunk(c, nc0, nc)
    qkv = lambda f, off: pl.BlockSpec((L, hs * dh), lambda h, c: (f(c), off + h))
    gc_spec = lambda f: pl.BlockSpec((hs, L, LANES), lambda h, c: (h, f(c), 0))
    gr_spec = lambda f: pl.BlockSpec((hs, 8, L), lambda h, c: (h, 0, f(c)))
    return pl.pallas_call(
        _mlstm_body,
        grid=(ng, nc),
        in_specs=[qkv(idf, 0), qkv(idf, ng), qkv(idf, 0), gc_spec(idf), gr_spec(idf),
                  qkv(idb, 0), qkv(idb, ng), qkv(idb, 0), gc_spec(idb), gr_spec(idb)],
        out_specs=[qkv(idf, 0), qkv(idb, 0)],
        out_shape=[jax.ShapeDtypeStruct((t, D_MODEL), F32)] * 2,
        scratch_shapes=[pltpu.VMEM((hs, 2, dh, dh), F32), pltpu.VMEM((hs, 2, 1, dh), F32),
                        pltpu.VMEM((hs, 2, 8, LANES), F32)],
        compiler_params=pltpu.CompilerParams(
            dimension_semantics=("arbitrary", "arbitrary"), vmem_limit_bytes=VMEM_LIMIT),
        name="mlstm_scan",
    )(qk, qk, v, gcol, grow, qk, qk, v, gcol, grow)


RW_PASSES = 1
RW_CHUNKS_PER_STEP = 4


def _stack2(x, lane_head):
    return jnp.concatenate([jnp.where(lane_head == 0, x, 0.0), jnp.where(lane_head == 1, x, 0.0)], axis=0)


def _unstack2(x):
    L = x.shape[0] // 2
    return x[:L] + x[L:]


def _rwkv_chunks(chains, k_k, k_a):
    L = chains[0][0].shape[0]
    n2 = 2 * L
    p = RW_PASSES
    ds = [c[5] for c in chains]
    each = lambda f, *ls: [f(*xs) for xs in zip(*ls)]
    dot = lambda dims: (lambda a, b: _mdot(a, b, dims, p))
    lw, r, k_raw, v, a = ([c[i] for c in chains] for i in range(5))

    lane_r = lax.broadcasted_iota(jnp.int32, (LANES, LANES), 0) // RWKV_N
    lane_c = lax.broadcasted_iota(jnp.int32, (LANES, LANES), 1) // RWKV_N
    head_ones = jnp.where(lane_r == lane_c, 1.0, 0.0)
    kkr = each(jnp.multiply, k_raw, k_k)
    ss = each(lambda x: _dot_01_rhs(x * x, head_ones), kkr)
    kap = each(lambda x, q: x / jnp.maximum(jnp.sqrt(q), 1e-12), kkr, ss)
    alp = each(jnp.multiply, kap, a)
    k = each(lambda x, y, z: x * (1.0 + (y - 1.0) * z), k_raw, a, k_a)

    row = lax.broadcasted_iota(jnp.int32, (L, L), 0)
    col = lax.broadcasted_iota(jnp.int32, (L, L), 1)
    tris = (jnp.where(col <= row, 1.0, 0.0), jnp.where(col >= row, 1.0, 0.0))
    lp = [_dot_01_lhs(tris[d], x) for d, x in zip(ds, lw)]
    lp_end = [x[0:1, :] if d else x[L - 1:L, :] for d, x in zip(ds, lp)]
    e_neg = each(lambda x: jnp.exp(-x), lp)
    e_end = each(lambda x, xe: jnp.exp(xe - x), lp, lp_end)
    kap_t = each(lambda x, y, z: x * jnp.exp(y - z), kap, lp, lw)
    r_t = each(lambda x, y: x * jnp.exp(y), r, lp)
    k_h = each(jnp.multiply, k, e_neg)
    a_h = each(jnp.multiply, alp, e_neg)
    k_e = each(jnp.multiply, k, e_end)
    a_e = each(jnp.multiply, alp, e_end)

    lane_head = lax.broadcasted_iota(jnp.int32, (L, LANES), 1) // RWKV_N
    st = lambda x: _stack2(x, lane_head)
    kap_s, r_s, v_s, k_s, a_s = (each(st, x) for x in (kap_t, r_t, v, k_e, a_e))
    rhs_k = each(lambda x: jnp.concatenate([x, x], axis=0), k_h)
    rhs_a = each(lambda x: jnp.concatenate([x, x], axis=0), a_h)

    row2 = lax.broadcasted_iota(jnp.int32, (n2, n2), 0)
    col2 = lax.broadcasted_iota(jnp.int32, (n2, n2), 1)
    same_head = (row2 // L) == (col2 // L)
    strict = (same_head & (col2 < row2), same_head & (col2 > row2))
    incl = (same_head & (col2 <= row2), same_head & (col2 >= row2))
    zero = jnp.zeros((n2, n2), F32)
    masked = lambda masks: (lambda d, x: jnp.where(masks[d], x, zero))
    kr_s = each(lambda x, y: jnp.concatenate([x, y], axis=0), kap_s, r_s)
    p_a = each(dot(NT), kr_s, rhs_a)
    p_k = each(dot(NT), kr_s, rhs_k)
    n_ka = each(masked(strict), ds, each(lambda x: x[:n2], p_a))
    m_kk = each(masked(strict), ds, each(lambda x: x[:n2], p_k))
    a_rk = each(masked(incl), ds, each(lambda x: x[n2:], p_k))
    a_ra = each(masked(incl), ds, each(lambda x: x[n2:], p_a))

    b16 = (row2 // 16) == (col2 // 16)
    b32 = (row2 // 32) == (col2 // 32)
    eye = jnp.where(row2 == col2, 1.0, 0.0)
    n16 = each(lambda x: jnp.where(b16, x, zero), n_ka)
    n_2 = each(dot(NN), n16, n16)
    n_4 = each(dot(NN), n_2, n_2)
    n_8 = each(dot(NN), n_4, n_4)
    inv = each(lambda x: eye - x, n16)
    for pw in (n_2, n_4, n_8):
        inv = each(jnp.add, inv, each(dot(NN), inv, pw))
    for sel in (lambda x: jnp.where(b32 & ~b16, x, zero), lambda x: jnp.where(b32, zero, x)):
        t1 = each(dot(NN), inv, each(sel, n_ka))
        inv = each(jnp.subtract, inv, each(dot(NN), t1, inv))

    mav = each(dot(NN), each(lambda x, y: jnp.concatenate([x, y], axis=0), m_kk, a_rk), v_s)
    mv = each(lambda x: x[:n2], mav)
    av = each(lambda x: x[n2:], mav)
    w1u0 = each(dot(NN), inv, each(lambda x, y: jnp.concatenate([x, y], axis=1), kap_s, mv))
    ar = each(dot(NN), a_ra, w1u0)
    r2 = each(lambda x, y: _unstack2(x - y[:, :LANES]), r_s, ar)
    y0 = each(lambda x, y: _unstack2(x - y[:, LANES:]), av, ar)
    w1_t = each(lambda x: x[:, :LANES].T, w1u0)
    u0_t = each(lambda x: x[:, LANES:].T, w1u0)
    v_t = each(lambda x: x.T, v_s)
    wua = each(dot(NN), each(lambda x, y: jnp.concatenate([x, y], axis=0), w1_t, u0_t), a_s)
    wa = each(lambda x: x[:LANES], wua)
    ua = each(lambda x: x[LANES:], wua)
    vk = each(dot(NN), v_t, k_s)
    g = each(lambda xe, x: jnp.where(row2 == col2, jnp.exp(xe), zero) - x, lp_end, wa)
    b = each(jnp.subtract, vk, ua)
    return list(zip(y0, r2, g, b))


RW_PAIRS_PER_STEP = 2


def _rwkv_body(lwf, rf, kf, vf, af, lwb, rb, kb, vb, ab, kk_ref, ka_ref, yf_ref, yb_ref, s_ref):
    @pl.when(pl.program_id(1) == 0)
    def _():
        s_ref[...] = jnp.zeros(s_ref.shape, F32)

    L = RWKV_CHUNK
    n = RW_CHUNKS_PER_STEP
    rows = lambda j: slice(j * L, (j + 1) * L)
    lanes = lambda q: slice(q * LANES, (q + 1) * LANES)
    refs = ((lwf, rf, kf, vf, af), (lwb, rb, kb, vb, ab))
    y_refs = (yf_ref, yb_ref)
    visit = [(d, j if d == 0 else n - 1 - j) for j in range(n) for d in (0, 1)]
    pairs = range(RW_PAIRS_PER_STEP)
    pre = _rwkv_chunks([tuple(ref[rows(j), lanes(q)] for ref in refs[d]) + (d,) for d, j in visit for q in pairs],
                       [kk_ref[:, lanes(q)] for _ in visit for q in pairs],
                       [ka_ref[:, lanes(q)] for _ in visit for q in pairs])
    s = [[s_ref[q, 0], s_ref[q, 1]] for q in pairs]
    chain = iter(pre)
    for d, j in visit:
        for q in pairs:
            y0, r2, g, b = next(chain)
            y_refs[d][rows(j), lanes(q)] = y0 + _mdot(r2, s[q][d], NT, RW_PASSES)
            s[q][d] = _mdot(s[q][d], g, NN, RW_PASSES) + b
    for q in range(RW_PAIRS_PER_STEP):
        s_ref[q, 0] = s[q][0]
        s_ref[q, 1] = s[q][1]


def rwkv_scan(r, k, v, lw, a, k_k, k_a, n_ctx):
    t, d = r.shape
    L = RWKV_CHUNK
    blk = RW_CHUNKS_PER_STEP * L
    w = RW_PAIRS_PER_STEP * LANES
    assert 2 * L == LANES and t % blk == 0 and n_ctx % blk == 0 and d % w == 0
    nc, nc0 = t // blk, n_ctx // blk
    nh = d // w
    fwd = pl.BlockSpec((blk, w), lambda h, c: (c, h))
    bwd = pl.BlockSpec((blk, w), lambda h, c: (_bwd_chunk(c, nc0, nc), h))
    bwd2 = pl.BlockSpec((blk, w), lambda h, c: (_bwd_chunk(c, nc0, nc), nh + h))
    par = pl.BlockSpec((1, w), lambda h, c: (0, h))
    return pl.pallas_call(
        _rwkv_body,
        grid=(nh, nc),
        in_specs=[fwd] * 5 + [bwd2, bwd, bwd, bwd, bwd2, par, par],
        out_specs=[fwd, bwd],
        out_shape=[jax.ShapeDtypeStruct((t, d), F32)] * 2,
        scratch_shapes=[pltpu.VMEM((RW_PAIRS_PER_STEP, 2, LANES, LANES), F32)],
        compiler_params=pltpu.CompilerParams(
            dimension_semantics=("arbitrary", "arbitrary"), vmem_limit_bytes=VMEM_LIMIT),
        name="rwkv7_scan",
    )(lw, r, k, v, a, lw, r, k, v, a, k_k.reshape(1, d), k_a.reshape(1, d))


MIX_TM = 256


def _mix_body(cur_ref, prev_ref, next_ref, mu_ref, *o_refs, nct, nt):
    i = pl.program_id(0)
    is_ctx = i < nct
    cur = cur_ref[...]
    tm, d = cur.shape
    qd = d // 4
    above = jnp.where(is_ctx | (i == nct), 0.0, prev_ref[...])
    below = jnp.where(is_ctx | (i == nt - 1), 0.0, next_ref[...])
    ext = jnp.concatenate([above, cur, below], axis=0)
    n = ext.shape[0]
    colid = lax.broadcasted_iota(jnp.int32, (n, 2 * qd), 0) % GRID_W
    left = jnp.where(is_ctx | (colid != 0), pltpu.roll(ext[:, :2 * qd], 1, 0), 0.0)
    colid3 = lax.broadcasted_iota(jnp.int32, (n, 3 * qd), 0) % GRID_W
    right = jnp.where(is_ctx | (colid3 != GRID_W - 1), pltpu.roll(ext[:, qd:], n - 1, 0), 0.0)
    mid = slice(GRID_W, GRID_W + tm)
    sh = jnp.concatenate([
        left[mid, :qd],
        jnp.where(is_ctx, left[mid, qd:], right[mid, :qd]),
        jnp.where(is_ctx, right[mid, qd:2 * qd], ext[0:tm, 2 * qd:3 * qd]),
        jnp.where(is_ctx, right[mid, 2 * qd:], ext[2 * GRID_W:2 * GRID_W + tm, 3 * qd:])], axis=1)
    dx = sh - cur
    for b, o_ref in enumerate(o_refs):
        o_ref[...] = (cur + dx * mu_ref[b:b + 1, :]).astype(o_ref.dtype)


def rwkv_mix(h, mu, n_ctx):
    t, d = h.shape
    tm = MIX_TM
    assert t % tm == 0 and n_ctx % tm == 0 and tm % GRID_W == 0
    r = tm // GRID_W
    nt, nu = t // tm, t // GRID_W
    nb = mu.shape[0]
    return pl.pallas_call(
        functools.partial(_mix_body, nct=n_ctx // tm, nt=nt),
        grid=(nt,),
        in_specs=[pl.BlockSpec((tm, d), lambda i: (i, 0)),
                  pl.BlockSpec((GRID_W, d), lambda i: (jnp.maximum(i * r - 1, 0), 0)),
                  pl.BlockSpec((GRID_W, d), lambda i: (jnp.minimum((i + 1) * r, nu - 1), 0)),
                  pl.BlockSpec((nb, d), lambda i: (0, 0))],
        out_specs=[pl.BlockSpec((tm, d), lambda i: (i, 0))] * nb,
        out_shape=[jax.ShapeDtypeStruct((t, d), BF16)] * nb,
        compiler_params=pltpu.CompilerParams(dimension_semantics=("arbitrary",), vmem_limit_bytes=VMEM_LIMIT),
        name="rwkv_mix",
    )(h, h, h, mu)


def _rwkv_out_body(yf_ref, yb_ref, r_ref, k_ref, v_ref, g_ref, af_ref, ab_ref, p_ref, e_ref, et_ref, w_ref, o_ref):
    e, et = e_ref[...], et_ref[...]
    head_sum = lambda x: _dot_01_rhs(_dot_01_rhs(x, e, 2), et, 2)
    k_a, r_k, gain, bias = (p_ref[n:n + 1, :] for n in range(4))
    y = yf_ref[...] + yb_ref[...]
    yc = y - head_sum(y) * (1.0 / RWKV_N)
    var = head_sum(yc * yc) * (1.0 / RWKV_N)
    yn = yc * lax.rsqrt(var + RWKV_GN_EPS) * gain + bias
    kbar = k_ref[...] * (1.0 + (0.5 * (af_ref[...] + ab_ref[...]) - 1.0) * k_a)
    bonus = head_sum(r_ref[...] * kbar * r_k) * v_ref[...]
    z = (yn + bonus) * g_ref[...]
    o_ref[...] = _d(z.astype(BF16), w_ref[...], NN)


def rwkv_out(yf, yb, r, k, v, g, a, k_a, r_k, lnx_g, lnx_b, w_out):
    t, d = yf.shape
    tm = LN_TM
    nh = d // RWKV_N
    row = pl.BlockSpec((tm, d), lambda i: (i, 0))
    const = lambda shape: pl.BlockSpec(shape, lambda i: (0, 0))
    e = (jnp.arange(d)[:, None] // RWKV_N == jnp.arange(LANES)[None, :]).astype(BF16)
    params = jnp.stack([k_a, r_k, lnx_g, lnx_b])
    return pl.pallas_call(
        _rwkv_out_body,
        grid=(t // tm,),
        in_specs=[row] * 6 + [row, pl.BlockSpec((tm, d), lambda i: (i, 1)),
                  const((4, d)), const((d, LANES)), const((LANES, d)), const((d, d))],
        out_specs=row,
        out_shape=jax.ShapeDtypeStruct((t, d), F32),
        compiler_params=pltpu.CompilerParams(dimension_semantics=("arbitrary",), vmem_limit_bytes=VMEM_LIMIT),
        name="rwkv_out",
    )(yf, yb, r, k, v, g, a, a, params, e, e.T, w_out.astype(BF16))


def _peer_stats(hx_ref, wk_ref, sc_ref, hxb_ref, n_ref, f0_ref, r1_ref, e1_ref):
    tm = hx_ref.shape[0]
    hxb_ref[...] = hx_ref[...].astype(F32).T.astype(BF16)
    rows_per = 4 * LANES

    def scores(c, carry):
        rows = pl.ds(pl.multiple_of(c * rows_per, rows_per), rows_per)
        sc_ref[rows, :] = _d(wk_ref[rows, :], hxb_ref[...], NN)
        return carry

    lax.fori_loop(0, wk_ref.shape[0] // rows_per, scores, 0)

    def block(tb, carry):
        lanes = pl.ds(pl.multiple_of(tb * LANES, LANES), LANES)
        neg = jnp.full((N_KEYS, LANES), -jnp.inf, F32)
        head_row = lax.broadcasted_iota(jnp.int32, (PEER_HEADS, LANES), 0)

        def extract(h, tops):
            tops = [list(t) for t in tops]
            rows = [pl.ds(pl.multiple_of((2 * h + p) * N_KEYS, N_KEYS), N_KEYS) for p in range(2)]
            cur = [sc_ref[rows[p], lanes] for p in range(2)]
            rank = jnp.full((N_KEYS, LANES), float(PEER_TOPK), F32)
            for a in range(PEER_TOPK):
                m = [jnp.max(c, axis=0, keepdims=True) for c in cur]
                hit = [c >= mm for c, mm in zip(cur, m)]
                cur = [jnp.where(ht, neg, c) for ht, c in zip(hit, cur)]
                rank = jnp.where(hit[1], float(a), rank)
                for p in range(2):
                    tops[p][a] = jnp.where(head_row == h, m[p], tops[p][a])
            r1_ref[h, :, lanes] = rank.astype(BF16)
            return tuple(tuple(t) for t in tops)

        zero = jnp.zeros((PEER_HEADS, LANES), F32)
        top0, top1 = lax.fori_loop(0, PEER_HEADS, extract, ((zero,) * PEER_TOPK,) * 2)
        cands = [top0[a] + top1[b]
                 for a in range(PEER_TOPK) for b in range(PEER_TOPK) if (a + 1) * (b + 1) <= PEER_TOPK]
        c_max = cands[0]
        z = jnp.zeros_like(c_max)
        tau = c_max
        for a in range(PEER_TOPK):
            tau = functools.reduce(jnp.maximum, cands)
            z = z + jnp.exp(tau - c_max)
            cands = [jnp.where(cd >= tau, -jnp.inf, cd) for cd in cands]
        inv_z = 1.0 / z

        def factors(h, carry):
            row_of = lambda x: jnp.max(jnp.where(head_row == h, x, -jnp.inf), axis=0, keepdims=True)
            s0 = sc_ref[pl.ds(pl.multiple_of(2 * h * N_KEYS, N_KEYS), N_KEYS), lanes]
            s1 = sc_ref[pl.ds(pl.multiple_of((2 * h + 1) * N_KEYS, N_KEYS), N_KEYS), lanes]
            tau_h = row_of(tau)
            n = jnp.zeros((N_KEYS, LANES), F32)
            for b in range(PEER_TOPK):
                n = jnp.where(s0 + row_of(top1[b]) >= tau_h, float(b + 1), n)
            n_ref[h, :, lanes] = n
            f0_ref[h, :, lanes] = jnp.exp(s0 - row_of(top0[0])) * row_of(inv_z)
            e1_ref[h, :, lanes] = jnp.exp(s1 - row_of(top1[0])).astype(BF16)
            return carry

        lax.fori_loop(0, PEER_HEADS, factors, 0)
        return carry

    lax.fori_loop(0, tm // LANES, block, 0)


PEER_I_GROUP = 4
PEER_STREAMS = 2


def _peer_body(hx_ref, wk_ref, *rest):
    u_refs, vt_refs = rest[:PEER_STREAMS], rest[PEER_STREAMS:2 * PEER_STREAMS]
    o_ref, sc_ref, hxb_ref, n_ref, f0_ref, r1_ref, e1_ref, w_ref, acc_ref = rest[2 * PEER_STREAMS:]
    e = pl.program_id(1)
    tm = hx_ref.shape[0]
    nsb = tm // PEER_SB

    @pl.when(e == 0)
    def _():
        _peer_stats(hx_ref, wk_ref, sc_ref, hxb_ref, n_ref, f0_ref, r1_ref, e1_ref)
        acc_ref[...] = jnp.zeros(acc_ref.shape, F32)

    i_rows = pl.ds(pl.multiple_of(e * PEER_I_BLOCK, PEER_I_BLOCK), PEER_I_BLOCK)

    def activations(sb):
        parts = []
        for u_ref in u_refs:
            act = _d(u_ref[...], hxb_ref[:, sb * PEER_SB:(sb + 1) * PEER_SB], NN)
            act = act.astype(BF16)
            parts.append(0.5 * act * (1.0 + lax.erf(act * (2.0 ** -0.5))))
        return parts

    def gates(sb, act):
        for hb in range(PEER_SB // LANES):
            lanes = slice(sb * PEER_SB + hb * LANES, sb * PEER_SB + (hb + 1) * LANES)
            sub = slice(hb * LANES, (hb + 1) * LANES)
            n8 = [n_ref[h, i_rows, lanes] for h in range(PEER_HEADS)]
            f8 = [f0_ref[h, i_rows, lanes] for h in range(PEER_HEADS)]
            bcast = lambda x, ii: jnp.broadcast_to(x[ii:ii + 1], (N_KEYS, LANES)).astype(BF16)
            for ig in range(0, PEER_I_BLOCK, PEER_I_GROUP):
                g = [jnp.zeros((N_KEYS, LANES), BF16) for _ in range(PEER_I_GROUP)]
                for h in range(PEER_HEADS):
                    r1 = r1_ref[h, :, lanes]
                    e1 = e1_ref[h, :, lanes]
                    for k in range(PEER_I_GROUP):
                        ii = ig + k
                        g[k] = g[k] + jnp.where(r1 < bcast(n8[h], ii), e1 * bcast(f8[h], ii), jnp.zeros_like(e1))
                for k in range(PEER_I_GROUP):
                    rows = slice((ig + k) * N_KEYS, (ig + k + 1) * N_KEYS)
                    part, off = divmod((ig + k) * N_KEYS, PEER_TE // PEER_STREAMS)
                    w_ref[sb, rows, sub] = g[k] * act[part][off:off + N_KEYS, sub]

    def accumulate(sb):
        cols = slice(sb * PEER_SB, (sb + 1) * PEER_SB)
        dr = acc_ref.shape[0] // PEER_STREAMS
        for k, vt_ref in enumerate(vt_refs):
            acc_ref[k * dr:(k + 1) * dr, cols] += _d(vt_ref[...], w_ref[sb], NN)

    act = activations(0)
    for sb in range(nsb):
        nxt = activations(sb + 1) if sb + 1 < nsb else None
        gates(sb, act)
        accumulate(sb)
        act = nxt

    @pl.when(e == pl.num_programs(1) - 1)
    def _():
        o_ref[...] = acc_ref[...].T


def _fold_body(k_ref, w_ref, o_ref):
    o_ref[...] = _mdot(k_ref[...], w_ref[...], NN, 6)


def peer_fold_keys(wq, keys):
    d = wq.shape[0]
    nhp, nk, dk = keys.shape
    wqt = wq.T.reshape(nhp, dk, d)
    return pl.pallas_call(
        _fold_body,
        grid=(nhp,),
        in_specs=[pl.BlockSpec((None, nk, dk), lambda i: (i, 0, 0)),
                  pl.BlockSpec((None, dk, d), lambda i: (i, 0, 0))],
        out_specs=pl.BlockSpec((nk, d), lambda i: (i, 0)),
        out_shape=jax.ShapeDtypeStruct((nhp * nk, d), F32),
        compiler_params=pltpu.CompilerParams(dimension_semantics=("arbitrary",), vmem_limit_bytes=VMEM_LIMIT),
        name="peer_fold_keys",
    )(keys, wqt)


def peer(hx, wk, u_bf, vt_bf):
    t, d = hx.shape
    tm = PEER_TM
    assert t % tm == 0
    ne = u_bf.shape[0] // PEER_TE
    h = PEER_HEADS
    ns = PEER_STREAMS
    return pl.pallas_call(
        _peer_body,
        grid=(t // tm, ne),
        in_specs=[pl.BlockSpec((tm, d), lambda i, e: (i, 0), pipeline_mode=pl.Buffered(1)),
                  pl.BlockSpec(wk.shape, lambda i, e: (0, 0), pipeline_mode=pl.Buffered(1)),
                  *[pl.BlockSpec((PEER_TE // ns, d), functools.partial(lambda i, e, k: (e * ns + k, 0), k=k))
                    for k in range(ns)],
                  *[pl.BlockSpec((None, d // ns, PEER_TE), functools.partial(lambda i, e, k: (e, k, 0), k=k))
                    for k in range(ns)]],
        out_specs=pl.BlockSpec((tm, d), lambda i, e: (i, 0)),
        out_shape=jax.ShapeDtypeStruct((t, d), F32),
        scratch_shapes=[pltpu.VMEM((2 * h * N_KEYS, tm), F32),
                        pltpu.VMEM((d, tm), BF16),
                        pltpu.VMEM((h, N_KEYS, tm), F32), pltpu.VMEM((h, N_KEYS, tm), F32),
                        pltpu.VMEM((h, N_KEYS, tm), BF16), pltpu.VMEM((h, N_KEYS, tm), BF16),
                        pltpu.VMEM((tm // PEER_SB, PEER_TE, PEER_SB), BF16), pltpu.VMEM((d, tm), F32)],
        compiler_params=pltpu.CompilerParams(
            dimension_semantics=("arbitrary", "arbitrary"), vmem_limit_bytes=VMEM_LIMIT),
        name="peer_dense",
    )(hx, wk.astype(BF16), *([u_bf] * ns), *([vt_bf] * ns))


LN_TM = 256


def _ln_mod_body(xs_ref, y_ref, mg_ref, mn_ref, lng_ref, lnb_ref, xo_ref, ho_ref, *, gate_col, mod_col, nct, nt):
    i = pl.program_id(0)
    d = xs_ref.shape[1]
    is_ctx = i < nct

    def pick(ref, col):
        return jnp.where(is_ctx, ref[1:2, col * d:(col + 1) * d], ref[0:1, col * d:(col + 1) * d])

    z = DN_ALPHA * xs_ref[...] + pick(mg_ref, gate_col) * y_ref[...]
    mu = jnp.mean(z, axis=-1, keepdims=True)
    zc = z - mu
    var = jnp.mean(zc * zc, axis=-1, keepdims=True)
    xn = zc * lax.rsqrt(var + LN_EPS) * lng_ref[...] + lnb_ref[...]
    xo_ref[...] = xn
    h = xn * (1.0 + pick(mn_ref, mod_col + 1)) + pick(mn_ref, mod_col)
    ho_ref[...] = jnp.where(i < nt, h, 0.0).astype(ho_ref.dtype)


def ln_mod(xs, y, mod_gate, gate_col, mod_next, mod_col, ln_g, ln_b, n_ctx, pad_to, h_dtype):
    t, d = xs.shape
    tm = LN_TM
    assert t % tm == 0 and n_ctx % tm == 0 and pad_to % tm == 0 and pad_to >= t
    nt = t // tm
    row = lambda i: (jnp.minimum(i, nt - 1), 0)
    full = lambda a: pl.BlockSpec(a.shape, lambda i: (0, 0))
    return pl.pallas_call(
        functools.partial(_ln_mod_body, gate_col=gate_col, mod_col=mod_col, nct=n_ctx // tm, nt=nt),
        grid=(pad_to // tm,),
        in_specs=[pl.BlockSpec((tm, d), row), pl.BlockSpec((tm, d), row), full(mod_gate), full(mod_next),
                  pl.BlockSpec((1, d), lambda i: (0, 0)), pl.BlockSpec((1, d), lambda i: (0, 0))],
        out_specs=[pl.BlockSpec((tm, d), row), pl.BlockSpec((tm, d), lambda i: (i, 0))],
        out_shape=[jax.ShapeDtypeStruct((t, d), F32), jax.ShapeDtypeStruct((pad_to, d), h_dtype)],
        compiler_params=pltpu.CompilerParams(dimension_semantics=("arbitrary",), vmem_limit_bytes=VMEM_LIMIT),
        name="ln_mod",
    )(xs, y, mod_gate, mod_next, ln_g.reshape(1, d), ln_b.reshape(1, d))


def _pad_cols(w, n):
    return jnp.pad(w, ((0, 0), (0, n - w.shape[1])))


def _mlstm_layer(h, n_ctx, w_in, b_in, conv_w, conv_b, hn_g, w_out):
    d = D_MODEL
    t = h.shape[0]
    qk_pre = matmul(h, w_in[:, :2 * d], b_in[:2 * d])
    v = matmul(h, w_in[:, 2 * d:3 * d], b_in[2 * d:3 * d], out_dtype=BF16)
    o = matmul(h, w_in[:, 3 * d:4 * d], b_in[3 * d:4 * d], act="sigmoid", out_dtype=BF16)
    g = matmul(h, _pad_cols(w_in[:, 4 * d:], LANES), jnp.pad(b_in[4 * d:], (0, LANES - 4 * MLSTM_HEADS)))
    g = g[:, :4 * MLSTM_HEADS].reshape(t, 4, MLSTM_HEADS)
    g = jnp.concatenate([g[:, :2], jax.nn.log_sigmoid(g[:, 2:])], axis=1)
    gh = jnp.transpose(g, (2, 0, 1))
    gcol = jnp.pad(gh, ((0, 0), (0, 0), (0, LANES - 4)))
    grow = jnp.pad(jnp.transpose(gh, (0, 2, 1)), ((0, 0), (0, 4), (0, 0)))
    qk = mlstm_conv(qk_pre, conv_w, conv_b, n_ctx)
    hf, hb = mlstm_scan(qk, v, gcol, grow, n_ctx)
    return mlstm_out(hf, hb, o, hn_g, w_out)


def _rwkv_layer(h, n_ctx, mu, w_rkv, w0, w1, w2, a0, a1, a2, g1, g2, k_k, k_a, r_k, lnx_g, lnx_b, w_out):
    d = D_MODEL
    xm = rwkv_mix(h, mu, n_ctx)
    r = matmul(xm[0], w_rkv[0])
    k = matmul(xm[1], w_rkv[1])
    v = matmul(xm[2], w_rkv[2])

    def lora_pair(x, w_in, w_mid, bias, act_mid, act_out):
        rank = w_in.shape[-1]
        w_a = _pad_cols(jnp.concatenate([w_in[0], w_in[1]], axis=1), LANES)
        zpad = jnp.zeros((rank, d), F32)
        w_b = jnp.concatenate([jnp.concatenate([w_mid[0], zpad], axis=1),
                               jnp.concatenate([zpad, w_mid[1]], axis=1)], axis=0)
        w_b = jnp.pad(w_b, ((0, LANES - 2 * rank), (0, 0)))
        mid = matmul(x, w_a, act=act_mid, out_dtype=BF16)
        return matmul(mid, w_b, jnp.concatenate([bias[0], bias[1]]), act=act_out)

    lw = lora_pair(xm[3], w1, w2, w0, "tanh", "logdecay")
    a = lora_pair(xm[4], a1, a2, a0, None, "sigmoid")
    gpad = 2 * LANES
    gg = matmul(xm[5], _pad_cols(g1, gpad), act="sigmoid", out_dtype=BF16)
    g = matmul(gg, jnp.pad(g2, ((0, gpad - g1.shape[1]), (0, 0))))
    yf, yb = rwkv_scan(r, k, v, lw, a, k_k, k_a, n_ctx)
    return rwkv_out(yf, yb, r, k, v, g, a, k_a, r_k, lnx_g, lnx_b, w_out)


def _forward(x, c, ctx, c_ctx, ada_w, ada_b, ln_g, ln_b,
             ml_w_in, ml_b_in, ml_conv_w, ml_conv_b, ml_hn_g, ml_w_out,
             rw_mu, rw_w_rkv, rw_w0, rw_w1, rw_w2, rw_a0, rw_a1, rw_a2, rw_g1, rw_g2,
             rw_k_k, rw_k_a, rw_r_k, rw_lnx_g, rw_lnx_b, rw_w_out,
             pk_wq, pk_keys, pk_u, pk_v):
    d = D_MODEL
    n_ctx = ctx.shape[1]
    xs = jnp.concatenate([ctx[0], x[0]], axis=0)
    t = xs.shape[0]
    t_pad = -(-t // PEER_TM) * PEER_TM
    s_in = jnp.zeros((8, d), F32).at[0].set(jax.nn.silu(c[0])).at[1].set(jax.nn.silu(c_ctx))
    depth = ada_w.shape[0]
    mods = [matmul(s_in, ada_w[i], ada_b[i], passes=3) for i in range(depth)]
    is_ctx = (jnp.arange(t) < n_ctx)[:, None]
    m0 = [jnp.where(is_ctx, mods[0][1, n * d:(n + 1) * d], mods[0][0, n * d:(n + 1) * d]) for n in range(2)]
    mixer_dtype = lambda i: BF16 if i % 2 == 0 else F32
    h = (xs * (1.0 + m0[1]) + m0[0]).astype(mixer_dtype(0))
    for i in range(depth):
        j = i // 2
        if i % 2 == 0:
            y = _mlstm_layer(h, n_ctx, ml_w_in[j], ml_b_in[j], ml_conv_w[j], ml_conv_b[j],
                             ml_hn_g[j], ml_w_out[j])
        else:
            y = _rwkv_layer(h, n_ctx, rw_mu[j], rw_w_rkv[j], rw_w0[j], rw_w1[j], rw_w2[j],
                            rw_a0[j], rw_a1[j], rw_a2[j], rw_g1[j], rw_g2[j], rw_k_k[j],
                            rw_k_a[j], rw_r_k[j], rw_lnx_g[j], rw_lnx_b[j], rw_w_out[j])
        xs, h = ln_mod(xs, y, mods[i], 2, mods[i], 3, ln_g[i, 0], ln_b[i, 0], n_ctx, t_pad, BF16)
        wk = peer_fold_keys(pk_wq[i], pk_keys[i].reshape(2 * PEER_HEADS, N_KEYS, PEER_DQ // 2))
        vt = jnp.swapaxes(pk_v[i].astype(BF16).reshape(-1, PEER_TE, d), 1, 2)
        y = peer(h, wk, pk_u[i].astype(BF16), vt)
        xs, h = ln_mod(xs, y, mods[i], 5, mods[min(i + 1, depth - 1)], 0, ln_g[i, 1], ln_b[i, 1], n_ctx, t,
                       mixer_dtype(i + 1))
    return xs[n_ctx:][None]


def kernel(x, c, ctx, c_ctx, ada_w, ada_b, ln_g, ln_b, ml_w_in, ml_b_in, ml_conv_w, ml_conv_b, ml_hn_g, ml_w_out, rw_mu, rw_w_rkv, rw_w0, rw_w1, rw_w2, rw_a0, rw_a1, rw_a2, rw_g1, rw_g2, rw_k_k, rw_k_a, rw_r_k, rw_lnx_g, rw_lnx_b, rw_w_out, pk_wq, pk_keys, pk_u, pk_v):
    return _forward(x, c, ctx, c_ctx, ada_w, ada_b, ln_g, ln_b,
                    ml_w_in, ml_b_in, ml_conv_w, ml_conv_b, ml_hn_g, ml_w_out,
                    rw_mu, rw_w_rkv, rw_w0, rw_w1, rw_w2, rw_a0, rw_a1, rw_a2, rw_g1, rw_g2,
                    rw_k_k, rw_k_a, rw_r_k, rw_lnx_g, rw_lnx_b, rw_w_out,
                    pk_wq, pk_keys, pk_u, pk_v)
```

```python
import functools

import jax
import jax.numpy as jnp
from jax import lax
from jax.experimental import pallas as pl
from jax.experimental.pallas import tpu as pltpu

F32 = jnp.float32
BF16 = jnp.bfloat16

D_MODEL = 1024
DEPTH = 4
GRID_W = 64
DN_ALPHA = (2.0 * DEPTH) ** 0.25
LN_EPS = 1e-5

MLSTM_HEADS = 4
MLSTM_DH = D_MODEL // MLSTM_HEADS
MLSTM_CHUNK = 128
M_INIT = -1e30

RWKV_N = 64
RWKV_HEADS = D_MODEL // RWKV_N
RWKV_CHUNK = 64
RWKV_GN_EPS = 64e-5

N_KEYS = 128
PEER_HEADS = 8
PEER_DQ = 256
PEER_TOPK = 16
PEER_I_BLOCK = 16
PEER_TE = PEER_I_BLOCK * N_KEYS
PEER_TM = 768
PEER_SB = 256

LANES = 128
VMEM_LIMIT = 62 * 1024 * 1024

NN = ((1,), (0,))
NT = ((1,), (1,))


def _split(x, n):
    parts = []
    r = x.astype(F32)
    for i in range(n):
        p = r.astype(BF16)
        parts.append(p)
        if i + 1 < n:
            r = r - p.astype(F32)
    return parts


def _d(a, b, dims):
    return lax.dot_general(a, b, (dims, ((), ())), preferred_element_type=F32)


def _mdot(a, b, dims, passes):
    if passes == 1:
        return _d(a.astype(BF16), b.astype(BF16), dims)
    if passes == 3:
        a0, a1 = _split(a, 2)
        b0, b1 = _split(b, 2)
        return (_d(a0, b1, dims) + _d(a1, b0, dims)) + _d(a0, b0, dims)
    a0, a1, a2 = _split(a, 3)
    b0, b1, b2 = _split(b, 3)
    lo = (_d(a0, b2, dims) + _d(a2, b0, dims)) + _d(a1, b1, dims)
    mid = _d(a0, b1, dims) + _d(a1, b0, dims)
    return (lo + mid) + _d(a0, b0, dims)


def _dot_01_lhs(m01, x):
    mb = m01.astype(BF16)
    x0, x1, x2 = _split(x, 3)
    return (_d(mb, x2, NN) + _d(mb, x1, NN)) + _d(mb, x0, NN)


def _dot_01_rhs(x, m01, pieces=3):
    mb = m01.astype(BF16)
    return functools.reduce(jnp.add, [_d(xp, mb, NN) for xp in reversed(_split(x, pieces))])


def _pick(n, cands):
    for c in cands:
        if n % c == 0:
            return c
    raise ValueError(f"no tile for {n}")


_ACTS = {None: lambda x: x, "sigmoid": jax.nn.sigmoid, "tanh": jnp.tanh,
         "logdecay": lambda x: -(2.718281828459045 ** -0.5) * jax.nn.sigmoid(x)}


def _mm_body(x_ref, w_ref, b_ref, o_ref, *, passes, act):
    o_ref[...] = _ACTS[act](_mdot(x_ref[...], w_ref[...], NN, passes) + b_ref[...]).astype(o_ref.dtype)


def matmul(x, w, b=None, *, passes=1, act=None, out_dtype=F32):
    m, k = x.shape
    n = w.shape[1]
    assert n % LANES == 0 and w.shape[0] == k
    tm = m if m <= 1024 else _pick(m, (640, 512, 384, 256, 128))
    tn = _pick(n, (1024, 768, 640, 512, 384, 256, 128))
    if b is None:
        b = jnp.zeros((n,), F32)
    if passes == 1:
        w = w.astype(BF16)
    return pl.pallas_call(
        functools.partial(_mm_body, passes=passes, act=act),
        grid=(m // tm, n // tn),
        in_specs=[pl.BlockSpec((tm, k), lambda i, j: (i, 0)),
                  pl.BlockSpec((k, tn), lambda i, j: (0, j)),
                  pl.BlockSpec((1, tn), lambda i, j: (0, j))],
        out_specs=pl.BlockSpec((tm, tn), lambda i, j: (i, j)),
        out_shape=jax.ShapeDtypeStruct((m, n), out_dtype),
        compiler_params=pltpu.CompilerParams(
            dimension_semantics=("arbitrary", "arbitrary"), vmem_limit_bytes=VMEM_LIMIT),
        name="proj_matmul",
    )(x, w, b.reshape(1, n).astype(F32))


CONV_TM = 256
CONV_CB = 512


def _conv_body(cur_ref, prev_ref, next_ref, w_ref, b_ref, o_ref, *, nct, nt, q_blocks, q_scale):
    i = pl.program_id(0)
    j = pl.program_id(1)
    is_ctx = i < nct
    tm = cur_ref.shape[0]
    above = jnp.where(is_ctx | (i == nct), 0.0, prev_ref[...])
    below = jnp.where(is_ctx | (i == nt - 1), 0.0, next_ref[...])
    ext = jnp.concatenate([above, cur_ref[...], below], axis=0)
    n = ext.shape[0]
    colid = lax.broadcasted_iota(jnp.int32, ext.shape, 0) % GRID_W
    left = jnp.where(is_ctx | (colid != 0), pltpu.roll(ext, 1, 0), 0.0)
    right = jnp.where(is_ctx | (colid != GRID_W - 1), pltpu.roll(ext, n - 1, 0), 0.0)
    w = w_ref[...]
    acc = jnp.zeros((tm, ext.shape[1]), F32) + b_ref[...]
    for di in range(3):
        rows = slice(di * GRID_W, di * GRID_W + tm)
        tap = left[rows] * w[3 * di:3 * di + 1] + ext[rows] * w[3 * di + 1:3 * di + 2] \
            + right[rows] * w[3 * di + 2:3 * di + 3]
        acc = acc + (tap if di == 1 else jnp.where(is_ctx, 0.0, tap))
    y = acc * jax.nn.sigmoid(acc)
    o_ref[...] = (y * jnp.where(j < q_blocks, q_scale, 1.0)).astype(o_ref.dtype)


def mlstm_conv(qk_pre, conv_w, conv_b, n_ctx):
    t, c = qk_pre.shape
    tm, cb = CONV_TM, CONV_CB
    assert t % tm == 0 and n_ctx % tm == 0 and tm % GRID_W == 0 and c % (2 * cb) == 0
    r = tm // GRID_W
    nt, nu = t // tm, t // GRID_W
    return pl.pallas_call(
        functools.partial(_conv_body, nct=n_ctx // tm, nt=nt, q_blocks=c // (2 * cb), q_scale=MLSTM_DH ** -0.5),
        grid=(nt, c // cb),
        in_specs=[pl.BlockSpec((tm, cb), lambda i, j: (i, j)),
                  pl.BlockSpec((GRID_W, cb), lambda i, j: (jnp.maximum(i * r - 1, 0), j)),
                  pl.BlockSpec((GRID_W, cb), lambda i, j: (jnp.minimum((i + 1) * r, nu - 1), j)),
                  pl.BlockSpec((9, cb), lambda i, j: (0, j)),
                  pl.BlockSpec((1, cb), lambda i, j: (0, j))],
        out_specs=pl.BlockSpec((tm, cb), lambda i, j: (i, j)),
        out_shape=jax.ShapeDtypeStruct((t, c), BF16),
        compiler_params=pltpu.CompilerParams(
            dimension_semantics=("arbitrary", "arbitrary"), vmem_limit_bytes=VMEM_LIMIT),
        name="mlstm_conv",
    )(qk_pre, qk_pre, qk_pre, conv_w.reshape(9, c), conv_b.reshape(1, c))


def _mlstm_out_body(hf_ref, hb_ref, o_ref, g_ref, w_ref, y_ref):
    h = hf_ref[...] + hb_ref[...]
    parts = []
    for a in range(MLSTM_HEADS):
        x = h[:, a * MLSTM_DH:(a + 1) * MLSTM_DH]
        mu = jnp.mean(x, axis=-1, keepdims=True)
        xc = x - mu
        var = jnp.mean(xc * xc, axis=-1, keepdims=True)
        parts.append(xc * lax.rsqrt(var + LN_EPS))
    hn = jnp.concatenate(parts, axis=1)
    z = o_ref[...].astype(F32) * hn * g_ref[...]
    y_ref[...] = _d(z.astype(BF16), w_ref[...], NN)


def mlstm_out(hf, hb, o, hn_g, w_out):
    t, d = hf.shape
    tm = LN_TM
    row = pl.BlockSpec((tm, d), lambda i: (i, 0))
    return pl.pallas_call(
        _mlstm_out_body,
        grid=(t // tm,),
        in_specs=[row, row, row, pl.BlockSpec((1, d), lambda i: (0, 0)), pl.BlockSpec((d, d), lambda i: (0, 0))],
        out_specs=row,
        out_shape=jax.ShapeDtypeStruct((t, d), F32),
        compiler_params=pltpu.CompilerParams(dimension_semantics=("arbitrary",), vmem_limit_bytes=VMEM_LIMIT),
        name="mlstm_out",
    )(hf, hb, o, hn_g.reshape(1, d), w_out.astype(BF16))


ML_HEADS_PER_STEP = 4


def _mlstm_body(qf, kf, vf, gcf, grf, qb, kb, vb, gcb, grb, hf_ref, hb_ref, c_ref, n_ref, m_ref):
    @pl.when(pl.program_id(1) == 0)
    def _():
        c_ref[...] = jnp.zeros(c_ref.shape, F32)
        n_ref[...] = jnp.zeros(n_ref.shape, F32)
        m_ref[...] = jnp.full(m_ref.shape, M_INIT, F32)

    dh = MLSTM_DH
    cs = [(a, d) for a in range(ML_HEADS_PER_STEP) for d in (0, 1)]
    each = lambda f, *ls: [f(*xs) for xs in zip(*ls)]
    cols = lambda a: slice(a * dh, (a + 1) * dh)
    qs, ks, vs, gcs, grs = (qf, qb), (kf, kb), (vf, vb), (gcf, gcb), (grf, grb)
    qb16 = [qs[d][:, cols(a)] for a, d in cs]
    kb16 = [ks[d][:, cols(a)] for a, d in cs]
    vb16 = [vs[d][:, cols(a)] for a, d in cs]
    gc = [gcs[d][a] for a, d in cs]
    gr = [grs[d][a] for a, d in cs]
    q, k = (each(lambda x: x.astype(F32), x16) for x16 in (qb16, kb16))
    L = q[0].shape[0]
    row = lax.broadcasted_iota(jnp.int32, (L, L), 0)
    col = lax.broadcasted_iota(jnp.int32, (L, L), 1)
    seen = (col <= row, col >= row)
    tri = [jnp.where(m, 1.0, 0.0) for m in seen]
    b_col = [_dot_01_lhs(tri[d], g)[:, 2 + d:3 + d] for (a, d), g in zip(cs, gc)]
    b_row = [_dot_01_rhs(g, tri[1 - d])[2 + d:3 + d, :] for (a, d), g in zip(cs, gr)]
    ig_col = [g[:, d:d + 1] for (a, d), g in zip(cs, gc)]
    ig_row = [g[d:d + 1, :] for (a, d), g in zip(cs, gr)]
    m_st = [m_ref[a, d, 0:1, 0:1] for a, d in cs]
    c_st = [c_ref[a, d] for a, d in cs]
    n_st = [n_ref[a, d] for a, d in cs]

    dlog = [jnp.where(seen[d], bc - br + ir, -jnp.inf) for (a, d), bc, br, ir in zip(cs, b_col, b_row, ig_row)]
    m_inter = each(jnp.add, b_col, m_st)
    m_t = each(lambda mi, dl: jnp.maximum(mi, jnp.max(dl, axis=1, keepdims=True)), m_inter, dlog)
    qk = each(lambda x, y: _d(x, y, NT), qb16, kb16)
    s = each(lambda x, dl, mt: x * jnp.exp(dl - mt), qk, dlog, m_t)
    dec = each(lambda mi, mt: jnp.exp(mi - mt), m_inter, m_t)
    sv = each(lambda x, y: _d(x.astype(BF16), y, NN), s, vb16)
    qc = each(lambda x, y: _d(x, y.astype(BF16), NN), qb16, c_st)
    num = each(lambda x, dc, y: x + dc * y, sv, dec, qc)
    den = each(lambda x, dc, qq, nn: jnp.sum(x, axis=1, keepdims=True) + dc * jnp.sum(qq * nn, axis=1, keepdims=True),
               s, dec, q, n_st)
    h = each(lambda nu, de, mt: nu / jnp.maximum(jnp.abs(de), jnp.exp(-mt)), num, den, m_t)
    h_refs = (hf_ref, hb_ref)
    for (a, d), x in zip(cs, h):
        h_refs[d][:, cols(a)] = x

    b_last = [bc[0:1, :] if d else bc[L - 1:L, :] for (a, d), bc in zip(cs, b_col)]
    w_c = each(lambda bl, bc, ic: bl - bc + ic, b_last, b_col, ig_col)
    m_new = each(lambda bl, ms, w: jnp.maximum(bl + ms, jnp.max(w, axis=0, keepdims=True)), b_last, m_st, w_c)
    a_c = each(lambda w, mn: jnp.exp(w - mn), w_c, m_new)
    g_prev = each(lambda bl, ms, mn: jnp.exp(bl + ms - mn), b_last, m_st, m_new)
    ak = each(jnp.multiply, a_c, k)
    kv = each(lambda x, y: _d(x.T.astype(BF16), y, NN), ak, vb16)
    for i, (a, d) in enumerate(cs):
        c_ref[a, d] = g_prev[i] * c_st[i] + kv[i]
        n_ref[a, d] = g_prev[i] * n_st[i] + jnp.sum(ak[i], axis=0, keepdims=True)
        m_ref[a, d] = jnp.broadcast_to(m_new[i], m_ref.shape[2:])


def _bwd_chunk(c, nc0, nc):
    return jnp.where(c < nc0, nc0 - 1 - c, nc - 1 - (c - nc0))


def mlstm_scan(qk, v, gcol, grow, n_ctx):
    t = v.shape[0]
    L, dh, hs = MLSTM_CHUNK, MLSTM_DH, ML_HEADS_PER_STEP
    ng = MLSTM_HEADS // hs
    nc, nc0 = t // L, n_ctx // L
    idf = lambda c: c
    idb = lambda c: _bwd_chunk(c, nc0, nc)
    qkv = lambda f, off: pl.BlockSpec((L, hs * dh), lambda h, c: (f(c), off + h))
    gc_spec = lambda f: pl.BlockSpec((hs, L, LANES), lambda h, c: (h, f(c), 0))
    gr_spec = lambda f: pl.BlockSpec((hs, 8, L), lambda h, c: (h, 0, f(c)))
    return pl.pallas_call(
        _mlstm_body,
        grid=(ng, nc),
        in_specs=[qkv(idf, 0), qkv(idf, ng), qkv(idf, 0), gc_spec(idf), gr_spec(idf),
                  qkv(idb, 0), qkv(idb, ng), qkv(idb, 0), gc_spec(idb), gr_spec(idb)],
        out_specs=[qkv(idf, 0), qkv(idb, 0)],
        out_shape=[jax.ShapeDtypeStruct((t, D_MODEL), F32)] * 2,
        scratch_shapes=[pltpu.VMEM((hs, 2, dh, dh), F32), pltpu.VMEM((hs, 2, 1, dh), F32),
                        pltpu.VMEM((hs, 2, 8, LANES), F32)],
        compiler_params=pltpu.CompilerParams(
            dimension_semantics=("arbitrary", "arbitrary"), vmem_limit_bytes=VMEM_LIMIT),
        name="mlstm_scan",
    )(qk, qk, v, gcol, grow, qk, qk, v, gcol, grow)


RW_PASSES = 1
RW_CHUNKS_PER_STEP = 4


def _stack2(x, lane_head):
    return jnp.concatenate([jnp.where(lane_head == 0, x, 0.0), jnp.where(lane_head == 1, x, 0.0)], axis=0)


def _unstack2(x):
    L = x.shape[0] // 2
    return x[:L] + x[L:]


def _rwkv_chunks(chains, k_k, k_a):
    L = chains[0][0].shape[0]
    n2 = 2 * L
    p = RW_PASSES
    ds = [c[5] for c in chains]
    each = lambda f, *ls: [f(*xs) for xs in zip(*ls)]
    dot = lambda dims: (lambda a, b: _mdot(a, b, dims, p))
    lw, r, k_raw, v, a = ([c[i] for c in chains] for i in range(5))

    lane_r = lax.broadcasted_iota(jnp.int32, (LANES, LANES), 0) // RWKV_N
    lane_c = lax.broadcasted_iota(jnp.int32, (LANES, LANES), 1) // RWKV_N
    head_ones = jnp.where(lane_r == lane_c, 1.0, 0.0)
    kkr = each(jnp.multiply, k_raw, k_k)
    ss = each(lambda x: _dot_01_rhs(x * x, head_ones), kkr)
    kap = each(lambda x, q: x / jnp.maximum(jnp.sqrt(q), 1e-12), kkr, ss)
    alp = each(jnp.multiply, kap, a)
    k = each(lambda x, y, z: x * (1.0 + (y - 1.0) * z), k_raw, a, k_a)

    row = lax.broadcasted_iota(jnp.int32, (L, L), 0)
    col = lax.broadcasted_iota(jnp.int32, (L, L), 1)
    tris = (jnp.where(col <= row, 1.0, 0.0), jnp.where(col >= row, 1.0, 0.0))
    lp = [_dot_01_lhs(tris[d], x) for d, x in zip(ds, lw)]
    lp_end = [x[0:1, :] if d else x[L - 1:L, :] for d, x in zip(ds, lp)]
    e_neg = each(lambda x: jnp.exp(-x), lp)
    e_end = each(lambda x, xe: jnp.exp(xe - x), lp, lp_end)
    kap_t = each(lambda x, y, z: x * jnp.exp(y - z), kap, lp, lw)
    r_t = each(lambda x, y: x * jnp.exp(y), r, lp)
    k_h = each(jnp.multiply, k, e_neg)
    a_h = each(jnp.multiply, alp, e_neg)
    k_e = each(jnp.multiply, k, e_end)
    a_e = each(jnp.multiply, alp, e_end)

    lane_head = lax.broadcasted_iota(jnp.int32, (L, LANES), 1) // RWKV_N
    st = lambda x: _stack2(x, lane_head)
    kap_s, r_s, v_s, k_s, a_s = (each(st, x) for x in (kap_t, r_t, v, k_e, a_e))
    rhs_k = each(lambda x: jnp.concatenate([x, x], axis=0), k_h)
    rhs_a = each(lambda x: jnp.concatenate([x, x], axis=0), a_h)

    row2 = lax.broadcasted_iota(jnp.int32, (n2, n2), 0)
    col2 = lax.broadcasted_iota(jnp.int32, (n2, n2), 1)
    same_head = (row2 // L) == (col2 // L)
    strict = (same_head & (col2 < row2), same_head & (col2 > row2))
    incl = (same_head & (col2 <= row2), same_head & (col2 >= row2))
    zero = jnp.zeros((n2, n2), F32)
    masked = lambda masks: (lambda d, x: jnp.where(masks[d], x, zero))
    kr_s = each(lambda x, y: jnp.concatenate([x, y], axis=0), kap_s, r_s)
    p_a = each(dot(NT), kr_s, rhs_a)
    p_k = each(dot(NT), kr_s, rhs_k)
    n_ka = each(masked(strict), ds, each(lambda x: x[:n2], p_a))
    m_kk = each(masked(strict), ds, each(lambda x: x[:n2], p_k))
    a_rk = each(masked(incl), ds, each(lambda x: x[n2:], p_k))
    a_ra = each(masked(incl), ds, each(lambda x: x[n2:], p_a))

    b16 = (row2 // 16) == (col2 // 16)
    b32 = (row2 // 32) == (col2 // 32)
    eye = jnp.where(row2 == col2, 1.0, 0.0)
    n16 = each(lambda x: jnp.where(b16, x, zero), n_ka)
    n_2 = each(dot(NN), n16, n16)
    n_4 = each(dot(NN), n_2, n_2)
    n_8 = each(dot(NN), n_4, n_4)
    inv = each(lambda x: eye - x, n16)
    for pw in (n_2, n_4, n_8):
        inv = each(jnp.add, inv, each(dot(NN), inv, pw))
    for sel in (lambda x: jnp.where(b32 & ~b16, x, zero), lambda x: jnp.where(b32, zero, x)):
        t1 = each(dot(NN), inv, each(sel, n_ka))
        inv = each(jnp.subtract, inv, each(dot(NN), t1, inv))

    mav = each(dot(NN), each(lambda x, y: jnp.concatenate([x, y], axis=0), m_kk, a_rk), v_s)
    mv = each(lambda x: x[:n2], mav)
    av = each(lambda x: x[n2:], mav)
    w1u0 = each(dot(NN), inv, each(lambda x, y: jnp.concatenate([x, y], axis=1), kap_s, mv))
    ar = each(dot(NN), a_ra, w1u0)
    r2 = each(lambda x, y: _unstack2(x - y[:, :LANES]), r_s, ar)
    y0 = each(lambda x, y: _unstack2(x - y[:, LANES:]), av, ar)
    w1_t = each(lambda x: x[:, :LANES].T, w1u0)
    u0_t = each(lambda x: x[:, LANES:].T, w1u0)
    v_t = each(lambda x: x.T, v_s)
    wua = each(dot(NN), each(lambda x, y: jnp.concatenate([x, y], axis=0), w1_t, u0_t), a_s)
    wa = each(lambda x: x[:LANES], wua)
    ua = each(lambda x: x[LANES:], wua)
    vk = each(dot(NN), v_t, k_s)
    g = each(lambda xe, x: jnp.where(row2 == col2, jnp.exp(xe), zero) - x, lp_end, wa)
    b = each(jnp.subtract, vk, ua)
    return list(zip(y0, r2, g, b))


RW_PAIRS_PER_STEP = 2


def _rwkv_body(lwf, rf, kf, vf, af, lwb, rb, kb, vb, ab, kk_ref, ka_ref, yf_ref, yb_ref, s_ref):
    @pl.when(pl.program_id(1) == 0)
    def _():
        s_ref[...] = jnp.zeros(s_ref.shape, F32)

    L = RWKV_CHUNK
    n = RW_CHUNKS_PER_STEP
    rows = lambda j: slice(j * L, (j + 1) * L)
    lanes = lambda q: slice(q * LANES, (q + 1) * LANES)
    refs = ((lwf, rf, kf, vf, af), (lwb, rb, kb, vb, ab))
    y_refs = (yf_ref, yb_ref)
    visit = [(d, j if d == 0 else n - 1 - j) for j in range(n) for d in (0, 1)]
    pairs = range(RW_PAIRS_PER_STEP)
    pre = _rwkv_chunks([tuple(ref[rows(j), lanes(q)] for ref in refs[d]) + (d,) for d, j in visit for q in pairs],
                       [kk_ref[:, lanes(q)] for _ in visit for q in pairs],
                       [ka_ref[:, lanes(q)] for _ in visit for q in pairs])
    s = [[s_ref[q, 0], s_ref[q, 1]] for q in pairs]
    chain = iter(pre)
    for d, j in visit:
        for q in pairs:
            y0, r2, g, b = next(chain)
            y_refs[d][rows(j), lanes(q)] = y0 + _mdot(r2, s[q][d], NT, RW_PASSES)
            s[q][d] = _mdot(s[q][d], g, NN, RW_PASSES) + b
    for q in range(RW_PAIRS_PER_STEP):
        s_ref[q, 0] = s[q][0]
        s_ref[q, 1] = s[q][1]


def rwkv_scan(r, k, v, lw, a, k_k, k_a, n_ctx):
    t, d = r.shape
    L = RWKV_CHUNK
    blk = RW_CHUNKS_PER_STEP * L
    w = RW_PAIRS_PER_STEP * LANES
    assert 2 * L == LANES and t % blk == 0 and n_ctx % blk == 0 and d % w == 0
    nc, nc0 = t // blk, n_ctx // blk
    nh = d // w
    fwd = pl.BlockSpec((blk, w), lambda h, c: (c, h))
    bwd = pl.BlockSpec((blk, w), lambda h, c: (_bwd_chunk(c, nc0, nc), h))
    bwd2 = pl.BlockSpec((blk, w), lambda h, c: (_bwd_chunk(c, nc0, nc), nh + h))
    par = pl.BlockSpec((1, w), lambda h, c: (0, h))
    return pl.pallas_call(
        _rwkv_body,
        grid=(nh, nc),
        in_specs=[fwd] * 5 + [bwd2, bwd, bwd, bwd, bwd2, par, par],
        out_specs=[fwd, bwd],
        out_shape=[jax.ShapeDtypeStruct((t, d), F32)] * 2,
        scratch_shapes=[pltpu.VMEM((RW_PAIRS_PER_STEP, 2, LANES, LANES), F32)],
        compiler_params=pltpu.CompilerParams(
            dimension_semantics=("arbitrary", "arbitrary"), vmem_limit_bytes=VMEM_LIMIT),
        name="rwkv7_scan",
    )(lw, r, k, v, a, lw, r, k, v, a, k_k.reshape(1, d), k_a.reshape(1, d))


MIX_TM = 256


def _mix_body(cur_ref, prev_ref, next_ref, mu_ref, *o_refs, nct, nt):
    i = pl.program_id(0)
    is_ctx = i < nct
    cur = cur_ref[...]
    tm, d = cur.shape
    qd = d // 4
    above = jnp.where(is_ctx | (i == nct), 0.0, prev_ref[...])
    below = jnp.where(is_ctx | (i == nt - 1), 0.0, next_ref[...])
    ext = jnp.concatenate([above, cur, below], axis=0)
    n = ext.shape[0]
    colid = lax.broadcasted_iota(jnp.int32, (n, 2 * qd), 0) % GRID_W
    left = jnp.where(is_ctx | (colid != 0), pltpu.roll(ext[:, :2 * qd], 1, 0), 0.0)
    colid3 = lax.broadcasted_iota(jnp.int32, (n, 3 * qd), 0) % GRID_W
    right = jnp.where(is_ctx | (colid3 != GRID_W - 1), pltpu.roll(ext[:, qd:], n - 1, 0), 0.0)
    mid = slice(GRID_W, GRID_W + tm)
    sh = jnp.concatenate([
        left[mid, :qd],
        jnp.where(is_ctx, left[mid, qd:], right[mid, :qd]),
        jnp.where(is_ctx, right[mid, qd:2 * qd], ext[0:tm, 2 * qd:3 * qd]),
        jnp.where(is_ctx, right[mid, 2 * qd:], ext[2 * GRID_W:2 * GRID_W + tm, 3 * qd:])], axis=1)
    dx = sh - cur
    for b, o_ref in enumerate(o_refs):
        o_ref[...] = (cur + dx * mu_ref[b:b + 1, :]).astype(o_ref.dtype)


def rwkv_mix(h, mu, n_ctx):
    t, d = h.shape
    tm = MIX_TM
    assert t % tm == 0 and n_ctx % tm == 0 and tm % GRID_W == 0
    r = tm // GRID_W
    nt, nu = t // tm, t // GRID_W
    nb = mu.shape[0]
    return pl.pallas_call(
        functools.partial(_mix_body, nct=n_ctx // tm, nt=nt),
        grid=(nt,),
        in_specs=[pl.BlockSpec((tm, d), lambda i: (i, 0)),
                  pl.BlockSpec((GRID_W, d), lambda i: (jnp.maximum(i * r - 1, 0), 0)),
                  pl.BlockSpec((GRID_W, d), lambda i: (jnp.minimum((i + 1) * r, nu - 1), 0)),
                  pl.BlockSpec((nb, d), lambda i: (0, 0))],
        out_specs=[pl.BlockSpec((tm, d), lambda i: (i, 0))] * nb,
        out_shape=[jax.ShapeDtypeStruct((t, d), BF16)] * nb,
        compiler_params=pltpu.CompilerParams(dimension_semantics=("arbitrary",), vmem_limit_bytes=VMEM_LIMIT),
        name="rwkv_mix",
    )(h, h, h, mu)


def _rwkv_out_body(yf_ref, yb_ref, r_ref, k_ref, v_ref, g_ref, af_ref, ab_ref, p_ref, e_ref, et_ref, w_ref, o_ref):
    e, et = e_ref[...], et_ref[...]
    head_sum = lambda x: _dot_01_rhs(_dot_01_rhs(x, e, 2), et, 2)
    k_a, r_k, gain, bias = (p_ref[n:n + 1, :] for n in range(4))
    y = yf_ref[...] + yb_ref[...]
    yc = y - head_sum(y) * (1.0 / RWKV_N)
    var = head_sum(yc * yc) * (1.0 / RWKV_N)
    yn = yc * lax.rsqrt(var + RWKV_GN_EPS) * gain + bias
    kbar = k_ref[...] * (1.0 + (0.5 * (af_ref[...] + ab_ref[...]) - 1.0) * k_a)
    bonus = head_sum(r_ref[...] * kbar * r_k) * v_ref[...]
    z = (yn + bonus) * g_ref[...]
    o_ref[...] = _d(z.astype(BF16), w_ref[...], NN)


def rwkv_out(yf, yb, r, k, v, g, a, k_a, r_k, lnx_g, lnx_b, w_out):
    t, d = yf.shape
    tm = LN_TM
    nh = d // RWKV_N
    row = pl.BlockSpec((tm, d), lambda i: (i, 0))
    const = lambda shape: pl.BlockSpec(shape, lambda i: (0, 0))
    e = (jnp.arange(d)[:, None] // RWKV_N == jnp.arange(LANES)[None, :]).astype(BF16)
    params = jnp.stack([k_a, r_k, lnx_g, lnx_b])
    return pl.pallas_call(
        _rwkv_out_body,
        grid=(t // tm,),
        in_specs=[row] * 6 + [row, pl.BlockSpec((tm, d), lambda i: (i, 1)),
                  const((4, d)), const((d, LANES)), const((LANES, d)), const((d, d))],
        out_specs=row,
        out_shape=jax.ShapeDtypeStruct((t, d), F32),
        compiler_params=pltpu.CompilerParams(dimension_semantics=("arbitrary",), vmem_limit_bytes=VMEM_LIMIT),
        name="rwkv_out",
    )(yf, yb, r, k, v, g, a, a, params, e, e.T, w_out.astype(BF16))


def _sort16_pairs():
    pairs = []
    n, p = 16, 1
    while p < n:
        k = p
        while k >= 1:
            for j in range(k % p, n - k, 2 * k):
                for i in range(min(k, n - j - k)):
                    if (i + j) // (2 * p) == (i + j + k) // (2 * p):
                        pairs.append((i + j, i + j + k))
            k //= 2
        p *= 2
    return pairs


def _top16_sorted(x):
    v = [x[8 * i:8 * i + 8] for i in range(16)]

    def exchange(i, j):
        v[i], v[j] = jnp.maximum(v[i], v[j]), jnp.minimum(v[i], v[j])

    for i, j in _sort16_pairs():
        exchange(i, j)
    for shift in (4, 2, 1):
        other = [pltpu.roll(t, shift, 0) for t in v]
        v = [jnp.maximum(v[i], other[15 - i]) for i in range(16)]
        for dist in (8, 4, 2, 1):
            for i in range(16):
                if i & dist == 0:
                    exchange(i, i + dist)
    return [t[0:1] for t in v]


def _peer_stats(hx_ref, wk_ref, sc_ref, hxb_ref, n_ref, f0_ref, r1_ref, e1_ref):
    tm = hx_ref.shape[0]
    hxb_ref[...] = hx_ref[...].astype(F32).T.astype(BF16)
    rows_per = 4 * LANES

    def scores(c, carry):
        rows = pl.ds(pl.multiple_of(c * rows_per, rows_per), rows_per)
        sc_ref[rows, :] = _d(wk_ref[rows, :], hxb_ref[...], NN)
        return carry

    lax.fori_loop(0, wk_ref.shape[0] // rows_per, scores, 0)

    def block(tb, carry):
        lanes = pl.ds(pl.multiple_of(tb * LANES, LANES), LANES)
        neg = jnp.full((N_KEYS, LANES), -jnp.inf, F32)
        head_row = lax.broadcasted_iota(jnp.int32, (PEER_HEADS, LANES), 0)

        def extract(h, tops):
            tops = [list(t) for t in tops]
            rows = [pl.ds(pl.multiple_of((2 * h + p) * N_KEYS, N_KEYS), N_KEYS) for p in range(2)]
            cur = [sc_ref[rows[p], lanes] for p in range(2)]
            top = [_top16_sorted(c) for c in cur]
            rank = jnp.zeros((N_KEYS, LANES), F32)
            for b in range(PEER_TOPK):
                rank = jnp.where(cur[1] < top[1][b], float(b + 1), rank)
            for p in range(2):
                for a in range(PEER_TOPK):
                    tops[p][a] = jnp.where(head_row == h, top[p][a], tops[p][a])
            r1_ref[h, :, lanes] = rank.astype(BF16)
            return tuple(tuple(t) for t in tops)

        zero = jnp.zeros((PEER_HEADS, LANES), F32)
        top0, top1 = lax.fori_loop(0, PEER_HEADS, extract, ((zero,) * PEER_TOPK,) * 2)
        cands = [top0[a] + top1[b]
                 for a in range(PEER_TOPK) for b in range(PEER_TOPK) if (a + 1) * (b + 1) <= PEER_TOPK]
        c_max = cands[0]
        z = jnp.zeros_like(c_max)
        tau = c_max
        for a in range(PEER_TOPK):
            tau = functools.reduce(jnp.maximum, cands)
            z = z + jnp.exp(tau - c_max)
            cands = [jnp.where(cd >= tau, -jnp.inf, cd) for cd in cands]
        inv_z = 1.0 / z

        def factors(h, carry):
            row_of = lambda x: jnp.max(jnp.where(head_row == h, x, -jnp.inf), axis=0, keepdims=True)
            s0 = sc_ref[pl.ds(pl.multiple_of(2 * h * N_KEYS, N_KEYS), N_KEYS), lanes]
            s1 = sc_ref[pl.ds(pl.multiple_of((2 * h + 1) * N_KEYS, N_KEYS), N_KEYS), lanes]
            tau_h = row_of(tau)
            n = jnp.zeros((N_KEYS, LANES), F32)
            for b in range(PEER_TOPK):
                n = jnp.where(s0 + row_of(top1[b]) >= tau_h, float(b + 1), n)
            n_ref[h, :, lanes] = n
            f0_ref[h, :, lanes] = jnp.exp(s0 - row_of(top0[0])) * row_of(inv_z)
            e1_ref[h, :, lanes] = jnp.exp(s1 - row_of(top1[0])).astype(BF16)
            return carry

        lax.fori_loop(0, PEER_HEADS, factors, 0)
        return carry

    lax.fori_loop(0, tm // LANES, block, 0)


PEER_I_GROUP = 4
PEER_STREAMS = 2


def _peer_body(hx_ref, wk_ref, *rest):
    u_refs, vt_refs = rest[:PEER_STREAMS], rest[PEER_STREAMS:2 * PEER_STREAMS]
    o_ref, sc_ref, hxb_ref, n_ref, f0_ref, r1_ref, e1_ref, w_ref, acc_ref = rest[2 * PEER_STREAMS:]
    e = pl.program_id(1)
    tm = hx_ref.shape[0]
    nsb = tm // PEER_SB

    @pl.when(e == 0)
    def _():
        _peer_stats(hx_ref, wk_ref, sc_ref, hxb_ref, n_ref, f0_ref, r1_ref, e1_ref)
        acc_ref[...] = jnp.zeros(acc_ref.shape, F32)

    i_rows = pl.ds(pl.multiple_of(e * PEER_I_BLOCK, PEER_I_BLOCK), PEER_I_BLOCK)

    def activations(sb):
        parts = []
        for u_ref in u_refs:
            act = _d(u_ref[...], hxb_ref[:, sb * PEER_SB:(sb + 1) * PEER_SB], NN)
            act = act.astype(BF16)
            parts.append(0.5 * act * (1.0 + lax.erf(act * (2.0 ** -0.5))))
        return parts

    def gates(sb, act):
        for hb in range(PEER_SB // LANES):
            lanes = slice(sb * PEER_SB + hb * LANES, sb * PEER_SB + (hb + 1) * LANES)
            sub = slice(hb * LANES, (hb + 1) * LANES)
            n8 = [n_ref[h, i_rows, lanes] for h in range(PEER_HEADS)]
            f8 = [f0_ref[h, i_rows, lanes] for h in range(PEER_HEADS)]
            bcast = lambda x, ii: jnp.broadcast_to(x[ii:ii + 1], (N_KEYS, LANES)).astype(BF16)
            for ig in range(0, PEER_I_BLOCK, PEER_I_GROUP):
                g = [jnp.zeros((N_KEYS, LANES), BF16) for _ in range(PEER_I_GROUP)]
                for h in range(PEER_HEADS):
                    r1 = r1_ref[h, :, lanes]
                    e1 = e1_ref[h, :, lanes]
                    for k in range(PEER_I_GROUP):
                        ii = ig + k
                        g[k] = g[k] + jnp.where(r1 < bcast(n8[h], ii), e1 * bcast(f8[h], ii), jnp.zeros_like(e1))
                for k in range(PEER_I_GROUP):
                    rows = slice((ig + k) * N_KEYS, (ig + k + 1) * N_KEYS)
                    part, off = divmod((ig + k) * N_KEYS, PEER_TE // PEER_STREAMS)
                    w_ref[sb, rows, sub] = g[k] * act[part][off:off + N_KEYS, sub]

    def accumulate(sb):
        cols = slice(sb * PEER_SB, (sb + 1) * PEER_SB)
        dr = acc_ref.shape[0] // PEER_STREAMS
        for k, vt_ref in enumerate(vt_refs):
            acc_ref[k * dr:(k + 1) * dr, cols] += _d(vt_ref[...], w_ref[sb], NN)

    act = activations(0)
    for sb in range(nsb):
        nxt = activations(sb + 1) if sb + 1 < nsb else None
        gates(sb, act)
        accumulate(sb)
        act = nxt

    @pl.when(e == pl.num_programs(1) - 1)
    def _():
        o_ref[...] = acc_ref[...].T


def _fold_body(k_ref, w_ref, o_ref):
    o_ref[...] = _mdot(k_ref[...], w_ref[...], NN, 6)


def peer_fold_keys(wq, keys):
    d = wq.shape[0]
    nhp, nk, dk = keys.shape
    wqt = wq.T.reshape(nhp, dk, d)
    return pl.pallas_call(
        _fold_body,
        grid=(nhp,),
        in_specs=[pl.BlockSpec((None, nk, dk), lambda i: (i, 0, 0)),
                  pl.BlockSpec((None, dk, d), lambda i: (i, 0, 0))],
        out_specs=pl.BlockSpec((nk, d), lambda i: (i, 0)),
        out_shape=jax.ShapeDtypeStruct((nhp * nk, d), F32),
        compiler_params=pltpu.CompilerParams(dimension_semantics=("arbitrary",), vmem_limit_bytes=VMEM_LIMIT),
        name="peer_fold_keys",
    )(keys, wqt)


def peer(hx, wk, u_bf, vt_bf):
    t, d = hx.shape
    tm = PEER_TM
    assert t % tm == 0
    ne = u_bf.shape[0] // PEER_TE
    h = PEER_HEADS
    ns = PEER_STREAMS
    return pl.pallas_call(
        _peer_body,
        grid=(t // tm, ne),
        in_specs=[pl.BlockSpec((tm, d), lambda i, e: (i, 0), pipeline_mode=pl.Buffered(1)),
                  pl.BlockSpec(wk.shape, lambda i, e: (0, 0), pipeline_mode=pl.Buffered(1)),
                  *[pl.BlockSpec((PEER_TE // ns, d), functools.partial(lambda i, e, k: (e * ns + k, 0), k=k))
                    for k in range(ns)],
                  *[pl.BlockSpec((None, d // ns, PEER_TE), functools.partial(lambda i, e, k: (e, k, 0), k=k))
                    for k in range(ns)]],
        out_specs=pl.BlockSpec((tm, d), lambda i, e: (i, 0)),
        out_shape=jax.ShapeDtypeStruct((t, d), F32),
        scratch_shapes=[pltpu.VMEM((2 * h * N_KEYS, tm), F32),
                        pltpu.VMEM((d, tm), BF16),
                        pltpu.VMEM((h, N_KEYS, tm), F32), pltpu.VMEM((h, N_KEYS, tm), F32),
                        pltpu.VMEM((h, N_KEYS, tm), BF16), pltpu.VMEM((h, N_KEYS, tm), BF16),
                        pltpu.VMEM((tm // PEER_SB, PEER_TE, PEER_SB), BF16), pltpu.VMEM((d, tm), F32)],
        compiler_params=pltpu.CompilerParams(
            dimension_semantics=("arbitrary", "arbitrary"), vmem_limit_bytes=VMEM_LIMIT),
        name="peer_dense",
    )(hx, wk.astype(BF16), *([u_bf] * ns), *([vt_bf] * ns))


LN_TM = 256


def _ln_mod_body(xs_ref, y_ref, mg_ref, mn_ref, lng_ref, lnb_ref, xo_ref, ho_ref, *, gate_col, mod_col, nct, nt):
    i = pl.program_id(0)
    d = xs_ref.shape[1]
    is_ctx = i < nct

    def pick(ref, col):
        return jnp.where(is_ctx, ref[1:2, col * d:(col + 1) * d], ref[0:1, col * d:(col + 1) * d])

    z = DN_ALPHA * xs_ref[...] + pick(mg_ref, gate_col) * y_ref[...]
    mu = jnp.mean(z, axis=-1, keepdims=True)
    zc = z - mu
    var = jnp.mean(zc * zc, axis=-1, keepdims=True)
    xn = zc * lax.rsqrt(var + LN_EPS) * lng_ref[...] + lnb_ref[...]
    xo_ref[...] = xn
    h = xn * (1.0 + pick(mn_ref, mod_col + 1)) + pick(mn_ref, mod_col)
    ho_ref[...] = jnp.where(i < nt, h, 0.0).astype(ho_ref.dtype)


def ln_mod(xs, y, mod_gate, gate_col, mod_next, mod_col, ln_g, ln_b, n_ctx, pad_to, h_dtype):
    t, d = xs.shape
    tm = LN_TM
    assert t % tm == 0 and n_ctx % tm == 0 and pad_to % tm == 0 and pad_to >= t
    nt = t // tm
    row = lambda i: (jnp.minimum(i, nt - 1), 0)
    full = lambda a: pl.BlockSpec(a.shape, lambda i: (0, 0))
    return pl.pallas_call(
        functools.partial(_ln_mod_body, gate_col=gate_col, mod_col=mod_col, nct=n_ctx // tm, nt=nt),
        grid=(pad_to // tm,),
        in_specs=[pl.BlockSpec((tm, d), row), pl.BlockSpec((tm, d), row), full(mod_gate), full(mod_next),
                  pl.BlockSpec((1, d), lambda i: (0, 0)), pl.BlockSpec((1, d), lambda i: (0, 0))],
        out_specs=[pl.BlockSpec((tm, d), row), pl.BlockSpec((tm, d), lambda i: (i, 0))],
        out_shape=[jax.ShapeDtypeStruct((t, d), F32), jax.ShapeDtypeStruct((pad_to, d), h_dtype)],
        compiler_params=pltpu.CompilerParams(dimension_semantics=("arbitrary",), vmem_limit_bytes=VMEM_LIMIT),
        name="ln_mod",
    )(xs, y, mod_gate, mod_next, ln_g.reshape(1, d), ln_b.reshape(1, d))


def _pad_cols(w, n):
    return jnp.pad(w, ((0, 0), (0, n - w.shape[1])))


def _mlstm_layer(h, n_ctx, w_in, b_in, conv_w, conv_b, hn_g, w_out):
    d = D_MODEL
    t = h.shape[0]
    qk_pre = matmul(h, w_in[:, :2 * d], b_in[:2 * d])
    v = matmul(h, w_in[:, 2 * d:3 * d], b_in[2 * d:3 * d], out_dtype=BF16)
    o = matmul(h, w_in[:, 3 * d:4 * d], b_in[3 * d:4 * d], act="sigmoid", out_dtype=BF16)
    g = matmul(h, _pad_cols(w_in[:, 4 * d:], LANES), jnp.pad(b_in[4 * d:], (0, LANES - 4 * MLSTM_HEADS)))
    g = g[:, :4 * MLSTM_HEADS].reshape(t, 4, MLSTM_HEADS)
    g = jnp.concatenate([g[:, :2], jax.nn.log_sigmoid(g[:, 2:])], axis=1)
    gh = jnp.transpose(g, (2, 0, 1))
    gcol = jnp.pad(gh, ((0, 0), (0, 0), (0, LANES - 4)))
    grow = jnp.pad(jnp.transpose(gh, (0, 2, 1)), ((0, 0), (0, 4), (0, 0)))
    qk = mlstm_conv(qk_pre, conv_w, conv_b, n_ctx)
    hf, hb = mlstm_scan(qk, v, gcol, grow, n_ctx)
    return mlstm_out(hf, hb, o, hn_g, w_out)


def _rwkv_layer(h, n_ctx, mu, w_rkv, w0, w1, w2, a0, a1, a2, g1, g2, k_k, k_a, r_k, lnx_g, lnx_b, w_out):
    d = D_MODEL
    xm = rwkv_mix(h, mu, n_ctx)
    r = matmul(xm[0], w_rkv[0])
    k = matmul(xm[1], w_rkv[1])
    v = matmul(xm[2], w_rkv[2])

    def lora_pair(x, w_in, w_mid, bias, act_mid, act_out):
        rank = w_in.shape[-1]
        w_a = _pad_cols(jnp.concatenate([w_in[0], w_in[1]], axis=1), LANES)
        zpad = jnp.zeros((rank, d), F32)
        w_b = jnp.concatenate([jnp.concatenate([w_mid[0], zpad], axis=1),
                               jnp.concatenate([zpad, w_mid[1]], axis=1)], axis=0)
        w_b = jnp.pad(w_b, ((0, LANES - 2 * rank), (0, 0)))
        mid = matmul(x, w_a, act=act_mid, out_dtype=BF16)
        return matmul(mid, w_b, jnp.concatenate([bias[0], bias[1]]), act=act_out)

    lw = lora_pair(xm[3], w1, w2, w0, "tanh", "logdecay")
    a = lora_pair(xm[4], a1, a2, a0, None, "sigmoid")
    gpad = 2 * LANES
    gg = matmul(xm[5], _pad_cols(g1, gpad), act="sigmoid", out_dtype=BF16)
    g = matmul(gg, jnp.pad(g2, ((0, gpad - g1.shape[1]), (0, 0))))
    yf, yb = rwkv_scan(r, k, v, lw, a, k_k, k_a, n_ctx)
    return rwkv_out(yf, yb, r, k, v, g, a, k_a, r_k, lnx_g, lnx_b, w_out)


def _forward(x, c, ctx, c_ctx, ada_w, ada_b, ln_g, ln_b,
             ml_w_in, ml_b_in, ml_conv_w, ml_conv_b, ml_hn_g, ml_w_out,
             rw_mu, rw_w_rkv, rw_w0, rw_w1, rw_w2, rw_a0, rw_a1, rw_a2, rw_g1, rw_g2,
             rw_k_k, rw_k_a, rw_r_k, rw_lnx_g, rw_lnx_b, rw_w_out,
             pk_wq, pk_keys, pk_u, pk_v):
    d = D_MODEL
    n_ctx = ctx.shape[1]
    xs = jnp.concatenate([ctx[0], x[0]], axis=0)
    t = xs.shape[0]
    t_pad = -(-t // PEER_TM) * PEER_TM
    s_in = jnp.zeros((8, d), F32).at[0].set(jax.nn.silu(c[0])).at[1].set(jax.nn.silu(c_ctx))
    depth = ada_w.shape[0]
    mods = [matmul(s_in, ada_w[i], ada_b[i], passes=3) for i in range(depth)]
    is_ctx = (jnp.arange(t) < n_ctx)[:, None]
    m0 = [jnp.where(is_ctx, mods[0][1, n * d:(n + 1) * d], mods[0][0, n * d:(n + 1) * d]) for n in range(2)]
    mixer_dtype = lambda i: BF16 if i % 2 == 0 else F32
    h = (xs * (1.0 + m0[1]) + m0[0]).astype(mixer_dtype(0))
    for i in range(depth):
        j = i // 2
        if i % 2 == 0:
            y = _mlstm_layer(h, n_ctx, ml_w_in[j], ml_b_in[j], ml_conv_w[j], ml_conv_b[j],
                             ml_hn_g[j], ml_w_out[j])
        else:
            y = _rwkv_layer(h, n_ctx, rw_mu[j], rw_w_rkv[j], rw_w0[j], rw_w1[j], rw_w2[j],
                            rw_a0[j], rw_a1[j], rw_a2[j], rw_g1[j], rw_g2[j], rw_k_k[j],
                            rw_k_a[j], rw_r_k[j], rw_lnx_g[j], rw_lnx_b[j], rw_w_out[j])
        xs, h = ln_mod(xs, y, mods[i], 2, mods[i], 3, ln_g[i, 0], ln_b[i, 0], n_ctx, t_pad, BF16)
        wk = peer_fold_keys(pk_wq[i], pk_keys[i].reshape(2 * PEER_HEADS, N_KEYS, PEER_DQ // 2))
        vt = jnp.swapaxes(pk_v[i].astype(BF16).reshape(-1, PEER_TE, d), 1, 2)
        y = peer(h, wk, pk_u[i].astype(BF16), vt)
        xs, h = ln_mod(xs, y, mods[i], 5, mods[min(i + 1, depth - 1)], 0, ln_g[i, 1], ln_b[i, 1], n_ctx, t,
                       mixer_dtype(i + 1))
    return xs[n_ctx:][None]


def kernel(x, c, ctx, c_ctx, ada_w, ada_b, ln_g, ln_b, ml_w_in, ml_b_in, ml_conv_w, ml_conv_b, ml_hn_g, ml_w_out, rw_mu, rw_w_rkv, rw_w0, rw_w1, rw_w2, rw_a0, rw_a1, rw_a2, rw_g1, rw_g2, rw_k_k, rw_k_a, rw_r_k, rw_lnx_g, rw_lnx_b, rw_w_out, pk_wq, pk_keys, pk_u, pk_v):
    return _forward(x, c, ctx, c_ctx, ada_w, ada_b, ln_g, ln_b,
                    ml_w_in, ml_b_in, ml_conv_w, ml_conv_b, ml_hn_g, ml_w_out,
                    rw_mu, rw_w_rkv, rw_w0, rw_w1, rw_w2, rw_a0, rw_a1, rw_a2, rw_g1, rw_g2,
                    rw_k_k, rw_k_a, rw_r_k, rw_lnx_g, rw_lnx_b, rw_w_out,
                    pk_wq, pk_keys, pk_u, pk_v)
```

```python
import functools

import jax
import jax.numpy as jnp
from jax import lax
from jax.experimental import pallas as pl
from jax.experimental.pallas import tpu as pltpu

F32 = jnp.float32
BF16 = jnp.bfloat16

D_MODEL = 1024
DEPTH = 4
GRID_W = 64
DN_ALPHA = (2.0 * DEPTH) ** 0.25
LN_EPS = 1e-5

MLSTM_HEADS = 4
MLSTM_DH = D_MODEL // MLSTM_HEADS
MLSTM_CHUNK = 128
M_INIT = -1e30

RWKV_N = 64
RWKV_HEADS = D_MODEL // RWKV_N
RWKV_CHUNK = 64
RWKV_GN_EPS = 64e-5

N_KEYS = 128
PEER_HEADS = 8
PEER_DQ = 256
PEER_TOPK = 16
PEER_I_BLOCK = 16
PEER_TE = PEER_I_BLOCK * N_KEYS
PEER_TM = 768
PEER_SB = 256
PEER_STAT_LANES = 256

LANES = 128
VMEM_LIMIT = 62 * 1024 * 1024

NN = ((1,), (0,))
NT = ((1,), (1,))


def _split(x, n):
    parts = []
    r = x.astype(F32)
    for i in range(n):
        p = r.astype(BF16)
        parts.append(p)
        if i + 1 < n:
            r = r - p.astype(F32)
    return parts


def _d(a, b, dims):
    return lax.dot_general(a, b, (dims, ((), ())), preferred_element_type=F32)


def _mdot(a, b, dims, passes):
    if passes == 1:
        return _d(a.astype(BF16), b.astype(BF16), dims)
    if passes == 3:
        a0, a1 = _split(a, 2)
        b0, b1 = _split(b, 2)
        return (_d(a0, b1, dims) + _d(a1, b0, dims)) + _d(a0, b0, dims)
    a0, a1, a2 = _split(a, 3)
    b0, b1, b2 = _split(b, 3)
    lo = (_d(a0, b2, dims) + _d(a2, b0, dims)) + _d(a1, b1, dims)
    mid = _d(a0, b1, dims) + _d(a1, b0, dims)
    return (lo + mid) + _d(a0, b0, dims)


def _dot_01_lhs(m01, x):
    mb = m01.astype(BF16)
    x0, x1, x2 = _split(x, 3)
    return (_d(mb, x2, NN) + _d(mb, x1, NN)) + _d(mb, x0, NN)


def _dot_01_rhs(x, m01, pieces=3):
    mb = m01.astype(BF16)
    return functools.reduce(jnp.add, [_d(xp, mb, NN) for xp in reversed(_split(x, pieces))])


def _pick(n, cands):
    for c in cands:
        if n % c == 0:
            return c
    raise ValueError(f"no tile for {n}")


_ACTS = {None: lambda x: x, "sigmoid": jax.nn.sigmoid, "tanh": jnp.tanh,
         "logdecay": lambda x: -(2.718281828459045 ** -0.5) * jax.nn.sigmoid(x)}


def _mm_body(x_ref, w_ref, b_ref, o_ref, *, passes, act):
    o_ref[...] = _ACTS[act](_mdot(x_ref[...], w_ref[...], NN, passes) + b_ref[...]).astype(o_ref.dtype)


def matmul(x, w, b=None, *, passes=1, act=None, out_dtype=F32):
    m, k = x.shape
    n = w.shape[1]
    assert n % LANES == 0 and w.shape[0] == k
    tm = m if m <= 1024 else _pick(m, (640, 512, 384, 256, 128))
    tn = _pick(n, (1024, 768, 640, 512, 384, 256, 128))
    if b is None:
        b = jnp.zeros((n,), F32)
    if passes == 1:
        w = w.astype(BF16)
    return pl.pallas_call(
        functools.partial(_mm_body, passes=passes, act=act),
        grid=(m // tm, n // tn),
        in_specs=[pl.BlockSpec((tm, k), lambda i, j: (i, 0)),
                  pl.BlockSpec((k, tn), lambda i, j: (0, j)),
                  pl.BlockSpec((1, tn), lambda i, j: (0, j))],
        out_specs=pl.BlockSpec((tm, tn), lambda i, j: (i, j)),
        out_shape=jax.ShapeDtypeStruct((m, n), out_dtype),
        compiler_params=pltpu.CompilerParams(
            dimension_semantics=("arbitrary", "arbitrary"), vmem_limit_bytes=VMEM_LIMIT),
        name="proj_matmul",
    )(x, w, b.reshape(1, n).astype(F32))


CONV_TM = 256
CONV_CB = 512


def _conv_body(cur_ref, prev_ref, next_ref, w_ref, b_ref, o_ref, *, nct, nt, q_blocks, q_scale):
    i = pl.program_id(0)
    j = pl.program_id(1)
    is_ctx = i < nct
    tm = cur_ref.shape[0]
    above = jnp.where(is_ctx | (i == nct), 0.0, prev_ref[...])
    below = jnp.where(is_ctx | (i == nt - 1), 0.0, next_ref[...])
    ext = jnp.concatenate([above, cur_ref[...], below], axis=0)
    n = ext.shape[0]
    colid = lax.broadcasted_iota(jnp.int32, ext.shape, 0) % GRID_W
    left = jnp.where(is_ctx | (colid != 0), pltpu.roll(ext, 1, 0), 0.0)
    right = jnp.where(is_ctx | (colid != GRID_W - 1), pltpu.roll(ext, n - 1, 0), 0.0)
    w = w_ref[...]
    acc = jnp.zeros((tm, ext.shape[1]), F32) + b_ref[...]
    for di in range(3):
        rows = slice(di * GRID_W, di * GRID_W + tm)
        tap = left[rows] * w[3 * di:3 * di + 1] + ext[rows] * w[3 * di + 1:3 * di + 2] \
            + right[rows] * w[3 * di + 2:3 * di + 3]
        acc = acc + (tap if di == 1 else jnp.where(is_ctx, 0.0, tap))
    y = acc * jax.nn.sigmoid(acc)
    o_ref[...] = (y * jnp.where(j < q_blocks, q_scale, 1.0)).astype(o_ref.dtype)


def mlstm_conv(qk_pre, conv_w, conv_b, n_ctx):
    t, c = qk_pre.shape
    tm, cb = CONV_TM, CONV_CB
    assert t % tm == 0 and n_ctx % tm == 0 and tm % GRID_W == 0 and c % (2 * cb) == 0
    r = tm // GRID_W
    nt, nu = t // tm, t // GRID_W
    return pl.pallas_call(
        functools.partial(_conv_body, nct=n_ctx // tm, nt=nt, q_blocks=c // (2 * cb), q_scale=MLSTM_DH ** -0.5),
        grid=(nt, c // cb),
        in_specs=[pl.BlockSpec((tm, cb), lambda i, j: (i, j)),
                  pl.BlockSpec((GRID_W, cb), lambda i, j: (jnp.maximum(i * r - 1, 0), j)),
                  pl.BlockSpec((GRID_W, cb), lambda i, j: (jnp.minimum((i + 1) * r, nu - 1), j)),
                  pl.BlockSpec((9, cb), lambda i, j: (0, j)),
                  pl.BlockSpec((1, cb), lambda i, j: (0, j))],
        out_specs=pl.BlockSpec((tm, cb), lambda i, j: (i, j)),
        out_shape=jax.ShapeDtypeStruct((t, c), BF16),
        compiler_params=pltpu.CompilerParams(
            dimension_semantics=("arbitrary", "arbitrary"), vmem_limit_bytes=VMEM_LIMIT),
        name="mlstm_conv",
    )(qk_pre, qk_pre, qk_pre, conv_w.reshape(9, c), conv_b.reshape(1, c))


def _mlstm_out_body(hf_ref, hb_ref, o_ref, g_ref, w_ref, y_ref):
    h = hf_ref[...] + hb_ref[...]
    parts = []
    for a in range(MLSTM_HEADS):
        x = h[:, a * MLSTM_DH:(a + 1) * MLSTM_DH]
        mu = jnp.mean(x, axis=-1, keepdims=True)
        xc = x - mu
        var = jnp.mean(xc * xc, axis=-1, keepdims=True)
        parts.append(xc * lax.rsqrt(var + LN_EPS))
    hn = jnp.concatenate(parts, axis=1)
    z = o_ref[...].astype(F32) * hn * g_ref[...]
    y_ref[...] = _d(z.astype(BF16), w_ref[...], NN)


def mlstm_out(hf, hb, o, hn_g, w_out):
    t, d = hf.shape
    tm = LN_TM
    row = pl.BlockSpec((tm, d), lambda i: (i, 0))
    return pl.pallas_call(
        _mlstm_out_body,
        grid=(t // tm,),
        in_specs=[row, row, row, pl.BlockSpec((1, d), lambda i: (0, 0)), pl.BlockSpec((d, d), lambda i: (0, 0))],
        out_specs=row,
        out_shape=jax.ShapeDtypeStruct((t, d), F32),
        compiler_params=pltpu.CompilerParams(dimension_semantics=("arbitrary",), vmem_limit_bytes=VMEM_LIMIT),
        name="mlstm_out",
    )(hf, hb, o, hn_g.reshape(1, d), w_out.astype(BF16))


ML_HEADS_PER_STEP = 4


def _mlstm_body(qf, kf, vf, gcf, grf, qb, kb, vb, gcb, grb, hf_ref, hb_ref, c_ref, n_ref, m_ref):
    @pl.when(pl.program_id(1) == 0)
    def _():
        c_ref[...] = jnp.zeros(c_ref.shape, F32)
        n_ref[...] = jnp.zeros(n_ref.shape, F32)
        m_ref[...] = jnp.full(m_ref.shape, M_INIT, F32)

    dh = MLSTM_DH
    cs = [(a, d) for a in range(ML_HEADS_PER_STEP) for d in (0, 1)]
    each = lambda f, *ls: [f(*xs) for xs in zip(*ls)]
    cols = lambda a: slice(a * dh, (a + 1) * dh)
    qs, ks, vs, gcs, grs = (qf, qb), (kf, kb), (vf, vb), (gcf, gcb), (grf, grb)
    qb16 = [qs[d][:, cols(a)] for a, d in cs]
    kb16 = [ks[d][:, cols(a)] for a, d in cs]
    vb16 = [vs[d][:, cols(a)] for a, d in cs]
    gc = [gcs[d][a] for a, d in cs]
    gr = [grs[d][a] for a, d in cs]
    q, k = (each(lambda x: x.astype(F32), x16) for x16 in (qb16, kb16))
    L = q[0].shape[0]
    row = lax.broadcasted_iota(jnp.int32, (L, L), 0)
    col = lax.broadcasted_iota(jnp.int32, (L, L), 1)
    seen = (col <= row, col >= row)
    tri = [jnp.where(m, 1.0, 0.0) for m in seen]
    b_col = [_dot_01_lhs(tri[d], g)[:, 2 + d:3 + d] for (a, d), g in zip(cs, gc)]
    b_row = [_dot_01_rhs(g, tri[1 - d])[2 + d:3 + d, :] for (a, d), g in zip(cs, gr)]
    ig_col = [g[:, d:d + 1] for (a, d), g in zip(cs, gc)]
    ig_row = [g[d:d + 1, :] for (a, d), g in zip(cs, gr)]
    m_st = [m_ref[a, d, 0:1, 0:1] for a, d in cs]
    c_st = [c_ref[a, d] for a, d in cs]
    n_st = [n_ref[a, d] for a, d in cs]

    dlog = [jnp.where(seen[d], bc - br + ir, -jnp.inf) for (a, d), bc, br, ir in zip(cs, b_col, b_row, ig_row)]
    m_inter = each(jnp.add, b_col, m_st)
    m_t = each(lambda mi, dl: jnp.maximum(mi, jnp.max(dl, axis=1, keepdims=True)), m_inter, dlog)
    qk = each(lambda x, y: _d(x, y, NT), qb16, kb16)
    s = each(lambda x, dl, mt: x * jnp.exp(dl - mt), qk, dlog, m_t)
    dec = each(lambda mi, mt: jnp.exp(mi - mt), m_inter, m_t)
    sv = each(lambda x, y: _d(x.astype(BF16), y, NN), s, vb16)
    qc = each(lambda x, y: _d(x, y.astype(BF16), NN), qb16, c_st)
    num = each(lambda x, dc, y: x + dc * y, sv, dec, qc)
    den = each(lambda x, dc, qq, nn: jnp.sum(x, axis=1, keepdims=True) + dc * jnp.sum(qq * nn, axis=1, keepdims=True),
               s, dec, q, n_st)
    h = each(lambda nu, de, mt: nu / jnp.maximum(jnp.abs(de), jnp.exp(-mt)), num, den, m_t)
    h_refs = (hf_ref, hb_ref)
    for (a, d), x in zip(cs, h):
        h_refs[d][:, cols(a)] = x

    b_last = [bc[0:1, :] if d else bc[L - 1:L, :] for (a, d), bc in zip(cs, b_col)]
    w_c = each(lambda bl, bc, ic: bl - bc + ic, b_last, b_col, ig_col)
    m_new = each(lambda bl, ms, w: jnp.maximum(bl + ms, jnp.max(w, axis=0, keepdims=True)), b_last, m_st, w_c)
    a_c = each(lambda w, mn: jnp.exp(w - mn), w_c, m_new)
    g_prev = each(lambda bl, ms, mn: jnp.exp(bl + ms - mn), b_last, m_st, m_new)
    ak = each(jnp.multiply, a_c, k)
    kv = each(lambda x, y: _d(x.T.astype(BF16), y, NN), ak, vb16)
    for i, (a, d) in enumerate(cs):
        c_ref[a, d] = g_prev[i] * c_st[i] + kv[i]
        n_ref[a, d] = g_prev[i] * n_st[i] + jnp.sum(ak[i], axis=0, keepdims=True)
        m_ref[a, d] = jnp.broadcast_to(m_new[i], m_ref.shape[2:])


def _bwd_chunk(c, nc0, nc):
    return jnp.where(c < nc0, nc0 - 1 - c, nc - 1 - (c - nc0))


def mlstm_scan(qk, v, gcol, grow, n_ctx):
    t = v.shape[0]
    L, dh, hs = MLSTM_CHUNK, MLSTM_DH, ML_HEADS_PER_STEP
    ng = MLSTM_HEADS // hs
    nc, nc0 = t // L, n_ctx // L
    idf = lambda c: c
    idb = lambda c: _bwd_chunk(c, nc0, nc)
    qkv = lambda f, off: pl.BlockSpec((L, hs * dh), lambda h, c: (f(c), off + h))
    gc_spec = lambda f: pl.BlockSpec((hs, L, LANES), lambda h, c: (h, f(c), 0))
    gr_spec = lambda f: pl.BlockSpec((hs, 8, L), lambda h, c: (h, 0, f(c)))
    return pl.pallas_call(
        _mlstm_body,
        grid=(ng, nc),
        in_specs=[qkv(idf, 0), qkv(idf, ng), qkv(idf, 0), gc_spec(idf), gr_spec(idf),
                  qkv(idb, 0), qkv(idb, ng), qkv(idb, 0), gc_spec(idb), gr_spec(idb)],
        out_specs=[qkv(idf, 0), qkv(idb, 0)],
        out_shape=[jax.ShapeDtypeStruct((t, D_MODEL), F32)] * 2,
        scratch_shapes=[pltpu.VMEM((hs, 2, dh, dh), F32), pltpu.VMEM((hs, 2, 1, dh), F32),
                        pltpu.VMEM((hs, 2, 8, LANES), F32)],
        compiler_params=pltpu.CompilerParams(
            dimension_semantics=("arbitrary", "arbitrary"), vmem_limit_bytes=VMEM_LIMIT),
        name="mlstm_scan",
    )(qk, qk, v, gcol, grow, qk, qk, v, gcol, grow)


RW_PASSES = 1
RW_CHUNKS_PER_STEP = 4


def _stack2(x, lane_head):
    return jnp.concatenate([jnp.where(lane_head == 0, x, 0.0), jnp.where(lane_head == 1, x, 0.0)], axis=0)


def _unstack2(x):
    L = x.shape[0] // 2
    return x[:L] + x[L:]


def _rwkv_chunks(chains, k_k, k_a):
    L = chains[0][0].shape[0]
    n2 = 2 * L
    p = RW_PASSES
    ds = [c[5] for c in chains]
    each = lambda f, *ls: [f(*xs) for xs in zip(*ls)]
    dot = lambda dims: (lambda a, b: _mdot(a, b, dims, p))
    lw, r, k_raw, v, a = ([c[i] for c in chains] for i in range(5))

    lane_r = lax.broadcasted_iota(jnp.int32, (LANES, LANES), 0) // RWKV_N
    lane_c = lax.broadcasted_iota(jnp.int32, (LANES, LANES), 1) // RWKV_N
    head_ones = jnp.where(lane_r == lane_c, 1.0, 0.0)
    kkr = each(jnp.multiply, k_raw, k_k)
    ss = each(lambda x: _dot_01_rhs(x * x, head_ones), kkr)
    kap = each(lambda x, q: x / jnp.maximum(jnp.sqrt(q), 1e-12), kkr, ss)
    alp = each(jnp.multiply, kap, a)
    k = each(lambda x, y, z: x * (1.0 + (y - 1.0) * z), k_raw, a, k_a)

    row = lax.broadcasted_iota(jnp.int32, (L, L), 0)
    col = lax.broadcasted_iota(jnp.int32, (L, L), 1)
    tris = (jnp.where(col <= row, 1.0, 0.0), jnp.where(col >= row, 1.0, 0.0))
    lp = [_dot_01_lhs(tris[d], x) for d, x in zip(ds, lw)]
    lp_end = [x[0:1, :] if d else x[L - 1:L, :] for d, x in zip(ds, lp)]
    e_neg = each(lambda x: jnp.exp(-x), lp)
    e_end = each(lambda x, xe: jnp.exp(xe - x), lp, lp_end)
    kap_t = each(lambda x, y, z: x * jnp.exp(y - z), kap, lp, lw)
    r_t = each(lambda x, y: x * jnp.exp(y), r, lp)
    k_h = each(jnp.multiply, k, e_neg)
    a_h = each(jnp.multiply, alp, e_neg)
    k_e = each(jnp.multiply, k, e_end)
    a_e = each(jnp.multiply, alp, e_end)

    lane_head = lax.broadcasted_iota(jnp.int32, (L, LANES), 1) // RWKV_N
    st = lambda x: _stack2(x, lane_head)
    kap_s, r_s, v_s, k_s, a_s = (each(st, x) for x in (kap_t, r_t, v, k_e, a_e))
    rhs_k = each(lambda x: jnp.concatenate([x, x], axis=0), k_h)
    rhs_a = each(lambda x: jnp.concatenate([x, x], axis=0), a_h)

    row2 = lax.broadcasted_iota(jnp.int32, (n2, n2), 0)
    col2 = lax.broadcasted_iota(jnp.int32, (n2, n2), 1)
    same_head = (row2 // L) == (col2 // L)
    strict = (same_head & (col2 < row2), same_head & (col2 > row2))
    incl = (same_head & (col2 <= row2), same_head & (col2 >= row2))
    zero = jnp.zeros((n2, n2), F32)
    masked = lambda masks: (lambda d, x: jnp.where(masks[d], x, zero))
    kr_s = each(lambda x, y: jnp.concatenate([x, y], axis=0), kap_s, r_s)
    p_a = each(dot(NT), kr_s, rhs_a)
    p_k = each(dot(NT), kr_s, rhs_k)
    n_ka = each(masked(strict), ds, each(lambda x: x[:n2], p_a))
    m_kk = each(masked(strict), ds, each(lambda x: x[:n2], p_k))
    a_rk = each(masked(incl), ds, each(lambda x: x[n2:], p_k))
    a_ra = each(masked(incl), ds, each(lambda x: x[n2:], p_a))

    b16 = (row2 // 16) == (col2 // 16)
    b32 = (row2 // 32) == (col2 // 32)
    eye = jnp.where(row2 == col2, 1.0, 0.0)
    n16 = each(lambda x: jnp.where(b16, x, zero), n_ka)
    n_2 = each(dot(NN), n16, n16)
    n_4 = each(dot(NN), n_2, n_2)
    n_8 = each(dot(NN), n_4, n_4)
    inv = each(lambda x: eye - x, n16)
    for pw in (n_2, n_4, n_8):
        inv = each(jnp.add, inv, each(dot(NN), inv, pw))
    for sel in (lambda x: jnp.where(b32 & ~b16, x, zero), lambda x: jnp.where(b32, zero, x)):
        t1 = each(dot(NN), inv, each(sel, n_ka))
        inv = each(jnp.subtract, inv, each(dot(NN), t1, inv))

    mav = each(dot(NN), each(lambda x, y: jnp.concatenate([x, y], axis=0), m_kk, a_rk), v_s)
    mv = each(lambda x: x[:n2], mav)
    av = each(lambda x: x[n2:], mav)
    w1u0 = each(dot(NN), inv, each(lambda x, y: jnp.concatenate([x, y], axis=1), kap_s, mv))
    ar = each(dot(NN), a_ra, w1u0)
    r2 = each(lambda x, y: _unstack2(x - y[:, :LANES]), r_s, ar)
    y0 = each(lambda x, y: _unstack2(x - y[:, LANES:]), av, ar)
    w1_t = each(lambda x: x[:, :LANES].T, w1u0)
    u0_t = each(lambda x: x[:, LANES:].T, w1u0)
    v_t = each(lambda x: x.T, v_s)
    wua = each(dot(NN), each(lambda x, y: jnp.concatenate([x, y], axis=0), w1_t, u0_t), a_s)
    wa = each(lambda x: x[:LANES], wua)
    ua = each(lambda x: x[LANES:], wua)
    vk = each(dot(NN), v_t, k_s)
    g = each(lambda xe, x: jnp.where(row2 == col2, jnp.exp(xe), zero) - x, lp_end, wa)
    b = each(jnp.subtract, vk, ua)
    return list(zip(y0, r2, g, b))


RW_PAIRS_PER_STEP = 2


def _rwkv_body(lwf, rf, kf, vf, af, lwb, rb, kb, vb, ab, kk_ref, ka_ref, yf_ref, yb_ref, s_ref):
    @pl.when(pl.program_id(1) == 0)
    def _():
        s_ref[...] = jnp.zeros(s_ref.shape, F32)

    L = RWKV_CHUNK
    n = RW_CHUNKS_PER_STEP
    rows = lambda j: slice(j * L, (j + 1) * L)
    lanes = lambda q: slice(q * LANES, (q + 1) * LANES)
    refs = ((lwf, rf, kf, vf, af), (lwb, rb, kb, vb, ab))
    y_refs = (yf_ref, yb_ref)
    visit = [(d, j if d == 0 else n - 1 - j) for j in range(n) for d in (0, 1)]
    pairs = range(RW_PAIRS_PER_STEP)
    pre = _rwkv_chunks([tuple(ref[rows(j), lanes(q)] for ref in refs[d]) + (d,) for d, j in visit for q in pairs],
                       [kk_ref[:, lanes(q)] for _ in visit for q in pairs],
                       [ka_ref[:, lanes(q)] for _ in visit for q in pairs])
    s = [[s_ref[q, 0], s_ref[q, 1]] for q in pairs]
    chain = iter(pre)
    for d, j in visit:
        for q in pairs:
            y0, r2, g, b = next(chain)
            y_refs[d][rows(j), lanes(q)] = y0 + _mdot(r2, s[q][d], NT, RW_PASSES)
            s[q][d] = _mdot(s[q][d], g, NN, RW_PASSES) + b
    for q in range(RW_PAIRS_PER_STEP):
        s_ref[q, 0] = s[q][0]
        s_ref[q, 1] = s[q][1]


def rwkv_scan(r, k, v, lw, a, k_k, k_a, n_ctx):
    t, d = r.shape
    L = RWKV_CHUNK
    blk = RW_CHUNKS_PER_STEP * L
    w = RW_PAIRS_PER_STEP * LANES
    assert 2 * L == LANES and t % blk == 0 and n_ctx % blk == 0 and d % w == 0
    nc, nc0 = t // blk, n_ctx // blk
    nh = d // w
    fwd = pl.BlockSpec((blk, w), lambda h, c: (c, h))
    bwd = pl.BlockSpec((blk, w), lambda h, c: (_bwd_chunk(c, nc0, nc), h))
    bwd2 = pl.BlockSpec((blk, w), lambda h, c: (_bwd_chunk(c, nc0, nc), nh + h))
    par = pl.BlockSpec((1, w), lambda h, c: (0, h))
    return pl.pallas_call(
        _rwkv_body,
        grid=(nh, nc),
        in_specs=[fwd] * 5 + [bwd2, bwd, bwd, bwd, bwd2, par, par],
        out_specs=[fwd, bwd],
        out_shape=[jax.ShapeDtypeStruct((t, d), F32)] * 2,
        scratch_shapes=[pltpu.VMEM((RW_PAIRS_PER_STEP, 2, LANES, LANES), F32)],
        compiler_params=pltpu.CompilerParams(
            dimension_semantics=("arbitrary", "arbitrary"), vmem_limit_bytes=VMEM_LIMIT),
        name="rwkv7_scan",
    )(lw, r, k, v, a, lw, r, k, v, a, k_k.reshape(1, d), k_a.reshape(1, d))


MIX_TM = 256


def _mix_body(cur_ref, prev_ref, next_ref, mu_ref, *o_refs, nct, nt):
    i = pl.program_id(0)
    is_ctx = i < nct
    cur = cur_ref[...]
    tm, d = cur.shape
    qd = d // 4
    above = jnp.where(is_ctx | (i == nct), 0.0, prev_ref[...])
    below = jnp.where(is_ctx | (i == nt - 1), 0.0, next_ref[...])
    ext = jnp.concatenate([above, cur, below], axis=0)
    n = ext.shape[0]
    colid = lax.broadcasted_iota(jnp.int32, (n, 2 * qd), 0) % GRID_W
    left = jnp.where(is_ctx | (colid != 0), pltpu.roll(ext[:, :2 * qd], 1, 0), 0.0)
    colid3 = lax.broadcasted_iota(jnp.int32, (n, 3 * qd), 0) % GRID_W
    right = jnp.where(is_ctx | (colid3 != GRID_W - 1), pltpu.roll(ext[:, qd:], n - 1, 0), 0.0)
    mid = slice(GRID_W, GRID_W + tm)
    sh = jnp.concatenate([
        left[mid, :qd],
        jnp.where(is_ctx, left[mid, qd:], right[mid, :qd]),
        jnp.where(is_ctx, right[mid, qd:2 * qd], ext[0:tm, 2 * qd:3 * qd]),
        jnp.where(is_ctx, right[mid, 2 * qd:], ext[2 * GRID_W:2 * GRID_W + tm, 3 * qd:])], axis=1)
    dx = sh - cur
    for b, o_ref in enumerate(o_refs):
        o_ref[...] = (cur + dx * mu_ref[b:b + 1, :]).astype(o_ref.dtype)


def rwkv_mix(h, mu, n_ctx):
    t, d = h.shape
    tm = MIX_TM
    assert t % tm == 0 and n_ctx % tm == 0 and tm % GRID_W == 0
    r = tm // GRID_W
    nt, nu = t // tm, t // GRID_W
    nb = mu.shape[0]
    return pl.pallas_call(
        functools.partial(_mix_body, nct=n_ctx // tm, nt=nt),
        grid=(nt,),
        in_specs=[pl.BlockSpec((tm, d), lambda i: (i, 0)),
                  pl.BlockSpec((GRID_W, d), lambda i: (jnp.maximum(i * r - 1, 0), 0)),
                  pl.BlockSpec((GRID_W, d), lambda i: (jnp.minimum((i + 1) * r, nu - 1), 0)),
                  pl.BlockSpec((nb, d), lambda i: (0, 0))],
        out_specs=[pl.BlockSpec((tm, d), lambda i: (i, 0))] * nb,
        out_shape=[jax.ShapeDtypeStruct((t, d), BF16)] * nb,
        compiler_params=pltpu.CompilerParams(dimension_semantics=("arbitrary",), vmem_limit_bytes=VMEM_LIMIT),
        name="rwkv_mix",
    )(h, h, h, mu)


def _rwkv_out_body(yf_ref, yb_ref, r_ref, k_ref, v_ref, g_ref, af_ref, ab_ref, p_ref, e_ref, et_ref, w_ref, o_ref):
    e, et = e_ref[...], et_ref[...]
    head_sum = lambda x: _dot_01_rhs(_dot_01_rhs(x, e, 2), et, 2)
    k_a, r_k, gain, bias = (p_ref[n:n + 1, :] for n in range(4))
    y = yf_ref[...] + yb_ref[...]
    yc = y - head_sum(y) * (1.0 / RWKV_N)
    var = head_sum(yc * yc) * (1.0 / RWKV_N)
    yn = yc * lax.rsqrt(var + RWKV_GN_EPS) * gain + bias
    kbar = k_ref[...] * (1.0 + (0.5 * (af_ref[...] + ab_ref[...]) - 1.0) * k_a)
    bonus = head_sum(r_ref[...] * kbar * r_k) * v_ref[...]
    z = (yn + bonus) * g_ref[...]
    o_ref[...] = _d(z.astype(BF16), w_ref[...], NN)


def rwkv_out(yf, yb, r, k, v, g, a, k_a, r_k, lnx_g, lnx_b, w_out):
    t, d = yf.shape
    tm = LN_TM
    nh = d // RWKV_N
    row = pl.BlockSpec((tm, d), lambda i: (i, 0))
    const = lambda shape: pl.BlockSpec(shape, lambda i: (0, 0))
    e = (jnp.arange(d)[:, None] // RWKV_N == jnp.arange(LANES)[None, :]).astype(BF16)
    params = jnp.stack([k_a, r_k, lnx_g, lnx_b])
    return pl.pallas_call(
        _rwkv_out_body,
        grid=(t // tm,),
        in_specs=[row] * 6 + [row, pl.BlockSpec((tm, d), lambda i: (i, 1)),
                  const((4, d)), const((d, LANES)), const((LANES, d)), const((d, d))],
        out_specs=row,
        out_shape=jax.ShapeDtypeStruct((t, d), F32),
        compiler_params=pltpu.CompilerParams(dimension_semantics=("arbitrary",), vmem_limit_bytes=VMEM_LIMIT),
        name="rwkv_out",
    )(yf, yb, r, k, v, g, a, a, params, e, e.T, w_out.astype(BF16))


def _sort16_pairs():
    pairs = []
    n, p = 16, 1
    while p < n:
        k = p
        while k >= 1:
            for j in range(k % p, n - k, 2 * k):
                for i in range(min(k, n - j - k)):
                    if (i + j) // (2 * p) == (i + j + k) // (2 * p):
                        pairs.append((i + j, i + j + k))
            k //= 2
        p *= 2
    return pairs


def _top16_sorted(x):
    v = [x[8 * i:8 * i + 8] for i in range(16)]

    def exchange(i, j):
        v[i], v[j] = jnp.maximum(v[i], v[j]), jnp.minimum(v[i], v[j])

    for i, j in _sort16_pairs():
        exchange(i, j)
    for shift in (4, 2, 1):
        other = [pltpu.roll(t, shift, 0) for t in v]
        v = [jnp.maximum(v[i], other[15 - i]) for i in range(16)]
        for dist in (8, 4, 2, 1):
            for i in range(16):
                if i & dist == 0:
                    exchange(i, i + dist)
    return [t[0:1] for t in v]


def _peer_stats(hx_ref, wk_ref, sc_ref, hxb_ref, n_ref, f0_ref, r1_ref, e1_ref):
    tm = hx_ref.shape[0]
    hxb_ref[...] = hx_ref[...].astype(F32).T.astype(BF16)
    rows_per = 4 * LANES

    def scores(c, carry):
        rows = pl.ds(pl.multiple_of(c * rows_per, rows_per), rows_per)
        sc_ref[rows, :] = _d(wk_ref[rows, :], hxb_ref[...], NN)
        return carry

    lax.fori_loop(0, wk_ref.shape[0] // rows_per, scores, 0)

    def block(tb, carry):
        nq = PEER_STAT_LANES // LANES
        lanes = [pl.ds(pl.multiple_of(tb * PEER_STAT_LANES + q * LANES, LANES), LANES) for q in range(nq)]
        head_row = lax.broadcasted_iota(jnp.int32, (PEER_HEADS, LANES), 0)
        zero = jnp.zeros((PEER_HEADS, LANES), F32)

        def extract(q):
            def body(h, tops):
                tops = [list(t) for t in tops]
                rows = [pl.ds(pl.multiple_of((2 * h + p) * N_KEYS, N_KEYS), N_KEYS) for p in range(2)]
                cur = [sc_ref[rows[p], lanes[q]] for p in range(2)]
                top = [_top16_sorted(c) for c in cur]
                rank = jnp.zeros((N_KEYS, LANES), F32)
                for b in range(PEER_TOPK):
                    rank = jnp.where(cur[1] < top[1][b], float(b + 1), rank)
                for p in range(2):
                    for a in range(PEER_TOPK):
                        tops[p][a] = jnp.where(head_row == h, top[p][a], tops[p][a])
                r1_ref[h, :, lanes[q]] = rank.astype(BF16)
                return tuple(tuple(t) for t in tops)
            return lax.fori_loop(0, PEER_HEADS, body, ((zero,) * PEER_TOPK,) * 2)

        tops = [extract(q) for q in range(nq)]
        join = lambda p, a: jnp.concatenate([tops[q][p][a] for q in range(nq)], axis=1)
        top0 = [join(0, a) for a in range(PEER_TOPK)]
        top1 = [join(1, b) for b in range(PEER_TOPK)]
        cands = [top0[a] + top1[b]
                 for a in range(PEER_TOPK) for b in range(PEER_TOPK) if (a + 1) * (b + 1) <= PEER_TOPK]
        c_max = cands[0]
        z = jnp.zeros_like(c_max)
        tau = c_max
        for a in range(PEER_TOPK):
            tau = functools.reduce(jnp.maximum, cands)
            z = z + jnp.exp(tau - c_max)
            cands = [jnp.where(cd >= tau, -jnp.inf, cd) for cd in cands]
        inv_z = 1.0 / z
        nb = []
        for a in range(PEER_TOPK):
            cnt = jnp.zeros_like(tau)
            for b in range(PEER_TOPK):
                cnt = jnp.where(top0[a] + top1[b] >= tau, float(b + 1), cnt)
            nb.append(cnt)

        def factors(h, carry):
            for q in range(nq):
                sub = slice(q * LANES, (q + 1) * LANES)
                row_of = lambda x: jnp.max(jnp.where(head_row == h, x[:, sub], -jnp.inf), axis=0, keepdims=True)
                s0 = sc_ref[pl.ds(pl.multiple_of(2 * h * N_KEYS, N_KEYS), N_KEYS), lanes[q]]
                s1 = sc_ref[pl.ds(pl.multiple_of((2 * h + 1) * N_KEYS, N_KEYS), N_KEYS), lanes[q]]
                n = jnp.zeros((N_KEYS, LANES), F32)
                for a in reversed(range(PEER_TOPK)):
                    n = jnp.where(s0 >= row_of(top0[a]), row_of(nb[a]), n)
                n_ref[h, :, lanes[q]] = n
                f0_ref[h, :, lanes[q]] = jnp.exp(s0 - row_of(top0[0])) * row_of(inv_z)
                e1_ref[h, :, lanes[q]] = jnp.exp(s1 - row_of(top1[0])).astype(BF16)
            return carry

        lax.fori_loop(0, PEER_HEADS, factors, 0)
        return carry

    lax.fori_loop(0, tm // PEER_STAT_LANES, block, 0)


PEER_I_GROUP = 4
PEER_STREAMS = 2


def _peer_body(hx_ref, wk_ref, *rest):
    u_refs, vt_refs = rest[:PEER_STREAMS], rest[PEER_STREAMS:2 * PEER_STREAMS]
    o_ref, sc_ref, hxb_ref, n_ref, f0_ref, r1_ref, e1_ref, w_ref, acc_ref = rest[2 * PEER_STREAMS:]
    e = pl.program_id(1)
    tm = hx_ref.shape[0]
    nsb = tm // PEER_SB

    @pl.when(e == 0)
    def _():
        _peer_stats(hx_ref, wk_ref, sc_ref, hxb_ref, n_ref, f0_ref, r1_ref, e1_ref)
        acc_ref[...] = jnp.zeros(acc_ref.shape, F32)

    i_rows = pl.ds(pl.multiple_of(e * PEER_I_BLOCK, PEER_I_BLOCK), PEER_I_BLOCK)

    def activations(sb):
        parts = []
        for u_ref in u_refs:
            act = _d(u_ref[...], hxb_ref[:, sb * PEER_SB:(sb + 1) * PEER_SB], NN)
            act = act.astype(BF16)
            parts.append(0.5 * act * (1.0 + lax.erf(act * (2.0 ** -0.5))))
        return parts

    def gates(sb, act):
        for hb in range(PEER_SB // LANES):
            lanes = slice(sb * PEER_SB + hb * LANES, sb * PEER_SB + (hb + 1) * LANES)
            sub = slice(hb * LANES, (hb + 1) * LANES)
            n8 = [n_ref[h, i_rows, lanes] for h in range(PEER_HEADS)]
            f8 = [f0_ref[h, i_rows, lanes] for h in range(PEER_HEADS)]
            bcast = lambda x, ii: jnp.broadcast_to(x[ii:ii + 1], (N_KEYS, LANES)).astype(BF16)
            for ig in range(0, PEER_I_BLOCK, PEER_I_GROUP):
                g = [jnp.zeros((N_KEYS, LANES), BF16) for _ in range(PEER_I_GROUP)]
                for h in range(PEER_HEADS):
                    r1 = r1_ref[h, :, lanes]
                    e1 = e1_ref[h, :, lanes]
                    for k in range(PEER_I_GROUP):
                        ii = ig + k
                        g[k] = g[k] + jnp.where(r1 < bcast(n8[h], ii), e1 * bcast(f8[h], ii), jnp.zeros_like(e1))
                for k in range(PEER_I_GROUP):
                    rows = slice((ig + k) * N_KEYS, (ig + k + 1) * N_KEYS)
                    part, off = divmod((ig + k) * N_KEYS, PEER_TE // PEER_STREAMS)
                    w_ref[sb, rows, sub] = g[k] * act[part][off:off + N_KEYS, sub]

    def accumulate(sb):
        cols = slice(sb * PEER_SB, (sb + 1) * PEER_SB)
        dr = acc_ref.shape[0] // PEER_STREAMS
        for k, vt_ref in enumerate(vt_refs):
            acc_ref[k * dr:(k + 1) * dr, cols] += _d(vt_ref[...], w_ref[sb], NN)

    act = activations(0)
    for sb in range(nsb):
        nxt = activations(sb + 1) if sb + 1 < nsb else None
        gates(sb, act)
        accumulate(sb)
        act = nxt

    @pl.when(e == pl.num_programs(1) - 1)
    def _():
        o_ref[...] = acc_ref[...].T


def _fold_body(k_ref, w_ref, o_ref):
    o_ref[...] = _mdot(k_ref[...], w_ref[...], NN, 6)


def peer_fold_keys(wq, keys):
    d = wq.shape[0]
    nhp, nk, dk = keys.shape
    wqt = wq.T.reshape(nhp, dk, d)
    return pl.pallas_call(
        _fold_body,
        grid=(nhp,),
        in_specs=[pl.BlockSpec((None, nk, dk), lambda i: (i, 0, 0)),
                  pl.BlockSpec((None, dk, d), lambda i: (i, 0, 0))],
        out_specs=pl.BlockSpec((nk, d), lambda i: (i, 0)),
        out_shape=jax.ShapeDtypeStruct((nhp * nk, d), F32),
        compiler_params=pltpu.CompilerParams(dimension_semantics=("arbitrary",), vmem_limit_bytes=VMEM_LIMIT),
        name="peer_fold_keys",
    )(keys, wqt)


def peer(hx, wk, u_bf, vt_bf):
    t, d = hx.shape
    tm = PEER_TM
    assert t % tm == 0
    ne = u_bf.shape[0] // PEER_TE
    h = PEER_HEADS
    ns = PEER_STREAMS
    return pl.pallas_call(
        _peer_body,
        grid=(t // tm, ne),
        in_specs=[pl.BlockSpec((tm, d), lambda i, e: (i, 0), pipeline_mode=pl.Buffered(1)),
                  pl.BlockSpec(wk.shape, lambda i, e: (0, 0), pipeline_mode=pl.Buffered(1)),
                  *[pl.BlockSpec((PEER_TE // ns, d), functools.partial(lambda i, e, k: (e * ns + k, 0), k=k))
                    for k in range(ns)],
                  *[pl.BlockSpec((None, d // ns, PEER_TE), functools.partial(lambda i, e, k: (e, k, 0), k=k))
                    for k in range(ns)]],
        out_specs=pl.BlockSpec((tm, d), lambda i, e: (i, 0)),
        out_shape=jax.ShapeDtypeStruct((t, d), F32),
        scratch_shapes=[pltpu.VMEM((2 * h * N_KEYS, tm), F32),
                        pltpu.VMEM((d, tm), BF16),
                        pltpu.VMEM((h, N_KEYS, tm), F32), pltpu.VMEM((h, N_KEYS, tm), F32),
                        pltpu.VMEM((h, N_KEYS, tm), BF16), pltpu.VMEM((h, N_KEYS, tm), BF16),
                        pltpu.VMEM((tm // PEER_SB, PEER_TE, PEER_SB), BF16), pltpu.VMEM((d, tm), F32)],
        compiler_params=pltpu.CompilerParams(
            dimension_semantics=("arbitrary", "arbitrary"), vmem_limit_bytes=VMEM_LIMIT),
        name="peer_dense",
    )(hx, wk.astype(BF16), *([u_bf] * ns), *([vt_bf] * ns))


LN_TM = 256


def _ln_mod_body(xs_ref, y_ref, mg_ref, mn_ref, lng_ref, lnb_ref, xo_ref, ho_ref, *, gate_col, mod_col, nct, nt):
    i = pl.program_id(0)
    d = xs_ref.shape[1]
    is_ctx = i < nct

    def pick(ref, col):
        return jnp.where(is_ctx, ref[1:2, col * d:(col + 1) * d], ref[0:1, col * d:(col + 1) * d])

    z = DN_ALPHA * xs_ref[...] + pick(mg_ref, gate_col) * y_ref[...]
    mu = jnp.mean(z, axis=-1, keepdims=True)
    zc = z - mu
    var = jnp.mean(zc * zc, axis=-1, keepdims=True)
    xn = zc * lax.rsqrt(var + LN_EPS) * lng_ref[...] + lnb_ref[...]
    xo_ref[...] = xn
    h = xn * (1.0 + pick(mn_ref, mod_col + 1)) + pick(mn_ref, mod_col)
    ho_ref[...] = jnp.where(i < nt, h, 0.0).astype(ho_ref.dtype)


def ln_mod(xs, y, mod_gate, gate_col, mod_next, mod_col, ln_g, ln_b, n_ctx, pad_to, h_dtype):
    t, d = xs.shape
    tm = LN_TM
    assert t % tm == 0 and n_ctx % tm == 0 and pad_to % tm == 0 and pad_to >= t
    nt = t // tm
    row = lambda i: (jnp.minimum(i, nt - 1), 0)
    full = lambda a: pl.BlockSpec(a.shape, lambda i: (0, 0))
    return pl.pallas_call(
        functools.partial(_ln_mod_body, gate_col=gate_col, mod_col=mod_col, nct=n_ctx // tm, nt=nt),
        grid=(pad_to // tm,),
        in_specs=[pl.BlockSpec((tm, d), row), pl.BlockSpec((tm, d), row), full(mod_gate), full(mod_next),
                  pl.BlockSpec((1, d), lambda i: (0, 0)), pl.BlockSpec((1, d), lambda i: (0, 0))],
        out_specs=[pl.BlockSpec((tm, d), row), pl.BlockSpec((tm, d), lambda i: (i, 0))],
        out_shape=[jax.ShapeDtypeStruct((t, d), F32), jax.ShapeDtypeStruct((pad_to, d), h_dtype)],
        compiler_params=pltpu.CompilerParams(dimension_semantics=("arbitrary",), vmem_limit_bytes=VMEM_LIMIT),
        name="ln_mod",
    )(xs, y, mod_gate, mod_next, ln_g.reshape(1, d), ln_b.reshape(1, d))


def _pad_cols(w, n):
    return jnp.pad(w, ((0, 0), (0, n - w.shape[1])))


def _mlstm_layer(h, n_ctx, w_in, b_in, conv_w, conv_b, hn_g, w_out):
    d = D_MODEL
    t = h.shape[0]
    qk_pre = matmul(h, w_in[:, :2 * d], b_in[:2 * d])
    v = matmul(h, w_in[:, 2 * d:3 * d], b_in[2 * d:3 * d], out_dtype=BF16)
    o = matmul(h, w_in[:, 3 * d:4 * d], b_in[3 * d:4 * d], act="sigmoid", out_dtype=BF16)
    g = matmul(h, _pad_cols(w_in[:, 4 * d:], LANES), jnp.pad(b_in[4 * d:], (0, LANES - 4 * MLSTM_HEADS)))
    g = g[:, :4 * MLSTM_HEADS].reshape(t, 4, MLSTM_HEADS)
    g = jnp.concatenate([g[:, :2], jax.nn.log_sigmoid(g[:, 2:])], axis=1)
    gh = jnp.transpose(g, (2, 0, 1))
    gcol = jnp.pad(gh, ((0, 0), (0, 0), (0, LANES - 4)))
    grow = jnp.pad(jnp.transpose(gh, (0, 2, 1)), ((0, 0), (0, 4), (0, 0)))
    qk = mlstm_conv(qk_pre, conv_w, conv_b, n_ctx)
    hf, hb = mlstm_scan(qk, v, gcol, grow, n_ctx)
    return mlstm_out(hf, hb, o, hn_g, w_out)


def _rwkv_layer(h, n_ctx, mu, w_rkv, w0, w1, w2, a0, a1, a2, g1, g2, k_k, k_a, r_k, lnx_g, lnx_b, w_out):
    d = D_MODEL
    xm = rwkv_mix(h, mu, n_ctx)
    r = matmul(xm[0], w_rkv[0])
    k = matmul(xm[1], w_rkv[1])
    v = matmul(xm[2], w_rkv[2])

    def lora_pair(x, w_in, w_mid, bias, act_mid, act_out):
        rank = w_in.shape[-1]
        w_a = _pad_cols(jnp.concatenate([w_in[0], w_in[1]], axis=1), LANES)
        zpad = jnp.zeros((rank, d), F32)
        w_b = jnp.concatenate([jnp.concatenate([w_mid[0], zpad], axis=1),
                               jnp.concatenate([zpad, w_mid[1]], axis=1)], axis=0)
        w_b = jnp.pad(w_b, ((0, LANES - 2 * rank), (0, 0)))
        mid = matmul(x, w_a, act=act_mid, out_dtype=BF16)
        return matmul(mid, w_b, jnp.concatenate([bias[0], bias[1]]), act=act_out)

    lw = lora_pair(xm[3], w1, w2, w0, "tanh", "logdecay")
    a = lora_pair(xm[4], a1, a2, a0, None, "sigmoid")
    gpad = 2 * LANES
    gg = matmul(xm[5], _pad_cols(g1, gpad), act="sigmoid", out_dtype=BF16)
    g = matmul(gg, jnp.pad(g2, ((0, gpad - g1.shape[1]), (0, 0))))
    yf, yb = rwkv_scan(r, k, v, lw, a, k_k, k_a, n_ctx)
    return rwkv_out(yf, yb, r, k, v, g, a, k_a, r_k, lnx_g, lnx_b, w_out)


def _forward(x, c, ctx, c_ctx, ada_w, ada_b, ln_g, ln_b,
             ml_w_in, ml_b_in, ml_conv_w, ml_conv_b, ml_hn_g, ml_w_out,
             rw_mu, rw_w_rkv, rw_w0, rw_w1, rw_w2, rw_a0, rw_a1, rw_a2, rw_g1, rw_g2,
             rw_k_k, rw_k_a, rw_r_k, rw_lnx_g, rw_lnx_b, rw_w_out,
             pk_wq, pk_keys, pk_u, pk_v):
    d = D_MODEL
    n_ctx = ctx.shape[1]
    xs = jnp.concatenate([ctx[0], x[0]], axis=0)
    t = xs.shape[0]
    t_pad = -(-t // PEER_TM) * PEER_TM
    s_in = jnp.zeros((8, d), F32).at[0].set(jax.nn.silu(c[0])).at[1].set(jax.nn.silu(c_ctx))
    depth = ada_w.shape[0]
    mods = [matmul(s_in, ada_w[i], ada_b[i], passes=3) for i in range(depth)]
    is_ctx = (jnp.arange(t) < n_ctx)[:, None]
    m0 = [jnp.where(is_ctx, mods[0][1, n * d:(n + 1) * d], mods[0][0, n * d:(n + 1) * d]) for n in range(2)]
    mixer_dtype = lambda i: BF16 if i % 2 == 0 else F32
    h = (xs * (1.0 + m0[1]) + m0[0]).astype(mixer_dtype(0))
    for i in range(depth):
        j = i // 2
        if i % 2 == 0:
            y = _mlstm_layer(h, n_ctx, ml_w_in[j], ml_b_in[j], ml_conv_w[j], ml_conv_b[j],
                             ml_hn_g[j], ml_w_out[j])
        else:
            y = _rwkv_layer(h, n_ctx, rw_mu[j], rw_w_rkv[j], rw_w0[j], rw_w1[j], rw_w2[j],
                            rw_a0[j], rw_a1[j], rw_a2[j], rw_g1[j], rw_g2[j], rw_k_k[j],
                            rw_k_a[j], rw_r_k[j], rw_lnx_g[j], rw_lnx_b[j], rw_w_out[j])
        xs, h = ln_mod(xs, y, mods[i], 2, mods[i], 3, ln_g[i, 0], ln_b[i, 0], n_ctx, t_pad, BF16)
        wk = peer_fold_keys(pk_wq[i], pk_keys[i].reshape(2 * PEER_HEADS, N_KEYS, PEER_DQ // 2))
        vt = jnp.swapaxes(pk_v[i].astype(BF16).reshape(-1, PEER_TE, d), 1, 2)
        y = peer(h, wk, pk_u[i].astype(BF16), vt)
        xs, h = ln_mod(xs, y, mods[i], 5, mods[min(i + 1, depth - 1)], 0, ln_g[i, 1], ln_b[i, 1], n_ctx, t,
                       mixer_dtype(i + 1))
    return xs[n_ctx:][None]


def kernel(x, c, ctx, c_ctx, ada_w, ada_b, ln_g, ln_b, ml_w_in, ml_b_in, ml_conv_w, ml_conv_b, ml_hn_g, ml_w_out, rw_mu, rw_w_rkv, rw_w0, rw_w1, rw_w2, rw_a0, rw_a1, rw_a2, rw_g1, rw_g2, rw_k_k, rw_k_a, rw_r_k, rw_lnx_g, rw_lnx_b, rw_w_out, pk_wq, pk_keys, pk_u, pk_v):
    return _forward(x, c, ctx, c_ctx, ada_w, ada_b, ln_g, ln_b,
                    ml_w_in, ml_b_in, ml_conv_w, ml_conv_b, ml_hn_g, ml_w_out,
                    rw_mu, rw_w_rkv, rw_w0, rw_w1, rw_w2, rw_a0, rw_a1, rw_a2, rw_g1, rw_g2,
                    rw_k_k, rw_k_a, rw_r_k, rw_lnx_g, rw_lnx_b, rw_w_out,
                    pk_wq, pk_keys, pk_u, pk_v)
```

```python
import functools

import jax
import jax.numpy as jnp
from jax import lax
from jax.experimental import pallas as pl
from jax.experimental.pallas import tpu as pltpu

F32 = jnp.float32
BF16 = jnp.bfloat16

D_MODEL = 1024
DEPTH = 4
GRID_W = 64
DN_ALPHA = (2.0 * DEPTH) ** 0.25
LN_EPS = 1e-5

MLSTM_HEADS = 4
MLSTM_DH = D_MODEL // MLSTM_HEADS
MLSTM_CHUNK = 128
M_INIT = -1e30

RWKV_N = 64
RWKV_HEADS = D_MODEL // RWKV_N
RWKV_CHUNK = 64
RWKV_GN_EPS = 64e-5

N_KEYS = 128
PEER_HEADS = 8
PEER_DQ = 256
PEER_TOPK = 16
PEER_I_BLOCK = 16
PEER_TE = PEER_I_BLOCK * N_KEYS
PEER_TM = 768
PEER_SB = 256
PEER_STAT_LANES = 256

LANES = 128
VMEM_LIMIT = 62 * 1024 * 1024

NN = ((1,), (0,))
NT = ((1,), (1,))


def _split(x, n):
    parts = []
    r = x.astype(F32)
    for i in range(n):
        p = r.astype(BF16)
        parts.append(p)
        if i + 1 < n:
            r = r - p.astype(F32)
    return parts


def _d(a, b, dims):
    return lax.dot_general(a, b, (dims, ((), ())), preferred_element_type=F32)


def _mdot(a, b, dims, passes):
    if passes == 1:
        return _d(a.astype(BF16), b.astype(BF16), dims)
    if passes == 3:
        a0, a1 = _split(a, 2)
        b0, b1 = _split(b, 2)
        return (_d(a0, b1, dims) + _d(a1, b0, dims)) + _d(a0, b0, dims)
    a0, a1, a2 = _split(a, 3)
    b0, b1, b2 = _split(b, 3)
    lo = (_d(a0, b2, dims) + _d(a2, b0, dims)) + _d(a1, b1, dims)
    mid = _d(a0, b1, dims) + _d(a1, b0, dims)
    return (lo + mid) + _d(a0, b0, dims)


def _dot_01_lhs(m01, x):
    mb = m01.astype(BF16)
    x0, x1, x2 = _split(x, 3)
    return (_d(mb, x2, NN) + _d(mb, x1, NN)) + _d(mb, x0, NN)


def _dot_01_rhs(x, m01, pieces=3):
    mb = m01.astype(BF16)
    return functools.reduce(jnp.add, [_d(xp, mb, NN) for xp in reversed(_split(x, pieces))])


def _pick(n, cands):
    for c in cands:
        if n % c == 0:
            return c
    raise ValueError(f"no tile for {n}")


_ACTS = {None: lambda x: x, "sigmoid": jax.nn.sigmoid, "tanh": jnp.tanh,
         "logdecay": lambda x: -(2.718281828459045 ** -0.5) * jax.nn.sigmoid(x)}


def _mm_body(x_ref, w_ref, b_ref, o_ref, *, passes, act):
    o_ref[...] = _ACTS[act](_mdot(x_ref[...], w_ref[...], NN, passes) + b_ref[...]).astype(o_ref.dtype)


def matmul(x, w, b=None, *, passes=1, act=None, out_dtype=F32):
    m, k = x.shape
    n = w.shape[1]
    assert n % LANES == 0 and w.shape[0] == k
    tm = m if m <= 1024 else _pick(m, (640, 512, 384, 256, 128))
    tn = _pick(n, (1024, 768, 640, 512, 384, 256, 128))
    if b is None:
        b = jnp.zeros((n,), F32)
    if passes == 1:
        w = w.astype(BF16)
    return pl.pallas_call(
        functools.partial(_mm_body, passes=passes, act=act),
        grid=(m // tm, n // tn),
        in_specs=[pl.BlockSpec((tm, k), lambda i, j: (i, 0)),
                  pl.BlockSpec((k, tn), lambda i, j: (0, j)),
                  pl.BlockSpec((1, tn), lambda i, j: (0, j))],
        out_specs=pl.BlockSpec((tm, tn), lambda i, j: (i, j)),
        out_shape=jax.ShapeDtypeStruct((m, n), out_dtype),
        compiler_params=pltpu.CompilerParams(
            dimension_semantics=("arbitrary", "arbitrary"), vmem_limit_bytes=VMEM_LIMIT),
        name="proj_matmul",
    )(x, w, b.reshape(1, n).astype(F32))


LN_TM = 256


def _residual_ln_mod(y, xs_ref, mg_ref, mn_ref, lng_ref, lnb_ref, xo_ref, ho_ref, *, gate_col, mod_col, nct, nt):
    i = pl.program_id(0)
    d = xs_ref.shape[1]
    is_ctx = i < nct

    def pick(ref, col):
        return jnp.where(is_ctx, ref[1:2, col * d:(col + 1) * d], ref[0:1, col * d:(col + 1) * d])

    z = DN_ALPHA * xs_ref[...] + pick(mg_ref, gate_col) * y
    mu = jnp.mean(z, axis=-1, keepdims=True)
    zc = z - mu
    var = jnp.mean(zc * zc, axis=-1, keepdims=True)
    xn = zc * lax.rsqrt(var + LN_EPS) * lng_ref[...] + lnb_ref[...]
    xo_ref[...] = xn
    h = xn * (1.0 + pick(mn_ref, mod_col + 1)) + pick(mn_ref, mod_col)
    ho_ref[...] = jnp.where(i < nt, h, 0.0).astype(ho_ref.dtype)


def _ln_operands(xs, ln):
    t, d = xs.shape
    tm = LN_TM
    assert t % tm == 0 and ln["n_ctx"] % tm == 0 and ln["pad_to"] % tm == 0 and ln["pad_to"] >= t
    nt = t // tm
    row = lambda i: (jnp.minimum(i, nt - 1), 0)
    full = lambda a: pl.BlockSpec(a.shape, lambda i: (0, 0))
    vec = pl.BlockSpec((1, d), lambda i: (0, 0))
    in_specs = [pl.BlockSpec((tm, d), row), full(ln["mod_gate"]), full(ln["mod_next"]), vec, vec]
    operands = [xs, ln["mod_gate"], ln["mod_next"], ln["ln_g"].reshape(1, d), ln["ln_b"].reshape(1, d)]
    out_specs = [pl.BlockSpec((tm, d), row), pl.BlockSpec((tm, d), lambda i: (i, 0))]
    out_shape = [jax.ShapeDtypeStruct((t, d), F32), jax.ShapeDtypeStruct((ln["pad_to"], d), ln["h_dtype"])]
    static = dict(gate_col=ln["gate_col"], mod_col=ln["mod_col"], nct=ln["n_ctx"] // tm, nt=nt)
    return row, in_specs, operands, out_specs, out_shape, static


CONV_TM = 256
CONV_CB = 512


def _conv_body(cur_ref, prev_ref, next_ref, w_ref, b_ref, o_ref, *, nct, nt, q_blocks, q_scale):
    i = pl.program_id(0)
    j = pl.program_id(1)
    is_ctx = i < nct
    tm = cur_ref.shape[0]
    above = jnp.where(is_ctx | (i == nct), 0.0, prev_ref[...])
    below = jnp.where(is_ctx | (i == nt - 1), 0.0, next_ref[...])
    ext = jnp.concatenate([above, cur_ref[...], below], axis=0)
    n = ext.shape[0]
    colid = lax.broadcasted_iota(jnp.int32, ext.shape, 0) % GRID_W
    left = jnp.where(is_ctx | (colid != 0), pltpu.roll(ext, 1, 0), 0.0)
    right = jnp.where(is_ctx | (colid != GRID_W - 1), pltpu.roll(ext, n - 1, 0), 0.0)
    w = w_ref[...]
    acc = jnp.zeros((tm, ext.shape[1]), F32) + b_ref[...]
    for di in range(3):
        rows = slice(di * GRID_W, di * GRID_W + tm)
        tap = left[rows] * w[3 * di:3 * di + 1] + ext[rows] * w[3 * di + 1:3 * di + 2] \
            + right[rows] * w[3 * di + 2:3 * di + 3]
        acc = acc + (tap if di == 1 else jnp.where(is_ctx, 0.0, tap))
    y = acc * jax.nn.sigmoid(acc)
    o_ref[...] = (y * jnp.where(j < q_blocks, q_scale, 1.0)).astype(o_ref.dtype)


def mlstm_conv(qk_pre, conv_w, conv_b, n_ctx):
    t, c = qk_pre.shape
    tm, cb = CONV_TM, CONV_CB
    assert t % tm == 0 and n_ctx % tm == 0 and tm % GRID_W == 0 and c % (2 * cb) == 0
    r = tm // GRID_W
    nt, nu = t // tm, t // GRID_W
    return pl.pallas_call(
        functools.partial(_conv_body, nct=n_ctx // tm, nt=nt, q_blocks=c // (2 * cb), q_scale=MLSTM_DH ** -0.5),
        grid=(nt, c // cb),
        in_specs=[pl.BlockSpec((tm, cb), lambda i, j: (i, j)),
                  pl.BlockSpec((GRID_W, cb), lambda i, j: (jnp.maximum(i * r - 1, 0), j)),
                  pl.BlockSpec((GRID_W, cb), lambda i, j: (jnp.minimum((i + 1) * r, nu - 1), j)),
                  pl.BlockSpec((9, cb), lambda i, j: (0, j)),
                  pl.BlockSpec((1, cb), lambda i, j: (0, j))],
        out_specs=pl.BlockSpec((tm, cb), lambda i, j: (i, j)),
        out_shape=jax.ShapeDtypeStruct((t, c), BF16),
        compiler_params=pltpu.CompilerParams(
            dimension_semantics=("arbitrary", "arbitrary"), vmem_limit_bytes=VMEM_LIMIT),
        name="mlstm_conv",
    )(qk_pre, qk_pre, qk_pre, conv_w.reshape(9, c), conv_b.reshape(1, c))


def _mlstm_out_body(hf_ref, hb_ref, o_ref, g_ref, w_ref, *ln_refs, **ln_static):
    h = hf_ref[...] + hb_ref[...]
    parts = []
    for a in range(MLSTM_HEADS):
        x = h[:, a * MLSTM_DH:(a + 1) * MLSTM_DH]
        mu = jnp.mean(x, axis=-1, keepdims=True)
        xc = x - mu
        var = jnp.mean(xc * xc, axis=-1, keepdims=True)
        parts.append(xc * lax.rsqrt(var + LN_EPS))
    hn = jnp.concatenate(parts, axis=1)
    z = o_ref[...].astype(F32) * hn * g_ref[...]
    _residual_ln_mod(_d(z.astype(BF16), w_ref[...], NN), *ln_refs, **ln_static)


def mlstm_out(hf, hb, o, hn_g, w_out, xs, ln):
    t, d = hf.shape
    row_map, ln_in, ln_ops, out_specs, out_shape, static = _ln_operands(xs, ln)
    row = pl.BlockSpec((LN_TM, d), row_map)
    return pl.pallas_call(
        functools.partial(_mlstm_out_body, **static),
        grid=(ln["pad_to"] // LN_TM,),
        in_specs=[row, row, row, pl.BlockSpec((1, d), lambda i: (0, 0)), pl.BlockSpec((d, d), lambda i: (0, 0))] + ln_in,
        out_specs=out_specs,
        out_shape=out_shape,
        compiler_params=pltpu.CompilerParams(dimension_semantics=("arbitrary",), vmem_limit_bytes=VMEM_LIMIT),
        name="mlstm_out",
    )(hf, hb, o, hn_g.reshape(1, d), w_out.astype(BF16), *ln_ops)


ML_HEADS_PER_STEP = 4


def _mlstm_body(qf, kf, vf, gcf, grf, qb, kb, vb, gcb, grb, hf_ref, hb_ref, c_ref, n_ref, m_ref):
    @pl.when(pl.program_id(1) == 0)
    def _():
        c_ref[...] = jnp.zeros(c_ref.shape, F32)
        n_ref[...] = jnp.zeros(n_ref.shape, F32)
        m_ref[...] = jnp.full(m_ref.shape, M_INIT, F32)

    dh = MLSTM_DH
    cs = [(a, d) for a in range(ML_HEADS_PER_STEP) for d in (0, 1)]
    each = lambda f, *ls: [f(*xs) for xs in zip(*ls)]
    cols = lambda a: slice(a * dh, (a + 1) * dh)
    qs, ks, vs, gcs, grs = (qf, qb), (kf, kb), (vf, vb), (gcf, gcb), (grf, grb)
    qb16 = [qs[d][:, cols(a)] for a, d in cs]
    kb16 = [ks[d][:, cols(a)] for a, d in cs]
    vb16 = [vs[d][:, cols(a)] for a, d in cs]
    gc = [gcs[d][a] for a, d in cs]
    gr = [grs[d][a] for a, d in cs]
    q, k = (each(lambda x: x.astype(F32), x16) for x16 in (qb16, kb16))
    L = q[0].shape[0]
    row = lax.broadcasted_iota(jnp.int32, (L, L), 0)
    col = lax.broadcasted_iota(jnp.int32, (L, L), 1)
    seen = (col <= row, col >= row)
    tri = [jnp.where(m, 1.0, 0.0) for m in seen]
    b_col = [_dot_01_lhs(tri[d], g)[:, 2 + d:3 + d] for (a, d), g in zip(cs, gc)]
    b_row = [_dot_01_rhs(g, tri[1 - d])[2 + d:3 + d, :] for (a, d), g in zip(cs, gr)]
    ig_col = [g[:, d:d + 1] for (a, d), g in zip(cs, gc)]
    ig_row = [g[d:d + 1, :] for (a, d), g in zip(cs, gr)]
    m_st = [m_ref[a, d, 0:1, 0:1] for a, d in cs]
    c_st = [c_ref[a, d] for a, d in cs]
    n_st = [n_ref[a, d] for a, d in cs]

    dlog = [jnp.where(seen[d], bc - br + ir, -jnp.inf) for (a, d), bc, br, ir in zip(cs, b_col, b_row, ig_row)]
    m_inter = each(jnp.add, b_col, m_st)
    m_t = each(lambda mi, dl: jnp.maximum(mi, jnp.max(dl, axis=1, keepdims=True)), m_inter, dlog)
    qk = each(lambda x, y: _d(x, y, NT), qb16, kb16)
    s = each(lambda x, dl, mt: x * jnp.exp(dl - mt), qk, dlog, m_t)
    dec = each(lambda mi, mt: jnp.exp(mi - mt), m_inter, m_t)
    sv = each(lambda x, y: _d(x.astype(BF16), y, NN), s, vb16)
    qc = each(lambda x, y: _d(x, y.astype(BF16), NN), qb16, c_st)
    num = each(lambda x, dc, y: x + dc * y, sv, dec, qc)
    den = each(lambda x, dc, qq, nn: jnp.sum(x, axis=1, keepdims=True) + dc * jnp.sum(qq * nn, axis=1, keepdims=True),
               s, dec, q, n_st)
    h = each(lambda nu, de, mt: nu / jnp.maximum(jnp.abs(de), jnp.exp(-mt)), num, den, m_t)
    h_refs = (hf_ref, hb_ref)
    for (a, d), x in zip(cs, h):
        h_refs[d][:, cols(a)] = x

    b_last = [bc[0:1, :] if d else bc[L - 1:L, :] for (a, d), bc in zip(cs, b_col)]
    w_c = each(lambda bl, bc, ic: bl - bc + ic, b_last, b_col, ig_col)
    m_new = each(lambda bl, ms, w: jnp.maximum(bl + ms, jnp.max(w, axis=0, keepdims=True)), b_last, m_st, w_c)
    a_c = each(lambda w, mn: jnp.exp(w - mn), w_c, m_new)
    g_prev = each(lambda bl, ms, mn: jnp.exp(bl + ms - mn), b_last, m_st, m_new)
    ak = each(jnp.multiply, a_c, k)
    kv = each(lambda x, y: _d(x.T.astype(BF16), y, NN), ak, vb16)
    for i, (a, d) in enumerate(cs):
        c_ref[a, d] = g_prev[i] * c_st[i] + kv[i]
        n_ref[a, d] = g_prev[i] * n_st[i] + jnp.sum(ak[i], axis=0, keepdims=True)
        m_ref[a, d] = jnp.broadcast_to(m_new[i], m_ref.shape[2:])


def _bwd_chunk(c, nc0, nc):
    return jnp.where(c < nc0, nc0 - 1 - c, nc - 1 - (c - nc0))


def mlstm_scan(qk, v, gcol, grow, n_ctx):
    t = v.shape[0]
    L, dh, hs = MLSTM_CHUNK, MLSTM_DH, ML_HEADS_PER_STEP
    ng = MLSTM_HEADS // hs
    nc, nc0 = t // L, n_ctx // L
    idf = lambda c: c
    idb = lambda c: _bwd_chunk(c, nc0, nc)
    qkv = lambda f, off: pl.BlockSpec((L, hs * dh), lambda h, c: (f(c), off + h))
    gc_spec = lambda f: pl.BlockSpec((hs, L, LANES), lambda h, c: (h, f(c), 0))
    gr_spec = lambda f: pl.BlockSpec((hs, 8, L), lambda h, c: (h, 0, f(c)))
    return pl.pallas_call(
        _mlstm_body,
        grid=(ng, nc),
        in_specs=[qkv(idf, 0), qkv(idf, ng), qkv(idf, 0), gc_spec(idf), gr_spec(idf),
                  qkv(idb, 0), qkv(idb, ng), qkv(idb, 0), gc_spec(idb), gr_spec(idb)],
        out_specs=[qkv(idf, 0), qkv(idb, 0)],
        out_shape=[jax.ShapeDtypeStruct((t, D_MODEL), F32)] * 2,
        scratch_shapes=[pltpu.VMEM((hs, 2, dh, dh), F32), pltpu.VMEM((hs, 2, 1, dh), F32),
                        pltpu.VMEM((hs, 2, 8, LANES), F32)],
        compiler_params=pltpu.CompilerParams(
            dimension_semantics=("arbitrary", "arbitrary"), vmem_limit_bytes=VMEM_LIMIT),
        name="mlstm_scan",
    )(qk, qk, v, gcol, grow, qk, qk, v, gcol, grow)


RW_PASSES = 1
RW_CHUNKS_PER_STEP = 4


def _stack2(x, lane_head):
    return jnp.concatenate([jnp.where(lane_head == 0, x, 0.0), jnp.where(lane_head == 1, x, 0.0)], axis=0)


def _unstack2(x):
    L = x.shape[0] // 2
    return x[:L] + x[L:]


def _rwkv_chunks(chains, k_k, k_a):
    L = chains[0][0].shape[0]
    n2 = 2 * L
    p = RW_PASSES
    ds = [c[5] for c in chains]
    each = lambda f, *ls: [f(*xs) for xs in zip(*ls)]
    dot = lambda dims: (lambda a, b: _mdot(a, b, dims, p))
    lw, r, k_raw, v, a = ([c[i] for c in chains] for i in range(5))

    lane_r = lax.broadcasted_iota(jnp.int32, (LANES, LANES), 0) // RWKV_N
    lane_c = lax.broadcasted_iota(jnp.int32, (LANES, LANES), 1) // RWKV_N
    head_ones = jnp.where(lane_r == lane_c, 1.0, 0.0)
    kkr = each(jnp.multiply, k_raw, k_k)
    ss = each(lambda x: _dot_01_rhs(x * x, head_ones), kkr)
    kap = each(lambda x, q: x / jnp.maximum(jnp.sqrt(q), 1e-12), kkr, ss)
    alp = each(jnp.multiply, kap, a)
    k = each(lambda x, y, z: x * (1.0 + (y - 1.0) * z), k_raw, a, k_a)

    row = lax.broadcasted_iota(jnp.int32, (L, L), 0)
    col = lax.broadcasted_iota(jnp.int32, (L, L), 1)
    tris = (jnp.where(col <= row, 1.0, 0.0), jnp.where(col >= row, 1.0, 0.0))
    lp = [_dot_01_lhs(tris[d], x) for d, x in zip(ds, lw)]
    lp_end = [x[0:1, :] if d else x[L - 1:L, :] for d, x in zip(ds, lp)]
    e_neg = each(lambda x: jnp.exp(-x), lp)
    e_end = each(lambda x, xe: jnp.exp(xe - x), lp, lp_end)
    kap_t = each(lambda x, y, z: x * jnp.exp(y - z), kap, lp, lw)
    r_t = each(lambda x, y: x * jnp.exp(y), r, lp)
    k_h = each(jnp.multiply, k, e_neg)
    a_h = each(jnp.multiply, alp, e_neg)
    k_e = each(jnp.multiply, k, e_end)
    a_e = each(jnp.multiply, alp, e_end)

    lane_head = lax.broadcasted_iota(jnp.int32, (L, LANES), 1) // RWKV_N
    st = lambda x: _stack2(x, lane_head)
    kap_s, r_s, v_s, k_s, a_s = (each(st, x) for x in (kap_t, r_t, v, k_e, a_e))
    rhs_k = each(lambda x: jnp.concatenate([x, x], axis=0), k_h)
    rhs_a = each(lambda x: jnp.concatenate([x, x], axis=0), a_h)

    row2 = lax.broadcasted_iota(jnp.int32, (n2, n2), 0)
    col2 = lax.broadcasted_iota(jnp.int32, (n2, n2), 1)
    same_head = (row2 // L) == (col2 // L)
    strict = (same_head & (col2 < row2), same_head & (col2 > row2))
    incl = (same_head & (col2 <= row2), same_head & (col2 >= row2))
    zero = jnp.zeros((n2, n2), F32)
    masked = lambda masks: (lambda d, x: jnp.where(masks[d], x, zero))
    kr_s = each(lambda x, y: jnp.concatenate([x, y], axis=0), kap_s, r_s)
    p_a = each(dot(NT), kr_s, rhs_a)
    p_k = each(dot(NT), kr_s, rhs_k)
    n_ka = each(masked(strict), ds, each(lambda x: x[:n2], p_a))
    m_kk = each(masked(strict), ds, each(lambda x: x[:n2], p_k))
    a_rk = each(masked(incl), ds, each(lambda x: x[n2:], p_k))
    a_ra = each(masked(incl), ds, each(lambda x: x[n2:], p_a))

    b16 = (row2 // 16) == (col2 // 16)
    b32 = (row2 // 32) == (col2 // 32)
    eye = jnp.where(row2 == col2, 1.0, 0.0)
    n16 = each(lambda x: jnp.where(b16, x, zero), n_ka)
    n_2 = each(dot(NN), n16, n16)
    n_4 = each(dot(NN), n_2, n_2)
    n_8 = each(dot(NN), n_4, n_4)
    inv = each(lambda x: eye - x, n16)
    for pw in (n_2, n_4, n_8):
        inv = each(jnp.add, inv, each(dot(NN), inv, pw))
    for sel in (lambda x: jnp.where(b32 & ~b16, x, zero), lambda x: jnp.where(b32, zero, x)):
        t1 = each(dot(NN), inv, each(sel, n_ka))
        inv = each(jnp.subtract, inv, each(dot(NN), t1, inv))

    mav = each(dot(NN), each(lambda x, y: jnp.concatenate([x, y], axis=0), m_kk, a_rk), v_s)
    mv = each(lambda x: x[:n2], mav)
    av = each(lambda x: x[n2:], mav)
    w1u0 = each(dot(NN), inv, each(lambda x, y: jnp.concatenate([x, y], axis=1), kap_s, mv))
    ar = each(dot(NN), a_ra, w1u0)
    r2 = each(lambda x, y: _unstack2(x - y[:, :LANES]), r_s, ar)
    y0 = each(lambda x, y: _unstack2(x - y[:, LANES:]), av, ar)
    w1_t = each(lambda x: x[:, :LANES].T, w1u0)
    u0_t = each(lambda x: x[:, LANES:].T, w1u0)
    v_t = each(lambda x: x.T, v_s)
    wua = each(dot(NN), each(lambda x, y: jnp.concatenate([x, y], axis=0), w1_t, u0_t), a_s)
    wa = each(lambda x: x[:LANES], wua)
    ua = each(lambda x: x[LANES:], wua)
    vk = each(dot(NN), v_t, k_s)
    g = each(lambda xe, x: jnp.where(row2 == col2, jnp.exp(xe), zero) - x, lp_end, wa)
    b = each(jnp.subtract, vk, ua)
    return list(zip(y0, r2, g, b))


RW_PAIRS_PER_STEP = 2


def _rwkv_body(lwf, rf, kf, vf, af, lwb, rb, kb, vb, ab, kk_ref, ka_ref, yf_ref, yb_ref, s_ref):
    @pl.when(pl.program_id(1) == 0)
    def _():
        s_ref[...] = jnp.zeros(s_ref.shape, F32)

    L = RWKV_CHUNK
    n = RW_CHUNKS_PER_STEP
    rows = lambda j: slice(j * L, (j + 1) * L)
    lanes = lambda q: slice(q * LANES, (q + 1) * LANES)
    refs = ((lwf, rf, kf, vf, af), (lwb, rb, kb, vb, ab))
    y_refs = (yf_ref, yb_ref)
    visit = [(d, j if d == 0 else n - 1 - j) for j in range(n) for d in (0, 1)]
    pairs = range(RW_PAIRS_PER_STEP)
    pre = _rwkv_chunks([tuple(ref[rows(j), lanes(q)] for ref in refs[d]) + (d,) for d, j in visit for q in pairs],
                       [kk_ref[:, lanes(q)] for _ in visit for q in pairs],
                       [ka_ref[:, lanes(q)] for _ in visit for q in pairs])
    s = [[s_ref[q, 0], s_ref[q, 1]] for q in pairs]
    chain = iter(pre)
    for d, j in visit:
        for q in pairs:
            y0, r2, g, b = next(chain)
            y_refs[d][rows(j), lanes(q)] = y0 + _mdot(r2, s[q][d], NT, RW_PASSES)
            s[q][d] = _mdot(s[q][d], g, NN, RW_PASSES) + b
    for q in range(RW_PAIRS_PER_STEP):
        s_ref[q, 0] = s[q][0]
        s_ref[q, 1] = s[q][1]


def rwkv_scan(r, k, v, lw, a, k_k, k_a, n_ctx):
    t, d = r.shape
    L = RWKV_CHUNK
    blk = RW_CHUNKS_PER_STEP * L
    w = RW_PAIRS_PER_STEP * LANES
    assert 2 * L == LANES and t % blk == 0 and n_ctx % blk == 0 and d % w == 0
    nc, nc0 = t // blk, n_ctx // blk
    nh = d // w
    fwd = pl.BlockSpec((blk, w), lambda h, c: (c, h))
    bwd = pl.BlockSpec((blk, w), lambda h, c: (_bwd_chunk(c, nc0, nc), h))
    bwd2 = pl.BlockSpec((blk, w), lambda h, c: (_bwd_chunk(c, nc0, nc), nh + h))
    par = pl.BlockSpec((1, w), lambda h, c: (0, h))
    return pl.pallas_call(
        _rwkv_body,
        grid=(nh, nc),
        in_specs=[fwd] * 5 + [bwd2, bwd, bwd, bwd, bwd2, par, par],
        out_specs=[fwd, bwd],
        out_shape=[jax.ShapeDtypeStruct((t, d), F32)] * 2,
        scratch_shapes=[pltpu.VMEM((RW_PAIRS_PER_STEP, 2, LANES, LANES), F32)],
        compiler_params=pltpu.CompilerParams(
            dimension_semantics=("arbitrary", "arbitrary"), vmem_limit_bytes=VMEM_LIMIT),
        name="rwkv7_scan",
    )(lw, r, k, v, a, lw, r, k, v, a, k_k.reshape(1, d), k_a.reshape(1, d))


MIX_TM = 256


def _mix_body(cur_ref, prev_ref, next_ref, mu_ref, *o_refs, nct, nt):
    i = pl.program_id(0)
    is_ctx = i < nct
    cur = cur_ref[...]
    tm, d = cur.shape
    qd = d // 4
    above = jnp.where(is_ctx | (i == nct), 0.0, prev_ref[...])
    below = jnp.where(is_ctx | (i == nt - 1), 0.0, next_ref[...])
    ext = jnp.concatenate([above, cur, below], axis=0)
    n = ext.shape[0]
    colid = lax.broadcasted_iota(jnp.int32, (n, 2 * qd), 0) % GRID_W
    left = jnp.where(is_ctx | (colid != 0), pltpu.roll(ext[:, :2 * qd], 1, 0), 0.0)
    colid3 = lax.broadcasted_iota(jnp.int32, (n, 3 * qd), 0) % GRID_W
    right = jnp.where(is_ctx | (colid3 != GRID_W - 1), pltpu.roll(ext[:, qd:], n - 1, 0), 0.0)
    mid = slice(GRID_W, GRID_W + tm)
    sh = jnp.concatenate([
        left[mid, :qd],
        jnp.where(is_ctx, left[mid, qd:], right[mid, :qd]),
        jnp.where(is_ctx, right[mid, qd:2 * qd], ext[0:tm, 2 * qd:3 * qd]),
        jnp.where(is_ctx, right[mid, 2 * qd:], ext[2 * GRID_W:2 * GRID_W + tm, 3 * qd:])], axis=1)
    dx = sh - cur
    for b, o_ref in enumerate(o_refs):
        o_ref[...] = (cur + dx * mu_ref[b:b + 1, :]).astype(o_ref.dtype)


def rwkv_mix(h, mu, n_ctx):
    t, d = h.shape
    tm = MIX_TM
    assert t % tm == 0 and n_ctx % tm == 0 and tm % GRID_W == 0
    r = tm // GRID_W
    nt, nu = t // tm, t // GRID_W
    nb = mu.shape[0]
    return pl.pallas_call(
        functools.partial(_mix_body, nct=n_ctx // tm, nt=nt),
        grid=(nt,),
        in_specs=[pl.BlockSpec((tm, d), lambda i: (i, 0)),
                  pl.BlockSpec((GRID_W, d), lambda i: (jnp.maximum(i * r - 1, 0), 0)),
                  pl.BlockSpec((GRID_W, d), lambda i: (jnp.minimum((i + 1) * r, nu - 1), 0)),
                  pl.BlockSpec((nb, d), lambda i: (0, 0))],
        out_specs=[pl.BlockSpec((tm, d), lambda i: (i, 0))] * nb,
        out_shape=[jax.ShapeDtypeStruct((t, d), BF16)] * nb,
        compiler_params=pltpu.CompilerParams(dimension_semantics=("arbitrary",), vmem_limit_bytes=VMEM_LIMIT),
        name="rwkv_mix",
    )(h, h, h, mu)


def _rwkv_out_body(yf_ref, yb_ref, r_ref, k_ref, v_ref, g_ref, af_ref, ab_ref, p_ref, e_ref, et_ref, w_ref,
                   *ln_refs, **ln_static):
    e, et = e_ref[...], et_ref[...]
    head_sum = lambda x: _dot_01_rhs(_dot_01_rhs(x, e, 2), et, 2)
    k_a, r_k, gain, bias = (p_ref[n:n + 1, :] for n in range(4))
    y = yf_ref[...] + yb_ref[...]
    yc = y - head_sum(y) * (1.0 / RWKV_N)
    var = head_sum(yc * yc) * (1.0 / RWKV_N)
    yn = yc * lax.rsqrt(var + RWKV_GN_EPS) * gain + bias
    kbar = k_ref[...] * (1.0 + (0.5 * (af_ref[...] + ab_ref[...]) - 1.0) * k_a)
    bonus = head_sum(r_ref[...] * kbar * r_k) * v_ref[...]
    z = (yn + bonus) * g_ref[...]
    _residual_ln_mod(_d(z.astype(BF16), w_ref[...], NN), *ln_refs, **ln_static)


def rwkv_out(yf, yb, r, k, v, g, a, k_a, r_k, lnx_g, lnx_b, w_out, xs, ln):
    t, d = yf.shape
    row_map, ln_in, ln_ops, out_specs, out_shape, static = _ln_operands(xs, ln)
    row = pl.BlockSpec((LN_TM, d), row_map)
    row_b = pl.BlockSpec((LN_TM, d), lambda i: (row_map(i)[0], 1))
    const = lambda shape: pl.BlockSpec(shape, lambda i: (0, 0))
    e = (jnp.arange(d)[:, None] // RWKV_N == jnp.arange(LANES)[None, :]).astype(BF16)
    params = jnp.stack([k_a, r_k, lnx_g, lnx_b])
    return pl.pallas_call(
        functools.partial(_rwkv_out_body, **static),
        grid=(ln["pad_to"] // LN_TM,),
        in_specs=[row] * 6 + [row, row_b, const((4, d)), const((d, LANES)), const((LANES, d)), const((d, d))] + ln_in,
        out_specs=out_specs,
        out_shape=out_shape,
        compiler_params=pltpu.CompilerParams(dimension_semantics=("arbitrary",), vmem_limit_bytes=VMEM_LIMIT),
        name="rwkv_out",
    )(yf, yb, r, k, v, g, a, a, params, e, e.T, w_out.astype(BF16), *ln_ops)


def _sort16_pairs():
    pairs = []
    n, p = 16, 1
    while p < n:
        k = p
        while k >= 1:
            for j in range(k % p, n - k, 2 * k):
                for i in range(min(k, n - j - k)):
                    if (i + j) // (2 * p) == (i + j + k) // (2 * p):
                        pairs.append((i + j, i + j + k))
            k //= 2
        p *= 2
    return pairs


def _top16_sorted(x):
    v = [x[8 * i:8 * i + 8] for i in range(16)]

    def exchange(i, j):
        v[i], v[j] = jnp.maximum(v[i], v[j]), jnp.minimum(v[i], v[j])

    for i, j in _sort16_pairs():
        exchange(i, j)
    for shift in (4, 2, 1):
        other = [pltpu.roll(t, shift, 0) for t in v]
        v = [jnp.maximum(v[i], other[15 - i]) for i in range(16)]
        for dist in (8, 4, 2, 1):
            for i in range(16):
                if i & dist == 0:
                    exchange(i, i + dist)
    return [t[0:1] for t in v]


def _peer_stats(hx_ref, wk_ref, sc_ref, hxb_ref, n_ref, f0_ref, r1_ref, e1_ref):
    tm = hx_ref.shape[0]
    hxb_ref[...] = hx_ref[...].astype(F32).T.astype(BF16)
    rows_per = 4 * LANES

    def scores(c, carry):
        rows = pl.ds(pl.multiple_of(c * rows_per, rows_per), rows_per)
        sc_ref[rows, :] = _d(wk_ref[rows, :], hxb_ref[...], NN)
        return carry

    lax.fori_loop(0, wk_ref.shape[0] // rows_per, scores, 0)

    def block(tb, carry):
        nq = PEER_STAT_LANES // LANES
        lanes = [pl.ds(pl.multiple_of(tb * PEER_STAT_LANES + q * LANES, LANES), LANES) for q in range(nq)]
        head_row = lax.broadcasted_iota(jnp.int32, (PEER_HEADS, LANES), 0)
        zero = jnp.zeros((PEER_HEADS, LANES), F32)

        def extract(q):
            def body(h, tops):
                tops = [list(t) for t in tops]
                rows = [pl.ds(pl.multiple_of((2 * h + p) * N_KEYS, N_KEYS), N_KEYS) for p in range(2)]
                cur = [sc_ref[rows[p], lanes[q]] for p in range(2)]
                top = [_top16_sorted(c) for c in cur]
                rank = jnp.zeros((N_KEYS, LANES), F32)
                for b in range(PEER_TOPK):
                    rank = jnp.where(cur[1] < top[1][b], float(b + 1), rank)
                for p in range(2):
                    for a in range(PEER_TOPK):
                        tops[p][a] = jnp.where(head_row == h, top[p][a], tops[p][a])
                r1_ref[h, :, lanes[q]] = rank.astype(BF16)
                return tuple(tuple(t) for t in tops)
            return lax.fori_loop(0, PEER_HEADS, body, ((zero,) * PEER_TOPK,) * 2)

        tops = [extract(q) for q in range(nq)]
        join = lambda p, a: jnp.concatenate([tops[q][p][a] for q in range(nq)], axis=1)
        top0 = [join(0, a) for a in range(PEER_TOPK)]
        top1 = [join(1, b) for b in range(PEER_TOPK)]
        cands = [top0[a] + top1[b]
                 for a in range(PEER_TOPK) for b in range(PEER_TOPK) if (a + 1) * (b + 1) <= PEER_TOPK]
        c_max = cands[0]
        z = jnp.zeros_like(c_max)
        tau = c_max
        for a in range(PEER_TOPK):
            tau = functools.reduce(jnp.maximum, cands)
            z = z + jnp.exp(tau - c_max)
            cands = [jnp.where(cd >= tau, -jnp.inf, cd) for cd in cands]
        inv_z = 1.0 / z
        nb = []
        for a in range(PEER_TOPK):
            cnt = jnp.zeros_like(tau)
            for b in range(PEER_TOPK):
                cnt = jnp.where(top0[a] + top1[b] >= tau, float(b + 1), cnt)
            nb.append(cnt)

        def factors(h, carry):
            for q in range(nq):
                sub = slice(q * LANES, (q + 1) * LANES)
                row_of = lambda x: jnp.max(jnp.where(head_row == h, x[:, sub], -jnp.inf), axis=0, keepdims=True)
                s0 = sc_ref[pl.ds(pl.multiple_of(2 * h * N_KEYS, N_KEYS), N_KEYS), lanes[q]]
                s1 = sc_ref[pl.ds(pl.multiple_of((2 * h + 1) * N_KEYS, N_KEYS), N_KEYS), lanes[q]]
                n = jnp.zeros((N_KEYS, LANES), F32)
                for a in reversed(range(PEER_TOPK)):
                    n = jnp.where(s0 >= row_of(top0[a]), row_of(nb[a]), n)
                n_ref[h, :, lanes[q]] = n
                f0_ref[h, :, lanes[q]] = jnp.exp(s0 - row_of(top0[0])) * row_of(inv_z)
                e1_ref[h, :, lanes[q]] = jnp.exp(s1 - row_of(top1[0])).astype(BF16)
            return carry

        lax.fori_loop(0, PEER_HEADS, factors, 0)
        return carry

    lax.fori_loop(0, tm // PEER_STAT_LANES, block, 0)


PEER_I_GROUP = 4
PEER_STREAMS = 2


def _peer_body(hx_ref, wk_ref, *rest):
    u_refs, vt_refs = rest[:PEER_STREAMS], rest[PEER_STREAMS:2 * PEER_STREAMS]
    o_ref, sc_ref, hxb_ref, n_ref, f0_ref, r1_ref, e1_ref, w_ref, acc_ref = rest[2 * PEER_STREAMS:]
    e = pl.program_id(1)
    tm = hx_ref.shape[0]
    nsb = tm // PEER_SB

    @pl.when(e == 0)
    def _():
        _peer_stats(hx_ref, wk_ref, sc_ref, hxb_ref, n_ref, f0_ref, r1_ref, e1_ref)
        acc_ref[...] = jnp.zeros(acc_ref.shape, F32)

    i_rows = pl.ds(pl.multiple_of(e * PEER_I_BLOCK, PEER_I_BLOCK), PEER_I_BLOCK)

    def activations(sb):
        parts = []
        for u_ref in u_refs:
            act = _d(u_ref[...], hxb_ref[:, sb * PEER_SB:(sb + 1) * PEER_SB], NN)
            act = act.astype(BF16)
            parts.append(0.5 * act * (1.0 + lax.erf(act * (2.0 ** -0.5))))
        return parts

    def gates(sb, act):
        for hb in range(PEER_SB // LANES):
            lanes = slice(sb * PEER_SB + hb * LANES, sb * PEER_SB + (hb + 1) * LANES)
            sub = slice(hb * LANES, (hb + 1) * LANES)
            n8 = [n_ref[h, i_rows, lanes] for h in range(PEER_HEADS)]
            f8 = [f0_ref[h, i_rows, lanes] for h in range(PEER_HEADS)]
            bcast = lambda x, ii: jnp.broadcast_to(x[ii:ii + 1], (N_KEYS, LANES)).astype(BF16)
            for ig in range(0, PEER_I_BLOCK, PEER_I_GROUP):
                g = [jnp.zeros((N_KEYS, LANES), BF16) for _ in range(PEER_I_GROUP)]
                for h in range(PEER_HEADS):
                    r1 = r1_ref[h, :, lanes]
                    e1 = e1_ref[h, :, lanes]
                    for k in range(PEER_I_GROUP):
                        ii = ig + k
                        g[k] = g[k] + jnp.where(r1 < bcast(n8[h], ii), e1 * bcast(f8[h], ii), jnp.zeros_like(e1))
                for k in range(PEER_I_GROUP):
                    rows = slice((ig + k) * N_KEYS, (ig + k + 1) * N_KEYS)
                    part, off = divmod((ig + k) * N_KEYS, PEER_TE // PEER_STREAMS)
                    w_ref[sb, rows, sub] = g[k] * act[part][off:off + N_KEYS, sub]

    def accumulate(sb):
        cols = slice(sb * PEER_SB, (sb + 1) * PEER_SB)
        dr = acc_ref.shape[0] // PEER_STREAMS
        for k, vt_ref in enumerate(vt_refs):
            acc_ref[k * dr:(k + 1) * dr, cols] += _d(vt_ref[...], w_ref[sb], NN)

    act = activations(0)
    for sb in range(nsb):
        nxt = activations(sb + 1) if sb + 1 < nsb else None
        gates(sb, act)
        accumulate(sb)
        act = nxt

    @pl.when(e == pl.num_programs(1) - 1)
    def _():
        o_ref[...] = acc_ref[...].T


def _fold_body(k_ref, w_ref, o_ref):
    o_ref[...] = _mdot(k_ref[...], w_ref[...], NN, 6)


def peer_fold_keys(wq, keys):
    d = wq.shape[0]
    nhp, nk, dk = keys.shape
    wqt = wq.T.reshape(nhp, dk, d)
    return pl.pallas_call(
        _fold_body,
        grid=(nhp,),
        in_specs=[pl.BlockSpec((None, nk, dk), lambda i: (i, 0, 0)),
                  pl.BlockSpec((None, dk, d), lambda i: (i, 0, 0))],
        out_specs=pl.BlockSpec((nk, d), lambda i: (i, 0)),
        out_shape=jax.ShapeDtypeStruct((nhp * nk, d), F32),
        compiler_params=pltpu.CompilerParams(dimension_semantics=("arbitrary",), vmem_limit_bytes=VMEM_LIMIT),
        name="peer_fold_keys",
    )(keys, wqt)


def peer(hx, wk, u_bf, vt_bf):
    t, d = hx.shape
    tm = PEER_TM
    assert t % tm == 0
    ne = u_bf.shape[0] // PEER_TE
    h = PEER_HEADS
    ns = PEER_STREAMS
    return pl.pallas_call(
        _peer_body,
        grid=(t // tm, ne),
        in_specs=[pl.BlockSpec((tm, d), lambda i, e: (i, 0), pipeline_mode=pl.Buffered(1)),
                  pl.BlockSpec(wk.shape, lambda i, e: (0, 0), pipeline_mode=pl.Buffered(1)),
                  *[pl.BlockSpec((PEER_TE // ns, d), functools.partial(lambda i, e, k: (e * ns + k, 0), k=k))
                    for k in range(ns)],
                  *[pl.BlockSpec((None, d // ns, PEER_TE), functools.partial(lambda i, e, k: (e, k, 0), k=k))
                    for k in range(ns)]],
        out_specs=pl.BlockSpec((tm, d), lambda i, e: (i, 0)),
        out_shape=jax.ShapeDtypeStruct((t, d), F32),
        scratch_shapes=[pltpu.VMEM((2 * h * N_KEYS, tm), F32),
                        pltpu.VMEM((d, tm), BF16),
                        pltpu.VMEM((h, N_KEYS, tm), F32), pltpu.VMEM((h, N_KEYS, tm), F32),
                        pltpu.VMEM((h, N_KEYS, tm), BF16), pltpu.VMEM((h, N_KEYS, tm), BF16),
                        pltpu.VMEM((tm // PEER_SB, PEER_TE, PEER_SB), BF16), pltpu.VMEM((d, tm), F32)],
        compiler_params=pltpu.CompilerParams(
            dimension_semantics=("arbitrary", "arbitrary"), vmem_limit_bytes=VMEM_LIMIT),
        name="peer_dense",
    )(hx, wk.astype(BF16), *([u_bf] * ns), *([vt_bf] * ns))


def _ln_mod_body(y_ref, *ln_refs, **ln_static):
    _residual_ln_mod(y_ref[...], *ln_refs, **ln_static)


def ln_mod(xs, y, ln):
    d = xs.shape[1]
    row_map, ln_in, ln_ops, out_specs, out_shape, static = _ln_operands(xs, ln)
    return pl.pallas_call(
        functools.partial(_ln_mod_body, **static),
        grid=(ln["pad_to"] // LN_TM,),
        in_specs=[pl.BlockSpec((LN_TM, d), row_map)] + ln_in,
        out_specs=out_specs,
        out_shape=out_shape,
        compiler_params=pltpu.CompilerParams(dimension_semantics=("arbitrary",), vmem_limit_bytes=VMEM_LIMIT),
        name="ln_mod",
    )(y, *ln_ops)


def _pad_cols(w, n):
    return jnp.pad(w, ((0, 0), (0, n - w.shape[1])))


def _mlstm_layer(h, n_ctx, w_in, b_in, conv_w, conv_b, hn_g, w_out, xs, ln):
    d = D_MODEL
    t = h.shape[0]
    qk_pre = matmul(h, w_in[:, :2 * d], b_in[:2 * d])
    v = matmul(h, w_in[:, 2 * d:3 * d], b_in[2 * d:3 * d], out_dtype=BF16)
    o = matmul(h, w_in[:, 3 * d:4 * d], b_in[3 * d:4 * d], act="sigmoid", out_dtype=BF16)
    g = matmul(h, _pad_cols(w_in[:, 4 * d:], LANES), jnp.pad(b_in[4 * d:], (0, LANES - 4 * MLSTM_HEADS)))
    g = g[:, :4 * MLSTM_HEADS].reshape(t, 4, MLSTM_HEADS)
    g = jnp.concatenate([g[:, :2], jax.nn.log_sigmoid(g[:, 2:])], axis=1)
    gh = jnp.transpose(g, (2, 0, 1))
    gcol = jnp.pad(gh, ((0, 0), (0, 0), (0, LANES - 4)))
    grow = jnp.pad(jnp.transpose(gh, (0, 2, 1)), ((0, 0), (0, 4), (0, 0)))
    qk = mlstm_conv(qk_pre, conv_w, conv_b, n_ctx)
    hf, hb = mlstm_scan(qk, v, gcol, grow, n_ctx)
    return mlstm_out(hf, hb, o, hn_g, w_out, xs, ln)


def _rwkv_layer(h, n_ctx, mu, w_rkv, w0, w1, w2, a0, a1, a2, g1, g2, k_k, k_a, r_k, lnx_g, lnx_b, w_out, xs, ln):
    d = D_MODEL
    xm = rwkv_mix(h, mu, n_ctx)
    r = matmul(xm[0], w_rkv[0])
    k = matmul(xm[1], w_rkv[1])
    v = matmul(xm[2], w_rkv[2])

    def lora_pair(x, w_in, w_mid, bias, act_mid, act_out):
        rank = w_in.shape[-1]
        w_a = _pad_cols(jnp.concatenate([w_in[0], w_in[1]], axis=1), LANES)
        zpad = jnp.zeros((rank, d), F32)
        w_b = jnp.concatenate([jnp.concatenate([w_mid[0], zpad], axis=1),
                               jnp.concatenate([zpad, w_mid[1]], axis=1)], axis=0)
        w_b = jnp.pad(w_b, ((0, LANES - 2 * rank), (0, 0)))
        mid = matmul(x, w_a, act=act_mid, out_dtype=BF16)
        return matmul(mid, w_b, jnp.concatenate([bias[0], bias[1]]), act=act_out)

    lw = lora_pair(xm[3], w1, w2, w0, "tanh", "logdecay")
    a = lora_pair(xm[4], a1, a2, a0, None, "sigmoid")
    gpad = 2 * LANES
    gg = matmul(xm[5], _pad_cols(g1, gpad), act="sigmoid", out_dtype=BF16)
    g = matmul(gg, jnp.pad(g2, ((0, gpad - g1.shape[1]), (0, 0))))
    yf, yb = rwkv_scan(r, k, v, lw, a, k_k, k_a, n_ctx)
    return rwkv_out(yf, yb, r, k, v, g, a, k_a, r_k, lnx_g, lnx_b, w_out, xs, ln)


def _forward(x, c, ctx, c_ctx, ada_w, ada_b, ln_g, ln_b,
             ml_w_in, ml_b_in, ml_conv_w, ml_conv_b, ml_hn_g, ml_w_out,
             rw_mu, rw_w_rkv, rw_w0, rw_w1, rw_w2, rw_a0, rw_a1, rw_a2, rw_g1, rw_g2,
             rw_k_k, rw_k_a, rw_r_k, rw_lnx_g, rw_lnx_b, rw_w_out,
             pk_wq, pk_keys, pk_u, pk_v):
    d = D_MODEL
    n_ctx = ctx.shape[1]
    xs = jnp.concatenate([ctx[0], x[0]], axis=0)
    t = xs.shape[0]
    t_pad = -(-t // PEER_TM) * PEER_TM
    s_in = jnp.zeros((8, d), F32).at[0].set(jax.nn.silu(c[0])).at[1].set(jax.nn.silu(c_ctx))
    depth = ada_w.shape[0]
    mods = [matmul(s_in, ada_w[i], ada_b[i], passes=3) for i in range(depth)]
    is_ctx = (jnp.arange(t) < n_ctx)[:, None]
    m0 = [jnp.where(is_ctx, mods[0][1, n * d:(n + 1) * d], mods[0][0, n * d:(n + 1) * d]) for n in range(2)]
    mixer_dtype = lambda i: BF16 if i % 2 == 0 else F32
    h = (xs * (1.0 + m0[1]) + m0[0]).astype(mixer_dtype(0))
    for i in range(depth):
        j = i // 2
        ln1 = dict(mod_gate=mods[i], gate_col=2, mod_next=mods[i], mod_col=3, ln_g=ln_g[i, 0], ln_b=ln_b[i, 0],
                   n_ctx=n_ctx, pad_to=t_pad, h_dtype=BF16)
        if i % 2 == 0:
            xs, h = _mlstm_layer(h, n_ctx, ml_w_in[j], ml_b_in[j], ml_conv_w[j], ml_conv_b[j],
                                 ml_hn_g[j], ml_w_out[j], xs, ln1)
        else:
            xs, h = _rwkv_layer(h, n_ctx, rw_mu[j], rw_w_rkv[j], rw_w0[j], rw_w1[j], rw_w2[j],
                                rw_a0[j], rw_a1[j], rw_a2[j], rw_g1[j], rw_g2[j], rw_k_k[j],
                                rw_k_a[j], rw_r_k[j], rw_lnx_g[j], rw_lnx_b[j], rw_w_out[j], xs, ln1)
        wk = peer_fold_keys(pk_wq[i], pk_keys[i].reshape(2 * PEER_HEADS, N_KEYS, PEER_DQ // 2))
        vt = jnp.swapaxes(pk_v[i].astype(BF16).reshape(-1, PEER_TE, d), 1, 2)
        y = peer(h, wk, pk_u[i].astype(BF16), vt)
        ln2 = dict(mod_gate=mods[i], gate_col=5, mod_next=mods[min(i + 1, depth - 1)], mod_col=0,
                   ln_g=ln_g[i, 1], ln_b=ln_b[i, 1], n_ctx=n_ctx, pad_to=t, h_dtype=mixer_dtype(i + 1))
        xs, h = ln_mod(xs, y, ln2)
    return xs[n_ctx:][None]


def kernel(x, c, ctx, c_ctx, ada_w, ada_b, ln_g, ln_b, ml_w_in, ml_b_in, ml_conv_w, ml_conv_b, ml_hn_g, ml_w_out, rw_mu, rw_w_rkv, rw_w0, rw_w1, rw_w2, rw_a0, rw_a1, rw_a2, rw_g1, rw_g2, rw_k_k, rw_k_a, rw_r_k, rw_lnx_g, rw_lnx_b, rw_w_out, pk_wq, pk_keys, pk_u, pk_v):
    return _forward(x, c, ctx, c_ctx, ada_w, ada_b, ln_g, ln_b,
                    ml_w_in, ml_b_in, ml_conv_w, ml_conv_b, ml_hn_g, ml_w_out,
                    rw_mu, rw_w_rkv, rw_w0, rw_w1, rw_w2, rw_a0, rw_a1, rw_a2, rw_g1, rw_g2,
                    rw_k_k, rw_k_a, rw_r_k, rw_lnx_g, rw_lnx_b, rw_w_out,
                    pk_wq, pk_keys, pk_u, pk_v)
```

```python
import functools

import jax
import jax.numpy as jnp
from jax import lax
from jax.experimental import pallas as pl
from jax.experimental.pallas import tpu as pltpu

F32 = jnp.float32
BF16 = jnp.bfloat16

D_MODEL = 1024
DEPTH = 4
GRID_W = 64
DN_ALPHA = (2.0 * DEPTH) ** 0.25
LN_EPS = 1e-5

MLSTM_HEADS = 4
MLSTM_DH = D_MODEL // MLSTM_HEADS
MLSTM_CHUNK = 128
M_INIT = -1e30

RWKV_N = 64
RWKV_HEADS = D_MODEL // RWKV_N
RWKV_CHUNK = 64
RWKV_GN_EPS = 64e-5

N_KEYS = 128
PEER_HEADS = 8
PEER_DQ = 256
PEER_TOPK = 16
PEER_I_BLOCK = 16
PEER_TE = PEER_I_BLOCK * N_KEYS
PEER_TM = 768
PEER_SB = 256
PEER_STAT_LANES = 256

LANES = 128
VMEM_LIMIT = 62 * 1024 * 1024

NN = ((1,), (0,))
NT = ((1,), (1,))


def _split(x, n):
    parts = []
    r = x.astype(F32)
    for i in range(n):
        p = r.astype(BF16)
        parts.append(p)
        if i + 1 < n:
            r = r - p.astype(F32)
    return parts


def _d(a, b, dims):
    return lax.dot_general(a, b, (dims, ((), ())), preferred_element_type=F32)


def _mdot(a, b, dims, passes):
    if passes == 1:
        return _d(a.astype(BF16), b.astype(BF16), dims)
    if passes == 3:
        a0, a1 = _split(a, 2)
        b0, b1 = _split(b, 2)
        return (_d(a0, b1, dims) + _d(a1, b0, dims)) + _d(a0, b0, dims)
    a0, a1, a2 = _split(a, 3)
    b0, b1, b2 = _split(b, 3)
    lo = (_d(a0, b2, dims) + _d(a2, b0, dims)) + _d(a1, b1, dims)
    mid = _d(a0, b1, dims) + _d(a1, b0, dims)
    return (lo + mid) + _d(a0, b0, dims)


def _dot_01_lhs(m01, x, pieces=3):
    mb = m01.astype(BF16)
    return functools.reduce(jnp.add, [_d(mb, xp, NN) for xp in reversed(_split(x, pieces))])


def _dot_01_rhs(x, m01, pieces=3):
    mb = m01.astype(BF16)
    return functools.reduce(jnp.add, [_d(xp, mb, NN) for xp in reversed(_split(x, pieces))])


def _pick(n, cands):
    for c in cands:
        if n % c == 0:
            return c
    raise ValueError(f"no tile for {n}")


_ACTS = {None: lambda x: x, "sigmoid": jax.nn.sigmoid, "tanh": jnp.tanh,
         "logdecay": lambda x: -(2.718281828459045 ** -0.5) * jax.nn.sigmoid(x)}


def _mm_body(x_ref, w_ref, b_ref, o_ref, *, passes, act):
    o_ref[...] = _ACTS[act](_mdot(x_ref[...], w_ref[...], NN, passes) + b_ref[...]).astype(o_ref.dtype)


def matmul(x, w, b=None, *, passes=1, act=None, out_dtype=F32):
    m, k = x.shape
    n = w.shape[1]
    assert n % LANES == 0 and w.shape[0] == k
    tm = m if m <= 1024 else _pick(m, (1280, 640, 512, 384, 256, 128))
    tn = _pick(n, (1024, 768, 640, 512, 384, 256, 128))
    if b is None:
        b = jnp.zeros((n,), F32)
    if passes == 1:
        w = w.astype(BF16)
    return pl.pallas_call(
        functools.partial(_mm_body, passes=passes, act=act),
        grid=(m // tm, n // tn),
        in_specs=[pl.BlockSpec((tm, k), lambda i, j: (i, 0)),
                  pl.BlockSpec((k, tn), lambda i, j: (0, j)),
                  pl.BlockSpec((1, tn), lambda i, j: (0, j))],
        out_specs=pl.BlockSpec((tm, tn), lambda i, j: (i, j)),
        out_shape=jax.ShapeDtypeStruct((m, n), out_dtype),
        compiler_params=pltpu.CompilerParams(
            dimension_semantics=("arbitrary", "arbitrary"), vmem_limit_bytes=VMEM_LIMIT),
        name="proj_matmul",
    )(x, w, b.reshape(1, n).astype(F32))


LN_TM = 256


def _residual_ln_mod(y, xs_ref, mg_ref, mn_ref, lng_ref, lnb_ref, xo_ref, ho_ref, *, gate_col, mod_col, nct, nt):
    i = pl.program_id(0)
    d = xs_ref.shape[1]
    is_ctx = i < nct

    def pick(ref, col):
        return jnp.where(is_ctx, ref[1:2, col * d:(col + 1) * d], ref[0:1, col * d:(col + 1) * d])

    z = DN_ALPHA * xs_ref[...] + pick(mg_ref, gate_col) * y
    mu = jnp.mean(z, axis=-1, keepdims=True)
    zc = z - mu
    var = jnp.mean(zc * zc, axis=-1, keepdims=True)
    xn = zc * lax.rsqrt(var + LN_EPS) * lng_ref[...] + lnb_ref[...]
    xo_ref[...] = xn
    h = xn * (1.0 + pick(mn_ref, mod_col + 1)) + pick(mn_ref, mod_col)
    ho_ref[...] = jnp.where(i < nt, h, 0.0).astype(ho_ref.dtype)


def _ln_operands(xs, ln):
    t, d = xs.shape
    tm = LN_TM
    assert t % tm == 0 and ln["n_ctx"] % tm == 0 and ln["pad_to"] % tm == 0 and ln["pad_to"] >= t
    nt = t // tm
    row = lambda i: (jnp.minimum(i, nt - 1), 0)
    full = lambda a: pl.BlockSpec(a.shape, lambda i: (0, 0))
    vec = pl.BlockSpec((1, d), lambda i: (0, 0))
    in_specs = [pl.BlockSpec((tm, d), row), full(ln["mod_gate"]), full(ln["mod_next"]), vec, vec]
    operands = [xs, ln["mod_gate"], ln["mod_next"], ln["ln_g"].reshape(1, d), ln["ln_b"].reshape(1, d)]
    out_specs = [pl.BlockSpec((tm, d), row), pl.BlockSpec((tm, d), lambda i: (i, 0))]
    out_shape = [jax.ShapeDtypeStruct((t, d), F32), jax.ShapeDtypeStruct((ln["pad_to"], d), ln["h_dtype"])]
    static = dict(gate_col=ln["gate_col"], mod_col=ln["mod_col"], nct=ln["n_ctx"] // tm, nt=nt)
    return row, in_specs, operands, out_specs, out_shape, static


CONV_TM = 256
CONV_CB = 512


def _conv_body(cur_ref, prev_ref, next_ref, w_ref, b_ref, o_ref, *, nct, nt, q_blocks, q_scale):
    i = pl.program_id(0)
    j = pl.program_id(1)
    is_ctx = i < nct
    tm = cur_ref.shape[0]
    above = jnp.where(is_ctx | (i == nct), 0.0, prev_ref[...])
    below = jnp.where(is_ctx | (i == nt - 1), 0.0, next_ref[...])
    ext = jnp.concatenate([above, cur_ref[...], below], axis=0)
    n = ext.shape[0]
    colid = lax.broadcasted_iota(jnp.int32, ext.shape, 0) % GRID_W
    left = jnp.where(is_ctx | (colid != 0), pltpu.roll(ext, 1, 0), 0.0)
    right = jnp.where(is_ctx | (colid != GRID_W - 1), pltpu.roll(ext, n - 1, 0), 0.0)
    w = w_ref[...]
    acc = jnp.zeros((tm, ext.shape[1]), F32) + b_ref[...]
    for di in range(3):
        rows = slice(di * GRID_W, di * GRID_W + tm)
        tap = left[rows] * w[3 * di:3 * di + 1] + ext[rows] * w[3 * di + 1:3 * di + 2] \
            + right[rows] * w[3 * di + 2:3 * di + 3]
        acc = acc + (tap if di == 1 else jnp.where(is_ctx, 0.0, tap))
    y = acc * jax.nn.sigmoid(acc)
    o_ref[...] = (y * jnp.where(j < q_blocks, q_scale, 1.0)).astype(o_ref.dtype)


def mlstm_conv(qk_pre, conv_w, conv_b, n_ctx):
    t, c = qk_pre.shape
    tm, cb = CONV_TM, CONV_CB
    assert t % tm == 0 and n_ctx % tm == 0 and tm % GRID_W == 0 and c % (2 * cb) == 0
    r = tm // GRID_W
    nt, nu = t // tm, t // GRID_W
    return pl.pallas_call(
        functools.partial(_conv_body, nct=n_ctx // tm, nt=nt, q_blocks=c // (2 * cb), q_scale=MLSTM_DH ** -0.5),
        grid=(nt, c // cb),
        in_specs=[pl.BlockSpec((tm, cb), lambda i, j: (i, j)),
                  pl.BlockSpec((GRID_W, cb), lambda i, j: (jnp.maximum(i * r - 1, 0), j)),
                  pl.BlockSpec((GRID_W, cb), lambda i, j: (jnp.minimum((i + 1) * r, nu - 1), j)),
                  pl.BlockSpec((9, cb), lambda i, j: (0, j)),
                  pl.BlockSpec((1, cb), lambda i, j: (0, j))],
        out_specs=pl.BlockSpec((tm, cb), lambda i, j: (i, j)),
        out_shape=jax.ShapeDtypeStruct((t, c), BF16),
        compiler_params=pltpu.CompilerParams(
            dimension_semantics=("arbitrary", "arbitrary"), vmem_limit_bytes=VMEM_LIMIT),
        name="mlstm_conv",
    )(qk_pre, qk_pre, qk_pre, conv_w.reshape(9, c), conv_b.reshape(1, c))


def _mlstm_out_body(hf_ref, hb_ref, o_ref, g_ref, w_ref, *ln_refs, **ln_static):
    h = hf_ref[...] + hb_ref[...]
    parts = []
    for a in range(MLSTM_HEADS):
        x = h[:, a * MLSTM_DH:(a + 1) * MLSTM_DH]
        mu = jnp.mean(x, axis=-1, keepdims=True)
        xc = x - mu
        var = jnp.mean(xc * xc, axis=-1, keepdims=True)
        parts.append(xc * lax.rsqrt(var + LN_EPS))
    hn = jnp.concatenate(parts, axis=1)
    z = o_ref[...].astype(F32) * hn * g_ref[...]
    _residual_ln_mod(_d(z.astype(BF16), w_ref[...], NN), *ln_refs, **ln_static)


def mlstm_out(hf, hb, o, hn_g, w_out, xs, ln):
    t, d = hf.shape
    row_map, ln_in, ln_ops, out_specs, out_shape, static = _ln_operands(xs, ln)
    row = pl.BlockSpec((LN_TM, d), row_map)
    return pl.pallas_call(
        functools.partial(_mlstm_out_body, **static),
        grid=(ln["pad_to"] // LN_TM,),
        in_specs=[row, row, row, pl.BlockSpec((1, d), lambda i: (0, 0)), pl.BlockSpec((d, d), lambda i: (0, 0))] + ln_in,
        out_specs=out_specs,
        out_shape=out_shape,
        compiler_params=pltpu.CompilerParams(dimension_semantics=("arbitrary",), vmem_limit_bytes=VMEM_LIMIT),
        name="mlstm_out",
    )(hf, hb, o, hn_g.reshape(1, d), w_out.astype(BF16), *ln_ops)


ML_HEADS_PER_STEP = 4


def _mlstm_body(qf, kf, vf, gcf, grf, qb, kb, vb, gcb, grb, hf_ref, hb_ref, c_ref, n_ref, m_ref):
    @pl.when(pl.program_id(1) == 0)
    def _():
        c_ref[...] = jnp.zeros(c_ref.shape, F32)
        n_ref[...] = jnp.zeros(n_ref.shape, F32)
        m_ref[...] = jnp.full(m_ref.shape, M_INIT, F32)

    dh = MLSTM_DH
    cs = [(a, d) for a in range(ML_HEADS_PER_STEP) for d in (0, 1)]
    each = lambda f, *ls: [f(*xs) for xs in zip(*ls)]
    cols = lambda a: slice(a * dh, (a + 1) * dh)
    qs, ks, vs, gcs, grs = (qf, qb), (kf, kb), (vf, vb), (gcf, gcb), (grf, grb)
    qb16 = [qs[d][:, cols(a)] for a, d in cs]
    kb16 = [ks[d][:, cols(a)] for a, d in cs]
    vb16 = [vs[d][:, cols(a)] for a, d in cs]
    gc = [gcs[d][a] for a, d in cs]
    gr = [grs[d][a] for a, d in cs]
    q, k = (each(lambda x: x.astype(F32), x16) for x16 in (qb16, kb16))
    L = q[0].shape[0]
    row = lax.broadcasted_iota(jnp.int32, (L, L), 0)
    col = lax.broadcasted_iota(jnp.int32, (L, L), 1)
    seen = (col <= row, col >= row)
    tri = [jnp.where(m, 1.0, 0.0) for m in seen]
    b_col = [_dot_01_lhs(tri[d], g, 2)[:, 2 + d:3 + d] for (a, d), g in zip(cs, gc)]
    b_row = [_dot_01_rhs(g, tri[1 - d], 2)[2 + d:3 + d, :] for (a, d), g in zip(cs, gr)]
    ig_col = [g[:, d:d + 1] for (a, d), g in zip(cs, gc)]
    ig_row = [g[d:d + 1, :] for (a, d), g in zip(cs, gr)]
    m_st = [m_ref[a, d, 0:1, 0:1] for a, d in cs]
    c_st = [c_ref[a, d] for a, d in cs]
    n_st = [n_ref[a, d] for a, d in cs]

    dlog = [jnp.where(seen[d], bc - br + ir, -jnp.inf) for (a, d), bc, br, ir in zip(cs, b_col, b_row, ig_row)]
    m_inter = each(jnp.add, b_col, m_st)
    m_t = each(lambda mi, dl: jnp.maximum(mi, jnp.max(dl, axis=1, keepdims=True)), m_inter, dlog)
    qk = each(lambda x, y: _d(x, y, NT), qb16, kb16)
    s = each(lambda x, dl, mt: x * jnp.exp(dl - mt), qk, dlog, m_t)
    dec = each(lambda mi, mt: jnp.exp(mi - mt), m_inter, m_t)
    sv = each(lambda x, y: _d(x.astype(BF16), y, NN), s, vb16)
    qc = each(lambda x, y: _d(x, y.astype(BF16), NN), qb16, c_st)
    num = each(lambda x, dc, y: x + dc * y, sv, dec, qc)
    den = each(lambda x, dc, qq, nn: jnp.sum(x, axis=1, keepdims=True) + dc * jnp.sum(qq * nn, axis=1, keepdims=True),
               s, dec, q, n_st)
    h = each(lambda nu, de, mt: nu / jnp.maximum(jnp.abs(de), jnp.exp(-mt)), num, den, m_t)
    h_refs = (hf_ref, hb_ref)
    for (a, d), x in zip(cs, h):
        h_refs[d][:, cols(a)] = x

    b_last = [bc[0:1, :] if d else bc[L - 1:L, :] for (a, d), bc in zip(cs, b_col)]
    w_c = each(lambda bl, bc, ic: bl - bc + ic, b_last, b_col, ig_col)
    m_new = each(lambda bl, ms, w: jnp.maximum(bl + ms, jnp.max(w, axis=0, keepdims=True)), b_last, m_st, w_c)
    a_c = each(lambda w, mn: jnp.exp(w - mn), w_c, m_new)
    g_prev = each(lambda bl, ms, mn: jnp.exp(bl + ms - mn), b_last, m_st, m_new)
    ak = each(jnp.multiply, a_c, k)
    kv = each(lambda x, y: _d(x.T.astype(BF16), y, NN), ak, vb16)
    for i, (a, d) in enumerate(cs):
        c_ref[a, d] = g_prev[i] * c_st[i] + kv[i]
        n_ref[a, d] = g_prev[i] * n_st[i] + jnp.sum(ak[i], axis=0, keepdims=True)
        m_ref[a, d] = jnp.broadcast_to(m_new[i], m_ref.shape[2:])


def _bwd_chunk(c, nc0, nc):
    return jnp.where(c < nc0, nc0 - 1 - c, nc - 1 - (c - nc0))


def mlstm_scan(qk, v, gcol, grow, n_ctx):
    t = v.shape[0]
    L, dh, hs = MLSTM_CHUNK, MLSTM_DH, ML_HEADS_PER_STEP
    ng = MLSTM_HEADS // hs
    nc, nc0 = t // L, n_ctx // L
    idf = lambda c: c
    idb = lambda c: _bwd_chunk(c, nc0, nc)
    qkv = lambda f, off: pl.BlockSpec((L, hs * dh), lambda h, c: (f(c), off + h))
    gc_spec = lambda f: pl.BlockSpec((hs, L, LANES), lambda h, c: (h, f(c), 0))
    gr_spec = lambda f: pl.BlockSpec((hs, 8, L), lambda h, c: (h, 0, f(c)))
    return pl.pallas_call(
        _mlstm_body,
        grid=(ng, nc),
        in_specs=[qkv(idf, 0), qkv(idf, ng), qkv(idf, 0), gc_spec(idf), gr_spec(idf),
                  qkv(idb, 0), qkv(idb, ng), qkv(idb, 0), gc_spec(idb), gr_spec(idb)],
        out_specs=[qkv(idf, 0), qkv(idb, 0)],
        out_shape=[jax.ShapeDtypeStruct((t, D_MODEL), F32)] * 2,
        scratch_shapes=[pltpu.VMEM((hs, 2, dh, dh), F32), pltpu.VMEM((hs, 2, 1, dh), F32),
                        pltpu.VMEM((hs, 2, 8, LANES), F32)],
        compiler_params=pltpu.CompilerParams(
            dimension_semantics=("arbitrary", "arbitrary"), vmem_limit_bytes=VMEM_LIMIT),
        name="mlstm_scan",
    )(qk, qk, v, gcol, grow, qk, qk, v, gcol, grow)


RW_PASSES = 1
RW_CHUNKS_PER_STEP = 4


def _stack2(x, lane_head):
    return jnp.concatenate([jnp.where(lane_head == 0, x, 0.0), jnp.where(lane_head == 1, x, 0.0)], axis=0)


def _unstack2(x):
    L = x.shape[0] // 2
    return x[:L] + x[L:]


def _rwkv_chunks(chains, k_k, k_a):
    L = chains[0][0].shape[0]
    n2 = 2 * L
    p = RW_PASSES
    ds = [c[5] for c in chains]
    each = lambda f, *ls: [f(*xs) for xs in zip(*ls)]
    dot = lambda dims: (lambda a, b: _mdot(a, b, dims, p))
    lw, r, k_raw, v, a = ([c[i] for c in chains] for i in range(5))

    lane_r = lax.broadcasted_iota(jnp.int32, (LANES, LANES), 0) // RWKV_N
    lane_c = lax.broadcasted_iota(jnp.int32, (LANES, LANES), 1) // RWKV_N
    head_ones = jnp.where(lane_r == lane_c, 1.0, 0.0)
    kkr = each(jnp.multiply, k_raw, k_k)
    ss = each(lambda x: _dot_01_rhs(x * x, head_ones, 2), kkr)
    kap = each(lambda x, q: x / jnp.maximum(jnp.sqrt(q), 1e-12), kkr, ss)
    alp = each(jnp.multiply, kap, a)
    k = each(lambda x, y, z: x * (1.0 + (y - 1.0) * z), k_raw, a, k_a)

    row = lax.broadcasted_iota(jnp.int32, (L, L), 0)
    col = lax.broadcasted_iota(jnp.int32, (L, L), 1)
    tris = (jnp.where(col <= row, 1.0, 0.0), jnp.where(col >= row, 1.0, 0.0))
    lp = [_dot_01_lhs(tris[d], x, 2) for d, x in zip(ds, lw)]
    lp_end = [x[0:1, :] if d else x[L - 1:L, :] for d, x in zip(ds, lp)]
    e_neg = each(lambda x: jnp.exp(-x), lp)
    e_end = each(lambda x, xe: jnp.exp(xe - x), lp, lp_end)
    kap_t = each(lambda x, y, z: x * jnp.exp(y - z), kap, lp, lw)
    r_t = each(lambda x, y: x * jnp.exp(y), r, lp)
    k_h = each(jnp.multiply, k, e_neg)
    a_h = each(jnp.multiply, alp, e_neg)
    k_e = each(jnp.multiply, k, e_end)
    a_e = each(jnp.multiply, alp, e_end)

    lane_head = lax.broadcasted_iota(jnp.int32, (L, LANES), 1) // RWKV_N
    st = lambda x: _stack2(x, lane_head)
    kap_s, r_s, v_s, k_s, a_s = (each(st, x) for x in (kap_t, r_t, v, k_e, a_e))
    rhs_k = each(lambda x: jnp.concatenate([x, x], axis=0), k_h)
    rhs_a = each(lambda x: jnp.concatenate([x, x], axis=0), a_h)

    row2 = lax.broadcasted_iota(jnp.int32, (n2, n2), 0)
    col2 = lax.broadcasted_iota(jnp.int32, (n2, n2), 1)
    same_head = (row2 // L) == (col2 // L)
    strict = (same_head & (col2 < row2), same_head & (col2 > row2))
    incl = (same_head & (col2 <= row2), same_head & (col2 >= row2))
    zero = jnp.zeros((n2, n2), F32)
    masked = lambda masks: (lambda d, x: jnp.where(masks[d], x, zero))
    kr_s = each(lambda x, y: jnp.concatenate([x, y], axis=0), kap_s, r_s)
    p_a = each(dot(NT), kr_s, rhs_a)
    p_k = each(dot(NT), kr_s, rhs_k)
    n_ka = each(masked(strict), ds, each(lambda x: x[:n2], p_a))
    m_kk = each(masked(strict), ds, each(lambda x: x[:n2], p_k))
    a_rk = each(masked(incl), ds, each(lambda x: x[n2:], p_k))
    a_ra = each(masked(incl), ds, each(lambda x: x[n2:], p_a))

    b16 = (row2 // 16) == (col2 // 16)
    b32 = (row2 // 32) == (col2 // 32)
    eye = jnp.where(row2 == col2, 1.0, 0.0)
    n16 = each(lambda x: jnp.where(b16, x, zero), n_ka)
    n_2 = each(dot(NN), n16, n16)
    n_4 = each(dot(NN), n_2, n_2)
    n_8 = each(dot(NN), n_4, n_4)
    inv = each(lambda x: eye - x, n16)
    for pw in (n_2, n_4, n_8):
        inv = each(jnp.add, inv, each(dot(NN), inv, pw))
    for sel in (lambda x: jnp.where(b32 & ~b16, x, zero), lambda x: jnp.where(b32, zero, x)):
        t1 = each(dot(NN), inv, each(sel, n_ka))
        inv = each(jnp.subtract, inv, each(dot(NN), t1, inv))

    mav = each(dot(NN), each(lambda x, y: jnp.concatenate([x, y], axis=0), m_kk, a_rk), v_s)
    mv = each(lambda x: x[:n2], mav)
    av = each(lambda x: x[n2:], mav)
    w1u0 = each(dot(NN), inv, each(lambda x, y: jnp.concatenate([x, y], axis=1), kap_s, mv))
    ar = each(dot(NN), a_ra, w1u0)
    r2 = each(lambda x, y: _unstack2(x - y[:, :LANES]), r_s, ar)
    y0 = each(lambda x, y: _unstack2(x - y[:, LANES:]), av, ar)
    w1_t = each(lambda x: x[:, :LANES].T, w1u0)
    u0_t = each(lambda x: x[:, LANES:].T, w1u0)
    v_t = each(lambda x: x.T, v_s)
    wua = each(dot(NN), each(lambda x, y: jnp.concatenate([x, y], axis=0), w1_t, u0_t), a_s)
    wa = each(lambda x: x[:LANES], wua)
    ua = each(lambda x: x[LANES:], wua)
    vk = each(dot(NN), v_t, k_s)
    g = each(lambda xe, x: jnp.where(row2 == col2, jnp.exp(xe), zero) - x, lp_end, wa)
    b = each(jnp.subtract, vk, ua)
    return list(zip(y0, r2, g, b))


RW_PAIRS_PER_STEP = 2


def _rwkv_body(lwf, rf, kf, vf, af, lwb, rb, kb, vb, ab, kk_ref, ka_ref, yf_ref, yb_ref, s_ref):
    @pl.when(pl.program_id(1) == 0)
    def _():
        s_ref[...] = jnp.zeros(s_ref.shape, F32)

    L = RWKV_CHUNK
    n = RW_CHUNKS_PER_STEP
    rows = lambda j: slice(j * L, (j + 1) * L)
    lanes = lambda q: slice(q * LANES, (q + 1) * LANES)
    refs = ((lwf, rf, kf, vf, af), (lwb, rb, kb, vb, ab))
    y_refs = (yf_ref, yb_ref)
    visit = [(d, j if d == 0 else n - 1 - j) for j in range(n) for d in (0, 1)]
    pairs = range(RW_PAIRS_PER_STEP)
    pre = _rwkv_chunks([tuple(ref[rows(j), lanes(q)] for ref in refs[d]) + (d,) for d, j in visit for q in pairs],
                       [kk_ref[:, lanes(q)] for _ in visit for q in pairs],
                       [ka_ref[:, lanes(q)] for _ in visit for q in pairs])
    s = [[s_ref[q, 0], s_ref[q, 1]] for q in pairs]
    chain = iter(pre)
    for d, j in visit:
        for q in pairs:
            y0, r2, g, b = next(chain)
            y_refs[d][rows(j), lanes(q)] = y0 + _mdot(r2, s[q][d], NT, RW_PASSES)
            s[q][d] = _mdot(s[q][d], g, NN, RW_PASSES) + b
    for q in range(RW_PAIRS_PER_STEP):
        s_ref[q, 0] = s[q][0]
        s_ref[q, 1] = s[q][1]


def rwkv_scan(r, k, v, lw, a, k_k, k_a, n_ctx):
    t, d = r.shape
    L = RWKV_CHUNK
    blk = RW_CHUNKS_PER_STEP * L
    w = RW_PAIRS_PER_STEP * LANES
    assert 2 * L == LANES and t % blk == 0 and n_ctx % blk == 0 and d % w == 0
    nc, nc0 = t // blk, n_ctx // blk
    nh = d // w
    fwd = pl.BlockSpec((blk, w), lambda h, c: (c, h))
    bwd = pl.BlockSpec((blk, w), lambda h, c: (_bwd_chunk(c, nc0, nc), h))
    bwd2 = pl.BlockSpec((blk, w), lambda h, c: (_bwd_chunk(c, nc0, nc), nh + h))
    par = pl.BlockSpec((1, w), lambda h, c: (0, h))
    return pl.pallas_call(
        _rwkv_body,
        grid=(nh, nc),
        in_specs=[fwd] * 5 + [bwd2, bwd, bwd, bwd, bwd2, par, par],
        out_specs=[fwd, bwd],
        out_shape=[jax.ShapeDtypeStruct((t, d), F32)] * 2,
        scratch_shapes=[pltpu.VMEM((RW_PAIRS_PER_STEP, 2, LANES, LANES), F32)],
        compiler_params=pltpu.CompilerParams(
            dimension_semantics=("arbitrary", "arbitrary"), vmem_limit_bytes=VMEM_LIMIT),
        name="rwkv7_scan",
    )(lw, r, k, v, a, lw, r, k, v, a, k_k.reshape(1, d), k_a.reshape(1, d))


MIX_TM = 256


def _mix_body(cur_ref, prev_ref, next_ref, mu_ref, *o_refs, nct, nt):
    i = pl.program_id(0)
    is_ctx = i < nct
    cur = cur_ref[...]
    tm, d = cur.shape
    qd = d // 4
    above = jnp.where(is_ctx | (i == nct), 0.0, prev_ref[...])
    below = jnp.where(is_ctx | (i == nt - 1), 0.0, next_ref[...])
    ext = jnp.concatenate([above, cur, below], axis=0)
    n = ext.shape[0]
    colid = lax.broadcasted_iota(jnp.int32, (n, 2 * qd), 0) % GRID_W
    left = jnp.where(is_ctx | (colid != 0), pltpu.roll(ext[:, :2 * qd], 1, 0), 0.0)
    colid3 = lax.broadcasted_iota(jnp.int32, (n, 3 * qd), 0) % GRID_W
    right = jnp.where(is_ctx | (colid3 != GRID_W - 1), pltpu.roll(ext[:, qd:], n - 1, 0), 0.0)
    mid = slice(GRID_W, GRID_W + tm)
    sh = jnp.concatenate([
        left[mid, :qd],
        jnp.where(is_ctx, left[mid, qd:], right[mid, :qd]),
        jnp.where(is_ctx, right[mid, qd:2 * qd], ext[0:tm, 2 * qd:3 * qd]),
        jnp.where(is_ctx, right[mid, 2 * qd:], ext[2 * GRID_W:2 * GRID_W + tm, 3 * qd:])], axis=1)
    dx = sh - cur
    for b, o_ref in enumerate(o_refs):
        o_ref[...] = (cur + dx * mu_ref[b:b + 1, :]).astype(o_ref.dtype)


def rwkv_mix(h, mu, n_ctx):
    t, d = h.shape
    tm = MIX_TM
    assert t % tm == 0 and n_ctx % tm == 0 and tm % GRID_W == 0
    r = tm // GRID_W
    nt, nu = t // tm, t // GRID_W
    nb = mu.shape[0]
    return pl.pallas_call(
        functools.partial(_mix_body, nct=n_ctx // tm, nt=nt),
        grid=(nt,),
        in_specs=[pl.BlockSpec((tm, d), lambda i: (i, 0)),
                  pl.BlockSpec((GRID_W, d), lambda i: (jnp.maximum(i * r - 1, 0), 0)),
                  pl.BlockSpec((GRID_W, d), lambda i: (jnp.minimum((i + 1) * r, nu - 1), 0)),
                  pl.BlockSpec((nb, d), lambda i: (0, 0))],
        out_specs=[pl.BlockSpec((tm, d), lambda i: (i, 0))] * nb,
        out_shape=[jax.ShapeDtypeStruct((t, d), BF16)] * nb,
        compiler_params=pltpu.CompilerParams(dimension_semantics=("arbitrary",), vmem_limit_bytes=VMEM_LIMIT),
        name="rwkv_mix",
    )(h, h, h, mu)


def _rwkv_out_body(yf_ref, yb_ref, r_ref, k_ref, v_ref, g_ref, af_ref, ab_ref, p_ref, e_ref, et_ref, w_ref,
                   *ln_refs, **ln_static):
    e, et = e_ref[...], et_ref[...]
    head_sum = lambda x: _dot_01_rhs(_dot_01_rhs(x, e, 2), et, 2)
    k_a, r_k, gain, bias = (p_ref[n:n + 1, :] for n in range(4))
    y = yf_ref[...] + yb_ref[...]
    yc = y - head_sum(y) * (1.0 / RWKV_N)
    var = head_sum(yc * yc) * (1.0 / RWKV_N)
    yn = yc * lax.rsqrt(var + RWKV_GN_EPS) * gain + bias
    kbar = k_ref[...] * (1.0 + (0.5 * (af_ref[...] + ab_ref[...]) - 1.0) * k_a)
    bonus = head_sum(r_ref[...] * kbar * r_k) * v_ref[...]
    z = (yn + bonus) * g_ref[...]
    _residual_ln_mod(_d(z.astype(BF16), w_ref[...], NN), *ln_refs, **ln_static)


def rwkv_out(yf, yb, r, k, v, g, a, k_a, r_k, lnx_g, lnx_b, w_out, xs, ln):
    t, d = yf.shape
    row_map, ln_in, ln_ops, out_specs, out_shape, static = _ln_operands(xs, ln)
    row = pl.BlockSpec((LN_TM, d), row_map)
    row_b = pl.BlockSpec((LN_TM, d), lambda i: (row_map(i)[0], 1))
    const = lambda shape: pl.BlockSpec(shape, lambda i: (0, 0))
    e = (jnp.arange(d)[:, None] // RWKV_N == jnp.arange(LANES)[None, :]).astype(BF16)
    params = jnp.stack([k_a, r_k, lnx_g, lnx_b])
    return pl.pallas_call(
        functools.partial(_rwkv_out_body, **static),
        grid=(ln["pad_to"] // LN_TM,),
        in_specs=[row] * 6 + [row, row_b, const((4, d)), const((d, LANES)), const((LANES, d)), const((d, d))] + ln_in,
        out_specs=out_specs,
        out_shape=out_shape,
        compiler_params=pltpu.CompilerParams(dimension_semantics=("arbitrary",), vmem_limit_bytes=VMEM_LIMIT),
        name="rwkv_out",
    )(yf, yb, r, k, v, g, a, a, params, e, e.T, w_out.astype(BF16), *ln_ops)


def _sort16_pairs():
    pairs = []
    n, p = 16, 1
    while p < n:
        k = p
        while k >= 1:
            for j in range(k % p, n - k, 2 * k):
                for i in range(min(k, n - j - k)):
                    if (i + j) // (2 * p) == (i + j + k) // (2 * p):
                        pairs.append((i + j, i + j + k))
            k //= 2
        p *= 2
    return pairs


def _top16_sorted(x):
    v = [x[8 * i:8 * i + 8] for i in range(16)]

    def exchange(i, j):
        v[i], v[j] = jnp.maximum(v[i], v[j]), jnp.minimum(v[i], v[j])

    for i, j in _sort16_pairs():
        exchange(i, j)
    for shift in (4, 2, 1):
        other = [pltpu.roll(t, shift, 0) for t in v]
        v = [jnp.maximum(v[i], other[15 - i]) for i in range(16)]
        for dist in (8, 4, 2, 1):
            for i in range(16):
                if i & dist == 0:
                    exchange(i, i + dist)
    return [t[0:1] for t in v]


def _peer_stats(hx_ref, wk_ref, sc_ref, hxb_ref, n_ref, f0_ref, r1_ref, e1_ref):
    tm = hx_ref.shape[0]
    hxb_ref[...] = hx_ref[...].astype(F32).T.astype(BF16)
    rows_per = 4 * LANES

    def scores(c, carry):
        rows = pl.ds(pl.multiple_of(c * rows_per, rows_per), rows_per)
        sc_ref[rows, :] = _d(wk_ref[rows, :], hxb_ref[...], NN)
        return carry

    lax.fori_loop(0, wk_ref.shape[0] // rows_per, scores, 0)

    def block(tb, carry):
        nq = PEER_STAT_LANES // LANES
        lanes = [pl.ds(pl.multiple_of(tb * PEER_STAT_LANES + q * LANES, LANES), LANES) for q in range(nq)]
        head_row = lax.broadcasted_iota(jnp.int32, (PEER_HEADS, LANES), 0)
        zero = jnp.zeros((PEER_HEADS, LANES), F32)

        def extract(q):
            def body(h, tops):
                tops = [list(t) for t in tops]
                rows = [pl.ds(pl.multiple_of((2 * h + p) * N_KEYS, N_KEYS), N_KEYS) for p in range(2)]
                cur = [sc_ref[rows[p], lanes[q]] for p in range(2)]
                top = [_top16_sorted(c) for c in cur]
                rank = jnp.zeros((N_KEYS, LANES), F32)
                for b in range(PEER_TOPK):
                    rank = jnp.where(cur[1] < top[1][b], float(b + 1), rank)
                for p in range(2):
                    for a in range(PEER_TOPK):
                        tops[p][a] = jnp.where(head_row == h, top[p][a], tops[p][a])
                r1_ref[h, :, lanes[q]] = rank.astype(BF16)
                return tuple(tuple(t) for t in tops)
            return lax.fori_loop(0, PEER_HEADS, body, ((zero,) * PEER_TOPK,) * 2)

        tops = [extract(q) for q in range(nq)]
        join = lambda p, a: jnp.concatenate([tops[q][p][a] for q in range(nq)], axis=1)
        top0 = [join(0, a) for a in range(PEER_TOPK)]
        top1 = [join(1, b) for b in range(PEER_TOPK)]
        cands = [top0[a] + top1[b]
                 for a in range(PEER_TOPK) for b in range(PEER_TOPK) if (a + 1) * (b + 1) <= PEER_TOPK]
        c_max = cands[0]
        z = jnp.zeros_like(c_max)
        tau = c_max
        for a in range(PEER_TOPK):
            tau = functools.reduce(jnp.maximum, cands)
            z = z + jnp.exp(tau - c_max)
            cands = [jnp.where(cd >= tau, -jnp.inf, cd) for cd in cands]
        inv_z = 1.0 / z
        nb = []
        for a in range(PEER_TOPK):
            cnt = jnp.zeros_like(tau)
            for b in range(PEER_TOPK):
                cnt = jnp.where(top0[a] + top1[b] >= tau, float(b + 1), cnt)
            nb.append(cnt)

        def factors(h, carry):
            for q in range(nq):
                sub = slice(q * LANES, (q + 1) * LANES)
                row_of = lambda x: jnp.max(jnp.where(head_row == h, x[:, sub], -jnp.inf), axis=0, keepdims=True)
                s0 = sc_ref[pl.ds(pl.multiple_of(2 * h * N_KEYS, N_KEYS), N_KEYS), lanes[q]]
                s1 = sc_ref[pl.ds(pl.multiple_of((2 * h + 1) * N_KEYS, N_KEYS), N_KEYS), lanes[q]]
                n = jnp.zeros((N_KEYS, LANES), F32)
                for a in reversed(range(PEER_TOPK)):
                    n = jnp.where(s0 >= row_of(top0[a]), row_of(nb[a]), n)
                n_ref[h, :, lanes[q]] = n
                f0_ref[h, :, lanes[q]] = jnp.exp(s0 - row_of(top0[0])) * row_of(inv_z)
                e1_ref[h, :, lanes[q]] = jnp.exp(s1 - row_of(top1[0])).astype(BF16)
            return carry

        lax.fori_loop(0, PEER_HEADS, factors, 0)
        return carry

    lax.fori_loop(0, tm // PEER_STAT_LANES, block, 0)


PEER_I_GROUP = 4
PEER_STREAMS = 2


def _peer_body(hx_ref, wk_ref, *rest):
    u_refs, vt_refs = rest[:PEER_STREAMS], rest[PEER_STREAMS:2 * PEER_STREAMS]
    o_ref, sc_ref, hxb_ref, n_ref, f0_ref, r1_ref, e1_ref, w_ref, acc_ref = rest[2 * PEER_STREAMS:]
    e = pl.program_id(1)
    tm = hx_ref.shape[0]
    nsb = tm // PEER_SB

    @pl.when(e == 0)
    def _():
        _peer_stats(hx_ref, wk_ref, sc_ref, hxb_ref, n_ref, f0_ref, r1_ref, e1_ref)
        acc_ref[...] = jnp.zeros(acc_ref.shape, F32)

    i_rows = pl.ds(pl.multiple_of(e * PEER_I_BLOCK, PEER_I_BLOCK), PEER_I_BLOCK)

    def activations(sb):
        parts = []
        for u_ref in u_refs:
            act = _d(u_ref[...], hxb_ref[:, sb * PEER_SB:(sb + 1) * PEER_SB], NN)
            act = act.astype(BF16)
            parts.append(0.5 * act * (1.0 + lax.erf(act * (2.0 ** -0.5))))
        return parts

    def gates(sb, act):
        for hb in range(PEER_SB // LANES):
            lanes = slice(sb * PEER_SB + hb * LANES, sb * PEER_SB + (hb + 1) * LANES)
            sub = slice(hb * LANES, (hb + 1) * LANES)
            n8 = [n_ref[h, i_rows, lanes] for h in range(PEER_HEADS)]
            f8 = [f0_ref[h, i_rows, lanes] for h in range(PEER_HEADS)]
            bcast = lambda x, ii: jnp.broadcast_to(x[ii:ii + 1], (N_KEYS, LANES)).astype(BF16)
            for ig in range(0, PEER_I_BLOCK, PEER_I_GROUP):
                g = [jnp.zeros((N_KEYS, LANES), BF16) for _ in range(PEER_I_GROUP)]
                for h in range(PEER_HEADS):
                    r1 = r1_ref[h, :, lanes]
                    e1 = e1_ref[h, :, lanes]
                    for k in range(PEER_I_GROUP):
                        ii = ig + k
                        g[k] = g[k] + jnp.where(r1 < bcast(n8[h], ii), e1 * bcast(f8[h], ii), jnp.zeros_like(e1))
                for k in range(PEER_I_GROUP):
                    rows = slice((ig + k) * N_KEYS, (ig + k + 1) * N_KEYS)
                    part, off = divmod((ig + k) * N_KEYS, PEER_TE // PEER_STREAMS)
                    w_ref[sb, rows, sub] = g[k] * act[part][off:off + N_KEYS, sub]

    def accumulate(sb):
        cols = slice(sb * PEER_SB, (sb + 1) * PEER_SB)
        dr = acc_ref.shape[0] // PEER_STREAMS
        for k, vt_ref in enumerate(vt_refs):
            acc_ref[k * dr:(k + 1) * dr, cols] += _d(vt_ref[...], w_ref[sb], NN)

    act = activations(0)
    for sb in range(nsb):
        nxt = activations(sb + 1) if sb + 1 < nsb else None
        gates(sb, act)
        accumulate(sb)
        act = nxt

    @pl.when(e == pl.num_programs(1) - 1)
    def _():
        o_ref[...] = acc_ref[...].T


def _fold_body(k_ref, w_ref, o_ref):
    o_ref[...] = _mdot(k_ref[...], w_ref[...], NN, 6)


def peer_fold_keys(wq, keys):
    d = wq.shape[0]
    nhp, nk, dk = keys.shape
    wqt = wq.T.reshape(nhp, dk, d)
    return pl.pallas_call(
        _fold_body,
        grid=(nhp,),
        in_specs=[pl.BlockSpec((None, nk, dk), lambda i: (i, 0, 0)),
                  pl.BlockSpec((None, dk, d), lambda i: (i, 0, 0))],
        out_specs=pl.BlockSpec((nk, d), lambda i: (i, 0)),
        out_shape=jax.ShapeDtypeStruct((nhp * nk, d), F32),
        compiler_params=pltpu.CompilerParams(dimension_semantics=("arbitrary",), vmem_limit_bytes=VMEM_LIMIT),
        name="peer_fold_keys",
    )(keys, wqt)


def peer(hx, wk, u_bf, vt_bf):
    t, d = hx.shape
    tm = PEER_TM
    assert t % tm == 0
    ne = u_bf.shape[0] // PEER_TE
    h = PEER_HEADS
    ns = PEER_STREAMS
    return pl.pallas_call(
        _peer_body,
        grid=(t // tm, ne),
        in_specs=[pl.BlockSpec((tm, d), lambda i, e: (i, 0), pipeline_mode=pl.Buffered(1)),
                  pl.BlockSpec(wk.shape, lambda i, e: (0, 0), pipeline_mode=pl.Buffered(1)),
                  *[pl.BlockSpec((PEER_TE // ns, d), functools.partial(lambda i, e, k: (e * ns + k, 0), k=k))
                    for k in range(ns)],
                  *[pl.BlockSpec((None, d // ns, PEER_TE), functools.partial(lambda i, e, k: (e, k, 0), k=k))
                    for k in range(ns)]],
        out_specs=pl.BlockSpec((tm, d), lambda i, e: (i, 0)),
        out_shape=jax.ShapeDtypeStruct((t, d), F32),
        scratch_shapes=[pltpu.VMEM((2 * h * N_KEYS, tm), F32),
                        pltpu.VMEM((d, tm), BF16),
                        pltpu.VMEM((h, N_KEYS, tm), F32), pltpu.VMEM((h, N_KEYS, tm), F32),
                        pltpu.VMEM((h, N_KEYS, tm), BF16), pltpu.VMEM((h, N_KEYS, tm), BF16),
                        pltpu.VMEM((tm // PEER_SB, PEER_TE, PEER_SB), BF16), pltpu.VMEM((d, tm), F32)],
        compiler_params=pltpu.CompilerParams(
            dimension_semantics=("arbitrary", "arbitrary"), vmem_limit_bytes=VMEM_LIMIT),
        name="peer_dense",
    )(hx, wk.astype(BF16), *([u_bf] * ns), *([vt_bf] * ns))


def _ln_mod_body(y_ref, *ln_refs, **ln_static):
    _residual_ln_mod(y_ref[...], *ln_refs, **ln_static)


def ln_mod(xs, y, ln):
    d = xs.shape[1]
    row_map, ln_in, ln_ops, out_specs, out_shape, static = _ln_operands(xs, ln)
    return pl.pallas_call(
        functools.partial(_ln_mod_body, **static),
        grid=(ln["pad_to"] // LN_TM,),
        in_specs=[pl.BlockSpec((LN_TM, d), row_map)] + ln_in,
        out_specs=out_specs,
        out_shape=out_shape,
        compiler_params=pltpu.CompilerParams(dimension_semantics=("arbitrary",), vmem_limit_bytes=VMEM_LIMIT),
        name="ln_mod",
    )(y, *ln_ops)


def _pad_cols(w, n):
    return jnp.pad(w, ((0, 0), (0, n - w.shape[1])))


def _mlstm_layer(h, n_ctx, w_in, b_in, conv_w, conv_b, hn_g, w_out, xs, ln):
    d = D_MODEL
    t = h.shape[0]
    qk_pre = matmul(h, w_in[:, :2 * d], b_in[:2 * d])
    v = matmul(h, w_in[:, 2 * d:3 * d], b_in[2 * d:3 * d], out_dtype=BF16)
    o = matmul(h, w_in[:, 3 * d:4 * d], b_in[3 * d:4 * d], act="sigmoid", out_dtype=BF16)
    g = matmul(h, _pad_cols(w_in[:, 4 * d:], LANES), jnp.pad(b_in[4 * d:], (0, LANES - 4 * MLSTM_HEADS)))
    g = g[:, :4 * MLSTM_HEADS].reshape(t, 4, MLSTM_HEADS)
    g = jnp.concatenate([g[:, :2], jax.nn.log_sigmoid(g[:, 2:])], axis=1)
    gh = jnp.transpose(g, (2, 0, 1))
    gcol = jnp.pad(gh, ((0, 0), (0, 0), (0, LANES - 4)))
    grow = jnp.pad(jnp.transpose(gh, (0, 2, 1)), ((0, 0), (0, 4), (0, 0)))
    qk = mlstm_conv(qk_pre, conv_w, conv_b, n_ctx)
    hf, hb = mlstm_scan(qk, v, gcol, grow, n_ctx)
    return mlstm_out(hf, hb, o, hn_g, w_out, xs, ln)


def _rwkv_layer(h, n_ctx, mu, w_rkv, w0, w1, w2, a0, a1, a2, g1, g2, k_k, k_a, r_k, lnx_g, lnx_b, w_out, xs, ln):
    d = D_MODEL
    xm = rwkv_mix(h, mu, n_ctx)
    r = matmul(xm[0], w_rkv[0])
    k = matmul(xm[1], w_rkv[1])
    v = matmul(xm[2], w_rkv[2])

    def lora_pair(x, w_in, w_mid, bias, act_mid, act_out):
        rank = w_in.shape[-1]
        w_a = _pad_cols(jnp.concatenate([w_in[0], w_in[1]], axis=1), LANES)
        zpad = jnp.zeros((rank, d), F32)
        w_b = jnp.concatenate([jnp.concatenate([w_mid[0], zpad], axis=1),
                               jnp.concatenate([zpad, w_mid[1]], axis=1)], axis=0)
        w_b = jnp.pad(w_b, ((0, LANES - 2 * rank), (0, 0)))
        mid = matmul(x, w_a, act=act_mid, out_dtype=BF16)
        return matmul(mid, w_b, jnp.concatenate([bias[0], bias[1]]), act=act_out)

    lw = lora_pair(xm[3], w1, w2, w0, "tanh", "logdecay")
    a = lora_pair(xm[4], a1, a2, a0, None, "sigmoid")
    gpad = 2 * LANES
    gg = matmul(xm[5], _pad_cols(g1, gpad), act="sigmoid", out_dtype=BF16)
    g = matmul(gg, jnp.pad(g2, ((0, gpad - g1.shape[1]), (0, 0))))
    yf, yb = rwkv_scan(r, k, v, lw, a, k_k, k_a, n_ctx)
    return rwkv_out(yf, yb, r, k, v, g, a, k_a, r_k, lnx_g, lnx_b, w_out, xs, ln)


def _forward(x, c, ctx, c_ctx, ada_w, ada_b, ln_g, ln_b,
             ml_w_in, ml_b_in, ml_conv_w, ml_conv_b, ml_hn_g, ml_w_out,
             rw_mu, rw_w_rkv, rw_w0, rw_w1, rw_w2, rw_a0, rw_a1, rw_a2, rw_g1, rw_g2,
             rw_k_k, rw_k_a, rw_r_k, rw_lnx_g, rw_lnx_b, rw_w_out,
             pk_wq, pk_keys, pk_u, pk_v):
    d = D_MODEL
    n_ctx = ctx.shape[1]
    xs = jnp.concatenate([ctx[0], x[0]], axis=0)
    t = xs.shape[0]
    t_pad = -(-t // PEER_TM) * PEER_TM
    s_in = jnp.zeros((8, d), F32).at[0].set(jax.nn.silu(c[0])).at[1].set(jax.nn.silu(c_ctx))
    depth = ada_w.shape[0]
    mods = [matmul(s_in, ada_w[i], ada_b[i], passes=3) for i in range(depth)]
    is_ctx = (jnp.arange(t) < n_ctx)[:, None]
    m0 = [jnp.where(is_ctx, mods[0][1, n * d:(n + 1) * d], mods[0][0, n * d:(n + 1) * d]) for n in range(2)]
    mixer_dtype = lambda i: BF16 if i % 2 == 0 else F32
    h = (xs * (1.0 + m0[1]) + m0[0]).astype(mixer_dtype(0))
    for i in range(depth):
        j = i // 2
        ln1 = dict(mod_gate=mods[i], gate_col=2, mod_next=mods[i], mod_col=3, ln_g=ln_g[i, 0], ln_b=ln_b[i, 0],
                   n_ctx=n_ctx, pad_to=t_pad, h_dtype=BF16)
        if i % 2 == 0:
            xs, h = _mlstm_layer(h, n_ctx, ml_w_in[j], ml_b_in[j], ml_conv_w[j], ml_conv_b[j],
                                 ml_hn_g[j], ml_w_out[j], xs, ln1)
        else:
            xs, h = _rwkv_layer(h, n_ctx, rw_mu[j], rw_w_rkv[j], rw_w0[j], rw_w1[j], rw_w2[j],
                                rw_a0[j], rw_a1[j], rw_a2[j], rw_g1[j], rw_g2[j], rw_k_k[j],
                                rw_k_a[j], rw_r_k[j], rw_lnx_g[j], rw_lnx_b[j], rw_w_out[j], xs, ln1)
        wk = peer_fold_keys(pk_wq[i], pk_keys[i].reshape(2 * PEER_HEADS, N_KEYS, PEER_DQ // 2))
        vt = jnp.swapaxes(pk_v[i].astype(BF16).reshape(-1, PEER_TE, d), 1, 2)
        y = peer(h, wk, pk_u[i].astype(BF16), vt)
        ln2 = dict(mod_gate=mods[i], gate_col=5, mod_next=mods[min(i + 1, depth - 1)], mod_col=0,
                   ln_g=ln_g[i, 1], ln_b=ln_b[i, 1], n_ctx=n_ctx, pad_to=t, h_dtype=mixer_dtype(i + 1))
        xs, h = ln_mod(xs, y, ln2)
    return xs[n_ctx:][None]


def kernel(x, c, ctx, c_ctx, ada_w, ada_b, ln_g, ln_b, ml_w_in, ml_b_in, ml_conv_w, ml_conv_b, ml_hn_g, ml_w_out, rw_mu, rw_w_rkv, rw_w0, rw_w1, rw_w2, rw_a0, rw_a1, rw_a2, rw_g1, rw_g2, rw_k_k, rw_k_a, rw_r_k, rw_lnx_g, rw_lnx_b, rw_w_out, pk_wq, pk_keys, pk_u, pk_v):
    return _forward(x, c, ctx, c_ctx, ada_w, ada_b, ln_g, ln_b,
                    ml_w_in, ml_b_in, ml_conv_w, ml_conv_b, ml_hn_g, ml_w_out,
                    rw_mu, rw_w_rkv, rw_w0, rw_w1, rw_w2, rw_a0, rw_a1, rw_a2, rw_g1, rw_g2,
                    rw_k_k, rw_k_a, rw_r_k, rw_lnx_g, rw_lnx_b, rw_w_out,
                    pk_wq, pk_keys, pk_u, pk_v)
```
